```python
import jax
import jax.numpy as jnp
from jax import lax
import numpy as np

D_MODEL = 1024
BATCH = 4
SEQ = 4096
DEPTH = 1

GRID_W = 64
CTX_LEN = 256
MIX_W = D_MODEL
NA_W = MIX_W // 2
NA_HEADS = 8
NA_HEAD_DIM = NA_W // NA_HEADS
NA_WIN_ROWS = 8
NA_WIN_COLS = 16
GLA_HEADS = 4
GLA_VAL_W = MIX_W - NA_W
GLA_DV = GLA_VAL_W // GLA_HEADS
GLA_DK = GLA_DV // 2
GLA_KEY_W = GLA_HEADS * GLA_DK
GLA_GATE_RANK = 16
GLA_GATE_TAU = 16.0
GLA_CHUNK = 16
ROPE_BASE = 10000.0
N_EXPERTS = 16
EC_CAPACITY_FACTOR = 2
D_EXPERT = 2816
NORM_EPS = 1e-6

KV_SIZES = (NA_W, NA_W, GLA_KEY_W, GLA_VAL_W, 2 * GLA_GATE_RANK)
Q_SIZES = (NA_W, GLA_KEY_W, GLA_VAL_W)
IN_SIZES = KV_SIZES + Q_SIZES
KV_COLS = sum(KV_SIZES)
IN_COLS = sum(IN_SIZES)

kernel_name = 'hybrid_na_gla_ec_diffusion_block'


def _split(t, sizes):
    cuts = [sum(sizes[:i + 1]) for i in range(len(sizes) - 1)]
    return jnp.split(t, cuts, axis=-1)


def _to_heads(t, n_heads):
    b, n, _ = t.shape
    return t.reshape(b, n, n_heads, -1).transpose(0, 2, 1, 3)


def _merge_heads(t):
    b, h, n, d = t.shape
    return t.transpose(0, 2, 1, 3).reshape(b, n, h * d)


def _rms_norm(x, g):
    xf = x.astype(jnp.float32)
    y = xf * lax.rsqrt(jnp.mean(xf * xf, axis=-1, keepdims=True) + NORM_EPS)
    return (y * g.astype(jnp.float32)).astype(x.dtype)


def _modulate(x, g, shift, scale):
    return _rms_norm(x, g) * (1 + scale) + shift


def _rope_1d(x, pos):
    half = x.shape[-1] // 2
    freqs = ROPE_BASE ** (-jnp.arange(half, dtype=jnp.float32) / half)
    ang = pos.astype(jnp.float32)[:, None] * freqs
    cos, sin = jnp.cos(ang).astype(x.dtype), jnp.sin(ang).astype(x.dtype)
    x1, x2 = x[..., :half], x[..., half:]
    return jnp.concatenate([x1 * cos - x2 * sin, x1 * sin + x2 * cos], axis=-1)


def _axial_rope(x, pos_row, pos_col):
    half = x.shape[-1] // 2
    return jnp.concatenate([_rope_1d(x[..., :half], pos_row), _rope_1d(x[..., half:], pos_col)], axis=-1)


def _neighbourhood_attention(q, k, v, k_ctx, v_ctx, rpb):
    b, h, n, dh = q.shape
    rows = n // GRID_W
    kr = min(NA_WIN_ROWS, rows)
    qg = q.reshape(b, h, rows, GRID_W, dh)
    kg = k.reshape(b, h, rows, GRID_W, dh)
    vg = v.reshape(b, h, rows, GRID_W, dh)
    r = jnp.arange(rows)
    row_start = jnp.clip(r - kr // 2, 0, rows - kr)
    key_rows = row_start[:, None] + jnp.arange(kr)
    k_blk = kg[:, :, key_rows]
    v_blk = vg[:, :, key_rows]
    col = jnp.arange(GRID_W)
    col_start = jnp.clip(col - NA_WIN_COLS // 2, 0, GRID_W - NA_WIN_COLS)
    in_win = (col[None, :] >= col_start[:, None]) & (col[None, :] < col_start[:, None] + NA_WIN_COLS)
    dr = key_rows - r[:, None] + NA_WIN_ROWS - 1
    dc = jnp.clip(col[None, :] - col[:, None] + NA_WIN_COLS - 1, 0, 2 * NA_WIN_COLS - 2)
    bias = rpb[:, dr[:, None, :, None], dc[None, :, None, :]]
    scale = dh ** -0.5
    s_win = jnp.einsum('bhrqd,bhrkwd->bhrqkw', qg, k_blk).astype(jnp.float32) * scale + bias.astype(jnp.float32)
    s_win = jnp.where(in_win[:, None, :], s_win, -jnp.inf)
    s_ctx = jnp.einsum('bhrqd,bhcd->bhrqc', qg, k_ctx).astype(jnp.float32) * scale
    scores = jnp.concatenate([s_win.reshape(b, h, rows, GRID_W, kr * GRID_W), s_ctx], axis=-1)
    p = jax.nn.softmax(scores, axis=-1).astype(v.dtype)
    p_win = p[..., :kr * GRID_W].reshape(b, h, rows, GRID_W, kr, GRID_W)
    p_ctx = p[..., kr * GRID_W:]
    o = jnp.einsum('bhrqkw,bhrkwd->bhrqd', p_win, v_blk) + jnp.einsum('bhrqc,bhcd->bhrqd', p_ctx, v_ctx)
    return o.reshape(b, h, n, dh)


def _context_attention(q, k, v):
    s = jnp.einsum('bhqd,bhkd->bhqk', q, k).astype(jnp.float32) * q.shape[-1] ** -0.5
    p = jax.nn.softmax(s, axis=-1).astype(v.dtype)
    return jnp.einsum('bhqk,bhkd->bhqd', p, v)


def _gla_log_decay(a_down, a_up, a_bias, direction):
    z = a_down[..., direction * GLA_GATE_RANK:(direction + 1) * GLA_GATE_RANK] @ a_up[direction] + a_bias[direction]
    return _to_heads(jax.nn.log_sigmoid(z.astype(jnp.float32)) / GLA_GATE_TAU, GLA_HEADS)


def _chunk(t):
    b, h, n, d = t.shape
    return t.reshape(b, h, n // GLA_CHUNK, GLA_CHUNK, d)


def _gla_states(kc, vc, lam, s0, keep_states):
    lam_last = lam[:, :, :, -1:, :]
    u = jnp.einsum('bhcld,bhcle->bhcde', kc * jnp.exp(lam_last - lam), vc)
    a_tot = jnp.exp(lam_last[:, :, :, 0, :])

    def step(s, xs):
        a_c, u_c = xs
        return a_c[..., None] * s + u_c, (s if keep_states else None)

    s_final, s_before = lax.scan(step, s0, (jnp.moveaxis(a_tot, 2, 0), jnp.moveaxis(u, 2, 0)))
    if keep_states:
        s_before = jnp.moveaxis(s_before, 0, 2)
    return s_before, s_final


def _gla_readout(qc, kc, vc, lam, s_before):
    ln = qc.shape[3]
    causal = jnp.tril(jnp.ones((ln, ln), dtype=bool))[:, :, None]
    diff = lam[:, :, :, :, None, :] - lam[:, :, :, None, :, :]
    decay = jnp.where(causal, jnp.exp(jnp.where(causal, diff, 0.0)), 0.0)
    scores = jnp.einsum('bhcid,bhcjd,bhcijd->bhcij', qc, kc, decay)
    o_intra = jnp.einsum('bhcij,bhcje->bhcie', scores, vc)
    o_inter = jnp.einsum('bhcid,bhcde->bhcie', qc * jnp.exp(lam), s_before)
    return o_intra + o_inter


def _gla_direction(q, k, v, log_a, s0, reverse):
    if reverse:
        q, k, v, log_a = (jnp.flip(t, axis=2) for t in (q, k, v, log_a))
    b, h, n, _ = v.shape
    qc, kc, vc = _chunk(q), _chunk(k), _chunk(v)
    lam = jnp.cumsum(_chunk(log_a), axis=3)
    s_before, s_final = _gla_states(kc, vc, lam, s0, True)
    o = _gla_readout(qc, kc, vc, lam, s_before).reshape(b, h, n, -1)
    return (jnp.flip(o, axis=2) if reverse else o), s_final


def _gla_final_state(k, v, log_a, s0, reverse):
    if reverse:
        k, v, log_a = (jnp.flip(t, axis=2) for t in (k, v, log_a))
    lam = jnp.cumsum(_chunk(log_a), axis=3)
    return _gla_states(_chunk(k), _chunk(v), lam, s0, False)[1]


def _gla_merge(o, g_norm, gate):
    o = _rms_norm(o, g_norm)
    return _merge_heads(o).astype(gate.dtype) * jax.nn.silu(gate)


def _expert_choice_ffn(h, router, w_gate, w_up, w_down):
    b, n, d = h.shape
    cap = EC_CAPACITY_FACTOR * n // N_EXPERTS
    affinity = jax.nn.softmax((h @ router).astype(jnp.float32), axis=-1)
    gates, idx = lax.top_k(jnp.swapaxes(affinity, 1, 2), cap)
    xs = jax.vmap(lambda hb, ib: hb[ib])(h, idx)
    hid = jax.nn.silu(jnp.einsum('becd,edf->becf', xs, w_gate)) * jnp.einsum('becd,edf->becf', xs, w_up)
    y = jnp.einsum('becf,efd->becd', hid, w_down) * gates[..., None].astype(h.dtype)
    return jax.vmap(lambda yb, ib: jnp.zeros((n, d), h.dtype).at[ib.reshape(-1)].add(yb.reshape(-1, d)))(y, idx)


def _normal(key, shape, scale):
    return jax.random.normal(key, shape, jnp.float32) * scale


def setup_inputs(seed: int = 0) -> dict:
    key = jax.random.key(seed)
    ks = jax.random.split(key, 20)
    D, L = D_MODEL, DEPTH
    return {
        'x': _normal(ks[0], (BATCH, SEQ, D), 1.0),
        'c': _normal(ks[1], (BATCH, D), 1.0),
        'ctx': _normal(ks[2], (BATCH, CTX_LEN, D), 1.0),
        'c_ctx': _normal(ks[3], (D,), 1.0),
        'w_mod': _normal(ks[4], (L, D, 6 * D), 0.5 * D ** -0.5),
        'b_mod': _normal(ks[5], (L, 6 * D), 0.01),
        'norm_mix_pre': 1.0 + _normal(ks[6], (L, D), 0.02),
        'norm_mix_post': 1.0 + _normal(ks[7], (L, D), 0.02),
        'norm_ffn_pre': 1.0 + _normal(ks[8], (L, D), 0.02),
        'norm_ffn_post': 1.0 + _normal(ks[9], (L, D), 0.02),
        'w_in': _normal(ks[10], (L, D, IN_COLS), D ** -0.5),
        'na_rpb': _normal(ks[11], (L, NA_HEADS, 2 * NA_WIN_ROWS - 1, 2 * NA_WIN_COLS - 1), 0.1),
        'gla_a_up': _normal(ks[12], (L, 2, GLA_GATE_RANK, GLA_KEY_W), GLA_GATE_RANK ** -0.5),
        'gla_a_bias': _normal(ks[13], (L, 2, GLA_KEY_W), 0.1),
        'gla_norm': 1.0 + _normal(ks[14], (L, GLA_DV), 0.02),
        'w_out': _normal(ks[15], (L, MIX_W, D), MIX_W ** -0.5),
        'router': _normal(ks[16], (L, D, N_EXPERTS), D ** -0.5),
        'w_gate': _normal(ks[17], (L, N_EXPERTS, D, D_EXPERT), D ** -0.5),
        'w_up': _normal(ks[18], (L, N_EXPERTS, D, D_EXPERT), D ** -0.5),
        'w_down': _normal(ks[19], (L, N_EXPERTS, D_EXPERT, D), D_EXPERT ** -0.5),
    }


def reference(x, c, ctx, c_ctx, w_mod, b_mod, norm_mix_pre, norm_mix_post, norm_ffn_pre, norm_ffn_post,
              w_in, na_rpb, gla_a_up, gla_a_bias, gla_norm, w_out, router, w_gate, w_up, w_down):
    b, n, d = x.shape
    t = jnp.arange(n)
    pos_row, pos_col = t // GRID_W, t % GRID_W
    s_zero = jnp.zeros((b, GLA_HEADS, GLA_DK, GLA_DV), jnp.float32)
    for layer in range(DEPTH):
        last = layer == DEPTH - 1
        mod = jax.nn.silu(c) @ w_mod[layer] + b_mod[layer]
        sh_m, sc_m, g_m, sh_f, sc_f, g_f = jnp.split(mod[:, None, :], 6, axis=-1)
        n_ctx_mods = 2 if last else 6
        mod_ctx = jax.nn.silu(c_ctx) @ w_mod[layer][:, :n_ctx_mods * d] + b_mod[layer][:n_ctx_mods * d]
        ctx_mods = jnp.split(mod_ctx, n_ctx_mods, axis=-1)

        h = _modulate(x, norm_mix_pre[layer], sh_m, sc_m)
        h_ctx = _modulate(ctx, norm_mix_pre[layer], ctx_mods[0], ctx_mods[1])
        na_k, na_v, gla_k, gla_v, a_down, na_q, gla_q, gla_g = _split(h @ w_in[layer], IN_SIZES)
        parts_ctx = _split(h_ctx @ w_in[layer][:, :(KV_COLS if last else IN_COLS)], KV_SIZES if last else IN_SIZES)
        c_na_k, c_na_v, c_gla_k, c_gla_v, c_a_down = parts_ctx[:5]
        c_na_k, c_na_v = _to_heads(c_na_k, NA_HEADS), _to_heads(c_na_v, NA_HEADS)

        o_na = _neighbourhood_attention(_to_heads(na_q, NA_HEADS), _to_heads(na_k, NA_HEADS),
                                        _to_heads(na_v, NA_HEADS), c_na_k, c_na_v, na_rpb[layer])

        c_k = _to_heads(c_gla_k, GLA_HEADS).astype(jnp.float32)
        c_v = _to_heads(c_gla_v, GLA_HEADS).astype(jnp.float32)
        c_la_f = _gla_log_decay(c_a_down, gla_a_up[layer], gla_a_bias[layer], 0)
        c_la_b = _gla_log_decay(c_a_down, gla_a_up[layer], gla_a_bias[layer], 1)
        if last:
            s_ctx_f = _gla_final_state(c_k, c_v, c_la_f, s_zero, False)
            s_ctx_b = _gla_final_state(c_k, c_v, c_la_b, s_zero, True)
        else:
            c_na_q, c_gla_q, c_gla_g = parts_ctx[5:]
            c_q = _to_heads(c_gla_q, GLA_HEADS).astype(jnp.float32) * GLA_DK ** -0.5
            oc_f, s_ctx_f = _gla_direction(c_q, c_k, c_v, c_la_f, s_zero, False)
            oc_b, s_ctx_b = _gla_direction(c_q, c_k, c_v, c_la_b, s_zero, True)
        q_l = _axial_rope(_to_heads(gla_q, GLA_HEADS).astype(jnp.float32), pos_row, pos_col) * GLA_DK ** -0.5
        k_l = _axial_rope(_to_heads(gla_k, GLA_HEADS).astype(jnp.float32), pos_row, pos_col)
        v_l = _to_heads(gla_v, GLA_HEADS).astype(jnp.float32)
        la_f = _gla_log_decay(a_down, gla_a_up[layer], gla_a_bias[layer], 0)
        la_b = _gla_log_decay(a_down, gla_a_up[layer], gla_a_bias[layer], 1)
        o_f, _ = _gla_direction(q_l, k_l, v_l, la_f, s_ctx_f, False)
        o_b, _ = _gla_direction(q_l, k_l, v_l, la_b, s_ctx_b, True)
        o_gla = _gla_merge(o_f + o_b, gla_norm[layer], gla_g)

        mix = jnp.concatenate([_merge_heads(o_na), o_gla], axis=-1) @ w_out[layer]
        x_new = x + g_m * _rms_norm(mix, norm_mix_post[layer])

        hf = _modulate(x_new, norm_ffn_pre[layer], sh_f, sc_f)
        y = _expert_choice_ffn(hf, router[layer], w_gate[layer], w_up[layer], w_down[layer])
        x_new = x_new + g_f * _rms_norm(y, norm_ffn_post[layer])

        if not last:
            o_na_c = _context_attention(_to_heads(c_na_q, NA_HEADS), c_na_k, c_na_v)
            o_gla_c = _gla_merge(oc_f + oc_b, gla_norm[layer], c_gla_g)
            mix_c = jnp.concatenate([_merge_heads(o_na_c), o_gla_c], axis=-1) @ w_out[layer]
            ctx = ctx + ctx_mods[2] * _rms_norm(mix_c, norm_mix_post[layer])
            hc = _modulate(ctx, norm_ffn_pre[layer], ctx_mods[3], ctx_mods[4])
            yc = _expert_choice_ffn(hc, router[layer], w_gate[layer], w_up[layer], w_down[layer])
            ctx = ctx + ctx_mods[5] * _rms_norm(yc, norm_ffn_post[layer])
        x = x_new
    return x
```

```python
import functools

import numpy as np
import jax
import jax.numpy as jnp
from jax import lax
from jax.experimental import pallas as pl
from jax.experimental.pallas import tpu as pltpu

F32 = jnp.float32
BF16 = jnp.bfloat16
HIGHEST = lax.Precision.HIGHEST

D_MODEL = 1024
GRID_W = 64
NA_W = 512
NA_HEADS = 8
NA_HEAD_DIM = 64
NA_WIN_ROWS = 8
NA_WIN_COLS = 16
GLA_HEADS = 4
GLA_DV = 128
GLA_DK = 64
GLA_KEY_W = 256
GLA_VAL_W = 512
GLA_GATE_RANK = 16
GLA_GATE_TAU = 16.0
ROPE_BASE = 10000.0
N_EXPERTS = 16
EC_CAPACITY_FACTOR = 2
NORM_EPS = 1e-6
NEG_BIG = -1e30

LANES = 128
VMEM_LIMIT = 56 * 1024 * 1024

TOK_TILE = 512
NA_QROWS = 4
NA_KROWS = NA_QROWS + NA_WIN_ROWS
GLA_T = 256
GLA_LEVELS = (128, 64, 32, 16)
GLA_DIAG = 16
FFN_TILE = 256
SLOT_TILE = 128


def _cparams(sem):
    return pltpu.CompilerParams(dimension_semantics=sem, vmem_limit_bytes=VMEM_LIMIT)


def _rms(v, g):
    return v * lax.rsqrt(jnp.mean(v * v, axis=-1, keepdims=True) + NORM_EPS) * g


def _dot(a, b):
    return jnp.dot(a, b, preferred_element_type=F32)


def _dot_nt(a, b):
    return lax.dot_general(a, b, (((1,), (1,)), ((), ())), preferred_element_type=F32)


def _mod_kernel(c_ref, w_ref, b_ref, o_ref):
    c = c_ref[...]
    s = c / (1.0 + jnp.exp(-c))
    o_ref[...] = jnp.dot(s, w_ref[...], preferred_element_type=F32, precision=HIGHEST) + b_ref[...]


def _mod(c8, w_mod, b_mod):
    d, n = w_mod.shape
    tn = 1536
    return pl.pallas_call(
        _mod_kernel,
        out_shape=jax.ShapeDtypeStruct((8, n), F32),
        grid=(n // tn,),
        in_specs=[pl.BlockSpec((8, d), lambda j: (0, 0)),
                  pl.BlockSpec((d, tn), lambda j: (0, j)),
                  pl.BlockSpec((1, tn), lambda j: (0, j))],
        out_specs=pl.BlockSpec((8, tn), lambda j: (0, j)),
        compiler_params=_cparams(("arbitrary",)),
        name="mod",
    )(c8, w_mod, b_mod.reshape(1, n))


def _inproj_kernel(x_ref, sh_ref, sc_ref, g_ref, *refs, transposed):
    n_w = len(transposed)
    w_refs, o_refs = refs[:n_w], refs[n_w:]
    x = x_ref[0]
    h = _rms(x, g_ref[...]) * (1.0 + sc_ref[0]) + sh_ref[0]
    hb = h.astype(BF16)
    for w_ref, o_ref, tr in zip(w_refs, o_refs, transposed):
        if tr:
            o_ref[0] = _dot_nt(w_ref[...], hb).astype(o_ref.dtype)
        else:
            o_ref[0] = _dot(hb, w_ref[...]).astype(o_ref.dtype)


def _inproj(x, shift, scale, g, weights, transposed, out_dtypes):
    b, n, d = x.shape
    tm = min(TOK_TILE, n)
    per_sample = shift.shape[0] == b
    mod_map = (lambda i, j: (i, 0, 0)) if per_sample else (lambda i, j: (0, 0, 0))
    in_specs = [pl.BlockSpec((1, tm, d), lambda i, j: (i, j, 0)),
                pl.BlockSpec((1, 1, d), mod_map),
                pl.BlockSpec((1, 1, d), mod_map),
                pl.BlockSpec((1, d), lambda i, j: (0, 0))]
    out_shapes, out_specs = [], []
    for w, tr, dt in zip(weights, transposed, out_dtypes):
        in_specs.append(pl.BlockSpec(w.shape, lambda i, j: (0, 0)))
        if tr:
            cols = w.shape[0]
            out_shapes.append(jax.ShapeDtypeStruct((b, cols, n), dt))
            out_specs.append(pl.BlockSpec((1, cols, tm), lambda i, j: (i, 0, j)))
        else:
            cols = w.shape[1]
            out_shapes.append(jax.ShapeDtypeStruct((b, n, cols), dt))
            out_specs.append(pl.BlockSpec((1, tm, cols), lambda i, j: (i, j, 0)))
    return pl.pallas_call(
        functools.partial(_inproj_kernel, transposed=tuple(transposed)),
        out_shape=out_shapes,
        grid=(b, n // tm),
        in_specs=in_specs,
        out_specs=out_specs,
        compiler_params=_cparams(("arbitrary", "arbitrary")),
        name="inproj",
    )(x, shift, scale, g.reshape(1, d), *weights)


def _na_bias_index_tables(rows):
    nq, nk = NA_QROWS * GRID_W, NA_KROWS * GRID_W
    kr = min(NA_WIN_ROWS, rows)
    n_blocks = rows // NA_QROWS
    dr = np.zeros((3, nq, nk), np.int32)
    dc = np.zeros((3, nq, nk), np.int32)
    ok = np.zeros((3, nq, nk), bool)
    qi, kj = np.arange(nq), np.arange(nk)
    for pat, blk in enumerate((0, 1, n_blocks - 1)):
        r0 = blk * NA_QROWS
        k0 = int(np.clip(r0 - kr // 2, 0, rows - NA_KROWS))
        rq = (r0 + qi // GRID_W)[:, None]
        cq = (qi % GRID_W)[:, None]
        rk = (k0 + kj // GRID_W)[None, :]
        ck = (kj % GRID_W)[None, :]
        r_start = np.clip(rq - kr // 2, 0, rows - kr)
        c_start = np.clip(cq - NA_WIN_COLS // 2, 0, GRID_W - NA_WIN_COLS)
        ok[pat] = (rk >= r_start) & (rk < r_start + kr) & (ck >= c_start) & (ck < c_start + NA_WIN_COLS)
        dr[pat] = np.clip(rk - rq + NA_WIN_ROWS - 1, 0, 2 * NA_WIN_ROWS - 2)
        dc[pat] = np.clip(ck - cq + NA_WIN_COLS - 1, 0, 2 * NA_WIN_COLS - 2)
    return dr, dc, ok


def _na_kernel(q_ref, k_ref, v_ref, kc_ref, vc_ref, bias_ref, o_ref, *, rows):
    nq, nk = NA_QROWS * GRID_W, NA_KROWS * GRID_W
    n_blocks = rows // NA_QROWS
    kr = min(NA_WIN_ROWS, rows)
    scale = NA_HEAD_DIM ** -0.5
    lane = lax.broadcasted_iota(jnp.int32, (nq, LANES), 1)
    first_head = lane < NA_HEAD_DIM
    kc = kc_ref[0]
    vc = vc_ref[0]

    def block(i, carry):
        r0 = i * NA_QROWS
        k0 = jnp.clip(r0 - kr // 2, 0, rows - NA_KROWS)
        pat = jnp.where(i == 0, 0, jnp.where(i == n_blocks - 1, 2, 1))
        q = q_ref[0, pl.ds(pl.multiple_of(r0 * GRID_W, GRID_W), nq), :] * scale
        kw = k_ref[0, pl.ds(pl.multiple_of(k0 * GRID_W, GRID_W), nk), :]
        vw = v_ref[0, pl.ds(pl.multiple_of(k0 * GRID_W, GRID_W), nk), :]
        outs = []
        for h in range(2):
            qh = jnp.where(first_head if h == 0 else jnp.logical_not(first_head), q, jnp.zeros_like(q))
            s_w = _dot_nt(qh, kw) + bias_ref[0, pat, h]
            s_c = _dot_nt(qh, kc)
            m = jnp.maximum(jnp.max(s_w, axis=-1, keepdims=True), jnp.max(s_c, axis=-1, keepdims=True))
            p_w = jnp.exp(s_w - m)
            p_c = jnp.exp(s_c - m)
            l = jnp.sum(p_w, axis=-1, keepdims=True) + jnp.sum(p_c, axis=-1, keepdims=True)
            o = _dot(p_w.astype(BF16), vw) + _dot(p_c.astype(BF16), vc)
            outs.append(o / l)
        o = jnp.where(first_head, outs[0], outs[1])
        o_ref[0, pl.ds(pl.multiple_of(r0 * GRID_W, GRID_W), nq), :] = o.astype(o_ref.dtype)
        return carry

    lax.fori_loop(0, n_blocks, block, 0)


def _na(q, k, v, kc, vc, bias_tab):
    b, n, w = q.shape
    n_ctx = kc.shape[1]
    pairs = w // LANES
    rows = n // GRID_W
    nq, nk = NA_QROWS * GRID_W, NA_KROWS * GRID_W
    tok = lambda i, p: (i, 0, p)
    return pl.pallas_call(
        functools.partial(_na_kernel, rows=rows),
        out_shape=jax.ShapeDtypeStruct((b, n, w), BF16),
        grid=(b, pairs),
        in_specs=[pl.BlockSpec((1, n, LANES), tok),
                  pl.BlockSpec((1, n, LANES), tok),
                  pl.BlockSpec((1, n, LANES), tok),
                  pl.BlockSpec((1, n_ctx, LANES), tok),
                  pl.BlockSpec((1, n_ctx, LANES), tok),
                  pl.BlockSpec((1, 3, 2, nq, nk), lambda i, p: (p, 0, 0, 0, 0))],
        out_specs=pl.BlockSpec((1, n, LANES), tok),
        compiler_params=_cparams(("arbitrary", "arbitrary")),
        name="na",
    )(q, k, v, kc, vc, bias_tab)


def _gla_range_matrices(t):
    n_mats = 2 + len(GLA_LEVELS) + 1
    c = np.zeros((2, n_mats, t, t), np.float32)
    for i in range(t):
        c[0, 0, i, :i + 1] = 1
        c[0, 1, i, i + 1:] = 1
        c[1, 0, i, i:] = 1
        c[1, 1, i, :i] = 1
        for l, half in enumerate(GLA_LEVELS + (GLA_DIAG // 2,)):
            mid = (i // (2 * half)) * 2 * half + half
            if i >= mid:
                c[0, 2 + l, i, mid:i + 1] = 1
                c[1, 2 + l, i, mid:i] = 1
            else:
                c[0, 2 + l, i, i + 1:mid] = 1
                c[1, 2 + l, i, i:mid] = 1
    return c


def _gla_kernel(q_ref, k_ref, v_ref, vt_ref, ad_ref, g_ref, ck_ref, cvt_ref, cad_ref,
                aup_ref, ab_ref, gn_ref, cos_ref, sin_ref, perm_ref, cm_ref, o_ref, acc_ref, *, n_tok):
    t = GLA_T
    n_chunks = n_tok // t
    n_lv = len(GLA_LEVELS)
    row = lax.broadcasted_iota(jnp.int32, (t, LANES), 0)
    lane = lax.broadcasted_iota(jnp.int32, (t, LANES), 1)
    head0_l = lane < GLA_DK
    row2 = lax.broadcasted_iota(jnp.int32, (t, 2 * t), 0)
    col2 = lax.broadcasted_iota(jnp.int32, (t, 2 * t), 1) & (t - 1)
    vrow = lax.broadcasted_iota(jnp.int32, (2 * t, 2 * GLA_DV), 0)
    vlane = lax.broadcasted_iota(jnp.int32, (2 * t, 2 * GLA_DV), 1)
    v_head_match = (vrow >= t) == (vlane >= GLA_DV)
    srow = lax.broadcasted_iota(jnp.int32, (2 * GLA_DV, LANES), 0)
    slane = lax.broadcasted_iota(jnp.int32, (2 * GLA_DV, LANES), 1)
    s_blockdiag = (srow >= GLA_DV) == (slane >= GLA_DK)

    def log_decay(ad, dirn):
        z = jnp.dot(ad, aup_ref[dirn], preferred_element_type=F32, precision=HIGHEST) + ab_ref[dirn]
        ls = jnp.minimum(z, 0.0) - jnp.log(1.0 + jnp.exp(-jnp.abs(z)))
        return ls * (1.0 / GLA_GATE_TAU)

    def range_sums(loga, dirn, idx):
        hi = loga.astype(BF16)
        r1 = loga - hi.astype(F32)
        mid = r1.astype(BF16)
        lo = (r1 - mid.astype(F32)).astype(BF16)
        pieces = jnp.concatenate([hi, mid, lo], axis=-1)
        e3 = _dot(cm_ref[dirn, idx], pieces)
        return e3[:, :LANES] + e3[:, LANES:2 * LANES] + e3[:, 2 * LANES:]

    def total_decay(e_q, dirn):
        return jnp.exp(e_q[t - 1:t, :] if dirn == 0 else e_q[0:1, :])

    def state_update(s, k, vt, loga, e_q, dirn):
        kh = (k * jnp.exp(range_sums(loga, dirn, 1))).astype(BF16)
        upd = _dot(vt, kh)
        return s * total_decay(e_q, dirn) + jnp.where(s_blockdiag, upd, 0.0)

    def rope(xb, tok0):
        x = xb.astype(F32)
        return x * cos_ref[pl.ds(tok0, t), :] + _dot(xb, perm_ref[...]) * sin_ref[pl.ds(tok0, t), :]

    def chunk(tok0, s, dirn):
        q = rope(q_ref[0, pl.ds(tok0, t), :], tok0) * (GLA_DK ** -0.5)
        k = rope(k_ref[0, pl.ds(tok0, t), :], tok0)
        v = v_ref[0, pl.ds(tok0, t), :]
        vt = vt_ref[0, :, pl.ds(tok0, t)]
        loga = log_decay(ad_ref[0, pl.ds(tok0, t), :], dirn)
        e_q = range_sums(loga, dirn, 0)
        qh = (q * jnp.exp(e_q)).astype(BF16)
        o = _dot_nt(qh, s.astype(BF16))
        a = jnp.zeros((t, 2 * t), F32)
        for l, half in enumerate(GLA_LEVELS):
            w = jnp.exp(range_sums(loga, dirn, 2 + l))
            later = (row & half) != 0
            q_side = later if dirn == 0 else jnp.logical_not(later)
            qt = jnp.where(q_side, q * w, 0.0).astype(BF16)
            kt = jnp.where(q_side, 0.0, k * w)
            kcat = jnp.concatenate([jnp.where(head0_l, kt, 0.0), jnp.where(head0_l, 0.0, kt)], axis=0).astype(BF16)
            same_blk = (row2 & ~(2 * half - 1)) == (col2 & ~(2 * half - 1))
            a = a + jnp.where(same_blk, _dot_nt(qt, kcat), 0.0)
        e_d = range_sums(loga, dirn, 2 + n_lv)
        w, wi = jnp.exp(e_d), jnp.exp(-e_d)
        later = (row & (GLA_DIAG // 2)) != 0
        shrink_q = later if dirn == 0 else jnp.logical_not(later)
        qt = (q * jnp.where(shrink_q, w, wi)).astype(BF16)
        kt = k * jnp.where(shrink_q, wi, w)
        kcat = jnp.concatenate([jnp.where(head0_l, kt, 0.0), jnp.where(head0_l, 0.0, kt)], axis=0).astype(BF16)
        same_blk = (row2 & ~(GLA_DIAG - 1)) == (col2 & ~(GLA_DIAG - 1))
        causal = (row2 >= col2) if dirn == 0 else (row2 <= col2)
        a = a + jnp.where(same_blk & causal, _dot_nt(qt, kcat), 0.0)
        vcat = jnp.concatenate([v, v], axis=0)
        vcat = jnp.where(v_head_match, vcat, jnp.zeros_like(vcat))
        o = o + _dot(a.astype(BF16), vcat)
        return o, state_update(s, k, vt, loga, e_q, dirn)

    def ctx_state(dirn):
        loga = log_decay(cad_ref[0], dirn)
        e_q = range_sums(loga, dirn, 0)
        s0 = jnp.zeros((2 * GLA_DV, LANES), F32)
        return state_update(s0, ck_ref[0].astype(F32), cvt_ref[0], loga, e_q, dirn)

    def fwd_body(c, s):
        tok0 = pl.multiple_of(c * t, t)
        o, s = chunk(tok0, s, 0)
        acc_ref[pl.ds(tok0, t), :] = o
        return s

    lax.fori_loop(0, n_chunks, fwd_body, ctx_state(0))

    def bwd_body(i, s):
        tok0 = pl.multiple_of((n_chunks - 1 - i) * t, t)
        o, s = chunk(tok0, s, 1)
        o = o + acc_ref[pl.ds(tok0, t), :]
        g = g_ref[0, pl.ds(tok0, t), :].astype(F32)
        gate = g / (1.0 + jnp.exp(-g))
        halves = [_rms(o[:, h * GLA_DV:(h + 1) * GLA_DV], gn_ref[...]) for h in range(2)]
        o_ref[0, pl.ds(tok0, t), :] = (jnp.concatenate(halves, axis=-1) * gate).astype(o_ref.dtype)
        return s

    lax.fori_loop(0, n_chunks, bwd_body, ctx_state(1))


def _gla(q, k, v, vt, ad, g, ck, cvt, cad, aup, abias, gnorm, cos, sin, perm, cmats):
    b, n, kw = q.shape
    n_ctx = ck.shape[1]
    pairs = kw // LANES
    vw = 2 * GLA_DV
    tok = lambda i, p: (i, 0, p)
    full3 = lambda i, p: (i, 0, 0)
    return pl.pallas_call(
        functools.partial(_gla_kernel, n_tok=n),
        out_shape=jax.ShapeDtypeStruct((b, n, v.shape[2]), BF16),
        grid=(b, pairs),
        in_specs=[pl.BlockSpec((1, n, LANES), tok),
                  pl.BlockSpec((1, n, LANES), tok),
                  pl.BlockSpec((1, n, vw), tok),
                  pl.BlockSpec((1, vw, n), lambda i, p: (i, p, 0)),
                  pl.BlockSpec((1, n, ad.shape[2]), full3),
                  pl.BlockSpec((1, n, vw), tok),
                  pl.BlockSpec((1, n_ctx, LANES), tok),
                  pl.BlockSpec((1, vw, n_ctx), lambda i, p: (i, p, 0)),
                  pl.BlockSpec((1, n_ctx, cad.shape[2]), full3),
                  pl.BlockSpec((2, aup.shape[1], LANES), lambda i, p: (0, 0, p)),
                  pl.BlockSpec((2, 1, LANES), lambda i, p: (0, 0, p)),
                  pl.BlockSpec((1, GLA_DV), lambda i, p: (0, 0)),
                  pl.BlockSpec((n, LANES), lambda i, p: (0, 0)),
                  pl.BlockSpec((n, LANES), lambda i, p: (0, 0)),
                  pl.BlockSpec((LANES, LANES), lambda i, p: (0, 0)),
                  pl.BlockSpec(cmats.shape, lambda i, p: (0, 0, 0, 0))],
        out_specs=pl.BlockSpec((1, n, vw), tok),
        scratch_shapes=[pltpu.VMEM((n, vw), F32)],
        compiler_params=_cparams(("arbitrary", "arbitrary")),
        name="gla",
    )(q, k, v, vt, ad, g, ck, cvt, cad, aup, abias, gnorm, cos, sin, perm, cmats)


def _rope_tables(n):
    t = jnp.arange(n)
    pos_row, pos_col = (t // GRID_W).astype(F32), (t % GRID_W).astype(F32)
    quarter = GLA_DK // 4
    freqs = ROPE_BASE ** (-jnp.arange(quarter, dtype=F32) / quarter)
    ang_r = pos_row[:, None] * freqs
    ang_c = pos_col[:, None] * freqs
    cos = jnp.concatenate([jnp.cos(ang_r), jnp.cos(ang_r), jnp.cos(ang_c), jnp.cos(ang_c)], axis=-1)
    sin = jnp.concatenate([-jnp.sin(ang_r), jnp.sin(ang_r), -jnp.sin(ang_c), jnp.sin(ang_c)], axis=-1)
    cos, sin = jnp.tile(cos, (1, LANES // GLA_DK)), jnp.tile(sin, (1, LANES // GLA_DK))
    d = np.arange(LANES)
    partner = np.where((d % (2 * quarter)) < quarter, d + quarter, d - quarter)
    perm = np.zeros((LANES, LANES), np.float32)
    perm[partner, d] = 1.0
    return cos, sin, jnp.asarray(perm, BF16)


def _outproj_kernel(ona_ref, ogla_ref, x_ref, w1_ref, w2_ref, gm_ref, shf_ref, scf_ref, npost_ref,
                    nfpre_ref, rt_ref, xnew_ref, hf_ref, aff_ref, afft_ref):
    mix = _dot(ona_ref[0], w1_ref[...]) + _dot(ogla_ref[0], w2_ref[...])
    xn = x_ref[0] + gm_ref[0] * _rms(mix, npost_ref[...])
    xnew_ref[0] = xn
    h = _rms(xn, nfpre_ref[...]) * (1.0 + scf_ref[0]) + shf_ref[0]
    hf_ref[0] = h.astype(hf_ref.dtype)
    logits = jnp.dot(h, rt_ref[...], preferred_element_type=F32, precision=HIGHEST)
    lane = lax.broadcasted_iota(jnp.int32, logits.shape, 1)
    logits = jnp.where(lane < N_EXPERTS, logits, NEG_BIG)
    p = jnp.exp(logits - jnp.max(logits, axis=-1, keepdims=True))
    aff = p / jnp.sum(p, axis=-1, keepdims=True)
    aff_ref[0] = aff
    afft_ref[0] = aff.T[:N_EXPERTS, :]


def _outproj(o_na, o_gla, x, w1, w2, gm, shf, scf, npost, nfpre, router_pad):
    b, n, d = x.shape
    tm = TOK_TILE
    tokmap = lambda i, j: (i, j, 0)
    smp = lambda i, j: (i, 0, 0)
    cst = lambda i, j: (0, 0)
    return pl.pallas_call(
        _outproj_kernel,
        out_shape=[jax.ShapeDtypeStruct((b, n, d), F32),
                   jax.ShapeDtypeStruct((b, n, d), BF16),
                   jax.ShapeDtypeStruct((b, n, LANES), F32),
                   jax.ShapeDtypeStruct((b, N_EXPERTS, n), F32)],
        grid=(b, n // tm),
        in_specs=[pl.BlockSpec((1, tm, o_na.shape[2]), tokmap),
                  pl.BlockSpec((1, tm, o_gla.shape[2]), tokmap),
                  pl.BlockSpec((1, tm, d), tokmap),
                  pl.BlockSpec(w1.shape, cst),
                  pl.BlockSpec(w2.shape, cst),
                  pl.BlockSpec((1, 1, d), smp),
                  pl.BlockSpec((1, 1, d), smp),
                  pl.BlockSpec((1, 1, d), smp),
                  pl.BlockSpec((1, d), cst),
                  pl.BlockSpec((1, d), cst),
                  pl.BlockSpec((d, LANES), cst)],
        out_specs=[pl.BlockSpec((1, tm, d), tokmap),
                   pl.BlockSpec((1, tm, d), tokmap),
                   pl.BlockSpec((1, tm, LANES), tokmap),
                   pl.BlockSpec((1, N_EXPERTS, tm), lambda i, j: (i, 0, j))],
        compiler_params=_cparams(("arbitrary", "arbitrary")),
        name="outproj",
    )(o_na, o_gla, x, w1, w2, gm, shf, scf, npost.reshape(1, d), nfpre.reshape(1, d), router_pad)


def _route_kernel(afft_ref, rt_ref, rc_ref, *, cap):
    a = afft_ref[0]
    e, n = a.shape
    capf = jnp.float32(cap)

    def search(i, thr_bits):
        cand = thr_bits | lax.shift_left(jnp.int32(1), 30 - i)
        cnt = jnp.sum(jnp.where(a >= lax.bitcast_convert_type(cand, F32), 1.0, 0.0), axis=-1, keepdims=True)
        return jnp.where(cnt >= capf, cand, thr_bits)

    thr_bits = lax.fori_loop(0, 31, search, jnp.zeros((e, 1), jnp.int32))
    thr = lax.bitcast_convert_type(thr_bits, F32)
    need = capf - jnp.sum(jnp.where(a > thr, 1.0, 0.0), axis=-1, keepdims=True)
    r_i = lax.broadcasted_iota(jnp.int32, (LANES, LANES), 0)
    c_i = lax.broadcasted_iota(jnp.int32, (LANES, LANES), 1)
    incl = jnp.where(r_i <= c_i, 1.0, 0.0).astype(BF16)
    off_eq = jnp.zeros((e, 1), F32)
    off_sel = jnp.zeros((e, 1), F32)
    pad = jnp.full((LANES - e, LANES), -1.0, F32)
    for j in range(n // LANES):
        sl = slice(j * LANES, (j + 1) * LANES)
        a_b = a[:, sl]
        eq_b = jnp.where(a_b == thr, 1.0, 0.0)
        tie_rank = _dot(eq_b.astype(BF16), incl) - eq_b + off_eq
        off_eq = off_eq + jnp.sum(eq_b, axis=-1, keepdims=True)
        sel_b = jnp.where(a_b > thr, 1.0, jnp.where(tie_rank < need, eq_b, 0.0))
        sel = sel_b > 0.5
        rank = _dot(sel_b.astype(BF16), incl) - sel_b + off_sel
        off_sel = off_sel + jnp.sum(sel_b, axis=-1, keepdims=True)
        rsel = jnp.where(sel, rank, -1.0)
        rt_ref[0, :, sl] = rsel.astype(jnp.int32)
        rc_ref[0, sl, :] = jnp.concatenate([rsel, pad], axis=0).T.astype(jnp.int32)


def _route(afft, cap):
    b, e, n = afft.shape
    return pl.pallas_call(
        functools.partial(_route_kernel, cap=cap),
        out_shape=[jax.ShapeDtypeStruct((b, e, n), jnp.int32),
                   jax.ShapeDtypeStruct((b, n, LANES), jnp.int32)],
        grid=(b,),
        in_specs=[pl.BlockSpec((1, e, n), lambda i: (i, 0, 0))],
        out_specs=[pl.BlockSpec((1, e, n), lambda i: (i, 0, 0)),
                   pl.BlockSpec((1, n, LANES), lambda i: (i, 0, 0))],
        compiler_params=_cparams(("arbitrary",)),
        name="route",
    )(afft)


def _gather_kernel(r_ref, hf_ref, o_ref, *, cap):
    r = r_ref[0, 0]
    n = r.shape[1]
    slot = lax.broadcasted_iota(jnp.int32, (SLOT_TILE, n), 0)
    for s0 in range(0, cap, SLOT_TILE):
        onehot = jnp.where(r == slot + s0, 1.0, 0.0).astype(BF16)
        o_ref[0, 0, s0:s0 + SLOT_TILE, :] = _dot(onehot, hf_ref[0]).astype(o_ref.dtype)


def _gather(rsel_t, hf, cap):
    b, e, n = rsel_t.shape
    d = hf.shape[2]
    return pl.pallas_call(
        functools.partial(_gather_kernel, cap=cap),
        out_shape=jax.ShapeDtypeStruct((b, e, cap, d), BF16),
        grid=(b, e),
        in_specs=[pl.BlockSpec((1, 1, 1, n), lambda i, j: (i, j, 0, 0)),
                  pl.BlockSpec((1, n, d), lambda i, j: (i, 0, 0))],
        out_specs=pl.BlockSpec((1, 1, cap, d), lambda i, j: (i, j, 0, 0)),
        compiler_params=_cparams(("arbitrary", "arbitrary")),
        name="gather",
    )(rsel_t.reshape(b, e, 1, n), hf)


def _ffn_kernel(x_ref, wg_ref, wu_ref, wd_ref, o_ref, acc_ref):
    f = pl.program_id(1)
    b = x_ref.shape[0]
    wg = wg_ref[0].astype(BF16)
    wu = wu_ref[0].astype(BF16)
    wd = wd_ref[0].astype(BF16)

    @pl.when(f == 0)
    def _():
        acc_ref[...] = jnp.zeros_like(acc_ref)

    for i in range(b):
        x = x_ref[i, 0]
        g = _dot(x, wg)
        u = _dot(x, wu)
        hid = (g / (1.0 + jnp.exp(-g)) * u).astype(BF16)
        acc_ref[i] += _dot(hid, wd)

    @pl.when(f == pl.num_programs(1) - 1)
    def _():
        for i in range(b):
            o_ref[i, 0] = acc_ref[i].astype(o_ref.dtype)


def _ffn(xs, w_gate, w_up, w_down):
    b, e, cap, d = xs.shape
    dff = w_gate.shape[2]
    tf = FFN_TILE
    return pl.pallas_call(
        _ffn_kernel,
        out_shape=jax.ShapeDtypeStruct((b, e, cap, d), BF16),
        grid=(e, dff // tf),
        in_specs=[pl.BlockSpec((b, 1, cap, d), lambda i, f: (0, i, 0, 0)),
                  pl.BlockSpec((1, d, tf), lambda i, f: (i, 0, f)),
                  pl.BlockSpec((1, d, tf), lambda i, f: (i, 0, f)),
                  pl.BlockSpec((1, tf, d), lambda i, f: (i, f, 0))],
        out_specs=pl.BlockSpec((b, 1, cap, d), lambda i, f: (0, i, 0, 0)),
        scratch_shapes=[pltpu.VMEM((b, cap, d), F32)],
        compiler_params=_cparams(("arbitrary", "arbitrary")),
        name="ffn",
    )(xs, w_gate, w_up, w_down)


def _combine_kernel(ys_ref, rc_ref, aff_ref, xn_ref, gf_ref, npost_ref, o_ref):
    tm = rc_ref.shape[1]
    cap = ys_ref.shape[2]
    slot = lax.broadcasted_iota(jnp.int32, (tm, cap), 1)
    acc = jnp.zeros(o_ref.shape[1:], F32)
    for e in range(ys_ref.shape[1]):
        onehot = jnp.where(rc_ref[0, :, e:e + 1] == slot, 1.0, 0.0).astype(BF16)
        acc = acc + aff_ref[0, :, e:e + 1] * _dot(onehot, ys_ref[0, e])
    o_ref[0] = xn_ref[0] + gf_ref[0] * _rms(acc, npost_ref[...])


def _combine(ys, rsel_c, aff, x_new, gf, npost):
    b, e, cap, d = ys.shape
    n = x_new.shape[1]
    tm = TOK_TILE
    tokmap = lambda i, j: (i, j, 0)
    return pl.pallas_call(
        _combine_kernel,
        out_shape=jax.ShapeDtypeStruct((b, n, d), F32),
        grid=(b, n // tm),
        in_specs=[pl.BlockSpec((1, e, cap, d), lambda i, j: (i, 0, 0, 0)),
                  pl.BlockSpec((1, tm, LANES), tokmap),
                  pl.BlockSpec((1, tm, LANES), tokmap),
                  pl.BlockSpec((1, tm, d), tokmap),
                  pl.BlockSpec((1, 1, d), lambda i, j: (i, 0, 0)),
                  pl.BlockSpec((1, d), lambda i, j: (0, 0))],
        out_specs=pl.BlockSpec((1, tm, d), tokmap),
        compiler_params=_cparams(("arbitrary", "arbitrary")),
        name="combine",
    )(ys, rsel_c, aff, x_new, gf, npost.reshape(1, d))


def kernel(x, c, ctx, c_ctx, w_mod, b_mod, norm_mix_pre, norm_mix_post, norm_ffn_pre, norm_ffn_post,
           w_in, na_rpb, gla_a_up, gla_a_bias, gla_norm, w_out, router, w_gate, w_up, w_down):
    b, n, d = x.shape
    assert w_mod.shape[0] == 1 and d == D_MODEL and n % (GRID_W * NA_QROWS) == 0 and n % GLA_T == 0
    assert ctx.shape[1] == GLA_T
    rows = n // GRID_W
    cap = EC_CAPACITY_FACTOR * n // N_EXPERTS

    c8 = jnp.concatenate([c, c_ctx[None, :], jnp.zeros((8 - b - 1, d), F32)], axis=0)
    mod = _mod(c8, w_mod[0], b_mod[0])
    sh_m, sc_m, g_m, sh_f, sc_f, g_f = [m[:b, None, :] for m in jnp.split(mod, 6, axis=-1)]
    sh_c, sc_c = mod[b:b + 1, None, :d], mod[b:b + 1, None, d:2 * d]

    wb = w_in[0].astype(BF16)
    cuts = np.cumsum([0, NA_W, NA_W, GLA_KEY_W, GLA_VAL_W, 2 * GLA_GATE_RANK, NA_W, GLA_KEY_W, GLA_VAL_W])
    w_nak, w_nav, w_gk, w_gv, w_ad, w_naq, w_gq, w_gg = [wb[:, cuts[i]:cuts[i + 1]] for i in range(8)]
    w_gvt = w_gv.T
    na_q, na_k, na_v, gq, gk, gv, gvt, ad, gg = _inproj(
        x, sh_m, sc_m, norm_mix_pre[0],
        [w_naq, w_nak, w_nav, w_gq, w_gk, w_gv, w_gvt, w_ad, w_gg],
        [False, False, False, False, False, False, True, False, False],
        [BF16, BF16, BF16, BF16, BF16, BF16, BF16, F32, BF16])
    c_nak, c_nav, c_gk, c_gvt, c_ad = _inproj(
        ctx, sh_c, sc_c, norm_mix_pre[0],
        [w_nak, w_nav, w_gk, w_gvt, w_ad],
        [False, False, False, True, False],
        [BF16, BF16, BF16, BF16, F32])

    dr, dc, ok = _na_bias_index_tables(rows)
    bias = jnp.where(ok[None], na_rpb[0][:, dr, dc], NEG_BIG)
    bias = bias.reshape(NA_HEADS // 2, 2, 3, *bias.shape[2:]).transpose(0, 2, 1, 3, 4)
    o_na = _na(na_q, na_k, na_v, c_nak, c_nav, bias)

    aup = jnp.zeros((2, 2 * GLA_GATE_RANK, GLA_KEY_W), F32)
    aup = aup.at[0, :GLA_GATE_RANK].set(gla_a_up[0, 0]).at[1, GLA_GATE_RANK:].set(gla_a_up[0, 1])
    cos, sin, perm = _rope_tables(n)
    cmats = jnp.asarray(_gla_range_matrices(GLA_T), BF16)
    o_gla = _gla(gq, gk, gv, gvt, ad, gg, c_gk, c_gvt, c_ad, aup, gla_a_bias[0][:, None, :],
                 gla_norm[0][None, :], cos, sin, perm, cmats)

    wo = w_out[0].astype(BF16)
    router_pad = jnp.zeros((d, LANES), F32).at[:, :N_EXPERTS].set(router[0])
    x_new, hf, aff, aff_t = _outproj(o_na, o_gla, x, wo[:NA_W], wo[NA_W:], g_m, sh_f, sc_f,
                                     norm_mix_post[0], norm_ffn_pre[0], router_pad)

    rsel_t, rsel_c = _route(aff_t, cap)
    xs = _gather(rsel_t, hf, cap)
    ys = _ffn(xs, w_gate[0], w_up[0], w_down[0])
    return _combine(ys, rsel_c, aff, x_new, g_f, norm_ffn_post[0])
```

```python
import functools

import numpy as np
import jax
import jax.numpy as jnp
from jax import lax
from jax.experimental import pallas as pl
from jax.experimental.pallas import tpu as pltpu

F32 = jnp.float32
BF16 = jnp.bfloat16
HIGHEST = lax.Precision.HIGHEST

D_MODEL = 1024
GRID_W = 64
NA_W = 512
NA_HEADS = 8
NA_HEAD_DIM = 64
NA_WIN_ROWS = 8
NA_WIN_COLS = 16
GLA_HEADS = 4
GLA_DV = 128
GLA_DK = 64
GLA_KEY_W = 256
GLA_VAL_W = 512
GLA_GATE_RANK = 16
GLA_GATE_TAU = 16.0
ROPE_BASE = 10000.0
N_EXPERTS = 16
EC_CAPACITY_FACTOR = 2
NORM_EPS = 1e-6
NEG_BIG = -1e30

LANES = 128
VMEM_LIMIT = 56 * 1024 * 1024

TOK_TILE = 512
NA_QROWS = 4
NA_KROWS = NA_QROWS + NA_WIN_ROWS
GLA_T = 256
GLA_LEVELS = (128, 64, 32, 16)
GLA_DIAG = 16
GLA_MATS = 2 + len(GLA_LEVELS) + 1
FFN_TILE = 256
SLOT_TILE = 128


def _cparams(sem):
    return pltpu.CompilerParams(dimension_semantics=sem, vmem_limit_bytes=VMEM_LIMIT)


def _rms(v, g):
    return v * lax.rsqrt(jnp.mean(v * v, axis=-1, keepdims=True) + NORM_EPS) * g


def _dot(a, b):
    return jnp.dot(a, b, preferred_element_type=F32)


def _dot_nt(a, b):
    return lax.dot_general(a, b, (((1,), (1,)), ((), ())), preferred_element_type=F32)


def _split_bf16(v):
    hi = v.astype(BF16)
    return hi, (v - hi.astype(F32)).astype(BF16)


def _mod_kernel(c_ref, w_ref, b_ref, o_ref):
    c = c_ref[...]
    s = c / (1.0 + jnp.exp(-c))
    o_ref[...] = jnp.dot(s, w_ref[...], preferred_element_type=F32, precision=HIGHEST) + b_ref[...]


def _mod(c8, w_mod, b_mod):
    d, n = w_mod.shape
    tn = 1536
    return pl.pallas_call(
        _mod_kernel,
        out_shape=jax.ShapeDtypeStruct((8, n), F32),
        grid=(n // tn,),
        in_specs=[pl.BlockSpec((8, d), lambda j: (0, 0)),
                  pl.BlockSpec((d, tn), lambda j: (0, j)),
                  pl.BlockSpec((1, tn), lambda j: (0, j))],
        out_specs=pl.BlockSpec((8, tn), lambda j: (0, j)),
        compiler_params=_cparams(("arbitrary",)),
        name="mod",
    )(c8, w_mod, b_mod.reshape(1, n))


def _inproj_kernel(x_ref, sh_ref, sc_ref, g_ref, *refs, modes, scales):
    n_rope = 2 if "rope" in modes else 0
    rope_refs, refs = refs[:n_rope], refs[n_rope:]
    n_w = len(modes)
    w_refs, o_refs = refs[:n_w], refs[n_w:]
    x = x_ref[0]
    h = _rms(x, g_ref[...]) * (1.0 + sc_ref[0]) + sh_ref[0]
    hb = h.astype(BF16)
    for w_ref, o_ref, mode, scale in zip(w_refs, o_refs, modes, scales):
        if mode == "t":
            o_ref[0] = _dot_nt(w_ref[...], hb).astype(o_ref.dtype)
            continue
        y = _dot(hb, w_ref[...])
        if mode == "rope":
            cols = y.shape[1]
            quarter = GLA_DK // 4
            lane = lax.broadcasted_iota(jnp.int32, y.shape, 1)
            partner = jnp.where((lane & (2 * quarter - 1)) < quarter,
                                pltpu.roll(y, cols - quarter, axis=1), pltpu.roll(y, quarter, axis=1))
            y = (y * rope_refs[0][...] + partner * rope_refs[1][...]) * scale
        elif mode == "split":
            hi, lo = _split_bf16(y)
            lane = lax.broadcasted_iota(jnp.int32, y.shape, 1)
            rank2 = 2 * GLA_GATE_RANK
            y = jnp.where((lane >= rank2) & (lane < 2 * rank2), lo, hi)
        o_ref[0] = y.astype(o_ref.dtype)


def _inproj(x, shift, scale, g, weights, modes, scales, out_dtypes, rope=None):
    b, n, d = x.shape
    tm = min(TOK_TILE, n)
    per_sample = shift.shape[0] == b
    mod_map = (lambda i, j: (i, 0, 0)) if per_sample else (lambda i, j: (0, 0, 0))
    in_specs = [pl.BlockSpec((1, tm, d), lambda i, j: (i, j, 0)),
                pl.BlockSpec((1, 1, d), mod_map),
                pl.BlockSpec((1, 1, d), mod_map),
                pl.BlockSpec((1, d), lambda i, j: (0, 0))]
    args = [x, shift, scale, g.reshape(1, d)]
    if rope is not None:
        in_specs += [pl.BlockSpec((tm, rope[0].shape[1]), lambda i, j: (j, 0))] * 2
        args += list(rope)
    out_shapes, out_specs = [], []
    for w, mode, dt in zip(weights, modes, out_dtypes):
        in_specs.append(pl.BlockSpec(w.shape, lambda i, j: (0, 0)))
        if mode == "t":
            cols = w.shape[0]
            out_shapes.append(jax.ShapeDtypeStruct((b, cols, n), dt))
            out_specs.append(pl.BlockSpec((1, cols, tm), lambda i, j: (i, 0, j)))
        else:
            cols = w.shape[1]
            out_shapes.append(jax.ShapeDtypeStruct((b, n, cols), dt))
            out_specs.append(pl.BlockSpec((1, tm, cols), lambda i, j: (i, j, 0)))
    return pl.pallas_call(
        functools.partial(_inproj_kernel, modes=tuple(modes), scales=tuple(scales)),
        out_shape=out_shapes,
        grid=(b, n // tm),
        in_specs=in_specs,
        out_specs=out_specs,
        compiler_params=_cparams(("arbitrary", "arbitrary")),
        name="inproj",
    )(*args, *weights)


def _rope_tables(n):
    t = jnp.arange(n)
    pos_row, pos_col = (t // GRID_W).astype(F32), (t % GRID_W).astype(F32)
    quarter = GLA_DK // 4
    freqs = ROPE_BASE ** (-jnp.arange(quarter, dtype=F32) / quarter)
    ang_r = pos_row[:, None] * freqs
    ang_c = pos_col[:, None] * freqs
    cos = jnp.concatenate([jnp.cos(ang_r), jnp.cos(ang_r), jnp.cos(ang_c), jnp.cos(ang_c)], axis=-1)
    sin = jnp.concatenate([-jnp.sin(ang_r), jnp.sin(ang_r), -jnp.sin(ang_c), jnp.sin(ang_c)], axis=-1)
    return jnp.tile(cos, (1, GLA_HEADS)), jnp.tile(sin, (1, GLA_HEADS))


def _na_bias_table(rpb, rows):
    heads = rpb.shape[0]
    kr = min(NA_WIN_ROWS, rows)
    n_blocks = rows // NA_QROWS
    col = np.arange(GRID_W)
    c_start = np.clip(col - NA_WIN_COLS // 2, 0, GRID_W - NA_WIN_COLS)
    col_ok = (col[None, :] >= c_start[:, None]) & (col[None, :] < c_start[:, None] + NA_WIN_COLS)
    dc = np.clip(col[None, :] - col[:, None] + NA_WIN_COLS - 1, 0, 2 * NA_WIN_COLS - 2)
    sel_c = (np.arange(2 * NA_WIN_COLS - 1)[:, None, None] == dc[None]) & col_ok[None]
    t = jnp.einsum("hrd,dqk->hrqk", rpb, jnp.asarray(sel_c, F32), precision=HIGHEST)
    t = jnp.where(jnp.asarray(col_ok)[None, None], t, NEG_BIG)
    t = jnp.concatenate([t, jnp.full((heads, 1, GRID_W, GRID_W), NEG_BIG, F32)], axis=1)
    n_dr = 2 * NA_WIN_ROWS - 1
    sel_r = np.zeros((3, NA_QROWS, NA_KROWS, n_dr + 1), np.float32)
    for pat, blk in enumerate((0, 1, n_blocks - 1)):
        r0 = blk * NA_QROWS
        k0 = int(np.clip(r0 - kr // 2, 0, rows - NA_KROWS))
        for a in range(NA_QROWS):
            r_start = int(np.clip(r0 + a - kr // 2, 0, rows - kr))
            for c in range(NA_KROWS):
                ok = r_start <= k0 + c < r_start + kr
                sel_r[pat, a, c, (k0 + c) - (r0 + a) + NA_WIN_ROWS - 1 if ok else n_dr] = 1.0
    bias = jnp.einsum("pacr,hrqk->hpaqck", jnp.asarray(sel_r), t, precision=HIGHEST)
    return bias.reshape(heads, 3, NA_QROWS * GRID_W, NA_KROWS * GRID_W)


def _na_kernel(q_ref, k_ref, v_ref, kc_ref, vc_ref, bias_ref, o_ref, *, rows):
    nq, nk = NA_QROWS * GRID_W, NA_KROWS * GRID_W
    n_blocks = rows // NA_QROWS
    kr = min(NA_WIN_ROWS, rows)
    scale = NA_HEAD_DIM ** -0.5
    lane = lax.broadcasted_iota(jnp.int32, (nq, LANES), 1)
    first_head = lane < NA_HEAD_DIM
    kc = kc_ref[0]
    vc = vc_ref[0]

    def block(i, carry):
        r0 = i * NA_QROWS
        k0 = jnp.clip(r0 - kr // 2, 0, rows - NA_KROWS)
        pat = jnp.where(i == 0, 0, jnp.where(i == n_blocks - 1, 2, 1))
        q = q_ref[0, pl.ds(pl.multiple_of(r0 * GRID_W, GRID_W), nq), :] * scale
        kw = k_ref[0, pl.ds(pl.multiple_of(k0 * GRID_W, GRID_W), nk), :]
        vw = v_ref[0, pl.ds(pl.multiple_of(k0 * GRID_W, GRID_W), nk), :]
        outs = []
        for h in range(2):
            qh = jnp.where(first_head if h == 0 else jnp.logical_not(first_head), q, jnp.zeros_like(q))
            s_w = _dot_nt(qh, kw) + bias_ref[0, pat, h]
            s_c = _dot_nt(qh, kc)
            m = jnp.maximum(jnp.max(s_w, axis=-1, keepdims=True), jnp.max(s_c, axis=-1, keepdims=True))
            p_w = jnp.exp(s_w - m)
            p_c = jnp.exp(s_c - m)
            l = jnp.sum(p_w, axis=-1, keepdims=True) + jnp.sum(p_c, axis=-1, keepdims=True)
            o = _dot(p_w.astype(BF16), vw) + _dot(p_c.astype(BF16), vc)
            outs.append(o / l)
        o = jnp.where(first_head, outs[0], outs[1])
        o_ref[0, pl.ds(pl.multiple_of(r0 * GRID_W, GRID_W), nq), :] = o.astype(o_ref.dtype)
        return carry

    lax.fori_loop(0, n_blocks, block, 0)


def _na(q, k, v, kc, vc, bias_tab):
    b, n, w = q.shape
    n_ctx = kc.shape[1]
    pairs = w // LANES
    rows = n // GRID_W
    nq, nk = NA_QROWS * GRID_W, NA_KROWS * GRID_W
    tok = lambda i, p: (i, 0, p)
    return pl.pallas_call(
        functools.partial(_na_kernel, rows=rows),
        out_shape=jax.ShapeDtypeStruct((b, n, w), BF16),
        grid=(b, pairs),
        in_specs=[pl.BlockSpec((1, n, LANES), tok),
                  pl.BlockSpec((1, n, LANES), tok),
                  pl.BlockSpec((1, n, LANES), tok),
                  pl.BlockSpec((1, n_ctx, LANES), tok),
                  pl.BlockSpec((1, n_ctx, LANES), tok),
                  pl.BlockSpec((1, 3, 2, nq, nk), lambda i, p: (p, 0, 0, 0, 0))],
        out_specs=pl.BlockSpec((1, n, LANES), tok),
        compiler_params=_cparams(("arbitrary", "arbitrary")),
        name="na",
    )(q, k, v, kc, vc, bias_tab)


def _gla_range_matrices(t):
    c = np.zeros((2, GLA_MATS, t, t), np.float32)
    for i in range(t):
        c[0, 0, i, :i + 1] = 1
        c[0, 1, i, i + 1:] = 1
        c[1, 0, i, i:] = 1
        c[1, 1, i, :i] = 1
        for l, half in enumerate(GLA_LEVELS + (GLA_DIAG // 2,)):
            mid = (i // (2 * half)) * 2 * half + half
            if i >= mid:
                c[0, 2 + l, i, mid:i + 1] = 1
                c[1, 2 + l, i, mid:i] = 1
            else:
                c[0, 2 + l, i, i + 1:mid] = 1
                c[1, 2 + l, i, i:mid] = 1
    return c.reshape(2, GLA_MATS * t, t)


def _gla_kernel(q_ref, k_ref, v_ref, vt_ref, ad_ref, g_ref, ck_ref, cvt_ref, cad_ref,
                u_ref, ab_ref, gn_ref, cm_ref, o_ref, accf_ref, accb_ref, *, n_tok):
    t = GLA_T
    n_chunks = n_tok // t
    n_lv = len(GLA_LEVELS)
    row = lax.broadcasted_iota(jnp.int32, (t, LANES), 0)
    lane = lax.broadcasted_iota(jnp.int32, (t, LANES), 1)
    head0_l = lane < GLA_DK
    row2 = lax.broadcasted_iota(jnp.int32, (t, 2 * t), 0)
    col2 = lax.broadcasted_iota(jnp.int32, (t, 2 * t), 1) & (t - 1)
    vrow = lax.broadcasted_iota(jnp.int32, (2 * t, 2 * GLA_DV), 0)
    vlane = lax.broadcasted_iota(jnp.int32, (2 * t, 2 * GLA_DV), 1)
    v_head_match = (vrow >= t) == (vlane >= GLA_DV)
    srow = lax.broadcasted_iota(jnp.int32, (2 * GLA_DV, LANES), 0)
    slane = lax.broadcasted_iota(jnp.int32, (2 * GLA_DV, LANES), 1)
    s_blockdiag = (srow >= GLA_DV) == (slane >= GLA_DK)
    blk_mask = {half: jnp.where((row2 & ~(2 * half - 1)) == (col2 & ~(2 * half - 1)), 1.0, 0.0)
                for half in GLA_LEVELS if 2 * half < t}
    diag_blk = (row2 & ~(GLA_DIAG - 1)) == (col2 & ~(GLA_DIAG - 1))
    diag_mask = (jnp.where(diag_blk & (row2 >= col2), 1.0, 0.0), jnp.where(diag_blk & (row2 <= col2), 1.0, 0.0))

    def range_sums(ad, dirn):
        z = _dot(ad, u_ref[dirn]) + ab_ref[dirn]
        loga = (jnp.minimum(z, 0.0) - jnp.log(1.0 + jnp.exp(-jnp.abs(z)))) * (1.0 / GLA_GATE_TAU)
        hi, lo = _split_bf16(loga)
        e2 = _dot(cm_ref[dirn], jnp.concatenate([hi, lo], axis=-1))
        return e2[:, :LANES] + e2[:, LANES:]

    def e_of(e, idx):
        return e[idx * t:(idx + 1) * t]

    def state_update(s, k, vt, e, dirn):
        kh = (k * jnp.exp(e_of(e, 1))).astype(BF16)
        e_q = e_of(e, 0)
        total = jnp.exp(e_q[t - 1:t, :] if dirn == 0 else e_q[0:1, :])
        return s * total + jnp.where(s_blockdiag, _dot(vt, kh), 0.0)

    def heads_stacked(kt):
        return jnp.concatenate([jnp.where(head0_l, kt, 0.0), jnp.where(head0_l, 0.0, kt)], axis=0).astype(BF16)

    def chunk(tok0, s, dirn):
        q = q_ref[0, pl.ds(tok0, t), :].astype(F32)
        k = k_ref[0, pl.ds(tok0, t), :].astype(F32)
        v = v_ref[0, pl.ds(tok0, t), :]
        vt = vt_ref[0, :, pl.ds(tok0, t)]
        e = range_sums(ad_ref[0, pl.ds(tok0, t), :], dirn)
        qh = (q * jnp.exp(e_of(e, 0))).astype(BF16)
        o = _dot_nt(qh, s.astype(BF16))
        a = None
        for l, half in enumerate(GLA_LEVELS):
            w = jnp.exp(e_of(e, 2 + l))
            later = (row & half) != 0
            q_side = later if dirn == 0 else jnp.logical_not(later)
            qt = jnp.where(q_side, q * w, 0.0).astype(BF16)
            kt = jnp.where(q_side, 0.0, k * w)
            part = _dot_nt(qt, heads_stacked(kt))
            if half in blk_mask:
                part = part * blk_mask[half]
            a = part if a is None else a + part
        e_d = e_of(e, 2 + n_lv)
        w, wi = jnp.exp(e_d), jnp.exp(-e_d)
        later = (row & (GLA_DIAG // 2)) != 0
        shrink_q = later if dirn == 0 else jnp.logical_not(later)
        qt = (q * jnp.where(shrink_q, w, wi)).astype(BF16)
        kt = k * jnp.where(shrink_q, wi, w)
        a = a + jnp.where(diag_mask[dirn] > 0.5, _dot_nt(qt, heads_stacked(kt)), 0.0)
        vcat = jnp.concatenate([v, v], axis=0)
        vcat = jnp.where(v_head_match, vcat, jnp.zeros_like(vcat))
        o = o + _dot(a.astype(BF16), vcat)
        return o, state_update(s, k, vt, e, dirn)

    def ctx_state(dirn):
        e = range_sums(cad_ref[0], dirn)
        s0 = jnp.zeros((2 * GLA_DV, LANES), F32)
        return state_update(s0, ck_ref[0].astype(F32), cvt_ref[0], e, dirn)

    def body(i, carry):
        s_f, s_b = carry
        tok_f = pl.multiple_of(i * t, t)
        tok_b = pl.multiple_of((n_chunks - 1 - i) * t, t)
        o_f, s_f = chunk(tok_f, s_f, 0)
        o_b, s_b = chunk(tok_b, s_b, 1)
        accf_ref[pl.ds(tok_f, t), :] = o_f
        accb_ref[pl.ds(tok_b, t), :] = o_b
        return s_f, s_b

    lax.fori_loop(0, n_chunks, body, (ctx_state(0), ctx_state(1)))

    def finish(c, carry):
        tok0 = pl.multiple_of(c * t, t)
        o = accf_ref[pl.ds(tok0, t), :] + accb_ref[pl.ds(tok0, t), :]
        g = g_ref[0, pl.ds(tok0, t), :].astype(F32)
        gate = g / (1.0 + jnp.exp(-g))
        halves = [_rms(o[:, h * GLA_DV:(h + 1) * GLA_DV], gn_ref[...]) for h in range(2)]
        o_ref[0, pl.ds(tok0, t), :] = (jnp.concatenate(halves, axis=-1) * gate).astype(o_ref.dtype)
        return carry

    lax.fori_loop(0, n_chunks, finish, 0)


def _gla(q, k, v, vt, ad, g, ck, cvt, cad, u, abias, gnorm, cmats):
    b, n, kw = q.shape
    n_ctx = ck.shape[1]
    pairs = kw // LANES
    vw = 2 * GLA_DV
    tok = lambda i, p: (i, 0, p)
    full3 = lambda i, p: (i, 0, 0)
    return pl.pallas_call(
        functools.partial(_gla_kernel, n_tok=n),
        out_shape=jax.ShapeDtypeStruct((b, n, v.shape[2]), BF16),
        grid=(b, pairs),
        in_specs=[pl.BlockSpec((1, n, LANES), tok),
                  pl.BlockSpec((1, n, LANES), tok),
                  pl.BlockSpec((1, n, vw), tok),
                  pl.BlockSpec((1, vw, n), lambda i, p: (i, p, 0)),
                  pl.BlockSpec((1, n, LANES), full3),
                  pl.BlockSpec((1, n, vw), tok),
                  pl.BlockSpec((1, n_ctx, LANES), tok),
                  pl.BlockSpec((1, vw, n_ctx), lambda i, p: (i, p, 0)),
                  pl.BlockSpec((1, n_ctx, LANES), full3),
                  pl.BlockSpec((2, LANES, LANES), lambda i, p: (0, 0, p)),
                  pl.BlockSpec((2, 1, LANES), lambda i, p: (0, 0, p)),
                  pl.BlockSpec((1, GLA_DV), lambda i, p: (0, 0)),
                  pl.BlockSpec(cmats.shape, lambda i, p: (0, 0, 0))],
        out_specs=pl.BlockSpec((1, n, vw), tok),
        scratch_shapes=[pltpu.VMEM((n, vw), F32), pltpu.VMEM((n, vw), F32)],
        compiler_params=_cparams(("arbitrary", "arbitrary")),
        name="gla",
    )(q, k, v, vt, ad, g, ck, cvt, cad, u, abias, gnorm, cmats)


def _gla_decay_up(a_up):
    r = GLA_GATE_RANK
    pad = jnp.zeros((2, 2 * r, GLA_KEY_W), F32)
    pad = pad.at[0, :r].set(a_up[0]).at[1, r:].set(a_up[1])
    hi, lo = _split_bf16(pad)
    return jnp.concatenate([hi, hi, lo, jnp.zeros_like(hi)], axis=1)


def _outproj_kernel(ona_ref, ogla_ref, x_ref, w1_ref, w2_ref, gm_ref, shf_ref, scf_ref, npost_ref,
                    nfpre_ref, rt_ref, xnew_ref, hf_ref, aff_ref, afft_ref):
    mix = _dot(ona_ref[0], w1_ref[...]) + _dot(ogla_ref[0], w2_ref[...])
    xn = x_ref[0] + gm_ref[0] * _rms(mix, npost_ref[...])
    xnew_ref[0] = xn
    h = _rms(xn, nfpre_ref[...]) * (1.0 + scf_ref[0]) + shf_ref[0]
    h_hi, h_lo = _split_bf16(h)
    hf_ref[0] = h_hi
    res = _dot(h_hi, rt_ref[...])
    logits = res[:, :LANES] + res[:, LANES:] + _dot(h_lo, rt_ref[:, :LANES])
    lane = lax.broadcasted_iota(jnp.int32, logits.shape, 1)
    logits = jnp.where(lane < N_EXPERTS, logits, NEG_BIG)
    p = jnp.exp(logits - jnp.max(logits, axis=-1, keepdims=True))
    aff = p / jnp.sum(p, axis=-1, keepdims=True)
    aff_ref[0] = aff
    afft_ref[0] = aff.T[:N_EXPERTS, :]


def _outproj(o_na, o_gla, x, w1, w2, gm, shf, scf, npost, nfpre, router_cat):
    b, n, d = x.shape
    tm = TOK_TILE
    tokmap = lambda i, j: (i, j, 0)
    smp = lambda i, j: (i, 0, 0)
    cst = lambda i, j: (0, 0)
    return pl.pallas_call(
        _outproj_kernel,
        out_shape=[jax.ShapeDtypeStruct((b, n, d), F32),
                   jax.ShapeDtypeStruct((b, n, d), BF16),
                   jax.ShapeDtypeStruct((b, n, LANES), F32),
                   jax.ShapeDtypeStruct((b, N_EXPERTS, n), F32)],
        grid=(b, n // tm),
        in_specs=[pl.BlockSpec((1, tm, o_na.shape[2]), tokmap),
                  pl.BlockSpec((1, tm, o_gla.shape[2]), tokmap),
                  pl.BlockSpec((1, tm, d), tokmap),
                  pl.BlockSpec(w1.shape, cst),
                  pl.BlockSpec(w2.shape, cst),
                  pl.BlockSpec((1, 1, d), smp),
                  pl.BlockSpec((1, 1, d), smp),
                  pl.BlockSpec((1, 1, d), smp),
                  pl.BlockSpec((1, d), cst),
                  pl.BlockSpec((1, d), cst),
                  pl.BlockSpec(router_cat.shape, cst)],
        out_specs=[pl.BlockSpec((1, tm, d), tokmap),
                   pl.BlockSpec((1, tm, d), tokmap),
                   pl.BlockSpec((1, tm, LANES), tokmap),
                   pl.BlockSpec((1, N_EXPERTS, tm), lambda i, j: (i, 0, j))],
        compiler_params=_cparams(("arbitrary", "arbitrary")),
        name="outproj",
    )(o_na, o_gla, x, w1, w2, gm, shf, scf, npost.reshape(1, d), nfpre.reshape(1, d), router_cat)


def _route_kernel(afft_ref, rt_ref, rc_ref, *, cap):
    a = afft_ref[0]
    e, n = a.shape
    capf = jnp.float32(cap)

    def search(i, thr_bits):
        cand = thr_bits | lax.shift_left(jnp.int32(1), 30 - i)
        cnt = jnp.sum(jnp.where(a >= lax.bitcast_convert_type(cand, F32), 1.0, 0.0), axis=-1, keepdims=True)
        return jnp.where(cnt >= capf, cand, thr_bits)

    thr_bits = lax.fori_loop(0, 31, search, jnp.zeros((e, 1), jnp.int32))
    thr = lax.bitcast_convert_type(thr_bits, F32)
    need = capf - jnp.sum(jnp.where(a > thr, 1.0, 0.0), axis=-1, keepdims=True)
    r_i = lax.broadcasted_iota(jnp.int32, (LANES, LANES), 0)
    c_i = lax.broadcasted_iota(jnp.int32, (LANES, LANES), 1)
    incl = jnp.where(r_i <= c_i, 1.0, 0.0).astype(BF16)
    off_eq = jnp.zeros((e, 1), F32)
    off_sel = jnp.zeros((e, 1), F32)
    pad = jnp.full((LANES - e, LANES), -1.0, F32)
    for j in range(n // LANES):
        sl = slice(j * LANES, (j + 1) * LANES)
        a_b = a[:, sl]
        eq_b = jnp.where(a_b == thr, 1.0, 0.0)
        tie_rank = _dot(eq_b.astype(BF16), incl) - eq_b + off_eq
        off_eq = off_eq + jnp.sum(eq_b, axis=-1, keepdims=True)
        sel_b = jnp.where(a_b > thr, 1.0, jnp.where(tie_rank < need, eq_b, 0.0))
        sel = sel_b > 0.5
        rank = _dot(sel_b.astype(BF16), incl) - sel_b + off_sel
        off_sel = off_sel + jnp.sum(sel_b, axis=-1, keepdims=True)
        rsel = jnp.where(sel, rank, -1.0)
        rt_ref[0, :, sl] = rsel.astype(jnp.int32)
        rc_ref[0, sl, :] = jnp.concatenate([rsel, pad], axis=0).T.astype(jnp.int32)


def _route(afft, cap):
    b, e, n = afft.shape
    return pl.pallas_call(
        functools.partial(_route_kernel, cap=cap),
        out_shape=[jax.ShapeDtypeStruct((b, e, n), jnp.int32),
                   jax.ShapeDtypeStruct((b, n, LANES), jnp.int32)],
        grid=(b,),
        in_specs=[pl.BlockSpec((1, e, n), lambda i: (i, 0, 0))],
        out_specs=[pl.BlockSpec((1, e, n), lambda i: (i, 0, 0)),
                   pl.BlockSpec((1, n, LANES), lambda i: (i, 0, 0))],
        compiler_params=_cparams(("arbitrary",)),
        name="route",
    )(afft)


def _gather_kernel(r_ref, hf_ref, o_ref, *, cap):
    r = r_ref[0, 0]
    n = r.shape[1]
    slot = lax.broadcasted_iota(jnp.int32, (SLOT_TILE, n), 0)
    for s0 in range(0, cap, SLOT_TILE):
        onehot = jnp.where(r == slot + s0, 1.0, 0.0).astype(BF16)
        o_ref[0, 0, s0:s0 + SLOT_TILE, :] = _dot(onehot, hf_ref[0]).astype(o_ref.dtype)


def _gather(rsel_t, hf, cap):
    b, e, n = rsel_t.shape
    d = hf.shape[2]
    return pl.pallas_call(
        functools.partial(_gather_kernel, cap=cap),
        out_shape=jax.ShapeDtypeStruct((b, e, cap, d), BF16),
        grid=(b, e),
        in_specs=[pl.BlockSpec((1, 1, 1, n), lambda i, j: (i, j, 0, 0)),
                  pl.BlockSpec((1, n, d), lambda i, j: (i, 0, 0))],
        out_specs=pl.BlockSpec((1, 1, cap, d), lambda i, j: (i, j, 0, 0)),
        compiler_params=_cparams(("arbitrary", "arbitrary")),
        name="gather",
    )(rsel_t.reshape(b, e, 1, n), hf)


def _ffn_kernel(x_ref, wg_ref, wu_ref, wd_ref, o_ref, acc_ref):
    f = pl.program_id(1)
    b = x_ref.shape[0]
    wg = wg_ref[0].astype(BF16)
    wu = wu_ref[0].astype(BF16)
    wd = wd_ref[0].astype(BF16)

    @pl.when(f == 0)
    def _():
        acc_ref[...] = jnp.zeros_like(acc_ref)

    for i in range(b):
        x = x_ref[i, 0]
        g = _dot(x, wg)
        u = _dot(x, wu)
        hid = (g / (1.0 + jnp.exp(-g)) * u).astype(BF16)
        acc_ref[i] += _dot(hid, wd)

    @pl.when(f == pl.num_programs(1) - 1)
    def _():
        for i in range(b):
            o_ref[i, 0] = acc_ref[i].astype(o_ref.dtype)


def _ffn(xs, w_gate, w_up, w_down):
    b, e, cap, d = xs.shape
    dff = w_gate.shape[2]
    tf = FFN_TILE
    return pl.pallas_call(
        _ffn_kernel,
        out_shape=jax.ShapeDtypeStruct((b, e, cap, d), BF16),
        grid=(e, dff // tf),
        in_specs=[pl.BlockSpec((b, 1, cap, d), lambda i, f: (0, i, 0, 0)),
                  pl.BlockSpec((1, d, tf), lambda i, f: (i, 0, f)),
                  pl.BlockSpec((1, d, tf), lambda i, f: (i, 0, f)),
                  pl.BlockSpec((1, tf, d), lambda i, f: (i, f, 0))],
        out_specs=pl.BlockSpec((b, 1, cap, d), lambda i, f: (0, i, 0, 0)),
        scratch_shapes=[pltpu.VMEM((b, cap, d), F32)],
        compiler_params=_cparams(("arbitrary", "arbitrary")),
        name="ffn",
    )(xs, w_gate, w_up, w_down)


def _combine_kernel(ys_ref, rc_ref, aff_ref, xn_ref, gf_ref, npost_ref, o_ref):
    tm = rc_ref.shape[1]
    cap = ys_ref.shape[2]
    slot = lax.broadcasted_iota(jnp.int32, (tm, cap), 1)
    acc = jnp.zeros(o_ref.shape[1:], F32)
    for e in range(ys_ref.shape[1]):
        onehot = jnp.where(rc_ref[0, :, e:e + 1] == slot, 1.0, 0.0).astype(BF16)
        acc = acc + aff_ref[0, :, e:e + 1] * _dot(onehot, ys_ref[0, e])
    o_ref[0] = xn_ref[0] + gf_ref[0] * _rms(acc, npost_ref[...])


def _combine(ys, rsel_c, aff, x_new, gf, npost):
    b, e, cap, d = ys.shape
    n = x_new.shape[1]
    tm = TOK_TILE
    tokmap = lambda i, j: (i, j, 0)
    return pl.pallas_call(
        _combine_kernel,
        out_shape=jax.ShapeDtypeStruct((b, n, d), F32),
        grid=(b, n // tm),
        in_specs=[pl.BlockSpec((1, e, cap, d), lambda i, j: (i, 0, 0, 0)),
                  pl.BlockSpec((1, tm, LANES), tokmap),
                  pl.BlockSpec((1, tm, LANES), tokmap),
                  pl.BlockSpec((1, tm, d), tokmap),
                  pl.BlockSpec((1, 1, d), lambda i, j: (i, 0, 0)),
                  pl.BlockSpec((1, d), lambda i, j: (0, 0))],
        out_specs=pl.BlockSpec((1, tm, d), tokmap),
        compiler_params=_cparams(("arbitrary", "arbitrary")),
        name="combine",
    )(ys, rsel_c, aff, x_new, gf, npost.reshape(1, d))


def kernel(x, c, ctx, c_ctx, w_mod, b_mod, norm_mix_pre, norm_mix_post, norm_ffn_pre, norm_ffn_post,
           w_in, na_rpb, gla_a_up, gla_a_bias, gla_norm, w_out, router, w_gate, w_up, w_down):
    b, n, d = x.shape
    assert w_mod.shape[0] == 1 and d == D_MODEL and n % (GRID_W * NA_QROWS) == 0 and n % GLA_T == 0
    assert ctx.shape[1] == GLA_T
    rows = n // GRID_W
    cap = EC_CAPACITY_FACTOR * n // N_EXPERTS

    c8 = jnp.concatenate([c, c_ctx[None, :], jnp.zeros((8 - b - 1, d), F32)], axis=0)
    mod = _mod(c8, w_mod[0], b_mod[0])
    sh_m, sc_m, g_m, sh_f, sc_f, g_f = [m[:b, None, :] for m in jnp.split(mod, 6, axis=-1)]
    sh_c, sc_c = mod[b:b + 1, None, :d], mod[b:b + 1, None, d:2 * d]

    wb = w_in[0].astype(BF16)
    cuts = np.cumsum([0, NA_W, NA_W, GLA_KEY_W, GLA_VAL_W, 2 * GLA_GATE_RANK, NA_W, GLA_KEY_W, GLA_VAL_W])
    w_nak, w_nav, w_gk, w_gv, w_ad, w_naq, w_gq, w_gg = [wb[:, cuts[i]:cuts[i + 1]] for i in range(8)]
    w_gvt = w_gv.T
    w_ad3 = jnp.concatenate([w_ad, w_ad, w_ad, jnp.zeros_like(w_ad)], axis=1)
    rope = _rope_tables(n)
    na_q, na_k, na_v, gq, gk, gv, gvt, ad, gg = _inproj(
        x, sh_m, sc_m, norm_mix_pre[0],
        [w_naq, w_nak, w_nav, w_gq, w_gk, w_gv, w_gvt, w_ad3, w_gg],
        ["plain", "plain", "plain", "rope", "rope", "plain", "t", "split", "plain"],
        [1.0, 1.0, 1.0, GLA_DK ** -0.5, 1.0, 1.0, 1.0, 1.0, 1.0],
        [BF16] * 9, rope=rope)
    c_nak, c_nav, c_gk, c_gvt, c_ad = _inproj(
        ctx, sh_c, sc_c, norm_mix_pre[0],
        [w_nak, w_nav, w_gk, w_gvt, w_ad3],
        ["plain", "plain", "plain", "t", "split"], [1.0] * 5, [BF16] * 5)

    bias = _na_bias_table(na_rpb[0], rows)
    bias = bias.reshape(NA_HEADS // 2, 2, 3, *bias.shape[2:]).transpose(0, 2, 1, 3, 4)
    o_na = _na(na_q, na_k, na_v, c_nak, c_nav, bias)

    cmats = jnp.asarray(_gla_range_matrices(GLA_T), BF16)
    o_gla = _gla(gq, gk, gv, gvt, ad, gg, c_gk, c_gvt, c_ad, _gla_decay_up(gla_a_up[0]),
                 gla_a_bias[0][:, None, :], gla_norm[0][None, :], cmats)

    wo = w_out[0].astype(BF16)
    router_pad = jnp.zeros((d, LANES), F32).at[:, :N_EXPERTS].set(router[0])
    x_new, hf, aff, aff_t = _outproj(o_na, o_gla, x, wo[:NA_W], wo[NA_W:], g_m, sh_f, sc_f,
                                     norm_mix_post[0], norm_ffn_pre[0],
                                     jnp.concatenate(_split_bf16(router_pad), axis=1))

    rsel_t, rsel_c = _route(aff_t, cap)
    xs = _gather(rsel_t, hf, cap)
    ys = _ffn(xs, w_gate[0], w_up[0], w_down[0])
    return _combine(ys, rsel_c, aff, x_new, g_f, norm_ffn_post[0])
```

```python
import functools

import numpy as np
import jax
import jax.numpy as jnp
from jax import lax
from jax.experimental import pallas as pl
from jax.experimental.pallas import tpu as pltpu

F32 = jnp.float32
BF16 = jnp.bfloat16
HIGHEST = lax.Precision.HIGHEST

D_MODEL = 1024
GRID_W = 64
NA_W = 512
NA_HEADS = 8
NA_HEAD_DIM = 64
NA_WIN_ROWS = 8
NA_WIN_COLS = 16
GLA_HEADS = 4
GLA_DV = 128
GLA_DK = 64
GLA_KEY_W = 256
GLA_VAL_W = 512
GLA_GATE_RANK = 16
GLA_GATE_TAU = 16.0
ROPE_BASE = 10000.0
N_EXPERTS = 16
EC_CAPACITY_FACTOR = 2
NORM_EPS = 1e-6
NEG_BIG = -1e30

LANES = 128
VMEM_LIMIT = 56 * 1024 * 1024

TOK_TILE = 512
NA_QROWS = 4
NA_KROWS = NA_QROWS + NA_WIN_ROWS
NA_TAB_PAD = NA_QROWS
NA_TAB_BLOCKS = NA_TAB_PAD + 2 * NA_WIN_ROWS - 1 + NA_QROWS + 1
GLA_T = 256
GLA_LEVELS = (128, 64, 32, 16)
GLA_DIAG = 16
FFN_TILE = 256
SLOT_TILE = 128


def _cparams(sem):
    return pltpu.CompilerParams(dimension_semantics=sem, vmem_limit_bytes=VMEM_LIMIT)


def _rms(v, g):
    return v * lax.rsqrt(jnp.mean(v * v, axis=-1, keepdims=True) + NORM_EPS) * g


def _dot(a, b):
    return jnp.dot(a, b, preferred_element_type=F32)


def _dot_nt(a, b):
    return lax.dot_general(a, b, (((1,), (1,)), ((), ())), preferred_element_type=F32)


def _split_bf16(v):
    hi = v.astype(BF16)
    return hi, (v - hi.astype(F32)).astype(BF16)


def _mod_kernel(c_ref, w_ref, b_ref, o_ref):
    c = c_ref[...]
    s = c / (1.0 + jnp.exp(-c))
    o_ref[...] = jnp.dot(s, w_ref[...], preferred_element_type=F32, precision=HIGHEST) + b_ref[...]


def _mod(c8, w_mod, b_mod):
    d, n = w_mod.shape
    tn = 1536
    return pl.pallas_call(
        _mod_kernel,
        out_shape=jax.ShapeDtypeStruct((8, n), F32),
        grid=(n // tn,),
        in_specs=[pl.BlockSpec((8, d), lambda j: (0, 0)),
                  pl.BlockSpec((d, tn), lambda j: (0, j)),
                  pl.BlockSpec((1, tn), lambda j: (0, j))],
        out_specs=pl.BlockSpec((8, tn), lambda j: (0, j)),
        compiler_params=_cparams(("arbitrary",)),
        name="mod",
    )(c8, w_mod, b_mod.reshape(1, n))


def _inproj_kernel(x_ref, sh_ref, sc_ref, g_ref, *refs, modes, scales):
    n_rope = 2 if "rope" in modes else 0
    rope_refs, refs = refs[:n_rope], refs[n_rope:]
    n_w = len(modes)
    w_refs, o_refs = refs[:n_w], refs[n_w:]
    x = x_ref[0]
    h = _rms(x, g_ref[...]) * (1.0 + sc_ref[0]) + sh_ref[0]
    hb = h.astype(BF16)
    for w_ref, o_ref, mode, scale in zip(w_refs, o_refs, modes, scales):
        if mode == "t":
            o_ref[0] = _dot_nt(w_ref[...], hb).astype(o_ref.dtype)
            continue
        y = _dot(hb, w_ref[...])
        if mode == "rope":
            cols = y.shape[1]
            quarter = GLA_DK // 4
            lane = lax.broadcasted_iota(jnp.int32, y.shape, 1)
            partner = jnp.where((lane & (2 * quarter - 1)) < quarter,
                                pltpu.roll(y, cols - quarter, axis=1), pltpu.roll(y, quarter, axis=1))
            y = (y * rope_refs[0][...] + partner * rope_refs[1][...]) * scale
        elif mode == "split":
            hi, lo = _split_bf16(y)
            lane = lax.broadcasted_iota(jnp.int32, y.shape, 1)
            rank2 = 2 * GLA_GATE_RANK
            y = jnp.where((lane >= rank2) & (lane < 2 * rank2), lo, hi)
        o_ref[0] = y.astype(o_ref.dtype)


def _inproj(x, shift, scale, g, weights, modes, scales, out_dtypes, rope=None):
    b, n, d = x.shape
    tm = min(TOK_TILE, n)
    per_sample = shift.shape[0] == b
    mod_map = (lambda i, j: (i, 0, 0)) if per_sample else (lambda i, j: (0, 0, 0))
    in_specs = [pl.BlockSpec((1, tm, d), lambda i, j: (i, j, 0)),
                pl.BlockSpec((1, 1, d), mod_map),
                pl.BlockSpec((1, 1, d), mod_map),
                pl.BlockSpec((1, d), lambda i, j: (0, 0))]
    args = [x, shift, scale, g.reshape(1, d)]
    if rope is not None:
        in_specs += [pl.BlockSpec((tm, rope[0].shape[1]), lambda i, j: (j, 0))] * 2
        args += list(rope)
    out_shapes, out_specs = [], []
    for w, mode, dt in zip(weights, modes, out_dtypes):
        in_specs.append(pl.BlockSpec(w.shape, lambda i, j: (0, 0)))
        if mode == "t":
            cols = w.shape[0]
            out_shapes.append(jax.ShapeDtypeStruct((b, cols, n), dt))
            out_specs.append(pl.BlockSpec((1, cols, tm), lambda i, j: (i, 0, j)))
        else:
            cols = w.shape[1]
            out_shapes.append(jax.ShapeDtypeStruct((b, n, cols), dt))
            out_specs.append(pl.BlockSpec((1, tm, cols), lambda i, j: (i, j, 0)))
    return pl.pallas_call(
        functools.partial(_inproj_kernel, modes=tuple(modes), scales=tuple(scales)),
        out_shape=out_shapes,
        grid=(b, n // tm),
        in_specs=in_specs,
        out_specs=out_specs,
        compiler_params=_cparams(("arbitrary", "arbitrary")),
        name="inproj",
    )(*args, *weights)


def _rope_tables(n):
    t = jnp.arange(n)
    pos_row, pos_col = (t // GRID_W).astype(F32), (t % GRID_W).astype(F32)
    quarter = GLA_DK // 4
    freqs = ROPE_BASE ** (-jnp.arange(quarter, dtype=F32) / quarter)
    ang_r = pos_row[:, None] * freqs
    ang_c = pos_col[:, None] * freqs
    cos = jnp.concatenate([jnp.cos(ang_r), jnp.cos(ang_r), jnp.cos(ang_c), jnp.cos(ang_c)], axis=-1)
    sin = jnp.concatenate([-jnp.sin(ang_r), jnp.sin(ang_r), -jnp.sin(ang_c), jnp.sin(ang_c)], axis=-1)
    return jnp.tile(cos, (1, GLA_HEADS)), jnp.tile(sin, (1, GLA_HEADS))


def _na_patterns(rows):
    kr = min(NA_WIN_ROWS, rows)
    n_blocks = rows // NA_QROWS
    pats = []
    for blk in (0, 1, n_blocks - 1):
        r0 = blk * NA_QROWS
        k0 = int(np.clip(r0 - kr // 2, 0, rows - NA_KROWS))
        strips = []
        for a in range(NA_QROWS):
            r_start = int(np.clip(r0 + a - kr // 2, 0, rows - kr))
            start = k0 - (r0 + a) + NA_WIN_ROWS - 1 + NA_TAB_PAD
            assert 0 <= start and start + NA_KROWS <= NA_TAB_BLOCKS
            strips.append((start, [r_start <= k0 + c < r_start + kr for c in range(NA_KROWS)]))
        pats.append(strips)
    return pats


def _na_bias_tables(rpb, rows):
    heads = rpb.shape[0]
    col = np.arange(GRID_W)
    c_start = np.clip(col - NA_WIN_COLS // 2, 0, GRID_W - NA_WIN_COLS)
    col_ok = (col[None, :] >= c_start[:, None]) & (col[None, :] < c_start[:, None] + NA_WIN_COLS)
    dc = np.clip(col[None, :] - col[:, None] + NA_WIN_COLS - 1, 0, 2 * NA_WIN_COLS - 2)
    sel_c = (np.arange(2 * NA_WIN_COLS - 1)[:, None, None] == dc[None]) & col_ok[None]
    t = jnp.einsum("hrd,dqk->hqrk", rpb, jnp.asarray(sel_c, F32), precision=HIGHEST)
    t = jnp.where(jnp.asarray(col_ok)[None, :, None, :], t, NEG_BIG)
    n_dr = 2 * NA_WIN_ROWS - 1
    t = t.reshape(heads, GRID_W, n_dr * GRID_W)
    back = NA_TAB_BLOCKS + 1 - NA_TAB_PAD - n_dr
    t = jnp.pad(t, ((0, 0), (0, 0), (NA_TAB_PAD * GRID_W, back * GRID_W)), constant_values=NEG_BIG)
    width = NA_TAB_BLOCKS * GRID_W
    tab = jnp.stack([t[:, :, :width], t[:, :, GRID_W:GRID_W + width]], axis=1)
    row_mask = np.zeros((3, NA_QROWS, 1, NA_KROWS * GRID_W), np.float32)
    for pat, strips in enumerate(_na_patterns(rows)):
        for a, (_, valid) in enumerate(strips):
            row_mask[pat, a, 0] = np.repeat(np.where(valid, 0.0, NEG_BIG), GRID_W)
    return tab, jnp.asarray(row_mask)


def _na_kernel(q_ref, k_ref, v_ref, kc_ref, vc_ref, tab_ref, rmask_ref, o_ref, bias_ref, *, rows):
    nq, nk = NA_QROWS * GRID_W, NA_KROWS * GRID_W
    n_blocks = rows // NA_QROWS
    kr = min(NA_WIN_ROWS, rows)
    scale = NA_HEAD_DIM ** -0.5
    lane = lax.broadcasted_iota(jnp.int32, (nq, LANES), 1)
    first_head = lane < NA_HEAD_DIM
    kc = kc_ref[0]
    vc = vc_ref[0]
    for pat, strips in enumerate(_na_patterns(rows)):
        for a, (start, _) in enumerate(strips):
            parity = start % 2
            off = (start - parity) * GRID_W
            for h in range(2):
                bias_ref[0, pat, h, a * GRID_W:(a + 1) * GRID_W, :] = (
                    tab_ref[h, parity, :, off:off + nk] + rmask_ref[pat, a])

    def block(i, carry):
        r0 = i * NA_QROWS
        k0 = jnp.clip(r0 - kr // 2, 0, rows - NA_KROWS)
        pat = jnp.where(i == 0, 0, jnp.where(i == n_blocks - 1, 2, 1))
        q = q_ref[0, pl.ds(pl.multiple_of(r0 * GRID_W, GRID_W), nq), :] * scale
        kw = k_ref[0, pl.ds(pl.multiple_of(k0 * GRID_W, GRID_W), nk), :]
        vw = v_ref[0, pl.ds(pl.multiple_of(k0 * GRID_W, GRID_W), nk), :]
        outs = []
        for h in range(2):
            qh = jnp.where(first_head if h == 0 else jnp.logical_not(first_head), q, jnp.zeros_like(q))
            s_w = _dot_nt(qh, kw) + bias_ref[0, pat, h]
            s_c = _dot_nt(qh, kc)
            m = jnp.maximum(jnp.max(s_w, axis=-1, keepdims=True), jnp.max(s_c, axis=-1, keepdims=True))
            p_w = jnp.exp(s_w - m)
            p_c = jnp.exp(s_c - m)
            l = jnp.sum(p_w, axis=-1, keepdims=True) + jnp.sum(p_c, axis=-1, keepdims=True)
            o = _dot(p_w.astype(BF16), vw) + _dot(p_c.astype(BF16), vc)
            outs.append(o / l)
        o = jnp.where(first_head, outs[0], outs[1])
        o_ref[0, pl.ds(pl.multiple_of(r0 * GRID_W, GRID_W), nq), :] = o.astype(o_ref.dtype)
        return carry

    lax.fori_loop(0, n_blocks, block, 0)


def _na(q, k, v, kc, vc, tab, row_mask):
    b, n, w = q.shape
    n_ctx = kc.shape[1]
    pairs = w // LANES
    rows = n // GRID_W
    nq, nk = NA_QROWS * GRID_W, NA_KROWS * GRID_W
    tok = lambda i, p: (i, 0, p)
    return pl.pallas_call(
        functools.partial(_na_kernel, rows=rows),
        out_shape=jax.ShapeDtypeStruct((b, n, w), BF16),
        grid=(b, pairs),
        in_specs=[pl.BlockSpec((1, n, LANES), tok),
                  pl.BlockSpec((1, n, LANES), tok),
                  pl.BlockSpec((1, n, LANES), tok),
                  pl.BlockSpec((1, n_ctx, LANES), tok),
                  pl.BlockSpec((1, n_ctx, LANES), tok),
                  pl.BlockSpec((2,) + tab.shape[1:], lambda i, p: (p, 0, 0, 0)),
                  pl.BlockSpec(row_mask.shape, lambda i, p: (0, 0, 0, 0))],
        out_specs=pl.BlockSpec((1, n, LANES), tok),
        scratch_shapes=[pltpu.VMEM((1, 3, 2, nq, nk), F32)],
        compiler_params=_cparams(("arbitrary", "arbitrary")),
        name="na",
    )(q, k, v, kc, vc, tab, row_mask)


def _gla_prefix_matrices(t):
    i = np.arange(t)
    return np.stack([i[:, None] >= i[None, :], i[:, None] <= i[None, :]]).astype(np.float32)


def _gla_kernel(q_ref, k_ref, v_ref, vt_ref, ad_ref, g_ref, ck_ref, cvt_ref, cad_ref,
                u_ref, ab_ref, gn_ref, cm_ref, o_ref, accf_ref, accb_ref, *, n_tok):
    t = GLA_T
    n_chunks = n_tok // t
    row = lax.broadcasted_iota(jnp.int32, (t, LANES), 0)
    lane = lax.broadcasted_iota(jnp.int32, (t, LANES), 1)
    head0_l = lane < GLA_DK
    row2 = lax.broadcasted_iota(jnp.int32, (t, 2 * t), 0)
    col2 = lax.broadcasted_iota(jnp.int32, (t, 2 * t), 1) & (t - 1)
    vrow = lax.broadcasted_iota(jnp.int32, (2 * t, 2 * GLA_DV), 0)
    vlane = lax.broadcasted_iota(jnp.int32, (2 * t, 2 * GLA_DV), 1)
    v_head_match = (vrow >= t) == (vlane >= GLA_DV)
    srow = lax.broadcasted_iota(jnp.int32, (2 * GLA_DV, LANES), 0)
    slane = lax.broadcasted_iota(jnp.int32, (2 * GLA_DV, LANES), 1)
    s_blockdiag = (srow >= GLA_DV) == (slane >= GLA_DK)
    blk_mask = {half: jnp.where((row2 & ~(2 * half - 1)) == (col2 & ~(2 * half - 1)), 1.0, 0.0)
                for half in GLA_LEVELS if 2 * half < t}
    diag_blk = (row2 & ~(GLA_DIAG - 1)) == (col2 & ~(GLA_DIAG - 1))
    diag_mask = (jnp.where(diag_blk & (row2 >= col2), 1.0, 0.0), jnp.where(diag_blk & (row2 <= col2), 1.0, 0.0))

    def prefix_sums(ad, dirn):
        z = _dot(ad, u_ref[dirn]) + ab_ref[dirn]
        loga = (jnp.minimum(z, 0.0) - jnp.log(1.0 + jnp.exp(-jnp.abs(z)))) * (1.0 / GLA_GATE_TAU)
        hi, lo = _split_bf16(loga)
        p2 = _dot(cm_ref[dirn], jnp.concatenate([hi, lo], axis=-1))
        return p2[:, :LANES] + p2[:, LANES:]

    def chunk_end(p, dirn):
        return p[t - 1:t, :] if dirn == 0 else p[0:1, :]

    def level_sums(p, half, dirn):
        blk = 2 * half
        p3 = p.reshape(t // blk, blk, LANES)
        edge = half - 1 if dirn == 0 else half
        ref = jnp.broadcast_to(p3[:, edge:edge + 1, :], p3.shape).reshape(t, LANES)
        later = (row & half) != 0
        return jnp.where(later == (dirn == 0), p - ref, ref - p)

    def state_update(s, k, vt, p, dirn):
        kh = (k * jnp.exp(chunk_end(p, dirn) - p)).astype(BF16)
        return s * jnp.exp(chunk_end(p, dirn)) + jnp.where(s_blockdiag, _dot(vt, kh), 0.0)

    def heads_stacked(kt):
        return jnp.concatenate([jnp.where(head0_l, kt, 0.0), jnp.where(head0_l, 0.0, kt)], axis=0).astype(BF16)

    def chunk(tok0, s, dirn):
        q = q_ref[0, pl.ds(tok0, t), :].astype(F32)
        k = k_ref[0, pl.ds(tok0, t), :].astype(F32)
        v = v_ref[0, pl.ds(tok0, t), :]
        vt = vt_ref[0, :, pl.ds(tok0, t)]
        p = prefix_sums(ad_ref[0, pl.ds(tok0, t), :], dirn)
        qh = (q * jnp.exp(p)).astype(BF16)
        o = _dot_nt(qh, s.astype(BF16))
        a = None
        for half in GLA_LEVELS:
            w = jnp.exp(level_sums(p, half, dirn))
            later = (row & half) != 0
            q_side = later if dirn == 0 else jnp.logical_not(later)
            qt = jnp.where(q_side, q * w, 0.0).astype(BF16)
            kt = jnp.where(q_side, 0.0, k * w)
            part = _dot_nt(qt, heads_stacked(kt))
            if half in blk_mask:
                part = part * blk_mask[half]
            a = part if a is None else a + part
        e_d = level_sums(p, GLA_DIAG // 2, dirn)
        w, wi = jnp.exp(e_d), jnp.exp(-e_d)
        later = (row & (GLA_DIAG // 2)) != 0
        shrink_q = later if dirn == 0 else jnp.logical_not(later)
        qt = (q * jnp.where(shrink_q, w, wi)).astype(BF16)
        kt = k * jnp.where(shrink_q, wi, w)
        a = a + jnp.where(diag_mask[dirn] > 0.5, _dot_nt(qt, heads_stacked(kt)), 0.0)
        vcat = jnp.concatenate([v, v], axis=0)
        vcat = jnp.where(v_head_match, vcat, jnp.zeros_like(vcat))
        o = o + _dot(a.astype(BF16), vcat)
        return o, state_update(s, k, vt, p, dirn)

    def ctx_state(dirn):
        p = prefix_sums(cad_ref[0], dirn)
        s0 = jnp.zeros((2 * GLA_DV, LANES), F32)
        return state_update(s0, ck_ref[0].astype(F32), cvt_ref[0], p, dirn)

    def body(i, carry):
        s_f, s_b = carry
        tok_f = pl.multiple_of(i * t, t)
        tok_b = pl.multiple_of((n_chunks - 1 - i) * t, t)
        o_f, s_f = chunk(tok_f, s_f, 0)
        o_b, s_b = chunk(tok_b, s_b, 1)
        accf_ref[pl.ds(tok_f, t), :] = o_f
        accb_ref[pl.ds(tok_b, t), :] = o_b
        return s_f, s_b

    lax.fori_loop(0, n_chunks, body, (ctx_state(0), ctx_state(1)))

    def finish(c, carry):
        tok0 = pl.multiple_of(c * t, t)
        o = accf_ref[pl.ds(tok0, t), :] + accb_ref[pl.ds(tok0, t), :]
        g = g_ref[0, pl.ds(tok0, t), :].astype(F32)
        gate = g / (1.0 + jnp.exp(-g))
        halves = [_rms(o[:, h * GLA_DV:(h + 1) * GLA_DV], gn_ref[...]) for h in range(2)]
        o_ref[0, pl.ds(tok0, t), :] = (jnp.concatenate(halves, axis=-1) * gate).astype(o_ref.dtype)
        return carry

    lax.fori_loop(0, n_chunks, finish, 0)


def _gla(q, k, v, vt, ad, g, ck, cvt, cad, u, abias, gnorm, cmats):
    b, n, kw = q.shape
    n_ctx = ck.shape[1]
    pairs = kw // LANES
    vw = 2 * GLA_DV
    tok = lambda i, p: (i, 0, p)
    full3 = lambda i, p: (i, 0, 0)
    return pl.pallas_call(
        functools.partial(_gla_kernel, n_tok=n),
        out_shape=jax.ShapeDtypeStruct((b, n, v.shape[2]), BF16),
        grid=(b, pairs),
        in_specs=[pl.BlockSpec((1, n, LANES), tok),
                  pl.BlockSpec((1, n, LANES), tok),
                  pl.BlockSpec((1, n, vw), tok),
                  pl.BlockSpec((1, vw, n), lambda i, p: (i, p, 0)),
                  pl.BlockSpec((1, n, LANES), full3),
                  pl.BlockSpec((1, n, vw), tok),
                  pl.BlockSpec((1, n_ctx, LANES), tok),
                  pl.BlockSpec((1, vw, n_ctx), lambda i, p: (i, p, 0)),
                  pl.BlockSpec((1, n_ctx, LANES), full3),
                  pl.BlockSpec((2, LANES, LANES), lambda i, p: (0, 0, p)),
                  pl.BlockSpec((2, 1, LANES), lambda i, p: (0, 0, p)),
                  pl.BlockSpec((1, GLA_DV), lambda i, p: (0, 0)),
                  pl.BlockSpec(cmats.shape, lambda i, p: (0, 0, 0))],
        out_specs=pl.BlockSpec((1, n, vw), tok),
        scratch_shapes=[pltpu.VMEM((n, vw), F32), pltpu.VMEM((n, vw), F32)],
        compiler_params=_cparams(("arbitrary", "arbitrary")),
        name="gla",
    )(q, k, v, vt, ad, g, ck, cvt, cad, u, abias, gnorm, cmats)


def _gla_decay_up(a_up):
    r = GLA_GATE_RANK
    pad = jnp.zeros((2, 2 * r, GLA_KEY_W), F32)
    pad = pad.at[0, :r].set(a_up[0]).at[1, r:].set(a_up[1])
    hi, lo = _split_bf16(pad)
    return jnp.concatenate([hi, hi, lo, jnp.zeros_like(hi)], axis=1)


def _outproj_kernel(ona_ref, ogla_ref, x_ref, w1_ref, w2_ref, gm_ref, shf_ref, scf_ref, npost_ref,
                    nfpre_ref, rt_ref, xnew_ref, hf_ref, aff_ref, afft_ref):
    mix = _dot(ona_ref[0], w1_ref[...]) + _dot(ogla_ref[0], w2_ref[...])
    xn = x_ref[0] + gm_ref[0] * _rms(mix, npost_ref[...])
    xnew_ref[0] = xn
    h = _rms(xn, nfpre_ref[...]) * (1.0 + scf_ref[0]) + shf_ref[0]
    h_hi, h_lo = _split_bf16(h)
    hf_ref[0] = h_hi
    res = _dot(h_hi, rt_ref[...])
    logits = res[:, :LANES] + res[:, LANES:] + _dot(h_lo, rt_ref[:, :LANES])
    lane = lax.broadcasted_iota(jnp.int32, logits.shape, 1)
    logits = jnp.where(lane < N_EXPERTS, logits, NEG_BIG)
    p = jnp.exp(logits - jnp.max(logits, axis=-1, keepdims=True))
    aff = p / jnp.sum(p, axis=-1, keepdims=True)
    aff_ref[0] = aff
    afft_ref[0] = aff.T[:N_EXPERTS, :]


def _outproj(o_na, o_gla, x, w1, w2, gm, shf, scf, npost, nfpre, router_cat):
    b, n, d = x.shape
    tm = TOK_TILE
    tokmap = lambda i, j: (i, j, 0)
    smp = lambda i, j: (i, 0, 0)
    cst = lambda i, j: (0, 0)
    return pl.pallas_call(
        _outproj_kernel,
        out_shape=[jax.ShapeDtypeStruct((b, n, d), F32),
                   jax.ShapeDtypeStruct((b, n, d), BF16),
                   jax.ShapeDtypeStruct((b, n, LANES), F32),
                   jax.ShapeDtypeStruct((b, N_EXPERTS, n), F32)],
        grid=(b, n // tm),
        in_specs=[pl.BlockSpec((1, tm, o_na.shape[2]), tokmap),
                  pl.BlockSpec((1, tm, o_gla.shape[2]), tokmap),
                  pl.BlockSpec((1, tm, d), tokmap),
                  pl.BlockSpec(w1.shape, cst),
                  pl.BlockSpec(w2.shape, cst),
                  pl.BlockSpec((1, 1, d), smp),
                  pl.BlockSpec((1, 1, d), smp),
                  pl.BlockSpec((1, 1, d), smp),
                  pl.BlockSpec((1, d), cst),
                  pl.BlockSpec((1, d), cst),
                  pl.BlockSpec(router_cat.shape, cst)],
        out_specs=[pl.BlockSpec((1, tm, d), tokmap),
                   pl.BlockSpec((1, tm, d), tokmap),
                   pl.BlockSpec((1, tm, LANES), tokmap),
                   pl.BlockSpec((1, N_EXPERTS, tm), lambda i, j: (i, 0, j))],
        compiler_params=_cparams(("arbitrary", "arbitrary")),
        name="outproj",
    )(o_na, o_gla, x, w1, w2, gm, shf, scf, npost.reshape(1, d), nfpre.reshape(1, d), router_cat)


def _route_kernel(afft_ref, rt_ref, rc_ref, *, cap):
    a = afft_ref[0]
    e, n = a.shape
    capf = jnp.float32(cap)

    def search(i, thr_bits):
        cand = thr_bits | lax.shift_left(jnp.int32(1), 30 - i)
        cnt = jnp.sum(jnp.where(a >= lax.bitcast_convert_type(cand, F32), 1.0, 0.0), axis=-1, keepdims=True)
        return jnp.where(cnt >= capf, cand, thr_bits)

    thr_bits = lax.fori_loop(0, 31, search, jnp.zeros((e, 1), jnp.int32))
    thr = lax.bitcast_convert_type(thr_bits, F32)
    need = capf - jnp.sum(jnp.where(a > thr, 1.0, 0.0), axis=-1, keepdims=True)
    r_i = lax.broadcasted_iota(jnp.int32, (LANES, LANES), 0)
    c_i = lax.broadcasted_iota(jnp.int32, (LANES, LANES), 1)
    incl = jnp.where(r_i <= c_i, 1.0, 0.0).astype(BF16)
    off_eq = jnp.zeros((e, 1), F32)
    off_sel = jnp.zeros((e, 1), F32)
    pad = jnp.full((LANES - e, LANES), -1.0, F32)
    for j in range(n // LANES):
        sl = slice(j * LANES, (j + 1) * LANES)
        a_b = a[:, sl]
        eq_b = jnp.where(a_b == thr, 1.0, 0.0)
        tie_rank = _dot(eq_b.astype(BF16), incl) - eq_b + off_eq
        off_eq = off_eq + jnp.sum(eq_b, axis=-1, keepdims=True)
        sel_b = jnp.where(a_b > thr, 1.0, jnp.where(tie_rank < need, eq_b, 0.0))
        sel = sel_b > 0.5
        rank = _dot(sel_b.astype(BF16), incl) - sel_b + off_sel
        off_sel = off_sel + jnp.sum(sel_b, axis=-1, keepdims=True)
        rsel = jnp.where(sel, rank, -1.0)
        rt_ref[0, :, sl] = rsel.astype(jnp.int32)
        rc_ref[0, sl, :] = jnp.concatenate([rsel, pad], axis=0).T.astype(jnp.int32)


def _route(afft, cap):
    b, e, n = afft.shape
    return pl.pallas_call(
        functools.partial(_route_kernel, cap=cap),
        out_shape=[jax.ShapeDtypeStruct((b, e, n), jnp.int32),
                   jax.ShapeDtypeStruct((b, n, LANES), jnp.int32)],
        grid=(b,),
        in_specs=[pl.BlockSpec((1, e, n), lambda i: (i, 0, 0))],
        out_specs=[pl.BlockSpec((1, e, n), lambda i: (i, 0, 0)),
                   pl.BlockSpec((1, n, LANES), lambda i: (i, 0, 0))],
        compiler_params=_cparams(("arbitrary",)),
        name="route",
    )(afft)


def _gather_kernel(r_ref, hf_ref, o_ref, *, cap):
    r = r_ref[0, 0]
    n = r.shape[1]
    slot = lax.broadcasted_iota(jnp.int32, (SLOT_TILE, n), 0)
    for s0 in range(0, cap, SLOT_TILE):
        onehot = jnp.where(r == slot + s0, 1.0, 0.0).astype(BF16)
        o_ref[0, 0, s0:s0 + SLOT_TILE, :] = _dot(onehot, hf_ref[0]).astype(o_ref.dtype)


def _gather(rsel_t, hf, cap):
    b, e, n = rsel_t.shape
    d = hf.shape[2]
    return pl.pallas_call(
        functools.partial(_gather_kernel, cap=cap),
        out_shape=jax.ShapeDtypeStruct((b, e, cap, d), BF16),
        grid=(b, e),
        in_specs=[pl.BlockSpec((1, 1, 1, n), lambda i, j: (i, j, 0, 0)),
                  pl.BlockSpec((1, n, d), lambda i, j: (i, 0, 0))],
        out_specs=pl.BlockSpec((1, 1, cap, d), lambda i, j: (i, j, 0, 0)),
        compiler_params=_cparams(("arbitrary", "arbitrary")),
        name="gather",
    )(rsel_t.reshape(b, e, 1, n), hf)


def _ffn_kernel(x_ref, wg_ref, wu_ref, wd_ref, o_ref, acc_ref):
    f = pl.program_id(1)
    b = x_ref.shape[0]
    wg = wg_ref[0].astype(BF16)
    wu = wu_ref[0].astype(BF16)
    wd = wd_ref[0].astype(BF16)

    @pl.when(f == 0)
    def _():
        acc_ref[...] = jnp.zeros_like(acc_ref)

    for i in range(b):
        x = x_ref[i, 0]
        g = _dot(x, wg)
        u = _dot(x, wu)
        hid = (g / (1.0 + jnp.exp(-g)) * u).astype(BF16)
        acc_ref[i] += _dot(hid, wd)

    @pl.when(f == pl.num_programs(1) - 1)
    def _():
        for i in range(b):
            o_ref[i, 0] = acc_ref[i].astype(o_ref.dtype)


def _ffn(xs, w_gate, w_up, w_down):
    b, e, cap, d = xs.shape
    dff = w_gate.shape[2]
    tf = FFN_TILE
    return pl.pallas_call(
        _ffn_kernel,
        out_shape=jax.ShapeDtypeStruct((b, e, cap, d), BF16),
        grid=(e, dff // tf),
        in_specs=[pl.BlockSpec((b, 1, cap, d), lambda i, f: (0, i, 0, 0)),
                  pl.BlockSpec((1, d, tf), lambda i, f: (i, 0, f)),
                  pl.BlockSpec((1, d, tf), lambda i, f: (i, 0, f)),
                  pl.BlockSpec((1, tf, d), lambda i, f: (i, f, 0))],
        out_specs=pl.BlockSpec((b, 1, cap, d), lambda i, f: (0, i, 0, 0)),
        scratch_shapes=[pltpu.VMEM((b, cap, d), F32)],
        compiler_params=_cparams(("arbitrary", "arbitrary")),
        name="ffn",
    )(xs, w_gate, w_up, w_down)


def _combine_kernel(ys_ref, rc_ref, aff_ref, xn_ref, gf_ref, npost_ref, o_ref):
    tm = rc_ref.shape[1]
    cap = ys_ref.shape[2]
    slot = lax.broadcasted_iota(jnp.int32, (tm, cap), 1)
    acc = jnp.zeros(o_ref.shape[1:], F32)
    for e in range(ys_ref.shape[1]):
        onehot = jnp.where(rc_ref[0, :, e:e + 1] == slot, 1.0, 0.0).astype(BF16)
        acc = acc + aff_ref[0, :, e:e + 1] * _dot(onehot, ys_ref[0, e])
    o_ref[0] = xn_ref[0] + gf_ref[0] * _rms(acc, npost_ref[...])


def _combine(ys, rsel_c, aff, x_new, gf, npost):
    b, e, cap, d = ys.shape
    n = x_new.shape[1]
    tm = TOK_TILE
    tokmap = lambda i, j: (i, j, 0)
    return pl.pallas_call(
        _combine_kernel,
        out_shape=jax.ShapeDtypeStruct((b, n, d), F32),
        grid=(b, n // tm),
        in_specs=[pl.BlockSpec((1, e, cap, d), lambda i, j: (i, 0, 0, 0)),
                  pl.BlockSpec((1, tm, LANES), tokmap),
                  pl.BlockSpec((1, tm, LANES), tokmap),
                  pl.BlockSpec((1, tm, d), tokmap),
                  pl.BlockSpec((1, 1, d), lambda i, j: (i, 0, 0)),
                  pl.BlockSpec((1, d), lambda i, j: (0, 0))],
        out_specs=pl.BlockSpec((1, tm, d), tokmap),
        compiler_params=_cparams(("arbitrary", "arbitrary")),
        name="combine",
    )(ys, rsel_c, aff, x_new, gf, npost.reshape(1, d))


def kernel(x, c, ctx, c_ctx, w_mod, b_mod, norm_mix_pre, norm_mix_post, norm_ffn_pre, norm_ffn_post,
           w_in, na_rpb, gla_a_up, gla_a_bias, gla_norm, w_out, router, w_gate, w_up, w_down):
    b, n, d = x.shape
    assert w_mod.shape[0] == 1 and d == D_MODEL and n % (GRID_W * NA_QROWS) == 0 and n % GLA_T == 0
    assert ctx.shape[1] == GLA_T
    rows = n // GRID_W
    cap = EC_CAPACITY_FACTOR * n // N_EXPERTS

    c8 = jnp.concatenate([c, c_ctx[None, :], jnp.zeros((8 - b - 1, d), F32)], axis=0)
    mod = _mod(c8, w_mod[0], b_mod[0])
    sh_m, sc_m, g_m, sh_f, sc_f, g_f = [m[:b, None, :] for m in jnp.split(mod, 6, axis=-1)]
    sh_c, sc_c = mod[b:b + 1, None, :d], mod[b:b + 1, None, d:2 * d]

    wb = w_in[0].astype(BF16)
    cuts = np.cumsum([0, NA_W, NA_W, GLA_KEY_W, GLA_VAL_W, 2 * GLA_GATE_RANK, NA_W, GLA_KEY_W, GLA_VAL_W])
    w_nak, w_nav, w_gk, w_gv, w_ad, w_naq, w_gq, w_gg = [wb[:, cuts[i]:cuts[i + 1]] for i in range(8)]
    w_gvt = w_gv.T
    w_ad3 = jnp.concatenate([w_ad, w_ad, w_ad, jnp.zeros_like(w_ad)], axis=1)
    rope = _rope_tables(n)
    na_q, na_k, na_v, gq, gk, gv, gvt, ad, gg = _inproj(
        x, sh_m, sc_m, norm_mix_pre[0],
        [w_naq, w_nak, w_nav, w_gq, w_gk, w_gv, w_gvt, w_ad3, w_gg],
        ["plain", "plain", "plain", "rope", "rope", "plain", "t", "split", "plain"],
        [1.0, 1.0, 1.0, GLA_DK ** -0.5, 1.0, 1.0, 1.0, 1.0, 1.0],
        [BF16] * 9, rope=rope)
    c_nak, c_nav, c_gk, c_gvt, c_ad = _inproj(
        ctx, sh_c, sc_c, norm_mix_pre[0],
        [w_nak, w_nav, w_gk, w_gvt, w_ad3],
        ["plain", "plain", "plain", "t", "split"], [1.0] * 5, [BF16] * 5)

    o_na = _na(na_q, na_k, na_v, c_nak, c_nav, *_na_bias_tables(na_rpb[0], rows))

    cmats = jnp.asarray(_gla_prefix_matrices(GLA_T), BF16)
    o_gla = _gla(gq, gk, gv, gvt, ad, gg, c_gk, c_gvt, c_ad, _gla_decay_up(gla_a_up[0]),
                 gla_a_bias[0][:, None, :], gla_norm[0][None, :], cmats)

    wo = w_out[0].astype(BF16)
    router_pad = jnp.zeros((d, LANES), F32).at[:, :N_EXPERTS].set(router[0])
    x_new, hf, aff, aff_t = _outproj(o_na, o_gla, x, wo[:NA_W], wo[NA_W:], g_m, sh_f, sc_f,
                                     norm_mix_post[0], norm_ffn_pre[0],
                                     jnp.concatenate(_split_bf16(router_pad), axis=1))

    rsel_t, rsel_c = _route(aff_t, cap)
    xs = _gather(rsel_t, hf, cap)
    ys = _ffn(xs, w_gate[0], w_up[0], w_down[0])
    return _combine(ys, rsel_c, aff, x_new, g_f, norm_ffn_post[0])
```

```python
import functools

import numpy as np
import jax
import jax.numpy as jnp
from jax import lax
from jax.experimental import pallas as pl
from jax.experimental.pallas import tpu as pltpu

F32 = jnp.float32
BF16 = jnp.bfloat16
HIGHEST = lax.Precision.HIGHEST

D_MODEL = 1024
GRID_W = 64
NA_W = 512
NA_HEADS = 8
NA_HEAD_DIM = 64
NA_WIN_ROWS = 8
NA_WIN_COLS = 16
GLA_HEADS = 4
GLA_DV = 128
GLA_DK = 64
GLA_KEY_W = 256
GLA_VAL_W = 512
GLA_GATE_RANK = 16
GLA_GATE_TAU = 16.0
ROPE_BASE = 10000.0
N_EXPERTS = 16
EC_CAPACITY_FACTOR = 2
NORM_EPS = 1e-6
NEG_BIG = -1e30

LANES = 128
VMEM_LIMIT = 56 * 1024 * 1024

TOK_TILE = 512
NA_QROWS = 4
NA_KROWS = NA_QROWS + NA_WIN_ROWS
NA_TAB_PAD = NA_QROWS
NA_TAB_BLOCKS = NA_TAB_PAD + 2 * NA_WIN_ROWS - 1 + NA_QROWS + 1
GLA_T = 256
GLA_LEVELS = (128, 64, 32, 16)
GLA_DIAG = 16
FFN_TILE = 256
SLOT_TILE = 128


def _cparams(sem):
    return pltpu.CompilerParams(dimension_semantics=sem, vmem_limit_bytes=VMEM_LIMIT)


def _rms(v, g):
    return v * lax.rsqrt(jnp.mean(v * v, axis=-1, keepdims=True) + NORM_EPS) * g


def _dot(a, b):
    return jnp.dot(a, b, preferred_element_type=F32)


def _dot_nt(a, b):
    return lax.dot_general(a, b, (((1,), (1,)), ((), ())), preferred_element_type=F32)


def _split_bf16(v):
    hi = v.astype(BF16)
    return hi, (v - hi.astype(F32)).astype(BF16)


def _mod_kernel(c_ref, w_ref, b_ref, o_ref):
    c = c_ref[...]
    s = c / (1.0 + jnp.exp(-c))
    o_ref[...] = jnp.dot(s, w_ref[...], preferred_element_type=F32, precision=HIGHEST) + b_ref[...]


def _mod(c8, w_mod, b_mod):
    d, n = w_mod.shape
    tn = 1536
    return pl.pallas_call(
        _mod_kernel,
        out_shape=jax.ShapeDtypeStruct((8, n), F32),
        grid=(n // tn,),
        in_specs=[pl.BlockSpec((8, d), lambda j: (0, 0)),
                  pl.BlockSpec((d, tn), lambda j: (0, j)),
                  pl.BlockSpec((1, tn), lambda j: (0, j))],
        out_specs=pl.BlockSpec((8, tn), lambda j: (0, j)),
        compiler_params=_cparams(("arbitrary",)),
        name="mod",
    )(c8, w_mod, b_mod.reshape(1, n))


def _inproj_kernel(x_ref, sh_ref, sc_ref, g_ref, *refs, modes, scales):
    n_rope = 2 if "rope" in modes else 0
    rope_refs, refs = refs[:n_rope], refs[n_rope:]
    n_w = len(modes)
    w_refs, o_refs = refs[:n_w], refs[n_w:]
    x = x_ref[0]
    h = _rms(x, g_ref[...]) * (1.0 + sc_ref[0]) + sh_ref[0]
    hb = h.astype(BF16)
    for w_ref, o_ref, mode, scale in zip(w_refs, o_refs, modes, scales):
        if mode == "t":
            o_ref[0] = _dot_nt(w_ref[...], hb).astype(o_ref.dtype)
            continue
        y = _dot(hb, w_ref[...])
        if mode == "rope":
            cols = y.shape[1]
            quarter = GLA_DK // 4
            lane = lax.broadcasted_iota(jnp.int32, y.shape, 1)
            partner = jnp.where((lane & (2 * quarter - 1)) < quarter,
                                pltpu.roll(y, cols - quarter, axis=1), pltpu.roll(y, quarter, axis=1))
            y = (y * rope_refs[0][...] + partner * rope_refs[1][...]) * scale
        elif mode == "split":
            hi, lo = _split_bf16(y)
            lane = lax.broadcasted_iota(jnp.int32, y.shape, 1)
            rank2 = 2 * GLA_GATE_RANK
            y = jnp.where((lane >= rank2) & (lane < 2 * rank2), lo, hi)
        o_ref[0] = y.astype(o_ref.dtype)


def _inproj(x, shift, scale, g, weights, modes, scales, out_dtypes, rope=None):
    b, n, d = x.shape
    tm = min(TOK_TILE, n)
    per_sample = shift.shape[0] == b
    mod_map = (lambda i, j: (i, 0, 0)) if per_sample else (lambda i, j: (0, 0, 0))
    in_specs = [pl.BlockSpec((1, tm, d), lambda i, j: (i, j, 0)),
                pl.BlockSpec((1, 1, d), mod_map),
                pl.BlockSpec((1, 1, d), mod_map),
                pl.BlockSpec((1, d), lambda i, j: (0, 0))]
    args = [x, shift, scale, g.reshape(1, d)]
    if rope is not None:
        in_specs += [pl.BlockSpec((tm, rope[0].shape[1]), lambda i, j: (j, 0))] * 2
        args += list(rope)
    out_shapes, out_specs = [], []
    for w, mode, dt in zip(weights, modes, out_dtypes):
        in_specs.append(pl.BlockSpec(w.shape, lambda i, j: (0, 0)))
        if mode == "t":
            cols = w.shape[0]
            out_shapes.append(jax.ShapeDtypeStruct((b, cols, n), dt))
            out_specs.append(pl.BlockSpec((1, cols, tm), lambda i, j: (i, 0, j)))
        else:
            cols = w.shape[1]
            out_shapes.append(jax.ShapeDtypeStruct((b, n, cols), dt))
            out_specs.append(pl.BlockSpec((1, tm, cols), lambda i, j: (i, j, 0)))
    return pl.pallas_call(
        functools.partial(_inproj_kernel, modes=tuple(modes), scales=tuple(scales)),
        out_shape=out_shapes,
        grid=(b, n // tm),
        in_specs=in_specs,
        out_specs=out_specs,
        compiler_params=_cparams(("arbitrary", "arbitrary")),
        name="inproj",
    )(*args, *weights)


def _rope_tables(n):
    t = jnp.arange(n)
    pos_row, pos_col = (t // GRID_W).astype(F32), (t % GRID_W).astype(F32)
    quarter = GLA_DK // 4
    freqs = ROPE_BASE ** (-jnp.arange(quarter, dtype=F32) / quarter)
    ang_r = pos_row[:, None] * freqs
    ang_c = pos_col[:, None] * freqs
    cos = jnp.concatenate([jnp.cos(ang_r), jnp.cos(ang_r), jnp.cos(ang_c), jnp.cos(ang_c)], axis=-1)
    sin = jnp.concatenate([-jnp.sin(ang_r), jnp.sin(ang_r), -jnp.sin(ang_c), jnp.sin(ang_c)], axis=-1)
    return jnp.tile(cos, (1, GLA_HEADS)), jnp.tile(sin, (1, GLA_HEADS))


def _na_patterns(rows):
    kr = min(NA_WIN_ROWS, rows)
    n_blocks = rows // NA_QROWS
    pats = []
    for blk in (0, 1, n_blocks - 1):
        r0 = blk * NA_QROWS
        k0 = int(np.clip(r0 - kr // 2, 0, rows - NA_KROWS))
        strips = []
        for a in range(NA_QROWS):
            r_start = int(np.clip(r0 + a - kr // 2, 0, rows - kr))
            start = k0 - (r0 + a) + NA_WIN_ROWS - 1 + NA_TAB_PAD
            assert 0 <= start and start + NA_KROWS <= NA_TAB_BLOCKS
            strips.append((start, [r_start <= k0 + c < r_start + kr for c in range(NA_KROWS)]))
        pats.append(strips)
    return pats


def _na_bias_tables(rpb, rows):
    heads = rpb.shape[0]
    col = np.arange(GRID_W)
    c_start = np.clip(col - NA_WIN_COLS // 2, 0, GRID_W - NA_WIN_COLS)
    col_ok = (col[None, :] >= c_start[:, None]) & (col[None, :] < c_start[:, None] + NA_WIN_COLS)
    dc = np.clip(col[None, :] - col[:, None] + NA_WIN_COLS - 1, 0, 2 * NA_WIN_COLS - 2)
    sel_c = (np.arange(2 * NA_WIN_COLS - 1)[:, None, None] == dc[None]) & col_ok[None]
    t = jnp.einsum("hrd,dqk->hqrk", rpb, jnp.asarray(sel_c, F32), precision=HIGHEST)
    t = jnp.where(jnp.asarray(col_ok)[None, :, None, :], t, NEG_BIG)
    n_dr = 2 * NA_WIN_ROWS - 1
    t = t.reshape(heads, GRID_W, n_dr * GRID_W)
    back = NA_TAB_BLOCKS + 1 - NA_TAB_PAD - n_dr
    t = jnp.pad(t, ((0, 0), (0, 0), (NA_TAB_PAD * GRID_W, back * GRID_W)), constant_values=NEG_BIG)
    width = NA_TAB_BLOCKS * GRID_W
    tab = jnp.stack([t[:, :, :width], t[:, :, GRID_W:GRID_W + width]], axis=1)
    row_mask = np.zeros((3, NA_QROWS, 1, NA_KROWS * GRID_W), np.float32)
    for pat, strips in enumerate(_na_patterns(rows)):
        for a, (_, valid) in enumerate(strips):
            row_mask[pat, a, 0] = np.repeat(np.where(valid, 0.0, NEG_BIG), GRID_W)
    return tab, jnp.asarray(row_mask)


def _na_kernel(q_ref, k_ref, v_ref, kc_ref, vc_ref, tab_ref, rmask_ref, o_ref, bias_ref,
               sw_ref, sc_ref, pw_ref, pc_ref, linv_ref, *, rows):
    nq, nk = NA_QROWS * GRID_W, NA_KROWS * GRID_W
    n_blocks = rows // NA_QROWS
    kr = min(NA_WIN_ROWS, rows)
    scale = NA_HEAD_DIM ** -0.5
    lane = lax.broadcasted_iota(jnp.int32, (nq, LANES), 1)
    first_head = lane < NA_HEAD_DIM
    kc = kc_ref[0]
    vc = vc_ref[0]
    for pat, strips in enumerate(_na_patterns(rows)):
        for a, (start, _) in enumerate(strips):
            parity = start % 2
            off = (start - parity) * GRID_W
            for h in range(2):
                bias_ref[0, pat, h, a * GRID_W:(a + 1) * GRID_W, :] = (
                    tab_ref[h, parity, :, off:off + nk] + rmask_ref[pat, a])

    def key_start(i):
        return pl.multiple_of(jnp.clip(i * NA_QROWS - kr // 2, 0, rows - NA_KROWS) * GRID_W, GRID_W)

    def scores(i, slot):
        pat = jnp.where(i == 0, 0, jnp.where(i == n_blocks - 1, 2, 1))
        q = q_ref[0, pl.ds(pl.multiple_of(i * nq, nq), nq), :] * scale
        kw = k_ref[0, pl.ds(key_start(i), nk), :]
        for h in range(2):
            qh = jnp.where(first_head if h == 0 else jnp.logical_not(first_head), q, jnp.zeros_like(q))
            sw_ref[slot, h] = _dot_nt(qh, kw) + bias_ref[0, pat, h]
            sc_ref[slot, h] = _dot_nt(qh, kc)

    def softmax(slot):
        for h in range(2):
            s_w = sw_ref[slot, h]
            s_c = sc_ref[slot, h]
            m = jnp.maximum(jnp.max(s_w, axis=-1, keepdims=True), jnp.max(s_c, axis=-1, keepdims=True))
            p_w = jnp.exp(s_w - m)
            p_c = jnp.exp(s_c - m)
            l = jnp.sum(p_w, axis=-1, keepdims=True) + jnp.sum(p_c, axis=-1, keepdims=True)
            pw_ref[slot, h] = p_w.astype(BF16)
            pc_ref[slot, h] = p_c.astype(BF16)
            linv_ref[slot, h] = jnp.broadcast_to(1.0 / l, (nq, LANES))

    def values(i, slot):
        vw = v_ref[0, pl.ds(key_start(i), nk), :]
        outs = [(_dot(pw_ref[slot, h], vw) + _dot(pc_ref[slot, h], vc)) * linv_ref[slot, h] for h in range(2)]
        o = jnp.where(first_head, outs[0], outs[1])
        o_ref[0, pl.ds(pl.multiple_of(i * nq, nq), nq), :] = o.astype(o_ref.dtype)

    assert n_blocks % 2 == 0 and n_blocks >= 4
    scores(0, 0)
    softmax(0)
    scores(1, 1)

    def trip(j, carry):
        i = 2 * j
        values(i - 2, 0)
        softmax(1)
        scores(i, 0)
        values(i - 1, 1)
        softmax(0)
        scores(i + 1, 1)
        return carry

    lax.fori_loop(1, n_blocks // 2, trip, 0)
    values(n_blocks - 2, 0)
    softmax(1)
    values(n_blocks - 1, 1)


def _na(q, k, v, kc, vc, tab, row_mask):
    b, n, w = q.shape
    n_ctx = kc.shape[1]
    pairs = w // LANES
    rows = n // GRID_W
    nq, nk = NA_QROWS * GRID_W, NA_KROWS * GRID_W
    tok = lambda i, p: (i, 0, p)
    return pl.pallas_call(
        functools.partial(_na_kernel, rows=rows),
        out_shape=jax.ShapeDtypeStruct((b, n, w), BF16),
        grid=(b, pairs),
        in_specs=[pl.BlockSpec((1, n, LANES), tok),
                  pl.BlockSpec((1, n, LANES), tok),
                  pl.BlockSpec((1, n, LANES), tok),
                  pl.BlockSpec((1, n_ctx, LANES), tok),
                  pl.BlockSpec((1, n_ctx, LANES), tok),
                  pl.BlockSpec((2,) + tab.shape[1:], lambda i, p: (p, 0, 0, 0)),
                  pl.BlockSpec(row_mask.shape, lambda i, p: (0, 0, 0, 0))],
        out_specs=pl.BlockSpec((1, n, LANES), tok),
        scratch_shapes=[pltpu.VMEM((1, 3, 2, nq, nk), F32),
                        pltpu.VMEM((2, 2, nq, nk), F32), pltpu.VMEM((2, 2, nq, n_ctx), F32),
                        pltpu.VMEM((2, 2, nq, nk), BF16), pltpu.VMEM((2, 2, nq, n_ctx), BF16),
                        pltpu.VMEM((2, 2, nq, LANES), F32)],
        compiler_params=_cparams(("arbitrary", "arbitrary")),
        name="na",
    )(q, k, v, kc, vc, tab, row_mask)


def _gla_prefix_matrices(t):
    i = np.arange(t)
    return np.stack([i[:, None] >= i[None, :], i[:, None] <= i[None, :]]).astype(np.float32)


def _gla_kernel(q_ref, k_ref, v_ref, vt_ref, ad_ref, g_ref, ck_ref, cvt_ref, cad_ref,
                u_ref, ab_ref, gn_ref, cm_ref, o_ref, accf_ref, accb_ref, *, n_tok):
    t = GLA_T
    n_chunks = n_tok // t
    row = lax.broadcasted_iota(jnp.int32, (t, LANES), 0)
    lane = lax.broadcasted_iota(jnp.int32, (t, LANES), 1)
    head0_l = lane < GLA_DK
    row2 = lax.broadcasted_iota(jnp.int32, (t, 2 * t), 0)
    col2 = lax.broadcasted_iota(jnp.int32, (t, 2 * t), 1) & (t - 1)
    vrow = lax.broadcasted_iota(jnp.int32, (2 * t, 2 * GLA_DV), 0)
    vlane = lax.broadcasted_iota(jnp.int32, (2 * t, 2 * GLA_DV), 1)
    v_head_match = (vrow >= t) == (vlane >= GLA_DV)
    srow = lax.broadcasted_iota(jnp.int32, (2 * GLA_DV, LANES), 0)
    slane = lax.broadcasted_iota(jnp.int32, (2 * GLA_DV, LANES), 1)
    s_blockdiag = (srow >= GLA_DV) == (slane >= GLA_DK)
    blk_mask = {half: jnp.where((row2 & ~(2 * half - 1)) == (col2 & ~(2 * half - 1)), 1.0, 0.0)
                for half in GLA_LEVELS if 2 * half < t}
    diag_blk = (row2 & ~(GLA_DIAG - 1)) == (col2 & ~(GLA_DIAG - 1))
    diag_mask = (jnp.where(diag_blk & (row2 >= col2), 1.0, 0.0), jnp.where(diag_blk & (row2 <= col2), 1.0, 0.0))

    def prefix_sums(ad, dirn):
        z = _dot(ad, u_ref[dirn]) + ab_ref[dirn]
        loga = (jnp.minimum(z, 0.0) - jnp.log(1.0 + jnp.exp(-jnp.abs(z)))) * (1.0 / GLA_GATE_TAU)
        hi, lo = _split_bf16(loga)
        p2 = _dot(cm_ref[dirn], jnp.concatenate([hi, lo], axis=-1))
        return p2[:, :LANES] + p2[:, LANES:]

    def chunk_end(p, dirn):
        return p[t - 1:t, :] if dirn == 0 else p[0:1, :]

    def level_sums(p, half, dirn):
        blk = 2 * half
        p3 = p.reshape(t // blk, blk, LANES)
        edge = half - 1 if dirn == 0 else half
        ref = jnp.broadcast_to(p3[:, edge:edge + 1, :], p3.shape).reshape(t, LANES)
        later = (row & half) != 0
        return jnp.where(later == (dirn == 0), p - ref, ref - p)

    def state_update(s, k, vt, p, dirn):
        kh = (k * jnp.exp(chunk_end(p, dirn) - p)).astype(BF16)
        return s * jnp.exp(chunk_end(p, dirn)) + jnp.where(s_blockdiag, _dot(vt, kh), 0.0)

    def heads_stacked(kt):
        return jnp.concatenate([jnp.where(head0_l, kt, 0.0), jnp.where(head0_l, 0.0, kt)], axis=0).astype(BF16)

    def chunk(tok0, s, dirn):
        q = q_ref[0, pl.ds(tok0, t), :].astype(F32)
        k = k_ref[0, pl.ds(tok0, t), :].astype(F32)
        v = v_ref[0, pl.ds(tok0, t), :]
        vt = vt_ref[0, :, pl.ds(tok0, t)]
        p = prefix_sums(ad_ref[0, pl.ds(tok0, t), :], dirn)
        qh = (q * jnp.exp(p)).astype(BF16)
        o = _dot_nt(qh, s.astype(BF16))
        a = None
        for half in GLA_LEVELS:
            w = jnp.exp(level_sums(p, half, dirn))
            later = (row & half) != 0
            q_side = later if dirn == 0 else jnp.logical_not(later)
            qt = jnp.where(q_side, q * w, 0.0).astype(BF16)
            kt = jnp.where(q_side, 0.0, k * w)
            part = _dot_nt(qt, heads_stacked(kt))
            if half in blk_mask:
                part = part * blk_mask[half]
            a = part if a is None else a + part
        e_d = level_sums(p, GLA_DIAG // 2, dirn)
        w, wi = jnp.exp(e_d), jnp.exp(-e_d)
        later = (row & (GLA_DIAG // 2)) != 0
        shrink_q = later if dirn == 0 else jnp.logical_not(later)
        qt = (q * jnp.where(shrink_q, w, wi)).astype(BF16)
        kt = k * jnp.where(shrink_q, wi, w)
        a = a + jnp.where(diag_mask[dirn] > 0.5, _dot_nt(qt, heads_stacked(kt)), 0.0)
        vcat = jnp.concatenate([v, v], axis=0)
        vcat = jnp.where(v_head_match, vcat, jnp.zeros_like(vcat))
        o = o + _dot(a.astype(BF16), vcat)
        return o, state_update(s, k, vt, p, dirn)

    def ctx_state(dirn):
        p = prefix_sums(cad_ref[0], dirn)
        s0 = jnp.zeros((2 * GLA_DV, LANES), F32)
        return state_update(s0, ck_ref[0].astype(F32), cvt_ref[0], p, dirn)

    def body(i, carry):
        s_f, s_b = carry
        tok_f = pl.multiple_of(i * t, t)
        tok_b = pl.multiple_of((n_chunks - 1 - i) * t, t)
        o_f, s_f = chunk(tok_f, s_f, 0)
        o_b, s_b = chunk(tok_b, s_b, 1)
        accf_ref[pl.ds(tok_f, t), :] = o_f
        accb_ref[pl.ds(tok_b, t), :] = o_b
        return s_f, s_b

    lax.fori_loop(0, n_chunks, body, (ctx_state(0), ctx_state(1)))

    def finish(c, carry):
        tok0 = pl.multiple_of(c * t, t)
        o = accf_ref[pl.ds(tok0, t), :] + accb_ref[pl.ds(tok0, t), :]
        g = g_ref[0, pl.ds(tok0, t), :].astype(F32)
        gate = g / (1.0 + jnp.exp(-g))
        halves = [_rms(o[:, h * GLA_DV:(h + 1) * GLA_DV], gn_ref[...]) for h in range(2)]
        o_ref[0, pl.ds(tok0, t), :] = (jnp.concatenate(halves, axis=-1) * gate).astype(o_ref.dtype)
        return carry

    lax.fori_loop(0, n_chunks, finish, 0)


def _gla(q, k, v, vt, ad, g, ck, cvt, cad, u, abias, gnorm, cmats):
    b, n, kw = q.shape
    n_ctx = ck.shape[1]
    pairs = kw // LANES
    vw = 2 * GLA_DV
    tok = lambda i, p: (i, 0, p)
    full3 = lambda i, p: (i, 0, 0)
    return pl.pallas_call(
        functools.partial(_gla_kernel, n_tok=n),
        out_shape=jax.ShapeDtypeStruct((b, n, v.shape[2]), BF16),
        grid=(b, pairs),
        in_specs=[pl.BlockSpec((1, n, LANES), tok),
                  pl.BlockSpec((1, n, LANES), tok),
                  pl.BlockSpec((1, n, vw), tok),
                  pl.BlockSpec((1, vw, n), lambda i, p: (i, p, 0)),
                  pl.BlockSpec((1, n, LANES), full3),
                  pl.BlockSpec((1, n, vw), tok),
                  pl.BlockSpec((1, n_ctx, LANES), tok),
                  pl.BlockSpec((1, vw, n_ctx), lambda i, p: (i, p, 0)),
                  pl.BlockSpec((1, n_ctx, LANES), full3),
                  pl.BlockSpec((2, LANES, LANES), lambda i, p: (0, 0, p)),
                  pl.BlockSpec((2, 1, LANES), lambda i, p: (0, 0, p)),
                  pl.BlockSpec((1, GLA_DV), lambda i, p: (0, 0)),
                  pl.BlockSpec(cmats.shape, lambda i, p: (0, 0, 0))],
        out_specs=pl.BlockSpec((1, n, vw), tok),
        scratch_shapes=[pltpu.VMEM((n, vw), F32), pltpu.VMEM((n, vw), F32)],
        compiler_params=_cparams(("arbitrary", "arbitrary")),
        name="gla",
    )(q, k, v, vt, ad, g, ck, cvt, cad, u, abias, gnorm, cmats)


def _gla_decay_up(a_up):
    r = GLA_GATE_RANK
    pad = jnp.zeros((2, 2 * r, GLA_KEY_W), F32)
    pad = pad.at[0, :r].set(a_up[0]).at[1, r:].set(a_up[1])
    hi, lo = _split_bf16(pad)
    return jnp.concatenate([hi, hi, lo, jnp.zeros_like(hi)], axis=1)


def _outproj_kernel(ona_ref, ogla_ref, x_ref, w1_ref, w2_ref, gm_ref, shf_ref, scf_ref, npost_ref,
                    nfpre_ref, rt_ref, xnew_ref, hf_ref, aff_ref, afft_ref):
    mix = _dot(ona_ref[0], w1_ref[...]) + _dot(ogla_ref[0], w2_ref[...])
    xn = x_ref[0] + gm_ref[0] * _rms(mix, npost_ref[...])
    xnew_ref[0] = xn
    h = _rms(xn, nfpre_ref[...]) * (1.0 + scf_ref[0]) + shf_ref[0]
    h_hi, h_lo = _split_bf16(h)
    hf_ref[0] = h_hi
    res = _dot(h_hi, rt_ref[...])
    logits = res[:, :LANES] + res[:, LANES:] + _dot(h_lo, rt_ref[:, :LANES])
    lane = lax.broadcasted_iota(jnp.int32, logits.shape, 1)
    logits = jnp.where(lane < N_EXPERTS, logits, NEG_BIG)
    p = jnp.exp(logits - jnp.max(logits, axis=-1, keepdims=True))
    aff = p / jnp.sum(p, axis=-1, keepdims=True)
    aff_ref[0] = aff
    afft_ref[0] = aff.T[:N_EXPERTS, :]


def _outproj(o_na, o_gla, x, w1, w2, gm, shf, scf, npost, nfpre, router_cat):
    b, n, d = x.shape
    tm = TOK_TILE
    tokmap = lambda i, j: (i, j, 0)
    smp = lambda i, j: (i, 0, 0)
    cst = lambda i, j: (0, 0)
    return pl.pallas_call(
        _outproj_kernel,
        out_shape=[jax.ShapeDtypeStruct((b, n, d), F32),
                   jax.ShapeDtypeStruct((b, n, d), BF16),
                   jax.ShapeDtypeStruct((b, n, LANES), F32),
                   jax.ShapeDtypeStruct((b, N_EXPERTS, n), F32)],
        grid=(b, n // tm),
        in_specs=[pl.BlockSpec((1, tm, o_na.shape[2]), tokmap),
                  pl.BlockSpec((1, tm, o_gla.shape[2]), tokmap),
                  pl.BlockSpec((1, tm, d), tokmap),
                  pl.BlockSpec(w1.shape, cst),
                  pl.BlockSpec(w2.shape, cst),
                  pl.BlockSpec((1, 1, d), smp),
                  pl.BlockSpec((1, 1, d), smp),
                  pl.BlockSpec((1, 1, d), smp),
                  pl.BlockSpec((1, d), cst),
                  pl.BlockSpec((1, d), cst),
                  pl.BlockSpec(router_cat.shape, cst)],
        out_specs=[pl.BlockSpec((1, tm, d), tokmap),
                   pl.BlockSpec((1, tm, d), tokmap),
                   pl.BlockSpec((1, tm, LANES), tokmap),
                   pl.BlockSpec((1, N_EXPERTS, tm), lambda i, j: (i, 0, j))],
        compiler_params=_cparams(("arbitrary", "arbitrary")),
        name="outproj",
    )(o_na, o_gla, x, w1, w2, gm, shf, scf, npost.reshape(1, d), nfpre.reshape(1, d), router_cat)


def _route_kernel(afft_ref, rt_ref, rc_ref, *, cap):
    a = afft_ref[0]
    e, n = a.shape
    capf = jnp.float32(cap)

    def search(i, thr_bits):
        cand = thr_bits | lax.shift_left(jnp.int32(1), 30 - i)
        cnt = jnp.sum(jnp.where(a >= lax.bitcast_convert_type(cand, F32), 1.0, 0.0), axis=-1, keepdims=True)
        return jnp.where(cnt >= capf, cand, thr_bits)

    thr_bits = lax.fori_loop(0, 31, search, jnp.zeros((e, 1), jnp.int32))
    thr = lax.bitcast_convert_type(thr_bits, F32)
    need = capf - jnp.sum(jnp.where(a > thr, 1.0, 0.0), axis=-1, keepdims=True)
    r_i = lax.broadcasted_iota(jnp.int32, (LANES, LANES), 0)
    c_i = lax.broadcasted_iota(jnp.int32, (LANES, LANES), 1)
    incl = jnp.where(r_i <= c_i, 1.0, 0.0).astype(BF16)
    off_eq = jnp.zeros((e, 1), F32)
    off_sel = jnp.zeros((e, 1), F32)
    pad = jnp.full((LANES - e, LANES), -1.0, F32)
    for j in range(n // LANES):
        sl = slice(j * LANES, (j + 1) * LANES)
        a_b = a[:, sl]
        eq_b = jnp.where(a_b == thr, 1.0, 0.0)
        tie_rank = _dot(eq_b.astype(BF16), incl) - eq_b + off_eq
        off_eq = off_eq + jnp.sum(eq_b, axis=-1, keepdims=True)
        sel_b = jnp.where(a_b > thr, 1.0, jnp.where(tie_rank < need, eq_b, 0.0))
        sel = sel_b > 0.5
        rank = _dot(sel_b.astype(BF16), incl) - sel_b + off_sel
        off_sel = off_sel + jnp.sum(sel_b, axis=-1, keepdims=True)
        rsel = jnp.where(sel, rank, -1.0)
        rt_ref[0, :, sl] = rsel.astype(jnp.int32)
        rc_ref[0, sl, :] = jnp.concatenate([rsel, pad], axis=0).T.astype(jnp.int32)


def _route(afft, cap):
    b, e, n = afft.shape
    return pl.pallas_call(
        functools.partial(_route_kernel, cap=cap),
        out_shape=[jax.ShapeDtypeStruct((b, e, n), jnp.int32),
                   jax.ShapeDtypeStruct((b, n, LANES), jnp.int32)],
        grid=(b,),
        in_specs=[pl.BlockSpec((1, e, n), lambda i: (i, 0, 0))],
        out_specs=[pl.BlockSpec((1, e, n), lambda i: (i, 0, 0)),
                   pl.BlockSpec((1, n, LANES), lambda i: (i, 0, 0))],
        compiler_params=_cparams(("arbitrary",)),
        name="route",
    )(afft)


def _gather_kernel(r_ref, hf_ref, o_ref, *, cap):
    r = r_ref[0, 0]
    n = r.shape[1]
    slot = lax.broadcasted_iota(jnp.int32, (SLOT_TILE, n), 0)
    for s0 in range(0, cap, SLOT_TILE):
        onehot = jnp.where(r == slot + s0, 1.0, 0.0).astype(BF16)
        o_ref[0, 0, s0:s0 + SLOT_TILE, :] = _dot(onehot, hf_ref[0]).astype(o_ref.dtype)


def _gather(rsel_t, hf, cap):
    b, e, n = rsel_t.shape
    d = hf.shape[2]
    return pl.pallas_call(
        functools.partial(_gather_kernel, cap=cap),
        out_shape=jax.ShapeDtypeStruct((b, e, cap, d), BF16),
        grid=(b, e),
        in_specs=[pl.BlockSpec((1, 1, 1, n), lambda i, j: (i, j, 0, 0)),
                  pl.BlockSpec((1, n, d), lambda i, j: (i, 0, 0))],
        out_specs=pl.BlockSpec((1, 1, cap, d), lambda i, j: (i, j, 0, 0)),
        compiler_params=_cparams(("arbitrary", "arbitrary")),
        name="gather",
    )(rsel_t.reshape(b, e, 1, n), hf)


def _ffn_kernel(x_ref, wg_ref, wu_ref, wd_ref, o_ref, acc_ref):
    f = pl.program_id(1)
    b = x_ref.shape[0]
    wg = wg_ref[0].astype(BF16)
    wu = wu_ref[0].astype(BF16)
    wd = wd_ref[0].astype(BF16)

    @pl.when(f == 0)
    def _():
        acc_ref[...] = jnp.zeros_like(acc_ref)

    for i in range(b):
        x = x_ref[i, 0]
        g = _dot(x, wg)
        u = _dot(x, wu)
        hid = (g / (1.0 + jnp.exp(-g)) * u).astype(BF16)
        acc_ref[i] += _dot(hid, wd)

    @pl.when(f == pl.num_programs(1) - 1)
    def _():
        for i in range(b):
            o_ref[i, 0] = acc_ref[i].astype(o_ref.dtype)


def _ffn(xs, w_gate, w_up, w_down):
    b, e, cap, d = xs.shape
    dff = w_gate.shape[2]
    tf = FFN_TILE
    return pl.pallas_call(
        _ffn_kernel,
        out_shape=jax.ShapeDtypeStruct((b, e, cap, d), BF16),
        grid=(e, dff // tf),
        in_specs=[pl.BlockSpec((b, 1, cap, d), lambda i, f: (0, i, 0, 0)),
                  pl.BlockSpec((1, d, tf), lambda i, f: (i, 0, f)),
                  pl.BlockSpec((1, d, tf), lambda i, f: (i, 0, f)),
                  pl.BlockSpec((1, tf, d), lambda i, f: (i, f, 0))],
        out_specs=pl.BlockSpec((b, 1, cap, d), lambda i, f: (0, i, 0, 0)),
        scratch_shapes=[pltpu.VMEM((b, cap, d), F32)],
        compiler_params=_cparams(("arbitrary", "arbitrary")),
        name="ffn",
    )(xs, w_gate, w_up, w_down)


def _combine_kernel(ys_ref, rc_ref, aff_ref, xn_ref, gf_ref, npost_ref, o_ref):
    tm = rc_ref.shape[1]
    cap = ys_ref.shape[2]
    slot = lax.broadcasted_iota(jnp.int32, (tm, cap), 1)
    acc = jnp.zeros(o_ref.shape[1:], F32)
    for e in range(ys_ref.shape[1]):
        onehot = jnp.where(rc_ref[0, :, e:e + 1] == slot, 1.0, 0.0).astype(BF16)
        acc = acc + aff_ref[0, :, e:e + 1] * _dot(onehot, ys_ref[0, e])
    o_ref[0] = xn_ref[0] + gf_ref[0] * _rms(acc, npost_ref[...])


def _combine(ys, rsel_c, aff, x_new, gf, npost):
    b, e, cap, d = ys.shape
    n = x_new.shape[1]
    tm = TOK_TILE
    tokmap = lambda i, j: (i, j, 0)
    return pl.pallas_call(
        _combine_kernel,
        out_shape=jax.ShapeDtypeStruct((b, n, d), F32),
        grid=(b, n // tm),
        in_specs=[pl.BlockSpec((1, e, cap, d), lambda i, j: (i, 0, 0, 0)),
                  pl.BlockSpec((1, tm, LANES), tokmap),
                  pl.BlockSpec((1, tm, LANES), tokmap),
                  pl.BlockSpec((1, tm, d), tokmap),
                  pl.BlockSpec((1, 1, d), lambda i, j: (i, 0, 0)),
                  pl.BlockSpec((1, d), lambda i, j: (0, 0))],
        out_specs=pl.BlockSpec((1, tm, d), tokmap),
        compiler_params=_cparams(("arbitrary", "arbitrary")),
        name="combine",
    )(ys, rsel_c, aff, x_new, gf, npost.reshape(1, d))


def kernel(x, c, ctx, c_ctx, w_mod, b_mod, norm_mix_pre, norm_mix_post, norm_ffn_pre, norm_ffn_post,
           w_in, na_rpb, gla_a_up, gla_a_bias, gla_norm, w_out, router, w_gate, w_up, w_down):
    b, n, d = x.shape
    assert w_mod.shape[0] == 1 and d == D_MODEL and n % (GRID_W * NA_QROWS) == 0 and n % GLA_T == 0
    assert ctx.shape[1] == GLA_T
    rows = n // GRID_W
    cap = EC_CAPACITY_FACTOR * n // N_EXPERTS

    c8 = jnp.concatenate([c, c_ctx[None, :], jnp.zeros((8 - b - 1, d), F32)], axis=0)
    mod = _mod(c8, w_mod[0], b_mod[0])
    sh_m, sc_m, g_m, sh_f, sc_f, g_f = [m[:b, None, :] for m in jnp.split(mod, 6, axis=-1)]
    sh_c, sc_c = mod[b:b + 1, None, :d], mod[b:b + 1, None, d:2 * d]

    wb = w_in[0].astype(BF16)
    cuts = np.cumsum([0, NA_W, NA_W, GLA_KEY_W, GLA_VAL_W, 2 * GLA_GATE_RANK, NA_W, GLA_KEY_W, GLA_VAL_W])
    w_nak, w_nav, w_gk, w_gv, w_ad, w_naq, w_gq, w_gg = [wb[:, cuts[i]:cuts[i + 1]] for i in range(8)]
    w_gvt = w_gv.T
    w_ad3 = jnp.concatenate([w_ad, w_ad, w_ad, jnp.zeros_like(w_ad)], axis=1)
    rope = _rope_tables(n)
    na_q, na_k, na_v, gq, gk, gv, gvt, ad, gg = _inproj(
        x, sh_m, sc_m, norm_mix_pre[0],
        [w_naq, w_nak, w_nav, w_gq, w_gk, w_gv, w_gvt, w_ad3, w_gg],
        ["plain", "plain", "plain", "rope", "rope", "plain", "t", "split", "plain"],
        [1.0, 1.0, 1.0, GLA_DK ** -0.5, 1.0, 1.0, 1.0, 1.0, 1.0],
        [BF16] * 9, rope=rope)
    c_nak, c_nav, c_gk, c_gvt, c_ad = _inproj(
        ctx, sh_c, sc_c, norm_mix_pre[0],
        [w_nak, w_nav, w_gk, w_gvt, w_ad3],
        ["plain", "plain", "plain", "t", "split"], [1.0] * 5, [BF16] * 5)

    o_na = _na(na_q, na_k, na_v, c_nak, c_nav, *_na_bias_tables(na_rpb[0], rows))

    cmats = jnp.asarray(_gla_prefix_matrices(GLA_T), BF16)
    o_gla = _gla(gq, gk, gv, gvt, ad, gg, c_gk, c_gvt, c_ad, _gla_decay_up(gla_a_up[0]),
                 gla_a_bias[0][:, None, :], gla_norm[0][None, :], cmats)

    wo = w_out[0].astype(BF16)
    router_pad = jnp.zeros((d, LANES), F32).at[:, :N_EXPERTS].set(router[0])
    x_new, hf, aff, aff_t = _outproj(o_na, o_gla, x, wo[:NA_W], wo[NA_W:], g_m, sh_f, sc_f,
                                     norm_mix_post[0], norm_ffn_pre[0],
                                     jnp.concatenate(_split_bf16(router_pad), axis=1))

    rsel_t, rsel_c = _route(aff_t, cap)
    xs = _gather(rsel_t, hf, cap)
    ys = _ffn(xs, w_gate[0], w_up[0], w_down[0])
    return _combine(ys, rsel_c, aff, x_new, g_f, norm_ffn_post[0])
```

```python
import functools

import numpy as np
import jax
import jax.numpy as jnp
from jax import lax
from jax.experimental import pallas as pl
from jax.experimental.pallas import tpu as pltpu
from jax.experimental.pallas import tpu_sc as plsc

F32 = jnp.float32
BF16 = jnp.bfloat16
HIGHEST = lax.Precision.HIGHEST

D_MODEL = 1024
GRID_W = 64
NA_W = 512
NA_HEADS = 8
NA_HEAD_DIM = 64
NA_WIN_ROWS = 8
NA_WIN_COLS = 16
GLA_HEADS = 4
GLA_DV = 128
GLA_DK = 64
GLA_KEY_W = 256
GLA_VAL_W = 512
GLA_GATE_RANK = 16
GLA_GATE_TAU = 16.0
ROPE_BASE = 10000.0
N_EXPERTS = 16
EC_CAPACITY_FACTOR = 2
NORM_EPS = 1e-6
NEG_BIG = -1e30

LANES = 128
VMEM_LIMIT = 56 * 1024 * 1024

TOK_TILE = 512
NA_QROWS = 4
NA_KROWS = NA_QROWS + NA_WIN_ROWS
NA_TAB_PAD = NA_QROWS
NA_TAB_BLOCKS = NA_TAB_PAD + 2 * NA_WIN_ROWS - 1 + NA_QROWS + 1
GLA_T = 256
GLA_LEVELS = (128, 64, 32, 16)
GLA_DIAG = 16
FFN_TILE = 256
SLOT_TILE = 128
SC_GATHER_ROWS = 64


def _cparams(sem):
    return pltpu.CompilerParams(dimension_semantics=sem, vmem_limit_bytes=VMEM_LIMIT)


def _rms(v, g):
    return v * lax.rsqrt(jnp.mean(v * v, axis=-1, keepdims=True) + NORM_EPS) * g


def _dot(a, b):
    return jnp.dot(a, b, preferred_element_type=F32)


def _dot_nt(a, b):
    return lax.dot_general(a, b, (((1,), (1,)), ((), ())), preferred_element_type=F32)


def _split_bf16(v):
    hi = v.astype(BF16)
    return hi, (v - hi.astype(F32)).astype(BF16)


_HIGH_HALF = -65536


def _pack_bf16_pairs(hb):
    bits = lax.bitcast_convert_type(hb.astype(F32), jnp.int32)
    half = hb.shape[1] // 2
    return lax.shift_right_logical(bits[:, :half], 16) | (bits[:, half:] & _HIGH_HALF)


def _unpack_bf16_pairs(w):
    lo = lax.bitcast_convert_type(lax.shift_left(w, 16), F32).astype(BF16)
    hi = lax.bitcast_convert_type(w & _HIGH_HALF, F32).astype(BF16)
    return jnp.concatenate([lo, hi], axis=1)


def _mod_kernel(c_ref, w_ref, b_ref, o_ref):
    c = c_ref[...]
    s = c / (1.0 + jnp.exp(-c))
    o_ref[...] = jnp.dot(s, w_ref[...], preferred_element_type=F32, precision=HIGHEST) + b_ref[...]


def _mod(c8, w_mod, b_mod):
    d, n = w_mod.shape
    tn = 1536
    return pl.pallas_call(
        _mod_kernel,
        out_shape=jax.ShapeDtypeStruct((8, n), F32),
        grid=(n // tn,),
        in_specs=[pl.BlockSpec((8, d), lambda j: (0, 0)),
                  pl.BlockSpec((d, tn), lambda j: (0, j)),
                  pl.BlockSpec((1, tn), lambda j: (0, j))],
        out_specs=pl.BlockSpec((8, tn), lambda j: (0, j)),
        compiler_params=_cparams(("arbitrary",)),
        name="mod",
    )(c8, w_mod, b_mod.reshape(1, n))


def _inproj_kernel(x_ref, sh_ref, sc_ref, g_ref, *refs, modes, scales):
    n_rope = 2 if "rope" in modes else 0
    rope_refs, refs = refs[:n_rope], refs[n_rope:]
    n_w = len(modes)
    w_refs, o_refs = refs[:n_w], refs[n_w:]
    x = x_ref[0]
    h = _rms(x, g_ref[...]) * (1.0 + sc_ref[0]) + sh_ref[0]
    hb = h.astype(BF16)
    for w_ref, o_ref, mode, scale in zip(w_refs, o_refs, modes, scales):
        if mode == "t":
            o_ref[0] = _dot_nt(w_ref[...], hb).astype(o_ref.dtype)
            continue
        y = _dot(hb, w_ref[...])
        if mode == "rope":
            cols = y.shape[1]
            quarter = GLA_DK // 4
            lane = lax.broadcasted_iota(jnp.int32, y.shape, 1)
            partner = jnp.where((lane & (2 * quarter - 1)) < quarter,
                                pltpu.roll(y, cols - quarter, axis=1), pltpu.roll(y, quarter, axis=1))
            y = (y * rope_refs[0][...] + partner * rope_refs[1][...]) * scale
        elif mode == "split":
            hi, lo = _split_bf16(y)
            lane = lax.broadcasted_iota(jnp.int32, y.shape, 1)
            rank2 = 2 * GLA_GATE_RANK
            y = jnp.where((lane >= rank2) & (lane < 2 * rank2), lo, hi)
        o_ref[0] = y.astype(o_ref.dtype)


def _inproj(x, shift, scale, g, weights, modes, scales, out_dtypes, rope=None):
    b, n, d = x.shape
    tm = min(TOK_TILE, n)
    per_sample = shift.shape[0] == b
    mod_map = (lambda i, j: (i, 0, 0)) if per_sample else (lambda i, j: (0, 0, 0))
    in_specs = [pl.BlockSpec((1, tm, d), lambda i, j: (i, j, 0)),
                pl.BlockSpec((1, 1, d), mod_map),
                pl.BlockSpec((1, 1, d), mod_map),
                pl.BlockSpec((1, d), lambda i, j: (0, 0))]
    args = [x, shift, scale, g.reshape(1, d)]
    if rope is not None:
        in_specs += [pl.BlockSpec((tm, rope[0].shape[1]), lambda i, j: (j, 0))] * 2
        args += list(rope)
    out_shapes, out_specs = [], []
    for w, mode, dt in zip(weights, modes, out_dtypes):
        in_specs.append(pl.BlockSpec(w.shape, lambda i, j: (0, 0)))
        if mode == "t":
            cols = w.shape[0]
            out_shapes.append(jax.ShapeDtypeStruct((b, cols, n), dt))
            out_specs.append(pl.BlockSpec((1, cols, tm), lambda i, j: (i, 0, j)))
        else:
            cols = w.shape[1]
            out_shapes.append(jax.ShapeDtypeStruct((b, n, cols), dt))
            out_specs.append(pl.BlockSpec((1, tm, cols), lambda i, j: (i, j, 0)))
    return pl.pallas_call(
        functools.partial(_inproj_kernel, modes=tuple(modes), scales=tuple(scales)),
        out_shape=out_shapes,
        grid=(b, n // tm),
        in_specs=in_specs,
        out_specs=out_specs,
        compiler_params=_cparams(("arbitrary", "arbitrary")),
        name="inproj",
    )(*args, *weights)


def _rope_tables(n):
    t = jnp.arange(n)
    pos_row, pos_col = (t // GRID_W).astype(F32), (t % GRID_W).astype(F32)
    quarter = GLA_DK // 4
    freqs = ROPE_BASE ** (-jnp.arange(quarter, dtype=F32) / quarter)
    ang_r = pos_row[:, None] * freqs
    ang_c = pos_col[:, None] * freqs
    cos = jnp.concatenate([jnp.cos(ang_r), jnp.cos(ang_r), jnp.cos(ang_c), jnp.cos(ang_c)], axis=-1)
    sin = jnp.concatenate([-jnp.sin(ang_r), jnp.sin(ang_r), -jnp.sin(ang_c), jnp.sin(ang_c)], axis=-1)
    return jnp.tile(cos, (1, GLA_HEADS)), jnp.tile(sin, (1, GLA_HEADS))


def _na_patterns(rows):
    kr = min(NA_WIN_ROWS, rows)
    n_blocks = rows // NA_QROWS
    pats = []
    for blk in (0, 1, n_blocks - 1):
        r0 = blk * NA_QROWS
        k0 = int(np.clip(r0 - kr // 2, 0, rows - NA_KROWS))
        strips = []
        for a in range(NA_QROWS):
            r_start = int(np.clip(r0 + a - kr // 2, 0, rows - kr))
            start = k0 - (r0 + a) + NA_WIN_ROWS - 1 + NA_TAB_PAD
            assert 0 <= start and start + NA_KROWS <= NA_TAB_BLOCKS
            strips.append((start, [r_start <= k0 + c < r_start + kr for c in range(NA_KROWS)]))
        pats.append(strips)
    return pats


def _na_bias_tables(rpb, rows):
    heads = rpb.shape[0]
    col = np.arange(GRID_W)
    c_start = np.clip(col - NA_WIN_COLS // 2, 0, GRID_W - NA_WIN_COLS)
    col_ok = (col[None, :] >= c_start[:, None]) & (col[None, :] < c_start[:, None] + NA_WIN_COLS)
    dc = np.clip(col[None, :] - col[:, None] + NA_WIN_COLS - 1, 0, 2 * NA_WIN_COLS - 2)
    sel_c = (np.arange(2 * NA_WIN_COLS - 1)[:, None, None] == dc[None]) & col_ok[None]
    t = jnp.einsum("hrd,dqk->hqrk", rpb, jnp.asarray(sel_c, F32), precision=HIGHEST)
    t = jnp.where(jnp.asarray(col_ok)[None, :, None, :], t, NEG_BIG)
    n_dr = 2 * NA_WIN_ROWS - 1
    t = t.reshape(heads, GRID_W, n_dr * GRID_W)
    back = NA_TAB_BLOCKS + 1 - NA_TAB_PAD - n_dr
    t = jnp.pad(t, ((0, 0), (0, 0), (NA_TAB_PAD * GRID_W, back * GRID_W)), constant_values=NEG_BIG)
    width = NA_TAB_BLOCKS * GRID_W
    tab = jnp.stack([t[:, :, :width], t[:, :, GRID_W:GRID_W + width]], axis=1)
    row_mask = np.zeros((3, NA_QROWS, 1, NA_KROWS * GRID_W), np.float32)
    for pat, strips in enumerate(_na_patterns(rows)):
        for a, (_, valid) in enumerate(strips):
            row_mask[pat, a, 0] = np.repeat(np.where(valid, 0.0, NEG_BIG), GRID_W)
    return tab, jnp.asarray(row_mask)


def _na_kernel(q_ref, k_ref, v_ref, kc_ref, vc_ref, tab_ref, rmask_ref, o_ref, bias_ref,
               sw_ref, sc_ref, pw_ref, pc_ref, linv_ref, *, rows):
    nq, nk = NA_QROWS * GRID_W, NA_KROWS * GRID_W
    n_blocks = rows // NA_QROWS
    kr = min(NA_WIN_ROWS, rows)
    scale = NA_HEAD_DIM ** -0.5
    lane = lax.broadcasted_iota(jnp.int32, (nq, LANES), 1)
    first_head = lane < NA_HEAD_DIM
    kc = kc_ref[0]
    vc = vc_ref[0]
    for pat, strips in enumerate(_na_patterns(rows)):
        for a, (start, _) in enumerate(strips):
            parity = start % 2
            off = (start - parity) * GRID_W
            for h in range(2):
                bias_ref[0, pat, h, a * GRID_W:(a + 1) * GRID_W, :] = (
                    tab_ref[h, parity, :, off:off + nk] + rmask_ref[pat, a])

    def key_start(i):
        return pl.multiple_of(jnp.clip(i * NA_QROWS - kr // 2, 0, rows - NA_KROWS) * GRID_W, GRID_W)

    def scores(i, slot):
        pat = jnp.where(i == 0, 0, jnp.where(i == n_blocks - 1, 2, 1))
        q = q_ref[0, pl.ds(pl.multiple_of(i * nq, nq), nq), :] * scale
        kw = k_ref[0, pl.ds(key_start(i), nk), :]
        for h in range(2):
            qh = jnp.where(first_head if h == 0 else jnp.logical_not(first_head), q, jnp.zeros_like(q))
            sw_ref[slot, h] = _dot_nt(qh, kw) + bias_ref[0, pat, h]
            sc_ref[slot, h] = _dot_nt(qh, kc)

    def softmax(slot):
        for h in range(2):
            s_w = sw_ref[slot, h]
            s_c = sc_ref[slot, h]
            m = jnp.maximum(jnp.max(s_w, axis=-1, keepdims=True), jnp.max(s_c, axis=-1, keepdims=True))
            p_w = jnp.exp(s_w - m)
            p_c = jnp.exp(s_c - m)
            l = jnp.sum(p_w, axis=-1, keepdims=True) + jnp.sum(p_c, axis=-1, keepdims=True)
            pw_ref[slot, h] = p_w.astype(BF16)
            pc_ref[slot, h] = p_c.astype(BF16)
            linv_ref[slot, h] = jnp.broadcast_to(1.0 / l, (nq, LANES))

    def values(i, slot):
        vw = v_ref[0, pl.ds(key_start(i), nk), :]
        outs = [(_dot(pw_ref[slot, h], vw) + _dot(pc_ref[slot, h], vc)) * linv_ref[slot, h] for h in range(2)]
        o = jnp.where(first_head, outs[0], outs[1])
        o_ref[0, pl.ds(pl.multiple_of(i * nq, nq), nq), :] = o.astype(o_ref.dtype)

    assert n_blocks % 2 == 0 and n_blocks >= 4
    scores(0, 0)
    softmax(0)
    scores(1, 1)

    def trip(j, carry):
        i = 2 * j
        values(i - 2, 0)
        softmax(1)
        scores(i, 0)
        values(i - 1, 1)
        softmax(0)
        scores(i + 1, 1)
        return carry

    lax.fori_loop(1, n_blocks // 2, trip, 0)
    values(n_blocks - 2, 0)
    softmax(1)
    values(n_blocks - 1, 1)


def _na(q, k, v, kc, vc, tab, row_mask):
    b, n, w = q.shape
    n_ctx = kc.shape[1]
    pairs = w // LANES
    rows = n // GRID_W
    nq, nk = NA_QROWS * GRID_W, NA_KROWS * GRID_W
    tok = lambda i, p: (i, 0, p)
    return pl.pallas_call(
        functools.partial(_na_kernel, rows=rows),
        out_shape=jax.ShapeDtypeStruct((b, n, w), BF16),
        grid=(b, pairs),
        in_specs=[pl.BlockSpec((1, n, LANES), tok),
                  pl.BlockSpec((1, n, LANES), tok),
                  pl.BlockSpec((1, n, LANES), tok),
                  pl.BlockSpec((1, n_ctx, LANES), tok),
                  pl.BlockSpec((1, n_ctx, LANES), tok),
                  pl.BlockSpec((2,) + tab.shape[1:], lambda i, p: (p, 0, 0, 0)),
                  pl.BlockSpec(row_mask.shape, lambda i, p: (0, 0, 0, 0))],
        out_specs=pl.BlockSpec((1, n, LANES), tok),
        scratch_shapes=[pltpu.VMEM((1, 3, 2, nq, nk), F32),
                        pltpu.VMEM((2, 2, nq, nk), F32), pltpu.VMEM((2, 2, nq, n_ctx), F32),
                        pltpu.VMEM((2, 2, nq, nk), BF16), pltpu.VMEM((2, 2, nq, n_ctx), BF16),
                        pltpu.VMEM((2, 2, nq, LANES), F32)],
        compiler_params=_cparams(("arbitrary", "arbitrary")),
        name="na",
    )(q, k, v, kc, vc, tab, row_mask)


def _gla_prefix_matrices(t):
    i = np.arange(t)
    return np.stack([i[:, None] >= i[None, :], i[:, None] <= i[None, :]]).astype(np.float32)


def _gla_kernel(q_ref, k_ref, v_ref, vt_ref, ad_ref, g_ref, ck_ref, cvt_ref, cad_ref,
                u_ref, ab_ref, gn_ref, cm_ref, o_ref, accf_ref, accb_ref, *, n_tok):
    t = GLA_T
    n_chunks = n_tok // t
    row = lax.broadcasted_iota(jnp.int32, (t, LANES), 0)
    lane = lax.broadcasted_iota(jnp.int32, (t, LANES), 1)
    head0_l = lane < GLA_DK
    row2 = lax.broadcasted_iota(jnp.int32, (t, 2 * t), 0)
    col2 = lax.broadcasted_iota(jnp.int32, (t, 2 * t), 1) & (t - 1)
    vrow = lax.broadcasted_iota(jnp.int32, (2 * t, 2 * GLA_DV), 0)
    vlane = lax.broadcasted_iota(jnp.int32, (2 * t, 2 * GLA_DV), 1)
    v_head_match = (vrow >= t) == (vlane >= GLA_DV)
    srow = lax.broadcasted_iota(jnp.int32, (2 * GLA_DV, LANES), 0)
    slane = lax.broadcasted_iota(jnp.int32, (2 * GLA_DV, LANES), 1)
    s_blockdiag = (srow >= GLA_DV) == (slane >= GLA_DK)
    blk_mask = {half: jnp.where((row2 & ~(2 * half - 1)) == (col2 & ~(2 * half - 1)), 1.0, 0.0)
                for half in GLA_LEVELS if 2 * half < t}
    diag_blk = (row2 & ~(GLA_DIAG - 1)) == (col2 & ~(GLA_DIAG - 1))
    diag_mask = (jnp.where(diag_blk & (row2 >= col2), 1.0, 0.0), jnp.where(diag_blk & (row2 <= col2), 1.0, 0.0))

    def prefix_sums(ad, dirn):
        z = _dot(ad, u_ref[dirn]) + ab_ref[dirn]
        loga = (jnp.minimum(z, 0.0) - jnp.log(1.0 + jnp.exp(-jnp.abs(z)))) * (1.0 / GLA_GATE_TAU)
        hi, lo = _split_bf16(loga)
        p2 = _dot(cm_ref[dirn], jnp.concatenate([hi, lo], axis=-1))
        return p2[:, :LANES] + p2[:, LANES:]

    def chunk_end(p, dirn):
        return p[t - 1:t, :] if dirn == 0 else p[0:1, :]

    def level_sums(p, half, dirn):
        blk = 2 * half
        p3 = p.reshape(t // blk, blk, LANES)
        edge = half - 1 if dirn == 0 else half
        ref = jnp.broadcast_to(p3[:, edge:edge + 1, :], p3.shape).reshape(t, LANES)
        later = (row & half) != 0
        return jnp.where(later == (dirn == 0), p - ref, ref - p)

    def state_update(s, k, vt, p, dirn):
        kh = (k * jnp.exp(chunk_end(p, dirn) - p)).astype(BF16)
        return s * jnp.exp(chunk_end(p, dirn)) + jnp.where(s_blockdiag, _dot(vt, kh), 0.0)

    def heads_stacked(kt):
        return jnp.concatenate([jnp.where(head0_l, kt, 0.0), jnp.where(head0_l, 0.0, kt)], axis=0).astype(BF16)

    def chunk(tok0, s, dirn):
        q = q_ref[0, pl.ds(tok0, t), :].astype(F32)
        k = k_ref[0, pl.ds(tok0, t), :].astype(F32)
        v = v_ref[0, pl.ds(tok0, t), :]
        vt = vt_ref[0, :, pl.ds(tok0, t)]
        p = prefix_sums(ad_ref[0, pl.ds(tok0, t), :], dirn)
        qh = (q * jnp.exp(p)).astype(BF16)
        o = _dot_nt(qh, s.astype(BF16))
        a = None
        for half in GLA_LEVELS:
            w = jnp.exp(level_sums(p, half, dirn))
            later = (row & half) != 0
            q_side = later if dirn == 0 else jnp.logical_not(later)
            qt = jnp.where(q_side, q * w, 0.0).astype(BF16)
            kt = jnp.where(q_side, 0.0, k * w)
            part = _dot_nt(qt, heads_stacked(kt))
            if half in blk_mask:
                part = part * blk_mask[half]
            a = part if a is None else a + part
        e_d = level_sums(p, GLA_DIAG // 2, dirn)
        w, wi = jnp.exp(e_d), jnp.exp(-e_d)
        later = (row & (GLA_DIAG // 2)) != 0
        shrink_q = later if dirn == 0 else jnp.logical_not(later)
        qt = (q * jnp.where(shrink_q, w, wi)).astype(BF16)
        kt = k * jnp.where(shrink_q, wi, w)
        a = a + jnp.where(diag_mask[dirn] > 0.5, _dot_nt(qt, heads_stacked(kt)), 0.0)
        vcat = jnp.concatenate([v, v], axis=0)
        vcat = jnp.where(v_head_match, vcat, jnp.zeros_like(vcat))
        o = o + _dot(a.astype(BF16), vcat)
        return o, state_update(s, k, vt, p, dirn)

    def ctx_state(dirn):
        p = prefix_sums(cad_ref[0], dirn)
        s0 = jnp.zeros((2 * GLA_DV, LANES), F32)
        return state_update(s0, ck_ref[0].astype(F32), cvt_ref[0], p, dirn)

    def body(i, carry):
        s_f, s_b = carry
        tok_f = pl.multiple_of(i * t, t)
        tok_b = pl.multiple_of((n_chunks - 1 - i) * t, t)
        o_f, s_f = chunk(tok_f, s_f, 0)
        o_b, s_b = chunk(tok_b, s_b, 1)
        accf_ref[pl.ds(tok_f, t), :] = o_f
        accb_ref[pl.ds(tok_b, t), :] = o_b
        return s_f, s_b

    lax.fori_loop(0, n_chunks, body, (ctx_state(0), ctx_state(1)))

    def finish(c, carry):
        tok0 = pl.multiple_of(c * t, t)
        o = accf_ref[pl.ds(tok0, t), :] + accb_ref[pl.ds(tok0, t), :]
        g = g_ref[0, pl.ds(tok0, t), :].astype(F32)
        gate = g / (1.0 + jnp.exp(-g))
        halves = [_rms(o[:, h * GLA_DV:(h + 1) * GLA_DV], gn_ref[...]) for h in range(2)]
        o_ref[0, pl.ds(tok0, t), :] = (jnp.concatenate(halves, axis=-1) * gate).astype(o_ref.dtype)
        return carry

    lax.fori_loop(0, n_chunks, finish, 0)


def _gla(q, k, v, vt, ad, g, ck, cvt, cad, u, abias, gnorm, cmats):
    b, n, kw = q.shape
    n_ctx = ck.shape[1]
    pairs = kw // LANES
    vw = 2 * GLA_DV
    tok = lambda i, p: (i, 0, p)
    full3 = lambda i, p: (i, 0, 0)
    return pl.pallas_call(
        functools.partial(_gla_kernel, n_tok=n),
        out_shape=jax.ShapeDtypeStruct((b, n, v.shape[2]), BF16),
        grid=(b, pairs),
        in_specs=[pl.BlockSpec((1, n, LANES), tok),
                  pl.BlockSpec((1, n, LANES), tok),
                  pl.BlockSpec((1, n, vw), tok),
                  pl.BlockSpec((1, vw, n), lambda i, p: (i, p, 0)),
                  pl.BlockSpec((1, n, LANES), full3),
                  pl.BlockSpec((1, n, vw), tok),
                  pl.BlockSpec((1, n_ctx, LANES), tok),
                  pl.BlockSpec((1, vw, n_ctx), lambda i, p: (i, p, 0)),
                  pl.BlockSpec((1, n_ctx, LANES), full3),
                  pl.BlockSpec((2, LANES, LANES), lambda i, p: (0, 0, p)),
                  pl.BlockSpec((2, 1, LANES), lambda i, p: (0, 0, p)),
                  pl.BlockSpec((1, GLA_DV), lambda i, p: (0, 0)),
                  pl.BlockSpec(cmats.shape, lambda i, p: (0, 0, 0))],
        out_specs=pl.BlockSpec((1, n, vw), tok),
        scratch_shapes=[pltpu.VMEM((n, vw), F32), pltpu.VMEM((n, vw), F32)],
        compiler_params=_cparams(("arbitrary", "arbitrary")),
        name="gla",
    )(q, k, v, vt, ad, g, ck, cvt, cad, u, abias, gnorm, cmats)


def _gla_decay_up(a_up):
    r = GLA_GATE_RANK
    pad = jnp.zeros((2, 2 * r, GLA_KEY_W), F32)
    pad = pad.at[0, :r].set(a_up[0]).at[1, r:].set(a_up[1])
    hi, lo = _split_bf16(pad)
    return jnp.concatenate([hi, hi, lo, jnp.zeros_like(hi)], axis=1)


def _outproj_kernel(ona_ref, ogla_ref, x_ref, w1_ref, w2_ref, gm_ref, shf_ref, scf_ref, npost_ref,
                    nfpre_ref, rt_ref, xnew_ref, hf_ref, aff_ref, afft_ref):
    mix = _dot(ona_ref[0], w1_ref[...]) + _dot(ogla_ref[0], w2_ref[...])
    xn = x_ref[0] + gm_ref[0] * _rms(mix, npost_ref[...])
    xnew_ref[0] = xn
    h = _rms(xn, nfpre_ref[...]) * (1.0 + scf_ref[0]) + shf_ref[0]
    h_hi, h_lo = _split_bf16(h)
    hf_ref[0] = _pack_bf16_pairs(h_hi)
    res = _dot(h_hi, rt_ref[...])
    logits = res[:, :LANES] + res[:, LANES:] + _dot(h_lo, rt_ref[:, :LANES])
    lane = lax.broadcasted_iota(jnp.int32, logits.shape, 1)
    logits = jnp.where(lane < N_EXPERTS, logits, NEG_BIG)
    p = jnp.exp(logits - jnp.max(logits, axis=-1, keepdims=True))
    aff = p / jnp.sum(p, axis=-1, keepdims=True)
    aff_ref[0] = aff
    afft_ref[0] = aff.T[:N_EXPERTS, :]


def _outproj(o_na, o_gla, x, w1, w2, gm, shf, scf, npost, nfpre, router_cat):
    b, n, d = x.shape
    tm = TOK_TILE
    tokmap = lambda i, j: (i, j, 0)
    smp = lambda i, j: (i, 0, 0)
    cst = lambda i, j: (0, 0)
    return pl.pallas_call(
        _outproj_kernel,
        out_shape=[jax.ShapeDtypeStruct((b, n, d), F32),
                   jax.ShapeDtypeStruct((b, n, d // 2), jnp.int32),
                   jax.ShapeDtypeStruct((b, n, LANES), F32),
                   jax.ShapeDtypeStruct((b, N_EXPERTS, n), F32)],
        grid=(b, n // tm),
        in_specs=[pl.BlockSpec((1, tm, o_na.shape[2]), tokmap),
                  pl.BlockSpec((1, tm, o_gla.shape[2]), tokmap),
                  pl.BlockSpec((1, tm, d), tokmap),
                  pl.BlockSpec(w1.shape, cst),
                  pl.BlockSpec(w2.shape, cst),
                  pl.BlockSpec((1, 1, d), smp),
                  pl.BlockSpec((1, 1, d), smp),
                  pl.BlockSpec((1, 1, d), smp),
                  pl.BlockSpec((1, d), cst),
                  pl.BlockSpec((1, d), cst),
                  pl.BlockSpec(router_cat.shape, cst)],
        out_specs=[pl.BlockSpec((1, tm, d), tokmap),
                   pl.BlockSpec((1, tm, d // 2), tokmap),
                   pl.BlockSpec((1, tm, LANES), tokmap),
                   pl.BlockSpec((1, N_EXPERTS, tm), lambda i, j: (i, 0, j))],
        compiler_params=_cparams(("arbitrary", "arbitrary")),
        name="outproj",
    )(o_na, o_gla, x, w1, w2, gm, shf, scf, npost.reshape(1, d), nfpre.reshape(1, d), router_cat)


def _route_kernel(afft_ref, rt_ref, rc_ref, *, cap):
    a = afft_ref[0]
    e, n = a.shape
    capf = jnp.float32(cap)

    def search(i, thr_bits):
        cand = thr_bits | lax.shift_left(jnp.int32(1), 30 - i)
        cnt = jnp.sum(jnp.where(a >= lax.bitcast_convert_type(cand, F32), 1.0, 0.0), axis=-1, keepdims=True)
        return jnp.where(cnt >= capf, cand, thr_bits)

    thr_bits = lax.fori_loop(0, 31, search, jnp.zeros((e, 1), jnp.int32))
    thr = lax.bitcast_convert_type(thr_bits, F32)
    need = capf - jnp.sum(jnp.where(a > thr, 1.0, 0.0), axis=-1, keepdims=True)
    r_i = lax.broadcasted_iota(jnp.int32, (LANES, LANES), 0)
    c_i = lax.broadcasted_iota(jnp.int32, (LANES, LANES), 1)
    incl = jnp.where(r_i <= c_i, 1.0, 0.0).astype(BF16)
    off_eq = jnp.zeros((e, 1), F32)
    off_sel = jnp.zeros((e, 1), F32)
    pad = jnp.full((LANES - e, LANES), -1.0, F32)
    for j in range(n // LANES):
        sl = slice(j * LANES, (j + 1) * LANES)
        a_b = a[:, sl]
        eq_b = jnp.where(a_b == thr, 1.0, 0.0)
        tie_rank = _dot(eq_b.astype(BF16), incl) - eq_b + off_eq
        off_eq = off_eq + jnp.sum(eq_b, axis=-1, keepdims=True)
        sel_b = jnp.where(a_b > thr, 1.0, jnp.where(tie_rank < need, eq_b, 0.0))
        sel = sel_b > 0.5
        rank = _dot(sel_b.astype(BF16), incl) - sel_b + off_sel
        off_sel = off_sel + jnp.sum(sel_b, axis=-1, keepdims=True)
        rsel = jnp.where(sel, rank, -1.0)
        rt_ref[0, :, sl] = rsel.astype(jnp.int32)
        rc_ref[0, sl, :] = jnp.concatenate([rsel, pad], axis=0).T.astype(jnp.int32)


def _route(afft, cap):
    b, e, n = afft.shape
    return pl.pallas_call(
        functools.partial(_route_kernel, cap=cap),
        out_shape=[jax.ShapeDtypeStruct((b, e, n), jnp.int32),
                   jax.ShapeDtypeStruct((b, n, LANES), jnp.int32)],
        grid=(b,),
        in_specs=[pl.BlockSpec((1, e, n), lambda i: (i, 0, 0))],
        out_specs=[pl.BlockSpec((1, e, n), lambda i: (i, 0, 0)),
                   pl.BlockSpec((1, n, LANES), lambda i: (i, 0, 0))],
        compiler_params=_cparams(("arbitrary",)),
        name="route",
    )(afft)


def _gather_kernel(r_ref, hf_ref, o_ref, *, cap):
    r = r_ref[0, 0]
    n = r.shape[1]
    slot = lax.broadcasted_iota(jnp.int32, (SLOT_TILE, n), 0)
    for s0 in range(0, cap, SLOT_TILE):
        onehot = jnp.where(r == slot + s0, 1.0, 0.0).astype(BF16)
        o_ref[0, 0, s0:s0 + SLOT_TILE, :] = _dot(onehot, hf_ref[0]).astype(o_ref.dtype)


def _gather(rsel_t, hf, cap):
    b, e, n = rsel_t.shape
    d = hf.shape[2]
    return pl.pallas_call(
        functools.partial(_gather_kernel, cap=cap),
        out_shape=jax.ShapeDtypeStruct((b, e, cap, d), BF16),
        grid=(b, e),
        in_specs=[pl.BlockSpec((1, 1, 1, n), lambda i, j: (i, j, 0, 0)),
                  pl.BlockSpec((1, n, d), lambda i, j: (i, 0, 0))],
        out_specs=pl.BlockSpec((1, 1, cap, d), lambda i, j: (i, j, 0, 0)),
        compiler_params=_cparams(("arbitrary", "arbitrary")),
        name="gather",
    )(rsel_t.reshape(b, e, 1, n), hf)


def _sc_gather(rsel_t, hf2, cap):
    b, e, n = rsel_t.shape
    width = hf2.shape[1]
    info = plsc.get_sparse_core_info()
    nc, lanes = info.num_cores, info.num_lanes
    workers = nc * info.num_subcores
    items = b * e
    assert items % workers == 0 and n % lanes == 0 and cap % SC_GATHER_ROWS == 0
    per_worker = items // workers
    mesh = plsc.VectorSubcoreMesh(core_axis_name="c", subcore_axis_name="s")

    def body(rank_hbm, hf_hbm, out_hbm, rank_v, idx_v, rows_v, sem):
        wid = lax.axis_index("s") * nc + lax.axis_index("c")
        for k in range(per_worker):
            item = wid * per_worker + k
            base_tok = (item // e) * n
            pltpu.sync_copy(rank_hbm.at[item], rank_v)

            @pl.loop(0, n // lanes)
            def _(j):
                r = rank_v[pl.ds(j * lanes, lanes)]
                tok = lax.iota(jnp.int32, lanes) + (j * lanes + base_tok)
                plsc.store_scatter(idx_v, [r], tok, mask=r >= 0)

            @pl.loop(0, cap // SC_GATHER_ROWS)
            def _(c):
                rows = pl.ds(c * SC_GATHER_ROWS, SC_GATHER_ROWS)
                pltpu.async_copy(hf_hbm.at[idx_v.at[rows]], rows_v, sem).wait()
                pltpu.sync_copy(rows_v, out_hbm.at[pl.ds(item * cap + c * SC_GATHER_ROWS, SC_GATHER_ROWS)])

    return pl.kernel(
        body, out_type=jax.ShapeDtypeStruct((items * cap, width), hf2.dtype), mesh=mesh,
        scratch_types=[pltpu.VMEM((n,), jnp.int32), pltpu.VMEM((cap,), jnp.int32),
                       pltpu.VMEM((SC_GATHER_ROWS, width), hf2.dtype), pltpu.SemaphoreType.DMA],
        compiler_params=pltpu.CompilerParams(needs_layout_passes=False),
        name="scgather",
    )(rsel_t.reshape(items, n), hf2)


def _ffn_kernel(x_ref, wg_ref, wu_ref, wd_ref, o_ref, acc_ref, xb_ref):
    f = pl.program_id(1)
    b = x_ref.shape[0]
    wg = wg_ref[0].astype(BF16)
    wu = wu_ref[0].astype(BF16)
    wd = wd_ref[0].astype(BF16)

    @pl.when(f == 0)
    def _():
        acc_ref[...] = jnp.zeros_like(acc_ref)
        for i in range(b):
            xb_ref[i] = _unpack_bf16_pairs(x_ref[i, 0])

    for i in range(b):
        x = xb_ref[i]
        g = _dot(x, wg)
        u = _dot(x, wu)
        hid = (g / (1.0 + jnp.exp(-g)) * u).astype(BF16)
        acc_ref[i] += _dot(hid, wd)

    @pl.when(f == pl.num_programs(1) - 1)
    def _():
        for i in range(b):
            o_ref[i, 0] = acc_ref[i].astype(o_ref.dtype)


def _ffn(xs, w_gate, w_up, w_down):
    b, e, cap, dp = xs.shape
    d = 2 * dp
    dff = w_gate.shape[2]
    tf = FFN_TILE
    return pl.pallas_call(
        _ffn_kernel,
        out_shape=jax.ShapeDtypeStruct((b, e, cap, d), BF16),
        grid=(e, dff // tf),
        in_specs=[pl.BlockSpec((b, 1, cap, dp), lambda i, f: (0, i, 0, 0)),
                  pl.BlockSpec((1, d, tf), lambda i, f: (i, 0, f)),
                  pl.BlockSpec((1, d, tf), lambda i, f: (i, 0, f)),
                  pl.BlockSpec((1, tf, d), lambda i, f: (i, f, 0))],
        out_specs=pl.BlockSpec((b, 1, cap, d), lambda i, f: (0, i, 0, 0)),
        scratch_shapes=[pltpu.VMEM((b, cap, d), F32), pltpu.VMEM((b, cap, d), BF16)],
        compiler_params=_cparams(("arbitrary", "arbitrary")),
        name="ffn",
    )(xs, w_gate, w_up, w_down)


def _combine_kernel(ys_ref, rc_ref, aff_ref, xn_ref, gf_ref, npost_ref, o_ref):
    tm = rc_ref.shape[1]
    cap = ys_ref.shape[2]
    slot = lax.broadcasted_iota(jnp.int32, (tm, cap), 1)
    acc = jnp.zeros(o_ref.shape[1:], F32)
    for e in range(ys_ref.shape[1]):
        onehot = jnp.where(rc_ref[0, :, e:e + 1] == slot, 1.0, 0.0).astype(BF16)
        acc = acc + aff_ref[0, :, e:e + 1] * _dot(onehot, ys_ref[0, e])
    o_ref[0] = xn_ref[0] + gf_ref[0] * _rms(acc, npost_ref[...])


def _combine(ys, rsel_c, aff, x_new, gf, npost):
    b, e, cap, d = ys.shape
    n = x_new.shape[1]
    tm = TOK_TILE
    tokmap = lambda i, j: (i, j, 0)
    return pl.pallas_call(
        _combine_kernel,
        out_shape=jax.ShapeDtypeStruct((b, n, d), F32),
        grid=(b, n // tm),
        in_specs=[pl.BlockSpec((1, e, cap, d), lambda i, j: (i, 0, 0, 0)),
                  pl.BlockSpec((1, tm, LANES), tokmap),
                  pl.BlockSpec((1, tm, LANES), tokmap),
                  pl.BlockSpec((1, tm, d), tokmap),
                  pl.BlockSpec((1, 1, d), lambda i, j: (i, 0, 0)),
                  pl.BlockSpec((1, d), lambda i, j: (0, 0))],
        out_specs=pl.BlockSpec((1, tm, d), tokmap),
        compiler_params=_cparams(("arbitrary", "arbitrary")),
        name="combine",
    )(ys, rsel_c, aff, x_new, gf, npost.reshape(1, d))


def kernel(x, c, ctx, c_ctx, w_mod, b_mod, norm_mix_pre, norm_mix_post, norm_ffn_pre, norm_ffn_post,
           w_in, na_rpb, gla_a_up, gla_a_bias, gla_norm, w_out, router, w_gate, w_up, w_down):
    b, n, d = x.shape
    assert w_mod.shape[0] == 1 and d == D_MODEL and n % (GRID_W * NA_QROWS) == 0 and n % GLA_T == 0
    assert ctx.shape[1] == GLA_T
    rows = n // GRID_W
    cap = EC_CAPACITY_FACTOR * n // N_EXPERTS

    c8 = jnp.concatenate([c, c_ctx[None, :], jnp.zeros((8 - b - 1, d), F32)], axis=0)
    mod = _mod(c8, w_mod[0], b_mod[0])
    sh_m, sc_m, g_m, sh_f, sc_f, g_f = [m[:b, None, :] for m in jnp.split(mod, 6, axis=-1)]
    sh_c, sc_c = mod[b:b + 1, None, :d], mod[b:b + 1, None, d:2 * d]

    wb = w_in[0].astype(BF16)
    cuts = np.cumsum([0, NA_W, NA_W, GLA_KEY_W, GLA_VAL_W, 2 * GLA_GATE_RANK, NA_W, GLA_KEY_W, GLA_VAL_W])
    w_nak, w_nav, w_gk, w_gv, w_ad, w_naq, w_gq, w_gg = [wb[:, cuts[i]:cuts[i + 1]] for i in range(8)]
    w_gvt = w_gv.T
    w_ad3 = jnp.concatenate([w_ad, w_ad, w_ad, jnp.zeros_like(w_ad)], axis=1)
    rope = _rope_tables(n)
    na_q, na_k, na_v, gq, gk, gv, gvt, ad, gg = _inproj(
        x, sh_m, sc_m, norm_mix_pre[0],
        [w_naq, w_nak, w_nav, w_gq, w_gk, w_gv, w_gvt, w_ad3, w_gg],
        ["plain", "plain", "plain", "rope", "rope", "plain", "t", "split", "plain"],
        [1.0, 1.0, 1.0, GLA_DK ** -0.5, 1.0, 1.0, 1.0, 1.0, 1.0],
        [BF16] * 9, rope=rope)
    c_nak, c_nav, c_gk, c_gvt, c_ad = _inproj(
        ctx, sh_c, sc_c, norm_mix_pre[0],
        [w_nak, w_nav, w_gk, w_gvt, w_ad3],
        ["plain", "plain", "plain", "t", "split"], [1.0] * 5, [BF16] * 5)

    o_na = _na(na_q, na_k, na_v, c_nak, c_nav, *_na_bias_tables(na_rpb[0], rows))

    cmats = jnp.asarray(_gla_prefix_matrices(GLA_T), BF16)
    o_gla = _gla(gq, gk, gv, gvt, ad, gg, c_gk, c_gvt, c_ad, _gla_decay_up(gla_a_up[0]),
                 gla_a_bias[0][:, None, :], gla_norm[0][None, :], cmats)

    wo = w_out[0].astype(BF16)
    router_pad = jnp.zeros((d, LANES), F32).at[:, :N_EXPERTS].set(router[0])
    x_new, hf, aff, aff_t = _outproj(o_na, o_gla, x, wo[:NA_W], wo[NA_W:], g_m, sh_f, sc_f,
                                     norm_mix_post[0], norm_ffn_pre[0],
                                     jnp.concatenate(_split_bf16(router_pad), axis=1))

    rsel_t, rsel_c = _route(aff_t, cap)
    xs = _sc_gather(rsel_t, hf.reshape(b * n, d // 2), cap).reshape(b, N_EXPERTS, cap, d // 2)
    ys = _ffn(xs, w_gate[0], w_up[0], w_down[0])
    return _combine(ys, rsel_c, aff, x_new, g_f, norm_ffn_post[0])
```

```python
import functools

import numpy as np
import jax
import jax.numpy as jnp
from jax import lax
from jax.experimental import pallas as pl
from jax.experimental.pallas import tpu as pltpu
from jax.experimental.pallas import tpu_sc as plsc

F32 = jnp.float32
BF16 = jnp.bfloat16
HIGHEST = lax.Precision.HIGHEST

D_MODEL = 1024
GRID_W = 64
NA_W = 512
NA_HEADS = 8
NA_HEAD_DIM = 64
NA_WIN_ROWS = 8
NA_WIN_COLS = 16
GLA_HEADS = 4
GLA_DV = 128
GLA_DK = 64
GLA_KEY_W = 256
GLA_VAL_W = 512
GLA_GATE_RANK = 16
GLA_GATE_TAU = 16.0
ROPE_BASE = 10000.0
N_EXPERTS = 16
EC_CAPACITY_FACTOR = 2
NORM_EPS = 1e-6
NEG_BIG = -1e30

LANES = 128
BF16_ROWS = 16
VMEM_LIMIT = 56 * 1024 * 1024

TOK_TILE = 512
NA_QROWS = 4
NA_KROWS = NA_QROWS + NA_WIN_ROWS
NA_TAB_PAD = NA_QROWS
NA_TAB_BLOCKS = NA_TAB_PAD + 2 * NA_WIN_ROWS - 1 + NA_QROWS + 1
GLA_T = 256
GLA_LEVELS = (128, 64, 32, 16)
GLA_DIAG = 16
FFN_TILE = 256
SLOT_TILE = 128
COMBINE_WIN = 128
SC_GATHER_ROWS = 64


def _cparams(sem):
    return pltpu.CompilerParams(dimension_semantics=sem, vmem_limit_bytes=VMEM_LIMIT)


def _rms(v, g):
    return v * lax.rsqrt(jnp.mean(v * v, axis=-1, keepdims=True) + NORM_EPS) * g


def _dot(a, b):
    return jnp.dot(a, b, preferred_element_type=F32)


def _dot_nt(a, b):
    return lax.dot_general(a, b, (((1,), (1,)), ((), ())), preferred_element_type=F32)


def _split_bf16(v):
    hi = v.astype(BF16)
    return hi, (v - hi.astype(F32)).astype(BF16)


_HIGH_HALF = -65536


def _pack_bf16_pairs(hb):
    bits = lax.bitcast_convert_type(hb.astype(F32), jnp.int32)
    half = hb.shape[1] // 2
    return lax.shift_right_logical(bits[:, :half], 16) | (bits[:, half:] & _HIGH_HALF)


def _unpack_bf16_pairs(w):
    lo = lax.bitcast_convert_type(lax.shift_left(w, 16), F32).astype(BF16)
    hi = lax.bitcast_convert_type(w & _HIGH_HALF, F32).astype(BF16)
    return jnp.concatenate([lo, hi], axis=1)


def _mod_kernel(c_ref, w_ref, b_ref, o_ref):
    c = c_ref[...]
    s = c / (1.0 + jnp.exp(-c))
    o_ref[...] = jnp.dot(s, w_ref[...], preferred_element_type=F32, precision=HIGHEST) + b_ref[...]


def _mod(c8, w_mod, b_mod):
    d, n = w_mod.shape
    tn = 1536
    return pl.pallas_call(
        _mod_kernel,
        out_shape=jax.ShapeDtypeStruct((8, n), F32),
        grid=(n // tn,),
        in_specs=[pl.BlockSpec((8, d), lambda j: (0, 0)),
                  pl.BlockSpec((d, tn), lambda j: (0, j)),
                  pl.BlockSpec((1, tn), lambda j: (0, j))],
        out_specs=pl.BlockSpec((8, tn), lambda j: (0, j)),
        compiler_params=_cparams(("arbitrary",)),
        name="mod",
    )(c8, w_mod, b_mod.reshape(1, n))


def _inproj_kernel(x_ref, sh_ref, sc_ref, g_ref, *refs, modes, scales):
    n_rope = 2 if "rope" in modes else 0
    rope_refs, refs = refs[:n_rope], refs[n_rope:]
    n_w = len(modes)
    w_refs, o_refs = refs[:n_w], refs[n_w:]
    x = x_ref[0]
    h = _rms(x, g_ref[...]) * (1.0 + sc_ref[0]) + sh_ref[0]
    hb = h.astype(BF16)
    for w_ref, o_ref, mode, scale in zip(w_refs, o_refs, modes, scales):
        if mode == "t":
            o_ref[0] = _dot_nt(w_ref[...], hb).astype(o_ref.dtype)
            continue
        y = _dot(hb, w_ref[...])
        if mode == "rope":
            cols = y.shape[1]
            quarter = GLA_DK // 4
            lane = lax.broadcasted_iota(jnp.int32, y.shape, 1)
            partner = jnp.where((lane & (2 * quarter - 1)) < quarter,
                                pltpu.roll(y, cols - quarter, axis=1), pltpu.roll(y, quarter, axis=1))
            y = (y * rope_refs[0][...] + partner * rope_refs[1][...]) * scale
        elif mode == "split":
            hi, lo = _split_bf16(y)
            lane = lax.broadcasted_iota(jnp.int32, y.shape, 1)
            rank2 = 2 * GLA_GATE_RANK
            y = jnp.where((lane >= rank2) & (lane < 2 * rank2), lo, hi)
        o_ref[0] = y.astype(o_ref.dtype)


def _inproj(x, shift, scale, g, weights, modes, scales, out_dtypes, rope=None):
    b, n, d = x.shape
    tm = min(TOK_TILE, n)
    per_sample = shift.shape[0] == b
    mod_map = (lambda i, j: (i, 0, 0)) if per_sample else (lambda i, j: (0, 0, 0))
    in_specs = [pl.BlockSpec((1, tm, d), lambda i, j: (i, j, 0)),
                pl.BlockSpec((1, 1, d), mod_map),
                pl.BlockSpec((1, 1, d), mod_map),
                pl.BlockSpec((1, d), lambda i, j: (0, 0))]
    args = [x, shift, scale, g.reshape(1, d)]
    if rope is not None:
        in_specs += [pl.BlockSpec((tm, rope[0].shape[1]), lambda i, j: (j, 0))] * 2
        args += list(rope)
    out_shapes, out_specs = [], []
    for w, mode, dt in zip(weights, modes, out_dtypes):
        in_specs.append(pl.BlockSpec(w.shape, lambda i, j: (0, 0)))
        if mode == "t":
            cols = w.shape[0]
            out_shapes.append(jax.ShapeDtypeStruct((b, cols, n), dt))
            out_specs.append(pl.BlockSpec((1, cols, tm), lambda i, j: (i, 0, j)))
        else:
            cols = w.shape[1]
            out_shapes.append(jax.ShapeDtypeStruct((b, n, cols), dt))
            out_specs.append(pl.BlockSpec((1, tm, cols), lambda i, j: (i, j, 0)))
    return pl.pallas_call(
        functools.partial(_inproj_kernel, modes=tuple(modes), scales=tuple(scales)),
        out_shape=out_shapes,
        grid=(b, n // tm),
        in_specs=in_specs,
        out_specs=out_specs,
        compiler_params=_cparams(("arbitrary", "arbitrary")),
        name="inproj",
    )(*args, *weights)


def _rope_tables(n):
    t = jnp.arange(n)
    pos_row, pos_col = (t // GRID_W).astype(F32), (t % GRID_W).astype(F32)
    quarter = GLA_DK // 4
    freqs = ROPE_BASE ** (-jnp.arange(quarter, dtype=F32) / quarter)
    ang_r = pos_row[:, None] * freqs
    ang_c = pos_col[:, None] * freqs
    cos = jnp.concatenate([jnp.cos(ang_r), jnp.cos(ang_r), jnp.cos(ang_c), jnp.cos(ang_c)], axis=-1)
    sin = jnp.concatenate([-jnp.sin(ang_r), jnp.sin(ang_r), -jnp.sin(ang_c), jnp.sin(ang_c)], axis=-1)
    return jnp.tile(cos, (1, GLA_HEADS)), jnp.tile(sin, (1, GLA_HEADS))


def _na_patterns(rows):
    kr = min(NA_WIN_ROWS, rows)
    n_blocks = rows // NA_QROWS
    pats = []
    for blk in (0, 1, n_blocks - 1):
        r0 = blk * NA_QROWS
        k0 = int(np.clip(r0 - kr // 2, 0, rows - NA_KROWS))
        strips = []
        for a in range(NA_QROWS):
            r_start = int(np.clip(r0 + a - kr // 2, 0, rows - kr))
            start = k0 - (r0 + a) + NA_WIN_ROWS - 1 + NA_TAB_PAD
            assert 0 <= start and start + NA_KROWS <= NA_TAB_BLOCKS
            strips.append((start, [r_start <= k0 + c < r_start + kr for c in range(NA_KROWS)]))
        pats.append(strips)
    return pats


def _na_bias_tables(rpb, rows):
    heads = rpb.shape[0]
    col = np.arange(GRID_W)
    c_start = np.clip(col - NA_WIN_COLS // 2, 0, GRID_W - NA_WIN_COLS)
    col_ok = (col[None, :] >= c_start[:, None]) & (col[None, :] < c_start[:, None] + NA_WIN_COLS)
    dc = np.clip(col[None, :] - col[:, None] + NA_WIN_COLS - 1, 0, 2 * NA_WIN_COLS - 2)
    sel_c = (np.arange(2 * NA_WIN_COLS - 1)[:, None, None] == dc[None]) & col_ok[None]
    t = jnp.einsum("hrd,dqk->hqrk", rpb, jnp.asarray(sel_c, F32), precision=HIGHEST)
    t = jnp.where(jnp.asarray(col_ok)[None, :, None, :], t, NEG_BIG)
    n_dr = 2 * NA_WIN_ROWS - 1
    t = t.reshape(heads, GRID_W, n_dr * GRID_W)
    back = NA_TAB_BLOCKS + 1 - NA_TAB_PAD - n_dr
    t = jnp.pad(t, ((0, 0), (0, 0), (NA_TAB_PAD * GRID_W, back * GRID_W)), constant_values=NEG_BIG)
    width = NA_TAB_BLOCKS * GRID_W
    tab = jnp.stack([t[:, :, :width], t[:, :, GRID_W:GRID_W + width]], axis=1)
    row_mask = np.zeros((3, NA_QROWS, 1, NA_KROWS * GRID_W), np.float32)
    for pat, strips in enumerate(_na_patterns(rows)):
        for a, (_, valid) in enumerate(strips):
            row_mask[pat, a, 0] = np.repeat(np.where(valid, 0.0, NEG_BIG), GRID_W)
    return tab, jnp.asarray(row_mask)


def _na_kernel(q_ref, k_ref, v_ref, kc_ref, vc_ref, tab_ref, rmask_ref, o_ref, bias_ref,
               sw_ref, sc_ref, pw_ref, pc_ref, linv_ref, *, rows):
    nq, nk = NA_QROWS * GRID_W, NA_KROWS * GRID_W
    n_blocks = rows // NA_QROWS
    kr = min(NA_WIN_ROWS, rows)
    scale = NA_HEAD_DIM ** -0.5
    lane = lax.broadcasted_iota(jnp.int32, (nq, LANES), 1)
    first_head = lane < NA_HEAD_DIM
    kc = kc_ref[0]
    vc = vc_ref[0]
    for pat, strips in enumerate(_na_patterns(rows)):
        for a, (start, _) in enumerate(strips):
            parity = start % 2
            off = (start - parity) * GRID_W
            for h in range(2):
                bias_ref[0, pat, h, a * GRID_W:(a + 1) * GRID_W, :] = (
                    tab_ref[h, parity, :, off:off + nk] + rmask_ref[pat, a])

    def key_start(i):
        return pl.multiple_of(jnp.clip(i * NA_QROWS - kr // 2, 0, rows - NA_KROWS) * GRID_W, GRID_W)

    def scores(i, slot):
        pat = jnp.where(i == 0, 0, jnp.where(i == n_blocks - 1, 2, 1))
        q = q_ref[0, pl.ds(pl.multiple_of(i * nq, nq), nq), :] * scale
        kw = k_ref[0, pl.ds(key_start(i), nk), :]
        for h in range(2):
            qh = jnp.where(first_head if h == 0 else jnp.logical_not(first_head), q, jnp.zeros_like(q))
            sw_ref[slot, h] = _dot_nt(qh, kw) + bias_ref[0, pat, h]
            sc_ref[slot, h] = _dot_nt(qh, kc)

    def softmax(slot):
        for h in range(2):
            s_w = sw_ref[slot, h]
            s_c = sc_ref[slot, h]
            m = jnp.maximum(jnp.max(s_w, axis=-1, keepdims=True), jnp.max(s_c, axis=-1, keepdims=True))
            p_w = jnp.exp(s_w - m)
            p_c = jnp.exp(s_c - m)
            l = jnp.sum(p_w, axis=-1, keepdims=True) + jnp.sum(p_c, axis=-1, keepdims=True)
            pw_ref[slot, h] = p_w.astype(BF16)
            pc_ref[slot, h] = p_c.astype(BF16)
            linv_ref[slot, h] = jnp.broadcast_to(1.0 / l, (nq, LANES))

    def values(i, slot):
        vw = v_ref[0, pl.ds(key_start(i), nk), :]
        outs = [(_dot(pw_ref[slot, h], vw) + _dot(pc_ref[slot, h], vc)) * linv_ref[slot, h] for h in range(2)]
        o = jnp.where(first_head, outs[0], outs[1])
        o_ref[0, pl.ds(pl.multiple_of(i * nq, nq), nq), :] = o.astype(o_ref.dtype)

    assert n_blocks % 2 == 0 and n_blocks >= 4
    scores(0, 0)
    softmax(0)
    scores(1, 1)

    def trip(j, carry):
        i = 2 * j
        values(i - 2, 0)
        softmax(1)
        scores(i, 0)
        values(i - 1, 1)
        softmax(0)
        scores(i + 1, 1)
        return carry

    lax.fori_loop(1, n_blocks // 2, trip, 0)
    values(n_blocks - 2, 0)
    softmax(1)
    values(n_blocks - 1, 1)


def _na(q, k, v, kc, vc, tab, row_mask):
    b, n, w = q.shape
    n_ctx = kc.shape[1]
    pairs = w // LANES
    rows = n // GRID_W
    nq, nk = NA_QROWS * GRID_W, NA_KROWS * GRID_W
    tok = lambda i, p: (i, 0, p)
    return pl.pallas_call(
        functools.partial(_na_kernel, rows=rows),
        out_shape=jax.ShapeDtypeStruct((b, n, w), BF16),
        grid=(b, pairs),
        in_specs=[pl.BlockSpec((1, n, LANES), tok),
                  pl.BlockSpec((1, n, LANES), tok),
                  pl.BlockSpec((1, n, LANES), tok),
                  pl.BlockSpec((1, n_ctx, LANES), tok),
                  pl.BlockSpec((1, n_ctx, LANES), tok),
                  pl.BlockSpec((2,) + tab.shape[1:], lambda i, p: (p, 0, 0, 0)),
                  pl.BlockSpec(row_mask.shape, lambda i, p: (0, 0, 0, 0))],
        out_specs=pl.BlockSpec((1, n, LANES), tok),
        scratch_shapes=[pltpu.VMEM((1, 3, 2, nq, nk), F32),
                        pltpu.VMEM((2, 2, nq, nk), F32), pltpu.VMEM((2, 2, nq, n_ctx), F32),
                        pltpu.VMEM((2, 2, nq, nk), BF16), pltpu.VMEM((2, 2, nq, n_ctx), BF16),
                        pltpu.VMEM((2, 2, nq, LANES), F32)],
        compiler_params=_cparams(("arbitrary", "arbitrary")),
        name="na",
    )(q, k, v, kc, vc, tab, row_mask)


def _gla_prefix_matrices(t):
    i = np.arange(t)
    return np.stack([i[:, None] >= i[None, :], i[:, None] <= i[None, :]]).astype(np.float32)


def _gla_kernel(q_ref, k_ref, v_ref, vt_ref, ad_ref, g_ref, ck_ref, cvt_ref, cad_ref,
                u_ref, ab_ref, gn_ref, cm_ref, o_ref, accf_ref, accb_ref, *, n_tok):
    t = GLA_T
    n_chunks = n_tok // t
    row = lax.broadcasted_iota(jnp.int32, (t, LANES), 0)
    lane = lax.broadcasted_iota(jnp.int32, (t, LANES), 1)
    head0_l = lane < GLA_DK
    row2 = lax.broadcasted_iota(jnp.int32, (t, 2 * t), 0)
    col2 = lax.broadcasted_iota(jnp.int32, (t, 2 * t), 1) & (t - 1)
    vrow = lax.broadcasted_iota(jnp.int32, (2 * t, 2 * GLA_DV), 0)
    vlane = lax.broadcasted_iota(jnp.int32, (2 * t, 2 * GLA_DV), 1)
    v_head_match = (vrow >= t) == (vlane >= GLA_DV)
    srow = lax.broadcasted_iota(jnp.int32, (2 * GLA_DV, LANES), 0)
    slane = lax.broadcasted_iota(jnp.int32, (2 * GLA_DV, LANES), 1)
    s_blockdiag = (srow >= GLA_DV) == (slane >= GLA_DK)
    blk_mask = {half: jnp.where((row2 & ~(2 * half - 1)) == (col2 & ~(2 * half - 1)), 1.0, 0.0)
                for half in GLA_LEVELS if 2 * half < t}
    diag_blk = (row2 & ~(GLA_DIAG - 1)) == (col2 & ~(GLA_DIAG - 1))
    diag_mask = (jnp.where(diag_blk & (row2 >= col2), 1.0, 0.0), jnp.where(diag_blk & (row2 <= col2), 1.0, 0.0))

    def prefix_sums(ad, dirn):
        z = _dot(ad, u_ref[dirn]) + ab_ref[dirn]
        loga = (jnp.minimum(z, 0.0) - jnp.log(1.0 + jnp.exp(-jnp.abs(z)))) * (1.0 / GLA_GATE_TAU)
        hi, lo = _split_bf16(loga)
        p2 = _dot(cm_ref[dirn], jnp.concatenate([hi, lo], axis=-1))
        return p2[:, :LANES] + p2[:, LANES:]

    def chunk_end(p, dirn):
        return p[t - 1:t, :] if dirn == 0 else p[0:1, :]

    def level_sums(p, half, dirn):
        blk = 2 * half
        p3 = p.reshape(t // blk, blk, LANES)
        edge = half - 1 if dirn == 0 else half
        ref = jnp.broadcast_to(p3[:, edge:edge + 1, :], p3.shape).reshape(t, LANES)
        later = (row & half) != 0
        return jnp.where(later == (dirn == 0), p - ref, ref - p)

    def state_update(s, k, vt, p, dirn):
        kh = (k * jnp.exp(chunk_end(p, dirn) - p)).astype(BF16)
        return s * jnp.exp(chunk_end(p, dirn)) + jnp.where(s_blockdiag, _dot(vt, kh), 0.0)

    def heads_stacked(kt):
        return jnp.concatenate([jnp.where(head0_l, kt, 0.0), jnp.where(head0_l, 0.0, kt)], axis=0).astype(BF16)

    def chunk(tok0, s, dirn):
        q = q_ref[0, pl.ds(tok0, t), :].astype(F32)
        k = k_ref[0, pl.ds(tok0, t), :].astype(F32)
        v = v_ref[0, pl.ds(tok0, t), :]
        vt = vt_ref[0, :, pl.ds(tok0, t)]
        p = prefix_sums(ad_ref[0, pl.ds(tok0, t), :], dirn)
        qh = (q * jnp.exp(p)).astype(BF16)
        o = _dot_nt(qh, s.astype(BF16))
        a = None
        for half in GLA_LEVELS:
            w = jnp.exp(level_sums(p, half, dirn))
            later = (row & half) != 0
            q_side = later if dirn == 0 else jnp.logical_not(later)
            qt = jnp.where(q_side, q * w, 0.0).astype(BF16)
            kt = jnp.where(q_side, 0.0, k * w)
            part = _dot_nt(qt, heads_stacked(kt))
            if half in blk_mask:
                part = part * blk_mask[half]
            a = part if a is None else a + part
        e_d = level_sums(p, GLA_DIAG // 2, dirn)
        w, wi = jnp.exp(e_d), jnp.exp(-e_d)
        later = (row & (GLA_DIAG // 2)) != 0
        shrink_q = later if dirn == 0 else jnp.logical_not(later)
        qt = (q * jnp.where(shrink_q, w, wi)).astype(BF16)
        kt = k * jnp.where(shrink_q, wi, w)
        a = a + jnp.where(diag_mask[dirn] > 0.5, _dot_nt(qt, heads_stacked(kt)), 0.0)
        vcat = jnp.concatenate([v, v], axis=0)
        vcat = jnp.where(v_head_match, vcat, jnp.zeros_like(vcat))
        o = o + _dot(a.astype(BF16), vcat)
        return o, state_update(s, k, vt, p, dirn)

    def ctx_state(dirn):
        p = prefix_sums(cad_ref[0], dirn)
        s0 = jnp.zeros((2 * GLA_DV, LANES), F32)
        return state_update(s0, ck_ref[0].astype(F32), cvt_ref[0], p, dirn)

    def body(i, carry):
        s_f, s_b = carry
        tok_f = pl.multiple_of(i * t, t)
        tok_b = pl.multiple_of((n_chunks - 1 - i) * t, t)
        o_f, s_f = chunk(tok_f, s_f, 0)
        o_b, s_b = chunk(tok_b, s_b, 1)
        accf_ref[pl.ds(tok_f, t), :] = o_f
        accb_ref[pl.ds(tok_b, t), :] = o_b
        return s_f, s_b

    lax.fori_loop(0, n_chunks, body, (ctx_state(0), ctx_state(1)))

    def finish(c, carry):
        tok0 = pl.multiple_of(c * t, t)
        o = accf_ref[pl.ds(tok0, t), :] + accb_ref[pl.ds(tok0, t), :]
        g = g_ref[0, pl.ds(tok0, t), :].astype(F32)
        gate = g / (1.0 + jnp.exp(-g))
        halves = [_rms(o[:, h * GLA_DV:(h + 1) * GLA_DV], gn_ref[...]) for h in range(2)]
        o_ref[0, pl.ds(tok0, t), :] = (jnp.concatenate(halves, axis=-1) * gate).astype(o_ref.dtype)
        return carry

    lax.fori_loop(0, n_chunks, finish, 0)


def _gla(q, k, v, vt, ad, g, ck, cvt, cad, u, abias, gnorm, cmats):
    b, n, kw = q.shape
    n_ctx = ck.shape[1]
    pairs = kw // LANES
    vw = 2 * GLA_DV
    tok = lambda i, p: (i, 0, p)
    full3 = lambda i, p: (i, 0, 0)
    return pl.pallas_call(
        functools.partial(_gla_kernel, n_tok=n),
        out_shape=jax.ShapeDtypeStruct((b, n, v.shape[2]), BF16),
        grid=(b, pairs),
        in_specs=[pl.BlockSpec((1, n, LANES), tok),
                  pl.BlockSpec((1, n, LANES), tok),
                  pl.BlockSpec((1, n, vw), tok),
                  pl.BlockSpec((1, vw, n), lambda i, p: (i, p, 0)),
                  pl.BlockSpec((1, n, LANES), full3),
                  pl.BlockSpec((1, n, vw), tok),
                  pl.BlockSpec((1, n_ctx, LANES), tok),
                  pl.BlockSpec((1, vw, n_ctx), lambda i, p: (i, p, 0)),
                  pl.BlockSpec((1, n_ctx, LANES), full3),
                  pl.BlockSpec((2, LANES, LANES), lambda i, p: (0, 0, p)),
                  pl.BlockSpec((2, 1, LANES), lambda i, p: (0, 0, p)),
                  pl.BlockSpec((1, GLA_DV), lambda i, p: (0, 0)),
                  pl.BlockSpec(cmats.shape, lambda i, p: (0, 0, 0))],
        out_specs=pl.BlockSpec((1, n, vw), tok),
        scratch_shapes=[pltpu.VMEM((n, vw), F32), pltpu.VMEM((n, vw), F32)],
        compiler_params=_cparams(("arbitrary", "arbitrary")),
        name="gla",
    )(q, k, v, vt, ad, g, ck, cvt, cad, u, abias, gnorm, cmats)


def _gla_decay_up(a_up):
    r = GLA_GATE_RANK
    pad = jnp.zeros((2, 2 * r, GLA_KEY_W), F32)
    pad = pad.at[0, :r].set(a_up[0]).at[1, r:].set(a_up[1])
    hi, lo = _split_bf16(pad)
    return jnp.concatenate([hi, hi, lo, jnp.zeros_like(hi)], axis=1)


def _outproj_kernel(ona_ref, ogla_ref, x_ref, w1_ref, w2_ref, gm_ref, shf_ref, scf_ref, npost_ref,
                    nfpre_ref, rt_ref, xnew_ref, hf_ref, aff_ref, afft_ref):
    mix = _dot(ona_ref[0], w1_ref[...]) + _dot(ogla_ref[0], w2_ref[...])
    xn = x_ref[0] + gm_ref[0] * _rms(mix, npost_ref[...])
    xnew_ref[0] = xn
    h = _rms(xn, nfpre_ref[...]) * (1.0 + scf_ref[0]) + shf_ref[0]
    h_hi, h_lo = _split_bf16(h)
    hf_ref[0] = _pack_bf16_pairs(h_hi)
    res = _dot(h_hi, rt_ref[...])
    logits = res[:, :LANES] + res[:, LANES:] + _dot(h_lo, rt_ref[:, :LANES])
    lane = lax.broadcasted_iota(jnp.int32, logits.shape, 1)
    logits = jnp.where(lane < N_EXPERTS, logits, NEG_BIG)
    p = jnp.exp(logits - jnp.max(logits, axis=-1, keepdims=True))
    aff = p / jnp.sum(p, axis=-1, keepdims=True)
    aff_ref[0] = aff
    afft_ref[0] = aff.T[:N_EXPERTS, :]


def _outproj(o_na, o_gla, x, w1, w2, gm, shf, scf, npost, nfpre, router_cat):
    b, n, d = x.shape
    tm = TOK_TILE
    tokmap = lambda i, j: (i, j, 0)
    smp = lambda i, j: (i, 0, 0)
    cst = lambda i, j: (0, 0)
    return pl.pallas_call(
        _outproj_kernel,
        out_shape=[jax.ShapeDtypeStruct((b, n, d), F32),
                   jax.ShapeDtypeStruct((b, n, d // 2), jnp.int32),
                   jax.ShapeDtypeStruct((b, n, LANES), F32),
                   jax.ShapeDtypeStruct((b, N_EXPERTS, n), F32)],
        grid=(b, n // tm),
        in_specs=[pl.BlockSpec((1, tm, o_na.shape[2]), tokmap),
                  pl.BlockSpec((1, tm, o_gla.shape[2]), tokmap),
                  pl.BlockSpec((1, tm, d), tokmap),
                  pl.BlockSpec(w1.shape, cst),
                  pl.BlockSpec(w2.shape, cst),
                  pl.BlockSpec((1, 1, d), smp),
                  pl.BlockSpec((1, 1, d), smp),
                  pl.BlockSpec((1, 1, d), smp),
                  pl.BlockSpec((1, d), cst),
                  pl.BlockSpec((1, d), cst),
                  pl.BlockSpec(router_cat.shape, cst)],
        out_specs=[pl.BlockSpec((1, tm, d), tokmap),
                   pl.BlockSpec((1, tm, d // 2), tokmap),
                   pl.BlockSpec((1, tm, LANES), tokmap),
                   pl.BlockSpec((1, N_EXPERTS, tm), lambda i, j: (i, 0, j))],
        compiler_params=_cparams(("arbitrary", "arbitrary")),
        name="outproj",
    )(o_na, o_gla, x, w1, w2, gm, shf, scf, npost.reshape(1, d), nfpre.reshape(1, d), router_cat)


def _route_kernel(afft_ref, rt_ref, rc_ref, *, cap):
    a = afft_ref[0]
    e, n = a.shape
    capf = jnp.float32(cap)

    def search(i, thr_bits):
        cand = thr_bits | lax.shift_left(jnp.int32(1), 30 - i)
        cnt = jnp.sum(jnp.where(a >= lax.bitcast_convert_type(cand, F32), 1.0, 0.0), axis=-1, keepdims=True)
        return jnp.where(cnt >= capf, cand, thr_bits)

    thr_bits = lax.fori_loop(0, 31, search, jnp.zeros((e, 1), jnp.int32))
    thr = lax.bitcast_convert_type(thr_bits, F32)
    need = capf - jnp.sum(jnp.where(a > thr, 1.0, 0.0), axis=-1, keepdims=True)
    r_i = lax.broadcasted_iota(jnp.int32, (LANES, LANES), 0)
    c_i = lax.broadcasted_iota(jnp.int32, (LANES, LANES), 1)
    incl = jnp.where(r_i <= c_i, 1.0, 0.0).astype(BF16)
    off_eq = jnp.zeros((e, 1), F32)
    off_sel = jnp.zeros((e, 1), F32)
    pad = jnp.full((LANES - e, LANES), -1.0, F32)
    for j in range(n // LANES):
        sl = slice(j * LANES, (j + 1) * LANES)
        a_b = a[:, sl]
        eq_b = jnp.where(a_b == thr, 1.0, 0.0)
        tie_rank = _dot(eq_b.astype(BF16), incl) - eq_b + off_eq
        off_eq = off_eq + jnp.sum(eq_b, axis=-1, keepdims=True)
        sel_b = jnp.where(a_b > thr, 1.0, jnp.where(tie_rank < need, eq_b, 0.0))
        sel = sel_b > 0.5
        rank = _dot(sel_b.astype(BF16), incl) - sel_b + off_sel
        off_sel = off_sel + jnp.sum(sel_b, axis=-1, keepdims=True)
        rsel = jnp.where(sel, rank, -1.0)
        rt_ref[0, :, sl] = rsel.astype(jnp.int32)
        rc_ref[0, sl, :] = jnp.concatenate([rsel, pad], axis=0).T.astype(jnp.int32)


def _route(afft, cap):
    b, e, n = afft.shape
    return pl.pallas_call(
        functools.partial(_route_kernel, cap=cap),
        out_shape=[jax.ShapeDtypeStruct((b, e, n), jnp.int32),
                   jax.ShapeDtypeStruct((b, n, LANES), jnp.int32)],
        grid=(b,),
        in_specs=[pl.BlockSpec((1, e, n), lambda i: (i, 0, 0))],
        out_specs=[pl.BlockSpec((1, e, n), lambda i: (i, 0, 0)),
                   pl.BlockSpec((1, n, LANES), lambda i: (i, 0, 0))],
        compiler_params=_cparams(("arbitrary",)),
        name="route",
    )(afft)


def _gather_kernel(r_ref, hf_ref, o_ref, *, cap):
    r = r_ref[0, 0]
    n = r.shape[1]
    slot = lax.broadcasted_iota(jnp.int32, (SLOT_TILE, n), 0)
    for s0 in range(0, cap, SLOT_TILE):
        onehot = jnp.where(r == slot + s0, 1.0, 0.0).astype(BF16)
        o_ref[0, 0, s0:s0 + SLOT_TILE, :] = _dot(onehot, hf_ref[0]).astype(o_ref.dtype)


def _gather(rsel_t, hf, cap):
    b, e, n = rsel_t.shape
    d = hf.shape[2]
    return pl.pallas_call(
        functools.partial(_gather_kernel, cap=cap),
        out_shape=jax.ShapeDtypeStruct((b, e, cap, d), BF16),
        grid=(b, e),
        in_specs=[pl.BlockSpec((1, 1, 1, n), lambda i, j: (i, j, 0, 0)),
                  pl.BlockSpec((1, n, d), lambda i, j: (i, 0, 0))],
        out_specs=pl.BlockSpec((1, 1, cap, d), lambda i, j: (i, j, 0, 0)),
        compiler_params=_cparams(("arbitrary", "arbitrary")),
        name="gather",
    )(rsel_t.reshape(b, e, 1, n), hf)


def _sc_gather(rsel_t, hf2, cap):
    b, e, n = rsel_t.shape
    width = hf2.shape[1]
    info = plsc.get_sparse_core_info()
    nc, lanes = info.num_cores, info.num_lanes
    workers = nc * info.num_subcores
    items = b * e
    assert items % workers == 0 and n % lanes == 0 and cap % SC_GATHER_ROWS == 0
    per_worker = items // workers
    mesh = plsc.VectorSubcoreMesh(core_axis_name="c", subcore_axis_name="s")

    def body(rank_hbm, hf_hbm, out_hbm, rank_v, idx_v, rows_v, sem):
        wid = lax.axis_index("s") * nc + lax.axis_index("c")
        for k in range(per_worker):
            item = wid * per_worker + k
            base_tok = (item // e) * n
            pltpu.sync_copy(rank_hbm.at[item], rank_v)

            @pl.loop(0, n // lanes)
            def _(j):
                r = rank_v[pl.ds(j * lanes, lanes)]
                tok = lax.iota(jnp.int32, lanes) + (j * lanes + base_tok)
                plsc.store_scatter(idx_v, [r], tok, mask=r >= 0)

            @pl.loop(0, cap // SC_GATHER_ROWS)
            def _(c):
                rows = pl.ds(c * SC_GATHER_ROWS, SC_GATHER_ROWS)
                pltpu.async_copy(hf_hbm.at[idx_v.at[rows]], rows_v, sem).wait()
                pltpu.sync_copy(rows_v, out_hbm.at[pl.ds(item * cap + c * SC_GATHER_ROWS, SC_GATHER_ROWS)])

    return pl.kernel(
        body, out_type=jax.ShapeDtypeStruct((items * cap, width), hf2.dtype), mesh=mesh,
        scratch_types=[pltpu.VMEM((n,), jnp.int32), pltpu.VMEM((cap,), jnp.int32),
                       pltpu.VMEM((SC_GATHER_ROWS, width), hf2.dtype), pltpu.SemaphoreType.DMA],
        compiler_params=pltpu.CompilerParams(needs_layout_passes=False),
        name="scgather",
    )(rsel_t.reshape(items, n), hf2)


def _ffn_kernel(x_ref, wg_ref, wu_ref, wd_ref, o_ref, acc_ref, xb_ref):
    f = pl.program_id(1)
    b = x_ref.shape[0]
    wg = wg_ref[0].astype(BF16)
    wu = wu_ref[0].astype(BF16)
    wd = wd_ref[0].astype(BF16)

    @pl.when(f == 0)
    def _():
        acc_ref[...] = jnp.zeros_like(acc_ref)
        for i in range(b):
            xb_ref[i] = _unpack_bf16_pairs(x_ref[i, 0])

    for i in range(b):
        x = xb_ref[i]
        g = _dot(x, wg)
        u = _dot(x, wu)
        hid = (g / (1.0 + jnp.exp(-g)) * u).astype(BF16)
        acc_ref[i] += _dot(hid, wd)

    @pl.when(f == pl.num_programs(1) - 1)
    def _():
        for i in range(b):
            o_ref[i, 0] = acc_ref[i].astype(o_ref.dtype)


def _ffn(xs, w_gate, w_up, w_down):
    b, e, cap, dp = xs.shape
    d = 2 * dp
    dff = w_gate.shape[2]
    tf = FFN_TILE
    return pl.pallas_call(
        _ffn_kernel,
        out_shape=jax.ShapeDtypeStruct((b, e, cap, d), BF16),
        grid=(e, dff // tf),
        in_specs=[pl.BlockSpec((b, 1, cap, dp), lambda i, f: (0, i, 0, 0)),
                  pl.BlockSpec((1, d, tf), lambda i, f: (i, 0, f)),
                  pl.BlockSpec((1, d, tf), lambda i, f: (i, 0, f)),
                  pl.BlockSpec((1, tf, d), lambda i, f: (i, f, 0))],
        out_specs=pl.BlockSpec((b, 1, cap, d), lambda i, f: (0, i, 0, 0)),
        scratch_shapes=[pltpu.VMEM((b, cap, d), F32), pltpu.VMEM((b, cap, d), BF16)],
        compiler_params=_cparams(("arbitrary", "arbitrary")),
        name="ffn",
    )(xs, w_gate, w_up, w_down)


def _combine_kernel(off_ref, ys_ref, rc_ref, aff_ref, xn_ref, gf_ref, npost_ref, o_ref, acc_ref):
    bi, tt = pl.program_id(0), pl.program_id(1)
    tm = rc_ref.shape[1]
    n_experts, cap = ys_ref.shape[1], ys_ref.shape[2]
    blocks = tm // LANES
    slot = lax.broadcasted_iota(jnp.int32, (tm, COMBINE_WIN), 1)

    def window(e, w0, j):
        nominal = w0 + j * COMBINE_WIN
        start = pl.multiple_of(jnp.minimum(nominal, cap - COMBINE_WIN), BF16_ROWS)
        rank = rc_ref[0, :, e:e + 1]
        hit = ((rank - start) == slot) & (rank >= nominal)
        weights = jnp.where(hit, aff_ref[0, :, e:e + 1], 0.0).astype(BF16)
        return weights, ys_ref[0, e, pl.ds(start, COMBINE_WIN), :]

    first, extra = [], []
    for e in range(n_experts):
        r0 = off_ref[bi, e, tt * blocks]
        r1 = off_ref[bi, e, (tt + 1) * blocks]
        w0 = (r0 // BF16_ROWS) * BF16_ROWS
        first.append(w0)
        extra.append(jnp.maximum((r1 - w0 + COMBINE_WIN - 1) // COMBINE_WIN - 1, 0))
    terms = []
    for e in range(0, n_experts, 2):
        (wa, ya), (wb, yb) = window(e, first[e], 0), window(e + 1, first[e + 1], 0)
        terms.append(_dot(jnp.concatenate([wa, wb], axis=1), jnp.concatenate([ya, yb], axis=0)))
    acc_ref[...] = functools.reduce(lambda a, c: a + c, terms)

    @pl.when(functools.reduce(lambda a, c: a + c, extra) > 0)
    def _():
        for e in range(n_experts):
            def more(j, carry, e=e):
                acc_ref[...] += _dot(*window(e, first[e], j))
                return carry
            lax.fori_loop(1, extra[e] + 1, more, 0)

    o_ref[0] = xn_ref[0] + gf_ref[0] * _rms(acc_ref[...], npost_ref[...])


def _combine(ys, slot_off, rsel_c, aff, x_new, gf, npost):
    b, e, cap, d = ys.shape
    n = x_new.shape[1]
    tm = TOK_TILE
    tokmap = lambda i, j, off: (i, j, 0)
    return pl.pallas_call(
        _combine_kernel,
        out_shape=jax.ShapeDtypeStruct((b, n, d), F32),
        grid_spec=pltpu.PrefetchScalarGridSpec(
            num_scalar_prefetch=1,
            grid=(b, n // tm),
            in_specs=[pl.BlockSpec((1, e, cap, d), lambda i, j, off: (i, 0, 0, 0)),
                      pl.BlockSpec((1, tm, LANES), tokmap),
                      pl.BlockSpec((1, tm, LANES), tokmap),
                      pl.BlockSpec((1, tm, d), tokmap),
                      pl.BlockSpec((1, 1, d), lambda i, j, off: (i, 0, 0)),
                      pl.BlockSpec((1, d), lambda i, j, off: (0, 0))],
            out_specs=pl.BlockSpec((1, tm, d), tokmap),
            scratch_shapes=[pltpu.VMEM((tm, d), F32)]),
        compiler_params=_cparams(("arbitrary", "arbitrary")),
        name="combine",
    )(slot_off, ys, rsel_c, aff, x_new, gf, npost.reshape(1, d))


def kernel(x, c, ctx, c_ctx, w_mod, b_mod, norm_mix_pre, norm_mix_post, norm_ffn_pre, norm_ffn_post,
           w_in, na_rpb, gla_a_up, gla_a_bias, gla_norm, w_out, router, w_gate, w_up, w_down):
    b, n, d = x.shape
    assert w_mod.shape[0] == 1 and d == D_MODEL and n % (GRID_W * NA_QROWS) == 0 and n % GLA_T == 0
    assert ctx.shape[1] == GLA_T
    rows = n // GRID_W
    cap = EC_CAPACITY_FACTOR * n // N_EXPERTS

    c8 = jnp.concatenate([c, c_ctx[None, :], jnp.zeros((8 - b - 1, d), F32)], axis=0)
    mod = _mod(c8, w_mod[0], b_mod[0])
    sh_m, sc_m, g_m, sh_f, sc_f, g_f = [m[:b, None, :] for m in jnp.split(mod, 6, axis=-1)]
    sh_c, sc_c = mod[b:b + 1, None, :d], mod[b:b + 1, None, d:2 * d]

    wb = w_in[0].astype(BF16)
    cuts = np.cumsum([0, NA_W, NA_W, GLA_KEY_W, GLA_VAL_W, 2 * GLA_GATE_RANK, NA_W, GLA_KEY_W, GLA_VAL_W])
    w_nak, w_nav, w_gk, w_gv, w_ad, w_naq, w_gq, w_gg = [wb[:, cuts[i]:cuts[i + 1]] for i in range(8)]
    w_gvt = w_gv.T
    w_ad3 = jnp.concatenate([w_ad, w_ad, w_ad, jnp.zeros_like(w_ad)], axis=1)
    rope = _rope_tables(n)
    na_q, na_k, na_v, gq, gk, gv, gvt, ad, gg = _inproj(
        x, sh_m, sc_m, norm_mix_pre[0],
        [w_naq, w_nak, w_nav, w_gq, w_gk, w_gv, w_gvt, w_ad3, w_gg],
        ["plain", "plain", "plain", "rope", "rope", "plain", "t", "split", "plain"],
        [1.0, 1.0, 1.0, GLA_DK ** -0.5, 1.0, 1.0, 1.0, 1.0, 1.0],
        [BF16] * 9, rope=rope)
    c_nak, c_nav, c_gk, c_gvt, c_ad = _inproj(
        ctx, sh_c, sc_c, norm_mix_pre[0],
        [w_nak, w_nav, w_gk, w_gvt, w_ad3],
        ["plain", "plain", "plain", "t", "split"], [1.0] * 5, [BF16] * 5)

    o_na = _na(na_q, na_k, na_v, c_nak, c_nav, *_na_bias_tables(na_rpb[0], rows))

    cmats = jnp.asarray(_gla_prefix_matrices(GLA_T), BF16)
    o_gla = _gla(gq, gk, gv, gvt, ad, gg, c_gk, c_gvt, c_ad, _gla_decay_up(gla_a_up[0]),
                 gla_a_bias[0][:, None, :], gla_norm[0][None, :], cmats)

    wo = w_out[0].astype(BF16)
    router_pad = jnp.zeros((d, LANES), F32).at[:, :N_EXPERTS].set(router[0])
    x_new, hf, aff, aff_t = _outproj(o_na, o_gla, x, wo[:NA_W], wo[NA_W:], g_m, sh_f, sc_f,
                                     norm_mix_post[0], norm_ffn_pre[0],
                                     jnp.concatenate(_split_bf16(router_pad), axis=1))

    rsel_t, rsel_c = _route(aff_t, cap)
    xs = _sc_gather(rsel_t, hf.reshape(b * n, d // 2), cap).reshape(b, N_EXPERTS, cap, d // 2)
    ys = _ffn(xs, w_gate[0], w_up[0], w_down[0])
    picked = (rsel_t >= 0).astype(jnp.int32).reshape(b, N_EXPERTS, n // LANES, LANES).sum(axis=-1)
    slot_off = jnp.concatenate([jnp.zeros((b, N_EXPERTS, 1), jnp.int32), jnp.cumsum(picked, axis=-1)], axis=-1)
    return _combine(ys, slot_off, rsel_c, aff, x_new, g_f, norm_ffn_post[0])
```

```python
import functools

import numpy as np
import jax
import jax.numpy as jnp
from jax import lax
from jax.experimental import pallas as pl
from jax.experimental.pallas import tpu as pltpu
from jax.experimental.pallas import tpu_sc as plsc

F32 = jnp.float32
BF16 = jnp.bfloat16
HIGHEST = lax.Precision.HIGHEST

D_MODEL = 1024
GRID_W = 64
NA_W = 512
NA_HEADS = 8
NA_HEAD_DIM = 64
NA_WIN_ROWS = 8
NA_WIN_COLS = 16
GLA_HEADS = 4
GLA_DV = 128
GLA_DK = 64
GLA_KEY_W = 256
GLA_VAL_W = 512
GLA_GATE_RANK = 16
GLA_GATE_TAU = 16.0
ROPE_BASE = 10000.0
N_EXPERTS = 16
EC_CAPACITY_FACTOR = 2
NORM_EPS = 1e-6
NEG_BIG = -1e30

LANES = 128
BF16_ROWS = 16
VMEM_LIMIT = 56 * 1024 * 1024

TOK_TILE = 512
NA_QROWS = 4
NA_KROWS = NA_QROWS + NA_WIN_ROWS
NA_TAB_PAD = NA_QROWS
NA_TAB_BLOCKS = NA_TAB_PAD + 2 * NA_WIN_ROWS - 1 + NA_QROWS + 1
GLA_T = 256
GLA_LEVELS = (128, 64, 32, 16)
GLA_DIAG = 16
GLA_UNROLL = 2
FFN_TILE = 256
COMBINE_WIN = 128
SC_GATHER_ROWS = 64


def _cparams(sem):
    return pltpu.CompilerParams(dimension_semantics=sem, vmem_limit_bytes=VMEM_LIMIT)


def _rms(v, g):
    return v * lax.rsqrt(jnp.mean(v * v, axis=-1, keepdims=True) + NORM_EPS) * g


def _dot(a, b):
    return jnp.dot(a, b, preferred_element_type=F32)


def _dot_nt(a, b):
    return lax.dot_general(a, b, (((1,), (1,)), ((), ())), preferred_element_type=F32)


def _split_bf16(v):
    hi = v.astype(BF16)
    return hi, (v - hi.astype(F32)).astype(BF16)


_HIGH_HALF = -65536


def _pack_bf16_pairs(hb):
    bits = lax.bitcast_convert_type(hb.astype(F32), jnp.int32)
    half = hb.shape[1] // 2
    return lax.shift_right_logical(bits[:, :half], 16) | (bits[:, half:] & _HIGH_HALF)


def _unpack_bf16_pairs(w):
    lo = lax.bitcast_convert_type(lax.shift_left(w, 16), F32).astype(BF16)
    hi = lax.bitcast_convert_type(w & _HIGH_HALF, F32).astype(BF16)
    return jnp.concatenate([lo, hi], axis=1)


def _mod_kernel(c_ref, w_ref, b_ref, o_ref):
    c = c_ref[...]
    s = c / (1.0 + jnp.exp(-c))
    o_ref[...] = jnp.dot(s, w_ref[...], preferred_element_type=F32, precision=HIGHEST) + b_ref[...]


def _mod(c8, w_mod, b_mod):
    d, n = w_mod.shape
    tn = 1536
    return pl.pallas_call(
        _mod_kernel,
        out_shape=jax.ShapeDtypeStruct((8, n), F32),
        grid=(n // tn,),
        in_specs=[pl.BlockSpec((8, d), lambda j: (0, 0)),
                  pl.BlockSpec((d, tn), lambda j: (0, j)),
                  pl.BlockSpec((1, tn), lambda j: (0, j))],
        out_specs=pl.BlockSpec((8, tn), lambda j: (0, j)),
        compiler_params=_cparams(("arbitrary",)),
        name="mod",
    )(c8, w_mod, b_mod.reshape(1, n))


def _inproj_kernel(x_ref, sh_ref, sc_ref, g_ref, *refs, modes, scales):
    n_rope = 2 if "rope" in modes else 0
    rope_refs, refs = refs[:n_rope], refs[n_rope:]
    n_w = len(modes)
    w_refs, o_refs = refs[:n_w], refs[n_w:]
    x = x_ref[0]
    h = _rms(x, g_ref[...]) * (1.0 + sc_ref[0]) + sh_ref[0]
    hb = h.astype(BF16)
    for w_ref, o_ref, mode, scale in zip(w_refs, o_refs, modes, scales):
        if mode == "t":
            o_ref[0] = _dot_nt(w_ref[...], hb).astype(o_ref.dtype)
            continue
        y = _dot(hb, w_ref[...])
        if mode == "rope":
            cols = y.shape[1]
            quarter = GLA_DK // 4
            lane = lax.broadcasted_iota(jnp.int32, y.shape, 1)
            partner = jnp.where((lane & (2 * quarter - 1)) < quarter,
                                pltpu.roll(y, cols - quarter, axis=1), pltpu.roll(y, quarter, axis=1))
            y = (y * rope_refs[0][...] + partner * rope_refs[1][...]) * scale
        elif mode == "split":
            hi, lo = _split_bf16(y)
            lane = lax.broadcasted_iota(jnp.int32, y.shape, 1)
            rank2 = 2 * GLA_GATE_RANK
            y = jnp.where((lane >= rank2) & (lane < 2 * rank2), lo, hi)
        o_ref[0] = y.astype(o_ref.dtype)


def _inproj(x, shift, scale, g, weights, modes, scales, out_dtypes, rope=None):
    b, n, d = x.shape
    tm = min(TOK_TILE, n)
    per_sample = shift.shape[0] == b
    mod_map = (lambda i, j: (i, 0, 0)) if per_sample else (lambda i, j: (0, 0, 0))
    in_specs = [pl.BlockSpec((1, tm, d), lambda i, j: (i, j, 0)),
                pl.BlockSpec((1, 1, d), mod_map),
                pl.BlockSpec((1, 1, d), mod_map),
                pl.BlockSpec((1, d), lambda i, j: (0, 0))]
    args = [x, shift, scale, g.reshape(1, d)]
    if rope is not None:
        in_specs += [pl.BlockSpec((tm, rope[0].shape[1]), lambda i, j: (j, 0))] * 2
        args += list(rope)
    out_shapes, out_specs = [], []
    for w, mode, dt in zip(weights, modes, out_dtypes):
        in_specs.append(pl.BlockSpec(w.shape, lambda i, j: (0, 0)))
        if mode == "t":
            cols = w.shape[0]
            out_shapes.append(jax.ShapeDtypeStruct((b, cols, n), dt))
            out_specs.append(pl.BlockSpec((1, cols, tm), lambda i, j: (i, 0, j)))
        else:
            cols = w.shape[1]
            out_shapes.append(jax.ShapeDtypeStruct((b, n, cols), dt))
            out_specs.append(pl.BlockSpec((1, tm, cols), lambda i, j: (i, j, 0)))
    return pl.pallas_call(
        functools.partial(_inproj_kernel, modes=tuple(modes), scales=tuple(scales)),
        out_shape=out_shapes,
        grid=(b, n // tm),
        in_specs=in_specs,
        out_specs=out_specs,
        compiler_params=_cparams(("arbitrary", "arbitrary")),
        name="inproj",
    )(*args, *weights)


def _rope_tables(n):
    t = jnp.arange(n)
    pos_row, pos_col = (t // GRID_W).astype(F32), (t % GRID_W).astype(F32)
    quarter = GLA_DK // 4
    freqs = ROPE_BASE ** (-jnp.arange(quarter, dtype=F32) / quarter)
    ang_r = pos_row[:, None] * freqs
    ang_c = pos_col[:, None] * freqs
    cos = jnp.concatenate([jnp.cos(ang_r), jnp.cos(ang_r), jnp.cos(ang_c), jnp.cos(ang_c)], axis=-1)
    sin = jnp.concatenate([-jnp.sin(ang_r), jnp.sin(ang_r), -jnp.sin(ang_c), jnp.sin(ang_c)], axis=-1)
    return jnp.tile(cos, (1, GLA_HEADS)), jnp.tile(sin, (1, GLA_HEADS))


def _na_patterns(rows):
    kr = min(NA_WIN_ROWS, rows)
    n_blocks = rows // NA_QROWS
    pats = []
    for blk in (0, 1, n_blocks - 1):
        r0 = blk * NA_QROWS
        k0 = int(np.clip(r0 - kr // 2, 0, rows - NA_KROWS))
        strips = []
        for a in range(NA_QROWS):
            r_start = int(np.clip(r0 + a - kr // 2, 0, rows - kr))
            start = k0 - (r0 + a) + NA_WIN_ROWS - 1 + NA_TAB_PAD
            assert 0 <= start and start + NA_KROWS <= NA_TAB_BLOCKS
            strips.append((start, [r_start <= k0 + c < r_start + kr for c in range(NA_KROWS)]))
        pats.append(strips)
    return pats


def _na_bias_tables(rpb, rows):
    heads = rpb.shape[0]
    col = np.arange(GRID_W)
    c_start = np.clip(col - NA_WIN_COLS // 2, 0, GRID_W - NA_WIN_COLS)
    col_ok = (col[None, :] >= c_start[:, None]) & (col[None, :] < c_start[:, None] + NA_WIN_COLS)
    dc = np.clip(col[None, :] - col[:, None] + NA_WIN_COLS - 1, 0, 2 * NA_WIN_COLS - 2)
    sel_c = (np.arange(2 * NA_WIN_COLS - 1)[:, None, None] == dc[None]) & col_ok[None]
    t = jnp.einsum("hrd,dqk->hqrk", rpb, jnp.asarray(sel_c, F32), precision=HIGHEST)
    t = jnp.where(jnp.asarray(col_ok)[None, :, None, :], t, NEG_BIG)
    n_dr = 2 * NA_WIN_ROWS - 1
    t = t.reshape(heads, GRID_W, n_dr * GRID_W)
    back = NA_TAB_BLOCKS + 1 - NA_TAB_PAD - n_dr
    t = jnp.pad(t, ((0, 0), (0, 0), (NA_TAB_PAD * GRID_W, back * GRID_W)), constant_values=NEG_BIG)
    width = NA_TAB_BLOCKS * GRID_W
    tab = jnp.stack([t[:, :, :width], t[:, :, GRID_W:GRID_W + width]], axis=1)
    row_mask = np.zeros((3, NA_QROWS, 1, NA_KROWS * GRID_W), np.float32)
    for pat, strips in enumerate(_na_patterns(rows)):
        for a, (_, valid) in enumerate(strips):
            row_mask[pat, a, 0] = np.repeat(np.where(valid, 0.0, NEG_BIG), GRID_W)
    return tab, jnp.asarray(row_mask)


def _na_kernel(q_ref, k_ref, v_ref, kc_ref, vc_ref, tab_ref, rmask_ref, o_ref, bias_ref,
               sw_ref, sc_ref, pw_ref, pc_ref, linv_ref, *, rows):
    nq, nk = NA_QROWS * GRID_W, NA_KROWS * GRID_W
    n_blocks = rows // NA_QROWS
    kr = min(NA_WIN_ROWS, rows)
    scale = NA_HEAD_DIM ** -0.5
    lane = lax.broadcasted_iota(jnp.int32, (nq, LANES), 1)
    first_head = lane < NA_HEAD_DIM
    kc = kc_ref[0]
    vc = vc_ref[0]
    for pat, strips in enumerate(_na_patterns(rows)):
        for a, (start, _) in enumerate(strips):
            parity = start % 2
            off = (start - parity) * GRID_W
            for h in range(2):
                bias_ref[0, pat, h, a * GRID_W:(a + 1) * GRID_W, :] = (
                    tab_ref[h, parity, :, off:off + nk] + rmask_ref[pat, a])

    def key_start(i):
        return pl.multiple_of(jnp.clip(i * NA_QROWS - kr // 2, 0, rows - NA_KROWS) * GRID_W, GRID_W)

    def scores(i, slot):
        pat = jnp.where(i == 0, 0, jnp.where(i == n_blocks - 1, 2, 1))
        q = q_ref[0, pl.ds(pl.multiple_of(i * nq, nq), nq), :] * scale
        kw = k_ref[0, pl.ds(key_start(i), nk), :]
        for h in range(2):
            qh = jnp.where(first_head if h == 0 else jnp.logical_not(first_head), q, jnp.zeros_like(q))
            sw_ref[slot, h] = _dot_nt(qh, kw) + bias_ref[0, pat, h]
            sc_ref[slot, h] = _dot_nt(qh, kc)

    def softmax(slot):
        for h in range(2):
            s_w = sw_ref[slot, h]
            s_c = sc_ref[slot, h]
            m = jnp.maximum(jnp.max(s_w, axis=-1, keepdims=True), jnp.max(s_c, axis=-1, keepdims=True))
            p_w = jnp.exp(s_w - m)
            p_c = jnp.exp(s_c - m)
            l = jnp.sum(p_w, axis=-1, keepdims=True) + jnp.sum(p_c, axis=-1, keepdims=True)
            pw_ref[slot, h] = p_w.astype(BF16)
            pc_ref[slot, h] = p_c.astype(BF16)
            linv_ref[slot, h] = jnp.broadcast_to(1.0 / l, (nq, LANES))

    def values(i, slot):
        vw = v_ref[0, pl.ds(key_start(i), nk), :]
        outs = [(_dot(pw_ref[slot, h], vw) + _dot(pc_ref[slot, h], vc)) * linv_ref[slot, h] for h in range(2)]
        o = jnp.where(first_head, outs[0], outs[1])
        o_ref[0, pl.ds(pl.multiple_of(i * nq, nq), nq), :] = o.astype(o_ref.dtype)

    assert n_blocks % 2 == 0 and n_blocks >= 4
    scores(0, 0)
    softmax(0)
    scores(1, 1)

    def trip(j, carry):
        i = 2 * j
        values(i - 2, 0)
        softmax(1)
        scores(i, 0)
        values(i - 1, 1)
        softmax(0)
        scores(i + 1, 1)
        return carry

    lax.fori_loop(1, n_blocks // 2, trip, 0)
    values(n_blocks - 2, 0)
    softmax(1)
    values(n_blocks - 1, 1)


def _na(q, k, v, kc, vc, tab, row_mask):
    b, n, w = q.shape
    n_ctx = kc.shape[1]
    pairs = w // LANES
    rows = n // GRID_W
    nq, nk = NA_QROWS * GRID_W, NA_KROWS * GRID_W
    tok = lambda i, p: (i, 0, p)
    return pl.pallas_call(
        functools.partial(_na_kernel, rows=rows),
        out_shape=jax.ShapeDtypeStruct((b, n, w), BF16),
        grid=(b, pairs),
        in_specs=[pl.BlockSpec((1, n, LANES), tok),
                  pl.BlockSpec((1, n, LANES), tok),
                  pl.BlockSpec((1, n, LANES), tok),
                  pl.BlockSpec((1, n_ctx, LANES), tok),
                  pl.BlockSpec((1, n_ctx, LANES), tok),
                  pl.BlockSpec((2,) + tab.shape[1:], lambda i, p: (p, 0, 0, 0)),
                  pl.BlockSpec(row_mask.shape, lambda i, p: (0, 0, 0, 0))],
        out_specs=pl.BlockSpec((1, n, LANES), tok),
        scratch_shapes=[pltpu.VMEM((1, 3, 2, nq, nk), F32),
                        pltpu.VMEM((2, 2, nq, nk), F32), pltpu.VMEM((2, 2, nq, n_ctx), F32),
                        pltpu.VMEM((2, 2, nq, nk), BF16), pltpu.VMEM((2, 2, nq, n_ctx), BF16),
                        pltpu.VMEM((2, 2, nq, LANES), F32)],
        compiler_params=_cparams(("arbitrary", "arbitrary")),
        name="na",
    )(q, k, v, kc, vc, tab, row_mask)


def _gla_prefix_matrices(t):
    i = np.arange(t)
    return np.stack([i[:, None] >= i[None, :], i[:, None] <= i[None, :]]).astype(np.float32)


def _gla_kernel(q_ref, k_ref, v_ref, vt_ref, ad_ref, g_ref, ck_ref, cvt_ref, cad_ref,
                u_ref, ab_ref, gn_ref, cm_ref, o_ref, accf_ref, accb_ref, *, n_tok):
    t = GLA_T
    n_chunks = n_tok // t
    row = lax.broadcasted_iota(jnp.int32, (t, LANES), 0)
    hb = t // 2
    row_b = lax.broadcasted_iota(jnp.int32, (hb, LANES), 0)
    head0_b = lax.broadcasted_iota(jnp.int32, (hb, LANES), 1) < GLA_DK
    row2 = lax.broadcasted_iota(jnp.int32, (hb, 2 * hb), 0)
    col2 = lax.broadcasted_iota(jnp.int32, (hb, 2 * hb), 1) & (hb - 1)
    vrow = lax.broadcasted_iota(jnp.int32, (2 * t, 2 * GLA_DV), 0)
    vlane = lax.broadcasted_iota(jnp.int32, (2 * t, 2 * GLA_DV), 1)
    v_head_match = (vrow >= t) == (vlane >= GLA_DV)
    srow = lax.broadcasted_iota(jnp.int32, (2 * GLA_DV, LANES), 0)
    slane = lax.broadcasted_iota(jnp.int32, (2 * GLA_DV, LANES), 1)
    s_blockdiag = (srow >= GLA_DV) == (slane >= GLA_DK)
    blk_mask = {half: jnp.where((row2 & ~(2 * half - 1)) == (col2 & ~(2 * half - 1)), 1.0, 0.0)
                for half in GLA_LEVELS if 2 * half < hb}
    diag_blk = (row2 & ~(GLA_DIAG - 1)) == (col2 & ~(GLA_DIAG - 1))
    diag_mask = (jnp.where(diag_blk & (row2 >= col2), 1.0, 0.0), jnp.where(diag_blk & (row2 <= col2), 1.0, 0.0))

    def prefix_sums(ad, dirn):
        z = _dot(ad, u_ref[dirn]) + ab_ref[dirn]
        loga = (jnp.minimum(z, 0.0) - jnp.log(1.0 + jnp.exp(-jnp.abs(z)))) * (1.0 / GLA_GATE_TAU)
        hi, lo = _split_bf16(loga)
        p2 = _dot(cm_ref[dirn], jnp.concatenate([hi, lo], axis=-1))
        return p2[:, :LANES] + p2[:, LANES:]

    def chunk_end(p, dirn):
        return p[t - 1:t, :] if dirn == 0 else p[0:1, :]

    def level_sums(p, half, dirn):
        blk = 2 * half
        p3 = p.reshape(t // blk, blk, LANES)
        edge = half - 1 if dirn == 0 else half
        ref = jnp.broadcast_to(p3[:, edge:edge + 1, :], p3.shape).reshape(t, LANES)
        later = (row & half) != 0
        return jnp.where(later == (dirn == 0), p - ref, ref - p)

    def state_update(s, k, vt, p, dirn):
        kh = (k * jnp.exp(chunk_end(p, dirn) - p)).astype(BF16)
        return s * jnp.exp(chunk_end(p, dirn)) + jnp.where(s_blockdiag, _dot(vt, kh), 0.0)

    def chunk(tok0, s, dirn):
        q = q_ref[0, pl.ds(tok0, t), :].astype(F32)
        k = k_ref[0, pl.ds(tok0, t), :].astype(F32)
        v = v_ref[0, pl.ds(tok0, t), :]
        vt = vt_ref[0, :, pl.ds(tok0, t)]
        p = prefix_sums(ad_ref[0, pl.ds(tok0, t), :], dirn)
        qh = (q * jnp.exp(p)).astype(BF16)
        o = _dot_nt(qh, s.astype(BF16))
        def rows(x, b):
            return x[b * hb:(b + 1) * hb]

        def scores(qt, kt):
            kcat = jnp.concatenate([jnp.where(head0_b, kt, 0.0), jnp.where(head0_b, 0.0, kt)], axis=0)
            return _dot_nt(qt.astype(BF16), kcat.astype(BF16))

        assert GLA_LEVELS[0] == hb
        w = jnp.exp(level_sums(p, hb, dirn))
        qb, kb = (1, 0) if dirn == 0 else (0, 1)
        wide = scores(rows(q, qb) * rows(w, qb), rows(k, kb) * rows(w, kb))
        fine_w = [jnp.exp(level_sums(p, half, dirn)) for half in GLA_LEVELS[1:]]
        e_d = level_sums(p, GLA_DIAG // 2, dirn)
        w_d, wi_d = jnp.exp(e_d), jnp.exp(-e_d)
        fine = []
        for b in range(2):
            qs, ks = rows(q, b), rows(k, b)
            acc = None
            for half, w in zip(GLA_LEVELS[1:], fine_w):
                later = (row_b & half) != 0
                q_side = later if dirn == 0 else jnp.logical_not(later)
                part = scores(jnp.where(q_side, qs * rows(w, b), 0.0), jnp.where(q_side, 0.0, ks * rows(w, b)))
                if half in blk_mask:
                    part = part * blk_mask[half]
                acc = part if acc is None else acc + part
            later = (row_b & (GLA_DIAG // 2)) != 0
            shrink_q = later if dirn == 0 else jnp.logical_not(later)
            part = scores(qs * jnp.where(shrink_q, rows(w_d, b), rows(wi_d, b)),
                          ks * jnp.where(shrink_q, rows(wi_d, b), rows(w_d, b)))
            fine.append(acc + jnp.where(diag_mask[dirn] > 0.5, part, 0.0))
        zero = jnp.zeros((hb, hb), F32)
        h0, h1 = slice(0, hb), slice(hb, 2 * hb)
        if dirn == 0:
            top = [fine[0][:, h0], zero, fine[0][:, h1], zero]
            bot = [wide[:, h0], fine[1][:, h0], wide[:, h1], fine[1][:, h1]]
        else:
            top = [fine[0][:, h0], wide[:, h0], fine[0][:, h1], wide[:, h1]]
            bot = [zero, fine[1][:, h0], zero, fine[1][:, h1]]
        a = jnp.concatenate([jnp.concatenate(top, axis=1), jnp.concatenate(bot, axis=1)], axis=0)
        vcat = jnp.concatenate([v, v], axis=0)
        vcat = jnp.where(v_head_match, vcat, jnp.zeros_like(vcat))
        o = o + _dot(a.astype(BF16), vcat)
        return o, state_update(s, k, vt, p, dirn)

    def ctx_state(dirn):
        p = prefix_sums(cad_ref[0], dirn)
        s0 = jnp.zeros((2 * GLA_DV, LANES), F32)
        return state_update(s0, ck_ref[0].astype(F32), cvt_ref[0], p, dirn)

    def body(i, carry):
        s_f, s_b = carry
        for u in range(GLA_UNROLL):
            c = i * GLA_UNROLL + u
            tok_f = pl.multiple_of(c * t, t)
            tok_b = pl.multiple_of((n_chunks - 1 - c) * t, t)
            o_f, s_f = chunk(tok_f, s_f, 0)
            o_b, s_b = chunk(tok_b, s_b, 1)
            accf_ref[pl.ds(tok_f, t), :] = o_f
            accb_ref[pl.ds(tok_b, t), :] = o_b
        return s_f, s_b

    assert n_chunks % GLA_UNROLL == 0
    lax.fori_loop(0, n_chunks // GLA_UNROLL, body, (ctx_state(0), ctx_state(1)))

    def finish(c, carry):
        tok0 = pl.multiple_of(c * t, t)
        o = accf_ref[pl.ds(tok0, t), :] + accb_ref[pl.ds(tok0, t), :]
        g = g_ref[0, pl.ds(tok0, t), :].astype(F32)
        gate = g / (1.0 + jnp.exp(-g))
        halves = [_rms(o[:, h * GLA_DV:(h + 1) * GLA_DV], gn_ref[...]) for h in range(2)]
        o_ref[0, pl.ds(tok0, t), :] = (jnp.concatenate(halves, axis=-1) * gate).astype(o_ref.dtype)
        return carry

    lax.fori_loop(0, n_chunks, finish, 0)


def _gla(q, k, v, vt, ad, g, ck, cvt, cad, u, abias, gnorm, cmats):
    b, n, kw = q.shape
    n_ctx = ck.shape[1]
    pairs = kw // LANES
    vw = 2 * GLA_DV
    tok = lambda i, p: (i, 0, p)
    full3 = lambda i, p: (i, 0, 0)
    return pl.pallas_call(
        functools.partial(_gla_kernel, n_tok=n),
        out_shape=jax.ShapeDtypeStruct((b, n, v.shape[2]), BF16),
        grid=(b, pairs),
        in_specs=[pl.BlockSpec((1, n, LANES), tok),
                  pl.BlockSpec((1, n, LANES), tok),
                  pl.BlockSpec((1, n, vw), tok),
                  pl.BlockSpec((1, vw, n), lambda i, p: (i, p, 0)),
                  pl.BlockSpec((1, n, LANES), full3),
                  pl.BlockSpec((1, n, vw), tok),
                  pl.BlockSpec((1, n_ctx, LANES), tok),
                  pl.BlockSpec((1, vw, n_ctx), lambda i, p: (i, p, 0)),
                  pl.BlockSpec((1, n_ctx, LANES), full3),
                  pl.BlockSpec((2, LANES, LANES), lambda i, p: (0, 0, p)),
                  pl.BlockSpec((2, 1, LANES), lambda i, p: (0, 0, p)),
                  pl.BlockSpec((1, GLA_DV), lambda i, p: (0, 0)),
                  pl.BlockSpec(cmats.shape, lambda i, p: (0, 0, 0))],
        out_specs=pl.BlockSpec((1, n, vw), tok),
        scratch_shapes=[pltpu.VMEM((n, vw), F32), pltpu.VMEM((n, vw), F32)],
        compiler_params=_cparams(("arbitrary", "arbitrary")),
        name="gla",
    )(q, k, v, vt, ad, g, ck, cvt, cad, u, abias, gnorm, cmats)


def _gla_decay_up(a_up):
    r = GLA_GATE_RANK
    pad = jnp.zeros((2, 2 * r, GLA_KEY_W), F32)
    pad = pad.at[0, :r].set(a_up[0]).at[1, r:].set(a_up[1])
    hi, lo = _split_bf16(pad)
    return jnp.concatenate([hi, hi, lo, jnp.zeros_like(hi)], axis=1)


def _outproj_kernel(ona_ref, ogla_ref, x_ref, w1_ref, w2_ref, gm_ref, shf_ref, scf_ref, npost_ref,
                    nfpre_ref, rt_ref, xnew_ref, hf_ref, aff_ref, afft_ref):
    mix = _dot(ona_ref[0], w1_ref[...]) + _dot(ogla_ref[0], w2_ref[...])
    xn = x_ref[0] + gm_ref[0] * _rms(mix, npost_ref[...])
    xnew_ref[0] = xn
    h = _rms(xn, nfpre_ref[...]) * (1.0 + scf_ref[0]) + shf_ref[0]
    h_hi, h_lo = _split_bf16(h)
    hf_ref[0] = _pack_bf16_pairs(h_hi)
    res = _dot(h_hi, rt_ref[...])
    logits = res[:, :LANES] + res[:, LANES:] + _dot(h_lo, rt_ref[:, :LANES])
    lane = lax.broadcasted_iota(jnp.int32, logits.shape, 1)
    logits = jnp.where(lane < N_EXPERTS, logits, NEG_BIG)
    p = jnp.exp(logits - jnp.max(logits, axis=-1, keepdims=True))
    aff = p / jnp.sum(p, axis=-1, keepdims=True)
    aff_ref[0] = aff
    afft_ref[0] = aff.T[:N_EXPERTS, :]


def _outproj(o_na, o_gla, x, w1, w2, gm, shf, scf, npost, nfpre, router_cat):
    b, n, d = x.shape
    tm = TOK_TILE
    tokmap = lambda i, j: (i, j, 0)
    smp = lambda i, j: (i, 0, 0)
    cst = lambda i, j: (0, 0)
    return pl.pallas_call(
        _outproj_kernel,
        out_shape=[jax.ShapeDtypeStruct((b, n, d), F32),
                   jax.ShapeDtypeStruct((b, n, d // 2), jnp.int32),
                   jax.ShapeDtypeStruct((b, n, LANES), F32),
                   jax.ShapeDtypeStruct((b, N_EXPERTS, n), F32)],
        grid=(b, n // tm),
        in_specs=[pl.BlockSpec((1, tm, o_na.shape[2]), tokmap),
                  pl.BlockSpec((1, tm, o_gla.shape[2]), tokmap),
                  pl.BlockSpec((1, tm, d), tokmap),
                  pl.BlockSpec(w1.shape, cst),
                  pl.BlockSpec(w2.shape, cst),
                  pl.BlockSpec((1, 1, d), smp),
                  pl.BlockSpec((1, 1, d), smp),
                  pl.BlockSpec((1, 1, d), smp),
                  pl.BlockSpec((1, d), cst),
                  pl.BlockSpec((1, d), cst),
                  pl.BlockSpec(router_cat.shape, cst)],
        out_specs=[pl.BlockSpec((1, tm, d), tokmap),
                   pl.BlockSpec((1, tm, d // 2), tokmap),
                   pl.BlockSpec((1, tm, LANES), tokmap),
                   pl.BlockSpec((1, N_EXPERTS, tm), lambda i, j: (i, 0, j))],
        compiler_params=_cparams(("arbitrary", "arbitrary")),
        name="outproj",
    )(o_na, o_gla, x, w1, w2, gm, shf, scf, npost.reshape(1, d), nfpre.reshape(1, d), router_cat)


def _route_kernel(afft_ref, rt_ref, rc_ref, *, cap):
    a = afft_ref[0]
    e, n = a.shape
    capf = jnp.float32(cap)

    def search(i, thr_bits):
        cand = thr_bits | lax.shift_left(jnp.int32(1), 30 - i)
        cnt = jnp.sum(jnp.where(a >= lax.bitcast_convert_type(cand, F32), 1.0, 0.0), axis=-1, keepdims=True)
        return jnp.where(cnt >= capf, cand, thr_bits)

    thr_bits = lax.fori_loop(0, 31, search, jnp.zeros((e, 1), jnp.int32))
    thr = lax.bitcast_convert_type(thr_bits, F32)
    need = capf - jnp.sum(jnp.where(a > thr, 1.0, 0.0), axis=-1, keepdims=True)
    r_i = lax.broadcasted_iota(jnp.int32, (LANES, LANES), 0)
    c_i = lax.broadcasted_iota(jnp.int32, (LANES, LANES), 1)
    incl = jnp.where(r_i <= c_i, 1.0, 0.0).astype(BF16)
    off_eq = jnp.zeros((e, 1), F32)
    off_sel = jnp.zeros((e, 1), F32)
    pad = jnp.full((LANES - e, LANES), -1.0, F32)
    for j in range(n // LANES):
        sl = slice(j * LANES, (j + 1) * LANES)
        a_b = a[:, sl]
        eq_b = jnp.where(a_b == thr, 1.0, 0.0)
        tie_rank = _dot(eq_b.astype(BF16), incl) - eq_b + off_eq
        off_eq = off_eq + jnp.sum(eq_b, axis=-1, keepdims=True)
        sel_b = jnp.where(a_b > thr, 1.0, jnp.where(tie_rank < need, eq_b, 0.0))
        sel = sel_b > 0.5
        rank = _dot(sel_b.astype(BF16), incl) - sel_b + off_sel
        off_sel = off_sel + jnp.sum(sel_b, axis=-1, keepdims=True)
        rsel = jnp.where(sel, rank, -1.0)
        rt_ref[0, :, sl] = rsel.astype(jnp.int32)
        rc_ref[0, sl, :] = jnp.concatenate([rsel, pad], axis=0).T.astype(jnp.int32)


def _route(afft, cap):
    b, e, n = afft.shape
    return pl.pallas_call(
        functools.partial(_route_kernel, cap=cap),
        out_shape=[jax.ShapeDtypeStruct((b, e, n), jnp.int32),
                   jax.ShapeDtypeStruct((b, n, LANES), jnp.int32)],
        grid=(b,),
        in_specs=[pl.BlockSpec((1, e, n), lambda i: (i, 0, 0))],
        out_specs=[pl.BlockSpec((1, e, n), lambda i: (i, 0, 0)),
                   pl.BlockSpec((1, n, LANES), lambda i: (i, 0, 0))],
        compiler_params=_cparams(("arbitrary",)),
        name="route",
    )(afft)


def _sc_gather(rsel_t, hf2, cap):
    b, e, n = rsel_t.shape
    width = hf2.shape[1]
    info = plsc.get_sparse_core_info()
    nc, lanes = info.num_cores, info.num_lanes
    workers = nc * info.num_subcores
    items = b * e
    assert items % workers == 0 and n % lanes == 0 and cap % SC_GATHER_ROWS == 0
    per_worker = items // workers
    mesh = plsc.VectorSubcoreMesh(core_axis_name="c", subcore_axis_name="s")

    def body(rank_hbm, hf_hbm, out_hbm, rank_v, idx_v, rows_v, sem):
        wid = lax.axis_index("s") * nc + lax.axis_index("c")
        for k in range(per_worker):
            item = wid * per_worker + k
            base_tok = (item // e) * n
            pltpu.sync_copy(rank_hbm.at[item], rank_v)

            @pl.loop(0, n // lanes)
            def _(j):
                r = rank_v[pl.ds(j * lanes, lanes)]
                tok = lax.iota(jnp.int32, lanes) + (j * lanes + base_tok)
                plsc.store_scatter(idx_v, [r], tok, mask=r >= 0)

            @pl.loop(0, cap // SC_GATHER_ROWS)
            def _(c):
                rows = pl.ds(c * SC_GATHER_ROWS, SC_GATHER_ROWS)
                pltpu.async_copy(hf_hbm.at[idx_v.at[rows]], rows_v, sem).wait()
                pltpu.sync_copy(rows_v, out_hbm.at[pl.ds(item * cap + c * SC_GATHER_ROWS, SC_GATHER_ROWS)])

    return pl.kernel(
        body, out_type=jax.ShapeDtypeStruct((items * cap, width), hf2.dtype), mesh=mesh,
        scratch_types=[pltpu.VMEM((n,), jnp.int32), pltpu.VMEM((cap,), jnp.int32),
                       pltpu.VMEM((SC_GATHER_ROWS, width), hf2.dtype), pltpu.SemaphoreType.DMA],
        compiler_params=pltpu.CompilerParams(needs_layout_passes=False),
        name="scgather",
    )(rsel_t.reshape(items, n), hf2)


def _ffn_kernel(x_ref, wg_ref, wu_ref, wd_ref, o_ref, acc_ref, xb_ref):
    f = pl.program_id(1)
    b = x_ref.shape[0]
    wg = wg_ref[0].astype(BF16)
    wu = wu_ref[0].astype(BF16)
    wd = wd_ref[0].astype(BF16)

    @pl.when(f == 0)
    def _():
        acc_ref[...] = jnp.zeros_like(acc_ref)
        for i in range(b):
            xb_ref[i] = _unpack_bf16_pairs(x_ref[i, 0])

    for i in range(b):
        x = xb_ref[i]
        g = _dot(x, wg)
        u = _dot(x, wu)
        hid = (g / (1.0 + jnp.exp(-g)) * u).astype(BF16)
        acc_ref[i] += _dot(hid, wd)

    @pl.when(f == pl.num_programs(1) - 1)
    def _():
        for i in range(b):
            o_ref[i, 0] = acc_ref[i].astype(o_ref.dtype)


def _ffn(xs, w_gate, w_up, w_down):
    b, e, cap, dp = xs.shape
    d = 2 * dp
    dff = w_gate.shape[2]
    tf = FFN_TILE
    return pl.pallas_call(
        _ffn_kernel,
        out_shape=jax.ShapeDtypeStruct((b, e, cap, d), BF16),
        grid=(e, dff // tf),
        in_specs=[pl.BlockSpec((b, 1, cap, dp), lambda i, f: (0, i, 0, 0)),
                  pl.BlockSpec((1, d, tf), lambda i, f: (i, 0, f)),
                  pl.BlockSpec((1, d, tf), lambda i, f: (i, 0, f)),
                  pl.BlockSpec((1, tf, d), lambda i, f: (i, f, 0))],
        out_specs=pl.BlockSpec((b, 1, cap, d), lambda i, f: (0, i, 0, 0)),
        scratch_shapes=[pltpu.VMEM((b, cap, d), F32), pltpu.VMEM((b, cap, d), BF16)],
        compiler_params=_cparams(("arbitrary", "arbitrary")),
        name="ffn",
    )(xs, w_gate, w_up, w_down)


def _combine_kernel(off_ref, ys_ref, rc_ref, aff_ref, xn_ref, gf_ref, npost_ref, o_ref, acc_ref):
    bi, tt = pl.program_id(0), pl.program_id(1)
    tm = rc_ref.shape[1]
    n_experts, cap = ys_ref.shape[1], ys_ref.shape[2]
    blocks = tm // LANES
    slot = lax.broadcasted_iota(jnp.int32, (tm, COMBINE_WIN), 1)

    def window(e, w0, j):
        nominal = w0 + j * COMBINE_WIN
        start = pl.multiple_of(jnp.minimum(nominal, cap - COMBINE_WIN), BF16_ROWS)
        rank = rc_ref[0, :, e:e + 1]
        hit = ((rank - start) == slot) & (rank >= nominal)
        weights = jnp.where(hit, aff_ref[0, :, e:e + 1], 0.0).astype(BF16)
        return weights, ys_ref[0, e, pl.ds(start, COMBINE_WIN), :]

    first, extra = [], []
    for e in range(n_experts):
        r0 = off_ref[bi, e, tt * blocks]
        r1 = off_ref[bi, e, (tt + 1) * blocks]
        w0 = (r0 // BF16_ROWS) * BF16_ROWS
        first.append(w0)
        extra.append(jnp.maximum((r1 - w0 + COMBINE_WIN - 1) // COMBINE_WIN - 1, 0))
    terms = []
    for e in range(0, n_experts, 2):
        (wa, ya), (wb, yb) = window(e, first[e], 0), window(e + 1, first[e + 1], 0)
        terms.append(_dot(jnp.concatenate([wa, wb], axis=1), jnp.concatenate([ya, yb], axis=0)))
    acc_ref[...] = functools.reduce(lambda a, c: a + c, terms)

    @pl.when(functools.reduce(lambda a, c: a + c, extra) > 0)
    def _():
        for e in range(n_experts):
            def more(j, carry, e=e):
                acc_ref[...] += _dot(*window(e, first[e], j))
                return carry
            lax.fori_loop(1, extra[e] + 1, more, 0)

    o_ref[0] = xn_ref[0] + gf_ref[0] * _rms(acc_ref[...], npost_ref[...])


def _combine(ys, slot_off, rsel_c, aff, x_new, gf, npost):
    b, e, cap, d = ys.shape
    n = x_new.shape[1]
    tm = TOK_TILE
    tokmap = lambda i, j, off: (i, j, 0)
    return pl.pallas_call(
        _combine_kernel,
        out_shape=jax.ShapeDtypeStruct((b, n, d), F32),
        grid_spec=pltpu.PrefetchScalarGridSpec(
            num_scalar_prefetch=1,
            grid=(b, n // tm),
            in_specs=[pl.BlockSpec((1, e, cap, d), lambda i, j, off: (i, 0, 0, 0)),
                      pl.BlockSpec((1, tm, LANES), tokmap),
                      pl.BlockSpec((1, tm, LANES), tokmap),
                      pl.BlockSpec((1, tm, d), tokmap),
                      pl.BlockSpec((1, 1, d), lambda i, j, off: (i, 0, 0)),
                      pl.BlockSpec((1, d), lambda i, j, off: (0, 0))],
            out_specs=pl.BlockSpec((1, tm, d), tokmap),
            scratch_shapes=[pltpu.VMEM((tm, d), F32)]),
        compiler_params=_cparams(("arbitrary", "arbitrary")),
        name="combine",
    )(slot_off, ys, rsel_c, aff, x_new, gf, npost.reshape(1, d))


def kernel(x, c, ctx, c_ctx, w_mod, b_mod, norm_mix_pre, norm_mix_post, norm_ffn_pre, norm_ffn_post,
           w_in, na_rpb, gla_a_up, gla_a_bias, gla_norm, w_out, router, w_gate, w_up, w_down):
    b, n, d = x.shape
    assert w_mod.shape[0] == 1 and d == D_MODEL and n % (GRID_W * NA_QROWS) == 0 and n % GLA_T == 0
    assert ctx.shape[1] == GLA_T
    rows = n // GRID_W
    cap = EC_CAPACITY_FACTOR * n // N_EXPERTS

    c8 = jnp.concatenate([c, c_ctx[None, :], jnp.zeros((8 - b - 1, d), F32)], axis=0)
    mod = _mod(c8, w_mod[0], b_mod[0])
    sh_m, sc_m, g_m, sh_f, sc_f, g_f = [m[:b, None, :] for m in jnp.split(mod, 6, axis=-1)]
    sh_c, sc_c = mod[b:b + 1, None, :d], mod[b:b + 1, None, d:2 * d]

    wb = w_in[0].astype(BF16)
    cuts = np.cumsum([0, NA_W, NA_W, GLA_KEY_W, GLA_VAL_W, 2 * GLA_GATE_RANK, NA_W, GLA_KEY_W, GLA_VAL_W])
    w_nak, w_nav, w_gk, w_gv, w_ad, w_naq, w_gq, w_gg = [wb[:, cuts[i]:cuts[i + 1]] for i in range(8)]
    w_gvt = w_gv.T
    w_ad3 = jnp.concatenate([w_ad, w_ad, w_ad, jnp.zeros_like(w_ad)], axis=1)
    rope = _rope_tables(n)
    na_q, na_k, na_v, gq, gk, gv, gvt, ad, gg = _inproj(
        x, sh_m, sc_m, norm_mix_pre[0],
        [w_naq, w_nak, w_nav, w_gq, w_gk, w_gv, w_gvt, w_ad3, w_gg],
        ["plain", "plain", "plain", "rope", "rope", "plain", "t", "split", "plain"],
        [1.0, 1.0, 1.0, GLA_DK ** -0.5, 1.0, 1.0, 1.0, 1.0, 1.0],
        [BF16] * 9, rope=rope)
    c_nak, c_nav, c_gk, c_gvt, c_ad = _inproj(
        ctx, sh_c, sc_c, norm_mix_pre[0],
        [w_nak, w_nav, w_gk, w_gvt, w_ad3],
        ["plain", "plain", "plain", "t", "split"], [1.0] * 5, [BF16] * 5)

    o_na = _na(na_q, na_k, na_v, c_nak, c_nav, *_na_bias_tables(na_rpb[0], rows))

    cmats = jnp.asarray(_gla_prefix_matrices(GLA_T), BF16)
    o_gla = _gla(gq, gk, gv, gvt, ad, gg, c_gk, c_gvt, c_ad, _gla_decay_up(gla_a_up[0]),
                 gla_a_bias[0][:, None, :], gla_norm[0][None, :], cmats)

    wo = w_out[0].astype(BF16)
    router_pad = jnp.zeros((d, LANES), F32).at[:, :N_EXPERTS].set(router[0])
    x_new, hf, aff, aff_t = _outproj(o_na, o_gla, x, wo[:NA_W], wo[NA_W:], g_m, sh_f, sc_f,
                                     norm_mix_post[0], norm_ffn_pre[0],
                                     jnp.concatenate(_split_bf16(router_pad), axis=1))

    rsel_t, rsel_c = _route(aff_t, cap)
    xs = _sc_gather(rsel_t, hf.reshape(b * n, d // 2), cap).reshape(b, N_EXPERTS, cap, d // 2)
    ys = _ffn(xs, w_gate[0], w_up[0], w_down[0])
    picked = (rsel_t >= 0).astype(jnp.int32).reshape(b, N_EXPERTS, n // LANES, LANES).sum(axis=-1)
    slot_off = jnp.concatenate([jnp.zeros((b, N_EXPERTS, 1), jnp.int32), jnp.cumsum(picked, axis=-1)], axis=-1)
    return _combine(ys, slot_off, rsel_c, aff, x_new, g_f, norm_ffn_post[0])
```

```python
import functools

import numpy as np
import jax
import jax.numpy as jnp
from jax import lax
from jax.experimental import pallas as pl
from jax.experimental.pallas import tpu as pltpu
from jax.experimental.pallas import tpu_sc as plsc

F32 = jnp.float32
BF16 = jnp.bfloat16
HIGHEST = lax.Precision.HIGHEST

D_MODEL = 1024
GRID_W = 64
NA_W = 512
NA_HEADS = 8
NA_HEAD_DIM = 64
NA_WIN_ROWS = 8
NA_WIN_COLS = 16
GLA_HEADS = 4
GLA_DV = 128
GLA_DK = 64
GLA_KEY_W = 256
GLA_VAL_W = 512
GLA_GATE_RANK = 16
GLA_GATE_TAU = 16.0
ROPE_BASE = 10000.0
N_EXPERTS = 16
EC_CAPACITY_FACTOR = 2
NORM_EPS = 1e-6
NEG_BIG = -1e30

LANES = 128
BF16_ROWS = 16
VMEM_LIMIT = 56 * 1024 * 1024

TOK_TILE = 512
NA_QROWS = 4
NA_KROWS = NA_QROWS + NA_WIN_ROWS
NA_TAB_PAD = NA_QROWS
NA_TAB_BLOCKS = NA_TAB_PAD + 2 * NA_WIN_ROWS - 1 + NA_QROWS + 1
GLA_T = 256
GLA_LEVELS = (128, 64, 32, 16)
GLA_DIAG = 16
GLA_UNROLL = 2
OUTPROJ_SUBTILES = 2
FFN_TILE = 256
COMBINE_WIN = 128
SC_GATHER_ROWS = 64


def _cparams(sem):
    return pltpu.CompilerParams(dimension_semantics=sem, vmem_limit_bytes=VMEM_LIMIT)


def _rms(v, g):
    return v * lax.rsqrt(jnp.mean(v * v, axis=-1, keepdims=True) + NORM_EPS) * g


def _dot(a, b):
    return jnp.dot(a, b, preferred_element_type=F32)


def _dot_nt(a, b):
    return lax.dot_general(a, b, (((1,), (1,)), ((), ())), preferred_element_type=F32)


def _split_bf16(v):
    hi = v.astype(BF16)
    return hi, (v - hi.astype(F32)).astype(BF16)


_HIGH_HALF = -65536


def _pack_bf16_pairs(hb):
    bits = lax.bitcast_convert_type(hb.astype(F32), jnp.int32)
    half = hb.shape[1] // 2
    return lax.shift_right_logical(bits[:, :half], 16) | (bits[:, half:] & _HIGH_HALF)


def _unpack_bf16_pairs(w):
    lo = lax.bitcast_convert_type(lax.shift_left(w, 16), F32).astype(BF16)
    hi = lax.bitcast_convert_type(w & _HIGH_HALF, F32).astype(BF16)
    return jnp.concatenate([lo, hi], axis=1)


def _mod_kernel(c_ref, w_ref, b_ref, o_ref):
    c = c_ref[...]
    s = c / (1.0 + jnp.exp(-c))
    o_ref[...] = jnp.dot(s, w_ref[...], preferred_element_type=F32, precision=HIGHEST) + b_ref[...]


def _mod(c8, w_mod, b_mod):
    d, n = w_mod.shape
    tn = 1536
    return pl.pallas_call(
        _mod_kernel,
        out_shape=jax.ShapeDtypeStruct((8, n), F32),
        grid=(n // tn,),
        in_specs=[pl.BlockSpec((8, d), lambda j: (0, 0)),
                  pl.BlockSpec((d, tn), lambda j: (0, j)),
                  pl.BlockSpec((1, tn), lambda j: (0, j))],
        out_specs=pl.BlockSpec((8, tn), lambda j: (0, j)),
        compiler_params=_cparams(("arbitrary",)),
        name="mod",
    )(c8, w_mod, b_mod.reshape(1, n))


def _inproj_kernel(x_ref, sh_ref, sc_ref, g_ref, *refs, modes, scales):
    n_rope = 2 if "rope" in modes else 0
    rope_refs, refs = refs[:n_rope], refs[n_rope:]
    n_w = len(modes)
    w_refs, o_refs = refs[:n_w], refs[n_w:]
    x = x_ref[0]
    h = _rms(x, g_ref[...]) * (1.0 + sc_ref[0]) + sh_ref[0]
    hb = h.astype(BF16)
    for w_ref, o_ref, mode, scale in zip(w_refs, o_refs, modes, scales):
        y = _dot(hb, w_ref[...])
        if mode == "rope":
            cols = y.shape[1]
            quarter = GLA_DK // 4
            lane = lax.broadcasted_iota(jnp.int32, y.shape, 1)
            partner = jnp.where((lane & (2 * quarter - 1)) < quarter,
                                pltpu.roll(y, cols - quarter, axis=1), pltpu.roll(y, quarter, axis=1))
            y = (y * rope_refs[0][...] + partner * rope_refs[1][...]) * scale
        elif mode == "split":
            hi, lo = _split_bf16(y)
            lane = lax.broadcasted_iota(jnp.int32, y.shape, 1)
            rank2 = 2 * GLA_GATE_RANK
            y = jnp.where((lane >= rank2) & (lane < 2 * rank2), lo, hi)
        o_ref[0] = y.astype(o_ref.dtype)


def _inproj(x, shift, scale, g, weights, modes, scales, out_dtypes, rope=None):
    b, n, d = x.shape
    tm = min(TOK_TILE, n)
    per_sample = shift.shape[0] == b
    mod_map = (lambda i, j: (i, 0, 0)) if per_sample else (lambda i, j: (0, 0, 0))
    in_specs = [pl.BlockSpec((1, tm, d), lambda i, j: (i, j, 0)),
                pl.BlockSpec((1, 1, d), mod_map),
                pl.BlockSpec((1, 1, d), mod_map),
                pl.BlockSpec((1, d), lambda i, j: (0, 0))]
    args = [x, shift, scale, g.reshape(1, d)]
    if rope is not None:
        in_specs += [pl.BlockSpec((tm, rope[0].shape[1]), lambda i, j: (j, 0))] * 2
        args += list(rope)
    out_shapes, out_specs = [], []
    for w, mode, dt in zip(weights, modes, out_dtypes):
        in_specs.append(pl.BlockSpec(w.shape, lambda i, j: (0, 0)))
        cols = w.shape[1]
        out_shapes.append(jax.ShapeDtypeStruct((b, n, cols), dt))
        out_specs.append(pl.BlockSpec((1, tm, cols), lambda i, j: (i, j, 0)))
    return pl.pallas_call(
        functools.partial(_inproj_kernel, modes=tuple(modes), scales=tuple(scales)),
        out_shape=out_shapes,
        grid=(b, n // tm),
        in_specs=in_specs,
        out_specs=out_specs,
        compiler_params=_cparams(("arbitrary", "arbitrary")),
        name="inproj",
    )(*args, *weights)


def _rope_tables(n):
    t = np.arange(n)
    pos_row, pos_col = (t // GRID_W).astype(np.float32), (t % GRID_W).astype(np.float32)
    quarter = GLA_DK // 4
    freqs = np.float32(ROPE_BASE) ** (-np.arange(quarter, dtype=np.float32) / quarter)
    ang_r = pos_row[:, None] * freqs
    ang_c = pos_col[:, None] * freqs
    cos = np.concatenate([np.cos(ang_r), np.cos(ang_r), np.cos(ang_c), np.cos(ang_c)], axis=-1)
    sin = np.concatenate([-np.sin(ang_r), np.sin(ang_r), -np.sin(ang_c), np.sin(ang_c)], axis=-1)
    return jnp.asarray(np.tile(cos, (1, GLA_HEADS))), jnp.asarray(np.tile(sin, (1, GLA_HEADS)))


def _na_patterns(rows):
    kr = min(NA_WIN_ROWS, rows)
    n_blocks = rows // NA_QROWS
    pats = []
    for blk in (0, 1, n_blocks - 1):
        r0 = blk * NA_QROWS
        k0 = int(np.clip(r0 - kr // 2, 0, rows - NA_KROWS))
        strips = []
        for a in range(NA_QROWS):
            r_start = int(np.clip(r0 + a - kr // 2, 0, rows - kr))
            start = k0 - (r0 + a) + NA_WIN_ROWS - 1 + NA_TAB_PAD
            assert 0 <= start and start + NA_KROWS <= NA_TAB_BLOCKS
            strips.append((start, [r_start <= k0 + c < r_start + kr for c in range(NA_KROWS)]))
        pats.append(strips)
    return pats


def _na_bias_tables(rpb, rows):
    heads = rpb.shape[0]
    col = np.arange(GRID_W)
    c_start = np.clip(col - NA_WIN_COLS // 2, 0, GRID_W - NA_WIN_COLS)
    col_ok = (col[None, :] >= c_start[:, None]) & (col[None, :] < c_start[:, None] + NA_WIN_COLS)
    dc = np.clip(col[None, :] - col[:, None] + NA_WIN_COLS - 1, 0, 2 * NA_WIN_COLS - 2)
    sel_c = (np.arange(2 * NA_WIN_COLS - 1)[:, None, None] == dc[None]) & col_ok[None]
    t = jnp.einsum("hrd,dqk->hqrk", rpb, jnp.asarray(sel_c, F32), precision=HIGHEST)
    t = jnp.where(jnp.asarray(col_ok)[None, :, None, :], t, NEG_BIG)
    n_dr = 2 * NA_WIN_ROWS - 1
    t = t.reshape(heads, GRID_W, n_dr * GRID_W)
    back = NA_TAB_BLOCKS + 1 - NA_TAB_PAD - n_dr
    t = jnp.pad(t, ((0, 0), (0, 0), (NA_TAB_PAD * GRID_W, back * GRID_W)), constant_values=NEG_BIG)
    width = NA_TAB_BLOCKS * GRID_W
    tab = jnp.stack([t[:, :, :width], t[:, :, GRID_W:GRID_W + width]], axis=1)
    row_mask = np.zeros((3, NA_QROWS, 1, NA_KROWS * GRID_W), np.float32)
    for pat, strips in enumerate(_na_patterns(rows)):
        for a, (_, valid) in enumerate(strips):
            row_mask[pat, a, 0] = np.repeat(np.where(valid, 0.0, NEG_BIG), GRID_W)
    return tab, jnp.asarray(row_mask)


def _na_kernel(q_ref, k_ref, v_ref, kc_ref, vc_ref, tab_ref, rmask_ref, o_ref, bias_ref,
               sw_ref, sc_ref, pw_ref, pc_ref, linv_ref, *, rows):
    nq, nk = NA_QROWS * GRID_W, NA_KROWS * GRID_W
    n_blocks = rows // NA_QROWS
    kr = min(NA_WIN_ROWS, rows)
    scale = NA_HEAD_DIM ** -0.5
    lane = lax.broadcasted_iota(jnp.int32, (nq, LANES), 1)
    first_head = lane < NA_HEAD_DIM
    kc = kc_ref[0]
    vc = vc_ref[0]
    for pat, strips in enumerate(_na_patterns(rows)):
        for a, (start, _) in enumerate(strips):
            parity = start % 2
            off = (start - parity) * GRID_W
            for h in range(2):
                bias_ref[0, pat, h, a * GRID_W:(a + 1) * GRID_W, :] = (
                    tab_ref[h, parity, :, off:off + nk] + rmask_ref[pat, a])

    def key_start(i):
        return pl.multiple_of(jnp.clip(i * NA_QROWS - kr // 2, 0, rows - NA_KROWS) * GRID_W, GRID_W)

    def scores(i, slot):
        pat = jnp.where(i == 0, 0, jnp.where(i == n_blocks - 1, 2, 1))
        q = q_ref[0, pl.ds(pl.multiple_of(i * nq, nq), nq), :] * scale
        kw = k_ref[0, pl.ds(key_start(i), nk), :]
        for h in range(2):
            qh = jnp.where(first_head if h == 0 else jnp.logical_not(first_head), q, jnp.zeros_like(q))
            sw_ref[slot, h] = _dot_nt(qh, kw) + bias_ref[0, pat, h]
            sc_ref[slot, h] = _dot_nt(qh, kc)

    def softmax(slot):
        for h in range(2):
            s_w = sw_ref[slot, h]
            s_c = sc_ref[slot, h]
            m = jnp.maximum(jnp.max(s_w, axis=-1, keepdims=True), jnp.max(s_c, axis=-1, keepdims=True))
            p_w = jnp.exp(s_w - m)
            p_c = jnp.exp(s_c - m)
            l = jnp.sum(p_w, axis=-1, keepdims=True) + jnp.sum(p_c, axis=-1, keepdims=True)
            pw_ref[slot, h] = p_w.astype(BF16)
            pc_ref[slot, h] = p_c.astype(BF16)
            linv_ref[slot, h] = jnp.broadcast_to(1.0 / l, (nq, LANES))

    def values(i, slot):
        vw = v_ref[0, pl.ds(key_start(i), nk), :]
        outs = [(_dot(pw_ref[slot, h], vw) + _dot(pc_ref[slot, h], vc)) * linv_ref[slot, h] for h in range(2)]
        o = jnp.where(first_head, outs[0], outs[1])
        o_ref[0, pl.ds(pl.multiple_of(i * nq, nq), nq), :] = o.astype(o_ref.dtype)

    assert n_blocks % 2 == 0 and n_blocks >= 4
    scores(0, 0)
    softmax(0)
    scores(1, 1)

    def trip(j, carry):
        i = 2 * j
        values(i - 2, 0)
        softmax(1)
        scores(i, 0)
        values(i - 1, 1)
        softmax(0)
        scores(i + 1, 1)
        return carry

    lax.fori_loop(1, n_blocks // 2, trip, 0)
    values(n_blocks - 2, 0)
    softmax(1)
    values(n_blocks - 1, 1)


def _na(q, k, v, kc, vc, tab, row_mask):
    b, n, w = q.shape
    n_ctx = kc.shape[1]
    pairs = w // LANES
    rows = n // GRID_W
    nq, nk = NA_QROWS * GRID_W, NA_KROWS * GRID_W
    tok = lambda i, p: (i, 0, p)
    return pl.pallas_call(
        functools.partial(_na_kernel, rows=rows),
        out_shape=jax.ShapeDtypeStruct((b, n, w), BF16),
        grid=(b, pairs),
        in_specs=[pl.BlockSpec((1, n, LANES), tok),
                  pl.BlockSpec((1, n, LANES), tok),
                  pl.BlockSpec((1, n, LANES), tok),
                  pl.BlockSpec((1, n_ctx, LANES), tok),
                  pl.BlockSpec((1, n_ctx, LANES), tok),
                  pl.BlockSpec((2,) + tab.shape[1:], lambda i, p: (p, 0, 0, 0)),
                  pl.BlockSpec(row_mask.shape, lambda i, p: (0, 0, 0, 0))],
        out_specs=pl.BlockSpec((1, n, LANES), tok),
        scratch_shapes=[pltpu.VMEM((1, 3, 2, nq, nk), F32),
                        pltpu.VMEM((2, 2, nq, nk), F32), pltpu.VMEM((2, 2, nq, n_ctx), F32),
                        pltpu.VMEM((2, 2, nq, nk), BF16), pltpu.VMEM((2, 2, nq, n_ctx), BF16),
                        pltpu.VMEM((2, 2, nq, LANES), F32)],
        compiler_params=_cparams(("arbitrary", "arbitrary")),
        name="na",
    )(q, k, v, kc, vc, tab, row_mask)


def _gla_prefix_matrices(t):
    i = np.arange(t)
    return np.stack([i[:, None] >= i[None, :], i[:, None] <= i[None, :]]).astype(np.float32)


def _gla_kernel(q_ref, k_ref, v_ref, ad_ref, g_ref, ck_ref, cv_ref, cad_ref,
                u_ref, ab_ref, gn_ref, cm_ref, o_ref, accf_ref, accb_ref, *, n_tok):
    t = GLA_T
    n_chunks = n_tok // t
    row = lax.broadcasted_iota(jnp.int32, (t, LANES), 0)
    hb = t // 2
    row_b = lax.broadcasted_iota(jnp.int32, (hb, LANES), 0)
    head0_b = lax.broadcasted_iota(jnp.int32, (hb, LANES), 1) < GLA_DK
    row2 = lax.broadcasted_iota(jnp.int32, (hb, 2 * hb), 0)
    col2 = lax.broadcasted_iota(jnp.int32, (hb, 2 * hb), 1) & (hb - 1)
    vrow = lax.broadcasted_iota(jnp.int32, (2 * t, 2 * GLA_DV), 0)
    vlane = lax.broadcasted_iota(jnp.int32, (2 * t, 2 * GLA_DV), 1)
    v_head_match = (vrow >= t) == (vlane >= GLA_DV)
    srow = lax.broadcasted_iota(jnp.int32, (2 * GLA_DV, LANES), 0)
    slane = lax.broadcasted_iota(jnp.int32, (2 * GLA_DV, LANES), 1)
    s_blockdiag = (srow >= GLA_DV) == (slane >= GLA_DK)
    blk_mask = {half: jnp.where((row2 & ~(2 * half - 1)) == (col2 & ~(2 * half - 1)), 1.0, 0.0)
                for half in GLA_LEVELS if 2 * half < hb}
    diag_blk = (row2 & ~(GLA_DIAG - 1)) == (col2 & ~(GLA_DIAG - 1))
    diag_mask = (jnp.where(diag_blk & (row2 >= col2), 1.0, 0.0), jnp.where(diag_blk & (row2 <= col2), 1.0, 0.0))

    def prefix_sums(ad, dirn):
        z = _dot(ad, u_ref[dirn]) + ab_ref[dirn]
        loga = (jnp.minimum(z, 0.0) - jnp.log(1.0 + jnp.exp(-jnp.abs(z)))) * (1.0 / GLA_GATE_TAU)
        hi, lo = _split_bf16(loga)
        p2 = _dot(cm_ref[dirn], jnp.concatenate([hi, lo], axis=-1))
        return p2[:, :LANES] + p2[:, LANES:]

    def chunk_end(p, dirn):
        return p[t - 1:t, :] if dirn == 0 else p[0:1, :]

    def level_sums(p, half, dirn):
        blk = 2 * half
        p3 = p.reshape(t // blk, blk, LANES)
        edge = half - 1 if dirn == 0 else half
        ref = jnp.broadcast_to(p3[:, edge:edge + 1, :], p3.shape).reshape(t, LANES)
        later = (row & half) != 0
        return jnp.where(later == (dirn == 0), p - ref, ref - p)

    def state_update(s, k, vt, p, dirn):
        kh = (k * jnp.exp(chunk_end(p, dirn) - p)).astype(BF16)
        return s * jnp.exp(chunk_end(p, dirn)) + jnp.where(s_blockdiag, _dot(vt, kh), 0.0)

    def chunk(tok0, s, dirn):
        q = q_ref[0, pl.ds(tok0, t), :].astype(F32)
        k = k_ref[0, pl.ds(tok0, t), :].astype(F32)
        v = v_ref[0, pl.ds(tok0, t), :]
        vt = v.T
        p = prefix_sums(ad_ref[0, pl.ds(tok0, t), :], dirn)
        qh = (q * jnp.exp(p)).astype(BF16)
        o = _dot_nt(qh, s.astype(BF16))
        def rows(x, b):
            return x[b * hb:(b + 1) * hb]

        def scores(qt, kt):
            kcat = jnp.concatenate([jnp.where(head0_b, kt, 0.0), jnp.where(head0_b, 0.0, kt)], axis=0)
            return _dot_nt(qt.astype(BF16), kcat.astype(BF16))

        assert GLA_LEVELS[0] == hb
        w = jnp.exp(level_sums(p, hb, dirn))
        qb, kb = (1, 0) if dirn == 0 else (0, 1)
        wide = scores(rows(q, qb) * rows(w, qb), rows(k, kb) * rows(w, kb))
        fine_w = [jnp.exp(level_sums(p, half, dirn)) for half in GLA_LEVELS[1:]]
        e_d = level_sums(p, GLA_DIAG // 2, dirn)
        w_d, wi_d = jnp.exp(e_d), jnp.exp(-e_d)
        fine = []
        for b in range(2):
            qs, ks = rows(q, b), rows(k, b)
            acc = None
            for half, w in zip(GLA_LEVELS[1:], fine_w):
                later = (row_b & half) != 0
                q_side = later if dirn == 0 else jnp.logical_not(later)
                part = scores(jnp.where(q_side, qs * rows(w, b), 0.0), jnp.where(q_side, 0.0, ks * rows(w, b)))
                if half in blk_mask:
                    part = part * blk_mask[half]
                acc = part if acc is None else acc + part
            later = (row_b & (GLA_DIAG // 2)) != 0
            shrink_q = later if dirn == 0 else jnp.logical_not(later)
            part = scores(qs * jnp.where(shrink_q, rows(w_d, b), rows(wi_d, b)),
                          ks * jnp.where(shrink_q, rows(wi_d, b), rows(w_d, b)))
            fine.append(acc + jnp.where(diag_mask[dirn] > 0.5, part, 0.0))
        zero = jnp.zeros((hb, hb), F32)
        h0, h1 = slice(0, hb), slice(hb, 2 * hb)
        if dirn == 0:
            top = [fine[0][:, h0], zero, fine[0][:, h1], zero]
            bot = [wide[:, h0], fine[1][:, h0], wide[:, h1], fine[1][:, h1]]
        else:
            top = [fine[0][:, h0], wide[:, h0], fine[0][:, h1], wide[:, h1]]
            bot = [zero, fine[1][:, h0], zero, fine[1][:, h1]]
        a = jnp.concatenate([jnp.concatenate(top, axis=1), jnp.concatenate(bot, axis=1)], axis=0)
        vcat = jnp.concatenate([v, v], axis=0)
        vcat = jnp.where(v_head_match, vcat, jnp.zeros_like(vcat))
        o = o + _dot(a.astype(BF16), vcat)
        return o, state_update(s, k, vt, p, dirn)

    def ctx_state(dirn):
        p = prefix_sums(cad_ref[0], dirn)
        s0 = jnp.zeros((2 * GLA_DV, LANES), F32)
        return state_update(s0, ck_ref[0].astype(F32), cv_ref[0].T, p, dirn)

    def body(i, carry):
        s_f, s_b = carry
        for u in range(GLA_UNROLL):
            c = i * GLA_UNROLL + u
            tok_f = pl.multiple_of(c * t, t)
            tok_b = pl.multiple_of((n_chunks - 1 - c) * t, t)
            o_f, s_f = chunk(tok_f, s_f, 0)
            o_b, s_b = chunk(tok_b, s_b, 1)
            accf_ref[pl.ds(tok_f, t), :] = o_f
            accb_ref[pl.ds(tok_b, t), :] = o_b
        return s_f, s_b

    assert n_chunks % GLA_UNROLL == 0
    lax.fori_loop(0, n_chunks // GLA_UNROLL, body, (ctx_state(0), ctx_state(1)))

    def finish(c, carry):
        tok0 = pl.multiple_of(c * t, t)
        o = accf_ref[pl.ds(tok0, t), :] + accb_ref[pl.ds(tok0, t), :]
        g = g_ref[0, pl.ds(tok0, t), :].astype(F32)
        gate = g / (1.0 + jnp.exp(-g))
        halves = [_rms(o[:, h * GLA_DV:(h + 1) * GLA_DV], gn_ref[...]) for h in range(2)]
        o_ref[0, pl.ds(tok0, t), :] = (jnp.concatenate(halves, axis=-1) * gate).astype(o_ref.dtype)
        return carry

    lax.fori_loop(0, n_chunks, finish, 0)


def _gla(q, k, v, ad, g, ck, cv, cad, u, abias, gnorm, cmats):
    b, n, kw = q.shape
    n_ctx = ck.shape[1]
    pairs = kw // LANES
    vw = 2 * GLA_DV
    tok = lambda i, p: (i, 0, p)
    full3 = lambda i, p: (i, 0, 0)
    return pl.pallas_call(
        functools.partial(_gla_kernel, n_tok=n),
        out_shape=jax.ShapeDtypeStruct((b, n, v.shape[2]), BF16),
        grid=(b, pairs),
        in_specs=[pl.BlockSpec((1, n, LANES), tok),
                  pl.BlockSpec((1, n, LANES), tok),
                  pl.BlockSpec((1, n, vw), tok),
                  pl.BlockSpec((1, n, LANES), full3),
                  pl.BlockSpec((1, n, vw), tok),
                  pl.BlockSpec((1, n_ctx, LANES), tok),
                  pl.BlockSpec((1, n_ctx, vw), tok),
                  pl.BlockSpec((1, n_ctx, LANES), full3),
                  pl.BlockSpec((2, LANES, LANES), lambda i, p: (0, 0, p)),
                  pl.BlockSpec((2, 1, LANES), lambda i, p: (0, 0, p)),
                  pl.BlockSpec((1, GLA_DV), lambda i, p: (0, 0)),
                  pl.BlockSpec(cmats.shape, lambda i, p: (0, 0, 0))],
        out_specs=pl.BlockSpec((1, n, vw), tok),
        scratch_shapes=[pltpu.VMEM((n, vw), F32), pltpu.VMEM((n, vw), F32)],
        compiler_params=_cparams(("arbitrary", "arbitrary")),
        name="gla",
    )(q, k, v, ad, g, ck, cv, cad, u, abias, gnorm, cmats)


def _gla_decay_up(a_up):
    r = GLA_GATE_RANK
    pad = jnp.zeros((2, 2 * r, GLA_KEY_W), F32)
    pad = pad.at[0, :r].set(a_up[0]).at[1, r:].set(a_up[1])
    hi, lo = _split_bf16(pad)
    return jnp.concatenate([hi, hi, lo, jnp.zeros_like(hi)], axis=1)


def _outproj_kernel(ona_ref, ogla_ref, x_ref, w1_ref, w2_ref, gm_ref, shf_ref, scf_ref, npost_ref,
                    nfpre_ref, rt_ref, xnew_ref, hf_ref, aff_ref, afft_ref):
    for r0 in range(0, x_ref.shape[1], TOK_TILE):
        rs = slice(r0, r0 + TOK_TILE)
        mix = _dot(ona_ref[0, rs], w1_ref[...]) + _dot(ogla_ref[0, rs], w2_ref[...])
        xn = x_ref[0, rs] + gm_ref[0] * _rms(mix, npost_ref[...])
        xnew_ref[0, rs] = xn
        h = _rms(xn, nfpre_ref[...]) * (1.0 + scf_ref[0]) + shf_ref[0]
        h_hi, h_lo = _split_bf16(h)
        hf_ref[0, rs] = _pack_bf16_pairs(h_hi)
        res = _dot(h_hi, rt_ref[...])
        logits = res[:, :LANES] + res[:, LANES:] + _dot(h_lo, rt_ref[:, :LANES])
        lane = lax.broadcasted_iota(jnp.int32, logits.shape, 1)
        logits = jnp.where(lane < N_EXPERTS, logits, NEG_BIG)
        p = jnp.exp(logits - jnp.max(logits, axis=-1, keepdims=True))
        aff = p / jnp.sum(p, axis=-1, keepdims=True)
        aff_ref[0, rs] = aff
        afft_ref[0, :, rs] = aff.T[:N_EXPERTS, :]


def _outproj(o_na, o_gla, x, w1, w2, gm, shf, scf, npost, nfpre, router_cat):
    b, n, d = x.shape
    tm = OUTPROJ_SUBTILES * TOK_TILE
    tokmap = lambda i, j: (i, j, 0)
    smp = lambda i, j: (i, 0, 0)
    cst = lambda i, j: (0, 0)
    return pl.pallas_call(
        _outproj_kernel,
        out_shape=[jax.ShapeDtypeStruct((b, n, d), F32),
                   jax.ShapeDtypeStruct((b, n, d // 2), jnp.int32),
                   jax.ShapeDtypeStruct((b, n, LANES), F32),
                   jax.ShapeDtypeStruct((b, N_EXPERTS, n), F32)],
        grid=(b, n // tm),
        in_specs=[pl.BlockSpec((1, tm, o_na.shape[2]), tokmap),
                  pl.BlockSpec((1, tm, o_gla.shape[2]), tokmap),
                  pl.BlockSpec((1, tm, d), tokmap),
                  pl.BlockSpec(w1.shape, cst),
                  pl.BlockSpec(w2.shape, cst),
                  pl.BlockSpec((1, 1, d), smp),
                  pl.BlockSpec((1, 1, d), smp),
                  pl.BlockSpec((1, 1, d), smp),
                  pl.BlockSpec((1, d), cst),
                  pl.BlockSpec((1, d), cst),
                  pl.BlockSpec(router_cat.shape, cst)],
        out_specs=[pl.BlockSpec((1, tm, d), tokmap),
                   pl.BlockSpec((1, tm, d // 2), tokmap),
                   pl.BlockSpec((1, tm, LANES), tokmap),
                   pl.BlockSpec((1, N_EXPERTS, tm), lambda i, j: (i, 0, j))],
        compiler_params=_cparams(("arbitrary", "arbitrary")),
        name="outproj",
    )(o_na, o_gla, x, w1, w2, gm, shf, scf, npost.reshape(1, d), nfpre.reshape(1, d), router_cat)


def _route_kernel(afft_ref, rt_ref, rc_ref, *, cap):
    a = afft_ref[0]
    e, n = a.shape
    capf = jnp.float32(cap)

    def search(i, thr_bits):
        cand = thr_bits | lax.shift_left(jnp.int32(1), 30 - i)
        cnt = jnp.sum(jnp.where(a >= lax.bitcast_convert_type(cand, F32), 1.0, 0.0), axis=-1, keepdims=True)
        return jnp.where(cnt >= capf, cand, thr_bits)

    thr_bits = lax.fori_loop(0, 31, search, jnp.zeros((e, 1), jnp.int32))
    thr = lax.bitcast_convert_type(thr_bits, F32)
    need = capf - jnp.sum(jnp.where(a > thr, 1.0, 0.0), axis=-1, keepdims=True)
    r_i = lax.broadcasted_iota(jnp.int32, (LANES, LANES), 0)
    c_i = lax.broadcasted_iota(jnp.int32, (LANES, LANES), 1)
    incl = jnp.where(r_i <= c_i, 1.0, 0.0).astype(BF16)
    off_eq = jnp.zeros((e, 1), F32)
    off_sel = jnp.zeros((e, 1), F32)
    pad = jnp.full((LANES - e, LANES), -1.0, F32)
    for j in range(n // LANES):
        sl = slice(j * LANES, (j + 1) * LANES)
        a_b = a[:, sl]
        eq_b = jnp.where(a_b == thr, 1.0, 0.0)
        tie_rank = _dot(eq_b.astype(BF16), incl) - eq_b + off_eq
        off_eq = off_eq + jnp.sum(eq_b, axis=-1, keepdims=True)
        sel_b = jnp.where(a_b > thr, 1.0, jnp.where(tie_rank < need, eq_b, 0.0))
        sel = sel_b > 0.5
        rank = _dot(sel_b.astype(BF16), incl) - sel_b + off_sel
        off_sel = off_sel + jnp.sum(sel_b, axis=-1, keepdims=True)
        rsel = jnp.where(sel, rank, -1.0)
        rt_ref[0, :, sl] = rsel.astype(jnp.int32)
        rc_ref[0, sl, :] = jnp.concatenate([rsel, pad], axis=0).T.astype(jnp.int32)


def _route(afft, cap):
    b, e, n = afft.shape
    return pl.pallas_call(
        functools.partial(_route_kernel, cap=cap),
        out_shape=[jax.ShapeDtypeStruct((b, e, n), jnp.int32),
                   jax.ShapeDtypeStruct((b, n, LANES), jnp.int32)],
        grid=(b,),
        in_specs=[pl.BlockSpec((1, e, n), lambda i: (i, 0, 0))],
        out_specs=[pl.BlockSpec((1, e, n), lambda i: (i, 0, 0)),
                   pl.BlockSpec((1, n, LANES), lambda i: (i, 0, 0))],
        compiler_params=_cparams(("arbitrary",)),
        name="route",
    )(afft)


def _sc_gather(rsel_t, hf2, cap):
    b, e, n = rsel_t.shape
    width = hf2.shape[1]
    info = plsc.get_sparse_core_info()
    nc, lanes = info.num_cores, info.num_lanes
    workers = nc * info.num_subcores
    items = b * e
    assert items % workers == 0 and n % lanes == 0 and cap % SC_GATHER_ROWS == 0
    per_worker = items // workers
    mesh = plsc.VectorSubcoreMesh(core_axis_name="c", subcore_axis_name="s")

    def body(rank_hbm, hf_hbm, out_hbm, rank_v, idx_v, rows_v, sem):
        wid = lax.axis_index("s") * nc + lax.axis_index("c")
        for k in range(per_worker):
            item = wid * per_worker + k
            base_tok = (item // e) * n
            pltpu.sync_copy(rank_hbm.at[item], rank_v)

            @pl.loop(0, n // lanes)
            def _(j):
                r = rank_v[pl.ds(j * lanes, lanes)]
                tok = lax.iota(jnp.int32, lanes) + (j * lanes + base_tok)
                plsc.store_scatter(idx_v, [r], tok, mask=r >= 0)

            @pl.loop(0, cap // SC_GATHER_ROWS)
            def _(c):
                rows = pl.ds(c * SC_GATHER_ROWS, SC_GATHER_ROWS)
                pltpu.async_copy(hf_hbm.at[idx_v.at[rows]], rows_v, sem).wait()
                pltpu.sync_copy(rows_v, out_hbm.at[pl.ds(item * cap + c * SC_GATHER_ROWS, SC_GATHER_ROWS)])

    return pl.kernel(
        body, out_type=jax.ShapeDtypeStruct((items * cap, width), hf2.dtype), mesh=mesh,
        scratch_types=[pltpu.VMEM((n,), jnp.int32), pltpu.VMEM((cap,), jnp.int32),
                       pltpu.VMEM((SC_GATHER_ROWS, width), hf2.dtype), pltpu.SemaphoreType.DMA],
        compiler_params=pltpu.CompilerParams(needs_layout_passes=False),
        name="scgather",
    )(rsel_t.reshape(items, n), hf2)


def _ffn_kernel(x_ref, wg_ref, wu_ref, wd_ref, o_ref, acc_ref, xb_ref):
    f = pl.program_id(1)
    b = x_ref.shape[0]
    wg = wg_ref[0].astype(BF16)
    wu = wu_ref[0].astype(BF16)
    wd = wd_ref[0].astype(BF16)

    @pl.when(f == 0)
    def _():
        acc_ref[...] = jnp.zeros_like(acc_ref)
        for i in range(b):
            xb_ref[i] = _unpack_bf16_pairs(x_ref[i, 0])

    for i in range(b):
        x = xb_ref[i]
        g = _dot(x, wg)
        u = _dot(x, wu)
        hid = (g / (1.0 + jnp.exp(-g)) * u).astype(BF16)
        acc_ref[i] += _dot(hid, wd)

    @pl.when(f == pl.num_programs(1) - 1)
    def _():
        for i in range(b):
            o_ref[i, 0] = acc_ref[i].astype(o_ref.dtype)


def _ffn(xs, w_gate, w_up, w_down):
    b, e, cap, dp = xs.shape
    d = 2 * dp
    dff = w_gate.shape[2]
    tf = FFN_TILE
    return pl.pallas_call(
        _ffn_kernel,
        out_shape=jax.ShapeDtypeStruct((b, e, cap, d), BF16),
        grid=(e, dff // tf),
        in_specs=[pl.BlockSpec((b, 1, cap, dp), lambda i, f: (0, i, 0, 0)),
                  pl.BlockSpec((1, d, tf), lambda i, f: (i, 0, f)),
                  pl.BlockSpec((1, d, tf), lambda i, f: (i, 0, f)),
                  pl.BlockSpec((1, tf, d), lambda i, f: (i, f, 0))],
        out_specs=pl.BlockSpec((b, 1, cap, d), lambda i, f: (0, i, 0, 0)),
        scratch_shapes=[pltpu.VMEM((b, cap, d), F32), pltpu.VMEM((b, cap, d), BF16)],
        compiler_params=_cparams(("arbitrary", "arbitrary")),
        name="ffn",
    )(xs, w_gate, w_up, w_down)


def _combine_kernel(off_ref, ys_ref, rc_ref, aff_ref, xn_ref, gf_ref, npost_ref, o_ref, acc_ref):
    bi, tt = pl.program_id(0), pl.program_id(1)
    tm = rc_ref.shape[1]
    n_experts, cap = ys_ref.shape[1], ys_ref.shape[2]
    blocks = tm // LANES
    slot = lax.broadcasted_iota(jnp.int32, (tm, COMBINE_WIN), 1)

    def window(e, w0, j):
        nominal = w0 + j * COMBINE_WIN
        start = pl.multiple_of(jnp.minimum(nominal, cap - COMBINE_WIN), BF16_ROWS)
        rank = rc_ref[0, :, e:e + 1]
        hit = ((rank - start) == slot) & (rank >= nominal)
        weights = jnp.where(hit, aff_ref[0, :, e:e + 1], 0.0).astype(BF16)
        return weights, ys_ref[0, e, pl.ds(start, COMBINE_WIN), :]

    first, extra = [], []
    for e in range(n_experts):
        r0 = off_ref[bi, e, tt * blocks]
        r1 = off_ref[bi, e, (tt + 1) * blocks]
        w0 = (r0 // BF16_ROWS) * BF16_ROWS
        first.append(w0)
        extra.append(jnp.maximum((r1 - w0 + COMBINE_WIN - 1) // COMBINE_WIN - 1, 0))
    terms = []
    for e in range(0, n_experts, 2):
        (wa, ya), (wb, yb) = window(e, first[e], 0), window(e + 1, first[e + 1], 0)
        terms.append(_dot(jnp.concatenate([wa, wb], axis=1), jnp.concatenate([ya, yb], axis=0)))
    acc_ref[...] = functools.reduce(lambda a, c: a + c, terms)

    @pl.when(functools.reduce(lambda a, c: a + c, extra) > 0)
    def _():
        for e in range(n_experts):
            def more(j, carry, e=e):
                acc_ref[...] += _dot(*window(e, first[e], j))
                return carry
            lax.fori_loop(1, extra[e] + 1, more, 0)

    o_ref[0] = xn_ref[0] + gf_ref[0] * _rms(acc_ref[...], npost_ref[...])


def _combine(ys, slot_off, rsel_c, aff, x_new, gf, npost):
    b, e, cap, d = ys.shape
    n = x_new.shape[1]
    tm = TOK_TILE
    tokmap = lambda i, j, off: (i, j, 0)
    return pl.pallas_call(
        _combine_kernel,
        out_shape=jax.ShapeDtypeStruct((b, n, d), F32),
        grid_spec=pltpu.PrefetchScalarGridSpec(
            num_scalar_prefetch=1,
            grid=(b, n // tm),
            in_specs=[pl.BlockSpec((1, e, cap, d), lambda i, j, off: (i, 0, 0, 0)),
                      pl.BlockSpec((1, tm, LANES), tokmap),
                      pl.BlockSpec((1, tm, LANES), tokmap),
                      pl.BlockSpec((1, tm, d), tokmap),
                      pl.BlockSpec((1, 1, d), lambda i, j, off: (i, 0, 0)),
                      pl.BlockSpec((1, d), lambda i, j, off: (0, 0))],
            out_specs=pl.BlockSpec((1, tm, d), tokmap),
            scratch_shapes=[pltpu.VMEM((tm, d), F32)]),
        compiler_params=_cparams(("arbitrary", "arbitrary")),
        name="combine",
    )(slot_off, ys, rsel_c, aff, x_new, gf, npost.reshape(1, d))


def kernel(x, c, ctx, c_ctx, w_mod, b_mod, norm_mix_pre, norm_mix_post, norm_ffn_pre, norm_ffn_post,
           w_in, na_rpb, gla_a_up, gla_a_bias, gla_norm, w_out, router, w_gate, w_up, w_down):
    b, n, d = x.shape
    assert w_mod.shape[0] == 1 and d == D_MODEL and n % (GRID_W * NA_QROWS) == 0 and n % GLA_T == 0
    assert ctx.shape[1] == GLA_T
    rows = n // GRID_W
    cap = EC_CAPACITY_FACTOR * n // N_EXPERTS

    c8 = jnp.concatenate([c, c_ctx[None, :], jnp.zeros((8 - b - 1, d), F32)], axis=0)
    mod = _mod(c8, w_mod[0], b_mod[0])
    sh_m, sc_m, g_m, sh_f, sc_f, g_f = [m[:b, None, :] for m in jnp.split(mod, 6, axis=-1)]
    sh_c, sc_c = mod[b:b + 1, None, :d], mod[b:b + 1, None, d:2 * d]

    wb = w_in[0].astype(BF16)
    cuts = np.cumsum([0, NA_W, NA_W, GLA_KEY_W, GLA_VAL_W, 2 * GLA_GATE_RANK, NA_W, GLA_KEY_W, GLA_VAL_W])
    w_nak, w_nav, w_gk, w_gv, w_ad, w_naq, w_gq, w_gg = [wb[:, cuts[i]:cuts[i + 1]] for i in range(8)]
    w_ad3 = jnp.concatenate([w_ad, w_ad, w_ad, jnp.zeros_like(w_ad)], axis=1)
    rope = _rope_tables(n)
    na_q, na_k, na_v, gq, gk, gv, ad, gg = _inproj(
        x, sh_m, sc_m, norm_mix_pre[0],
        [w_naq, w_nak, w_nav, w_gq, w_gk, w_gv, w_ad3, w_gg],
        ["plain", "plain", "plain", "rope", "rope", "plain", "split", "plain"],
        [1.0, 1.0, 1.0, GLA_DK ** -0.5, 1.0, 1.0, 1.0, 1.0],
        [BF16] * 8, rope=rope)
    c_nak, c_nav, c_gk, c_gv, c_ad = _inproj(
        ctx, sh_c, sc_c, norm_mix_pre[0],
        [w_nak, w_nav, w_gk, w_gv, w_ad3],
        ["plain", "plain", "plain", "plain", "split"], [1.0] * 5, [BF16] * 5)

    o_na = _na(na_q, na_k, na_v, c_nak, c_nav, *_na_bias_tables(na_rpb[0], rows))

    cmats = jnp.asarray(_gla_prefix_matrices(GLA_T), BF16)
    o_gla = _gla(gq, gk, gv, ad, gg, c_gk, c_gv, c_ad, _gla_decay_up(gla_a_up[0]),
                 gla_a_bias[0][:, None, :], gla_norm[0][None, :], cmats)

    wo = w_out[0].astype(BF16)
    router_pad = jnp.zeros((d, LANES), F32).at[:, :N_EXPERTS].set(router[0])
    x_new, hf, aff, aff_t = _outproj(o_na, o_gla, x, wo[:NA_W], wo[NA_W:], g_m, sh_f, sc_f,
                                     norm_mix_post[0], norm_ffn_pre[0],
                                     jnp.concatenate(_split_bf16(router_pad), axis=1))

    rsel_t, rsel_c = _route(aff_t, cap)
    xs = _sc_gather(rsel_t, hf.reshape(b * n, d // 2), cap).reshape(b, N_EXPERTS, cap, d // 2)
    ys = _ffn(xs, w_gate[0], w_up[0], w_down[0])
    picked = (rsel_t >= 0).astype(jnp.int32).reshape(b, N_EXPERTS, n // LANES, LANES).sum(axis=-1)
    slot_off = jnp.concatenate([jnp.zeros((b, N_EXPERTS, 1), jnp.int32), jnp.cumsum(picked, axis=-1)], axis=-1)
    return _combine(ys, slot_off, rsel_c, aff, x_new, g_f, norm_ffn_post[0])
```

```python
import functools

import numpy as np
import jax
import jax.numpy as jnp
from jax import lax
from jax.experimental import pallas as pl
from jax.experimental.pallas import tpu as pltpu
from jax.experimental.pallas import tpu_sc as plsc

F32 = jnp.float32
BF16 = jnp.bfloat16
HIGHEST = lax.Precision.HIGHEST

D_MODEL = 1024
GRID_W = 64
NA_W = 512
NA_HEADS = 8
NA_HEAD_DIM = 64
NA_WIN_ROWS = 8
NA_WIN_COLS = 16
GLA_HEADS = 4
GLA_DV = 128
GLA_DK = 64
GLA_KEY_W = 256
GLA_VAL_W = 512
GLA_GATE_RANK = 16
GLA_GATE_TAU = 16.0
ROPE_BASE = 10000.0
N_EXPERTS = 16
EC_CAPACITY_FACTOR = 2
NORM_EPS = 1e-6
NEG_BIG = -1e30

LANES = 128
BF16_ROWS = 16
VMEM_LIMIT = 56 * 1024 * 1024

TOK_TILE = 512
NA_QROWS = 4
NA_KROWS = NA_QROWS + NA_WIN_ROWS
NA_TAB_PAD = NA_QROWS
NA_TAB_BLOCKS = NA_TAB_PAD + 2 * NA_WIN_ROWS - 1 + NA_QROWS + 1
GLA_T = 256
GLA_LEVELS = (128, 64, 32, 16)
GLA_DIAG = 16
GLA_UNROLL = 2
OUTPROJ_SUBTILES = 2
FFN_TILE = 256
COMBINE_WIN = 128
SC_GATHER_ROWS = 64


def _cparams(sem):
    return pltpu.CompilerParams(dimension_semantics=sem, vmem_limit_bytes=VMEM_LIMIT)


def _rms(v, g):
    return v * lax.rsqrt(jnp.mean(v * v, axis=-1, keepdims=True) + NORM_EPS) * g


def _dot(a, b):
    return jnp.dot(a, b, preferred_element_type=F32)


def _dot_nt(a, b):
    return lax.dot_general(a, b, (((1,), (1,)), ((), ())), preferred_element_type=F32)


def _split_bf16(v):
    hi = v.astype(BF16)
    return hi, (v - hi.astype(F32)).astype(BF16)


_HIGH_HALF = -65536


def _pack_bf16_pairs(hb):
    bits = lax.bitcast_convert_type(hb.astype(F32), jnp.int32)
    half = hb.shape[1] // 2
    return lax.shift_right_logical(bits[:, :half], 16) | (bits[:, half:] & _HIGH_HALF)


def _unpack_bf16_pairs(w):
    lo = lax.bitcast_convert_type(lax.shift_left(w, 16), F32).astype(BF16)
    hi = lax.bitcast_convert_type(w & _HIGH_HALF, F32).astype(BF16)
    return jnp.concatenate([lo, hi], axis=1)


def _mod_kernel(c_ref, w_ref, b_ref, o_ref):
    c = c_ref[...]
    s = c / (1.0 + jnp.exp(-c))
    o_ref[...] = jnp.dot(s, w_ref[...], preferred_element_type=F32, precision=HIGHEST) + b_ref[...]


def _mod(c8, w_mod, b_mod):
    d, n = w_mod.shape
    tn = 1536
    return pl.pallas_call(
        _mod_kernel,
        out_shape=jax.ShapeDtypeStruct((8, n), F32),
        grid=(n // tn,),
        in_specs=[pl.BlockSpec((8, d), lambda j: (0, 0)),
                  pl.BlockSpec((d, tn), lambda j: (0, j)),
                  pl.BlockSpec((1, tn), lambda j: (0, j))],
        out_specs=pl.BlockSpec((8, tn), lambda j: (0, j)),
        compiler_params=_cparams(("arbitrary",)),
        name="mod",
    )(c8, w_mod, b_mod.reshape(1, n))


def _inproj_kernel(x_ref, sh_ref, sc_ref, g_ref, *refs, modes, scales):
    n_rope = 2 if "rope" in modes else 0
    rope_refs, refs = refs[:n_rope], refs[n_rope:]
    n_w = len(modes)
    w_refs, o_refs = refs[:n_w], refs[n_w:]
    x = x_ref[0]
    h = _rms(x, g_ref[...]) * (1.0 + sc_ref[0]) + sh_ref[0]
    hb = h.astype(BF16)
    for w_ref, o_ref, mode, scale in zip(w_refs, o_refs, modes, scales):
        y = _dot(hb, w_ref[...])
        if mode == "rope":
            cols = y.shape[1]
            quarter = GLA_DK // 4
            lane = lax.broadcasted_iota(jnp.int32, y.shape, 1)
            partner = jnp.where((lane & (2 * quarter - 1)) < quarter,
                                pltpu.roll(y, cols - quarter, axis=1), pltpu.roll(y, quarter, axis=1))
            y = (y * rope_refs[0][...] + partner * rope_refs[1][...]) * scale
        elif mode == "split":
            hi, lo = _split_bf16(y)
            lane = lax.broadcasted_iota(jnp.int32, y.shape, 1)
            rank2 = 2 * GLA_GATE_RANK
            y = jnp.where((lane >= rank2) & (lane < 2 * rank2), lo, hi)
        o_ref[0] = y.astype(o_ref.dtype)


def _inproj(x, shift, scale, g, weights, modes, scales, out_dtypes, rope=None):
    b, n, d = x.shape
    tm = min(TOK_TILE, n)
    per_sample = shift.shape[0] == b
    mod_map = (lambda i, j: (i, 0, 0)) if per_sample else (lambda i, j: (0, 0, 0))
    in_specs = [pl.BlockSpec((1, tm, d), lambda i, j: (i, j, 0)),
                pl.BlockSpec((1, 1, d), mod_map),
                pl.BlockSpec((1, 1, d), mod_map),
                pl.BlockSpec((1, d), lambda i, j: (0, 0))]
    args = [x, shift, scale, g.reshape(1, d)]
    if rope is not None:
        in_specs += [pl.BlockSpec((tm, rope[0].shape[1]), lambda i, j: (j, 0))] * 2
        args += list(rope)
    out_shapes, out_specs = [], []
    for w, mode, dt in zip(weights, modes, out_dtypes):
        in_specs.append(pl.BlockSpec(w.shape, lambda i, j: (0, 0)))
        cols = w.shape[1]
        out_shapes.append(jax.ShapeDtypeStruct((b, n, cols), dt))
        out_specs.append(pl.BlockSpec((1, tm, cols), lambda i, j: (i, j, 0)))
    return pl.pallas_call(
        functools.partial(_inproj_kernel, modes=tuple(modes), scales=tuple(scales)),
        out_shape=out_shapes,
        grid=(b, n // tm),
        in_specs=in_specs,
        out_specs=out_specs,
        compiler_params=_cparams(("arbitrary", "arbitrary")),
        name="inproj",
    )(*args, *weights)


def _rope_tables(n):
    t = np.arange(n)
    pos_row, pos_col = (t // GRID_W).astype(np.float32), (t % GRID_W).astype(np.float32)
    quarter = GLA_DK // 4
    freqs = np.float32(ROPE_BASE) ** (-np.arange(quarter, dtype=np.float32) / quarter)
    ang_r = pos_row[:, None] * freqs
    ang_c = pos_col[:, None] * freqs
    cos = np.concatenate([np.cos(ang_r), np.cos(ang_r), np.cos(ang_c), np.cos(ang_c)], axis=-1)
    sin = np.concatenate([-np.sin(ang_r), np.sin(ang_r), -np.sin(ang_c), np.sin(ang_c)], axis=-1)
    return jnp.asarray(np.tile(cos, (1, GLA_HEADS))), jnp.asarray(np.tile(sin, (1, GLA_HEADS)))


def _na_patterns(rows):
    kr = min(NA_WIN_ROWS, rows)
    n_blocks = rows // NA_QROWS
    pats = []
    for blk in (0, 1, n_blocks - 1):
        r0 = blk * NA_QROWS
        k0 = int(np.clip(r0 - kr // 2, 0, rows - NA_KROWS))
        strips = []
        for a in range(NA_QROWS):
            r_start = int(np.clip(r0 + a - kr // 2, 0, rows - kr))
            start = k0 - (r0 + a) + NA_WIN_ROWS - 1 + NA_TAB_PAD
            assert 0 <= start and start + NA_KROWS <= NA_TAB_BLOCKS
            strips.append((start, [r_start <= k0 + c < r_start + kr for c in range(NA_KROWS)]))
        pats.append(strips)
    return pats


def _na_bias_tables(rpb, rows):
    heads = rpb.shape[0]
    col = np.arange(GRID_W)
    c_start = np.clip(col - NA_WIN_COLS // 2, 0, GRID_W - NA_WIN_COLS)
    col_ok = (col[None, :] >= c_start[:, None]) & (col[None, :] < c_start[:, None] + NA_WIN_COLS)
    dc = np.clip(col[None, :] - col[:, None] + NA_WIN_COLS - 1, 0, 2 * NA_WIN_COLS - 2)
    sel_c = (np.arange(2 * NA_WIN_COLS - 1)[:, None, None] == dc[None]) & col_ok[None]
    t = jnp.einsum("hrd,dqk->hqrk", rpb, jnp.asarray(sel_c, F32), precision=HIGHEST)
    t = jnp.where(jnp.asarray(col_ok)[None, :, None, :], t, NEG_BIG)
    n_dr = 2 * NA_WIN_ROWS - 1
    t = t.reshape(heads, GRID_W, n_dr * GRID_W)
    back = NA_TAB_BLOCKS + 1 - NA_TAB_PAD - n_dr
    t = jnp.pad(t, ((0, 0), (0, 0), (NA_TAB_PAD * GRID_W, back * GRID_W)), constant_values=NEG_BIG)
    width = NA_TAB_BLOCKS * GRID_W
    tab = jnp.stack([t[:, :, :width], t[:, :, GRID_W:GRID_W + width]], axis=1)
    row_mask = np.zeros((3, NA_QROWS, 1, NA_KROWS * GRID_W), np.float32)
    for pat, strips in enumerate(_na_patterns(rows)):
        for a, (_, valid) in enumerate(strips):
            row_mask[pat, a, 0] = np.repeat(np.where(valid, 0.0, NEG_BIG), GRID_W)
    return tab, jnp.asarray(row_mask)


def _na_kernel(q_ref, k_ref, v_ref, kc_ref, vc_ref, tab_ref, rmask_ref, o_ref, bias_ref,
               sw_ref, sc_ref, pw_ref, pc_ref, linv_ref, *, rows):
    nq, nk = NA_QROWS * GRID_W, NA_KROWS * GRID_W
    n_blocks = rows // NA_QROWS
    kr = min(NA_WIN_ROWS, rows)
    scale = NA_HEAD_DIM ** -0.5
    lane = lax.broadcasted_iota(jnp.int32, (nq, LANES), 1)
    first_head = lane < NA_HEAD_DIM
    kc = kc_ref[0]
    vc = vc_ref[0]
    for pat, strips in enumerate(_na_patterns(rows)):
        for a, (start, _) in enumerate(strips):
            parity = start % 2
            off = (start - parity) * GRID_W
            for h in range(2):
                bias_ref[0, pat, h, a * GRID_W:(a + 1) * GRID_W, :] = (
                    tab_ref[h, parity, :, off:off + nk] + rmask_ref[pat, a])

    def key_start(i):
        return pl.multiple_of(jnp.clip(i * NA_QROWS - kr // 2, 0, rows - NA_KROWS) * GRID_W, GRID_W)

    def scores(i, slot):
        pat = jnp.where(i == 0, 0, jnp.where(i == n_blocks - 1, 2, 1))
        q = q_ref[0, pl.ds(pl.multiple_of(i * nq, nq), nq), :] * scale
        kw = k_ref[0, pl.ds(key_start(i), nk), :]
        for h in range(2):
            qh = jnp.where(first_head if h == 0 else jnp.logical_not(first_head), q, jnp.zeros_like(q))
            sw_ref[slot, h] = _dot_nt(qh, kw) + bias_ref[0, pat, h]
            sc_ref[slot, h] = _dot_nt(qh, kc)

    def softmax(slot):
        for h in range(2):
            s_w = sw_ref[slot, h]
            s_c = sc_ref[slot, h]
            m = jnp.maximum(jnp.max(s_w, axis=-1, keepdims=True), jnp.max(s_c, axis=-1, keepdims=True))
            p_w = jnp.exp(s_w - m)
            p_c = jnp.exp(s_c - m)
            l = jnp.sum(p_w, axis=-1, keepdims=True) + jnp.sum(p_c, axis=-1, keepdims=True)
            pw_ref[slot, h] = p_w.astype(BF16)
            pc_ref[slot, h] = p_c.astype(BF16)
            linv_ref[slot, h] = jnp.broadcast_to(1.0 / l, (nq, LANES))

    def values(i, slot):
        vw = v_ref[0, pl.ds(key_start(i), nk), :]
        outs = [(_dot(pw_ref[slot, h], vw) + _dot(pc_ref[slot, h], vc)) * linv_ref[slot, h] for h in range(2)]
        o = jnp.where(first_head, outs[0], outs[1])
        o_ref[0, pl.ds(pl.multiple_of(i * nq, nq), nq), :] = o.astype(o_ref.dtype)

    assert n_blocks % 2 == 0 and n_blocks >= 4
    scores(0, 0)
    softmax(0)
    scores(1, 1)

    def trip(j, carry):
        i = 2 * j
        values(i - 2, 0)
        softmax(1)
        scores(i, 0)
        values(i - 1, 1)
        softmax(0)
        scores(i + 1, 1)
        return carry

    lax.fori_loop(1, n_blocks // 2, trip, 0)
    values(n_blocks - 2, 0)
    softmax(1)
    values(n_blocks - 1, 1)


def _na(q, k, v, kc, vc, tab, row_mask):
    b, n, w = q.shape
    n_ctx = kc.shape[1]
    pairs = w // LANES
    rows = n // GRID_W
    nq, nk = NA_QROWS * GRID_W, NA_KROWS * GRID_W
    tok = lambda i, p: (i, 0, p)
    return pl.pallas_call(
        functools.partial(_na_kernel, rows=rows),
        out_shape=jax.ShapeDtypeStruct((b, n, w), BF16),
        grid=(b, pairs),
        in_specs=[pl.BlockSpec((1, n, LANES), tok),
                  pl.BlockSpec((1, n, LANES), tok),
                  pl.BlockSpec((1, n, LANES), tok),
                  pl.BlockSpec((1, n_ctx, LANES), tok),
                  pl.BlockSpec((1, n_ctx, LANES), tok),
                  pl.BlockSpec((2,) + tab.shape[1:], lambda i, p: (p, 0, 0, 0)),
                  pl.BlockSpec(row_mask.shape, lambda i, p: (0, 0, 0, 0))],
        out_specs=pl.BlockSpec((1, n, LANES), tok),
        scratch_shapes=[pltpu.VMEM((1, 3, 2, nq, nk), F32),
                        pltpu.VMEM((2, 2, nq, nk), F32), pltpu.VMEM((2, 2, nq, n_ctx), F32),
                        pltpu.VMEM((2, 2, nq, nk), BF16), pltpu.VMEM((2, 2, nq, n_ctx), BF16),
                        pltpu.VMEM((2, 2, nq, LANES), F32)],
        compiler_params=_cparams(("arbitrary", "arbitrary")),
        name="na",
    )(q, k, v, kc, vc, tab, row_mask)


def _gla_prefix_matrices(t):
    i = np.arange(t)
    return np.stack([i[:, None] >= i[None, :], i[:, None] <= i[None, :]]).astype(np.float32)


def _gla_kernel(q_ref, k_ref, v_ref, ad_ref, g_ref, ck_ref, cv_ref, cad_ref,
                u_ref, ab_ref, gn_ref, cm_ref, o_ref, accf_ref, accb_ref, *, n_tok):
    t = GLA_T
    n_chunks = n_tok // t
    row = lax.broadcasted_iota(jnp.int32, (t, LANES), 0)
    hb = t // 2
    row_b = lax.broadcasted_iota(jnp.int32, (hb, LANES), 0)
    head0_b = lax.broadcasted_iota(jnp.int32, (hb, LANES), 1) < GLA_DK
    row2 = lax.broadcasted_iota(jnp.int32, (hb, 2 * hb), 0)
    col2 = lax.broadcasted_iota(jnp.int32, (hb, 2 * hb), 1) & (hb - 1)
    vrow = lax.broadcasted_iota(jnp.int32, (2 * t, 2 * GLA_DV), 0)
    vlane = lax.broadcasted_iota(jnp.int32, (2 * t, 2 * GLA_DV), 1)
    v_head_match = (vrow >= t) == (vlane >= GLA_DV)
    srow = lax.broadcasted_iota(jnp.int32, (2 * GLA_DV, LANES), 0)
    slane = lax.broadcasted_iota(jnp.int32, (2 * GLA_DV, LANES), 1)
    s_blockdiag = (srow >= GLA_DV) == (slane >= GLA_DK)
    blk_mask = {half: jnp.where((row2 & ~(2 * half - 1)) == (col2 & ~(2 * half - 1)), 1.0, 0.0)
                for half in GLA_LEVELS if 2 * half < hb}
    diag_blk = (row2 & ~(GLA_DIAG - 1)) == (col2 & ~(GLA_DIAG - 1))
    diag_mask = (jnp.where(diag_blk & (row2 >= col2), 1.0, 0.0), jnp.where(diag_blk & (row2 <= col2), 1.0, 0.0))

    def prefix_sums(ad, dirn):
        z = _dot(ad, u_ref[dirn]) + ab_ref[dirn]
        loga = (jnp.minimum(z, 0.0) - jnp.log(1.0 + jnp.exp(-jnp.abs(z)))) * (1.0 / GLA_GATE_TAU)
        hi, lo = _split_bf16(loga)
        p2 = _dot(cm_ref[dirn], jnp.concatenate([hi, lo], axis=-1))
        return p2[:, :LANES] + p2[:, LANES:]

    def chunk_end(p, dirn):
        return p[t - 1:t, :] if dirn == 0 else p[0:1, :]

    def level_sums(p, half, dirn):
        blk = 2 * half
        p3 = p.reshape(t // blk, blk, LANES)
        edge = half - 1 if dirn == 0 else half
        ref = jnp.broadcast_to(p3[:, edge:edge + 1, :], p3.shape).reshape(t, LANES)
        later = (row & half) != 0
        return jnp.where(later == (dirn == 0), p - ref, ref - p)

    def state_update(s, k, vt, p, dirn):
        kh = (k * jnp.exp(chunk_end(p, dirn) - p)).astype(BF16)
        return s * jnp.exp(chunk_end(p, dirn)) + jnp.where(s_blockdiag, _dot(vt, kh), 0.0)

    def chunk(tok0, s, dirn):
        q = q_ref[0, pl.ds(tok0, t), :].astype(F32)
        k = k_ref[0, pl.ds(tok0, t), :].astype(F32)
        v = v_ref[0, pl.ds(tok0, t), :]
        vt = v.T
        p = prefix_sums(ad_ref[0, pl.ds(tok0, t), :], dirn)
        qh = (q * jnp.exp(p)).astype(BF16)
        o = _dot_nt(qh, s.astype(BF16))
        def rows(x, b):
            return x[b * hb:(b + 1) * hb]

        def scores(qt, kt):
            kcat = jnp.concatenate([jnp.where(head0_b, kt, 0.0), jnp.where(head0_b, 0.0, kt)], axis=0)
            return _dot_nt(qt.astype(BF16), kcat.astype(BF16))

        assert GLA_LEVELS[0] == hb
        w = jnp.exp(level_sums(p, hb, dirn))
        qb, kb = (1, 0) if dirn == 0 else (0, 1)
        wide = scores(rows(q, qb) * rows(w, qb), rows(k, kb) * rows(w, kb))
        fine_w = [jnp.exp(level_sums(p, half, dirn)) for half in GLA_LEVELS[1:]]
        e_d = level_sums(p, GLA_DIAG // 2, dirn)
        w_d, wi_d = jnp.exp(e_d), jnp.exp(-e_d)
        fine = []
        for b in range(2):
            qs, ks = rows(q, b), rows(k, b)
            acc = None
            for half, w in zip(GLA_LEVELS[1:], fine_w):
                later = (row_b & half) != 0
                q_side = later if dirn == 0 else jnp.logical_not(later)
                part = scores(jnp.where(q_side, qs * rows(w, b), 0.0), jnp.where(q_side, 0.0, ks * rows(w, b)))
                if half in blk_mask:
                    part = part * blk_mask[half]
                acc = part if acc is None else acc + part
            later = (row_b & (GLA_DIAG // 2)) != 0
            shrink_q = later if dirn == 0 else jnp.logical_not(later)
            part = scores(qs * jnp.where(shrink_q, rows(w_d, b), rows(wi_d, b)),
                          ks * jnp.where(shrink_q, rows(wi_d, b), rows(w_d, b)))
            fine.append(acc + jnp.where(diag_mask[dirn] > 0.5, part, 0.0))
        zero = jnp.zeros((hb, hb), F32)
        h0, h1 = slice(0, hb), slice(hb, 2 * hb)
        if dirn == 0:
            top = [fine[0][:, h0], zero, fine[0][:, h1], zero]
            bot = [wide[:, h0], fine[1][:, h0], wide[:, h1], fine[1][:, h1]]
        else:
            top = [fine[0][:, h0], wide[:, h0], fine[0][:, h1], wide[:, h1]]
            bot = [zero, fine[1][:, h0], zero, fine[1][:, h1]]
        a = jnp.concatenate([jnp.concatenate(top, axis=1), jnp.concatenate(bot, axis=1)], axis=0)
        vcat = jnp.concatenate([v, v], axis=0)
        vcat = jnp.where(v_head_match, vcat, jnp.zeros_like(vcat))
        o = o + _dot(a.astype(BF16), vcat)
        return o, state_update(s, k, vt, p, dirn)

    def ctx_state(dirn):
        p = prefix_sums(cad_ref[0], dirn)
        s0 = jnp.zeros((2 * GLA_DV, LANES), F32)
        return state_update(s0, ck_ref[0].astype(F32), cv_ref[0].T, p, dirn)

    def body(i, carry):
        s_f, s_b = carry
        for u in range(GLA_UNROLL):
            c = i * GLA_UNROLL + u
            tok_f = pl.multiple_of(c * t, t)
            tok_b = pl.multiple_of((n_chunks - 1 - c) * t, t)
            o_f, s_f = chunk(tok_f, s_f, 0)
            o_b, s_b = chunk(tok_b, s_b, 1)
            accf_ref[pl.ds(tok_f, t), :] = o_f
            accb_ref[pl.ds(tok_b, t), :] = o_b
        return s_f, s_b

    assert n_chunks % GLA_UNROLL == 0
    lax.fori_loop(0, n_chunks // GLA_UNROLL, body, (ctx_state(0), ctx_state(1)))

    def finish(c, carry):
        tok0 = pl.multiple_of(c * t, t)
        o = accf_ref[pl.ds(tok0, t), :] + accb_ref[pl.ds(tok0, t), :]
        g = g_ref[0, pl.ds(tok0, t), :].astype(F32)
        gate = g / (1.0 + jnp.exp(-g))
        halves = [_rms(o[:, h * GLA_DV:(h + 1) * GLA_DV], gn_ref[...]) for h in range(2)]
        o_ref[0, pl.ds(tok0, t), :] = (jnp.concatenate(halves, axis=-1) * gate).astype(o_ref.dtype)
        return carry

    lax.fori_loop(0, n_chunks, finish, 0)


def _gla(q, k, v, ad, g, ck, cv, cad, u, abias, gnorm, cmats):
    b, n, kw = q.shape
    n_ctx = ck.shape[1]
    pairs = kw // LANES
    vw = 2 * GLA_DV
    tok = lambda i, p: (i, 0, p)
    full3 = lambda i, p: (i, 0, 0)
    return pl.pallas_call(
        functools.partial(_gla_kernel, n_tok=n),
        out_shape=jax.ShapeDtypeStruct((b, n, v.shape[2]), BF16),
        grid=(b, pairs),
        in_specs=[pl.BlockSpec((1, n, LANES), tok),
                  pl.BlockSpec((1, n, LANES), tok),
                  pl.BlockSpec((1, n, vw), tok),
                  pl.BlockSpec((1, n, LANES), full3),
                  pl.BlockSpec((1, n, vw), tok),
                  pl.BlockSpec((1, n_ctx, LANES), tok),
                  pl.BlockSpec((1, n_ctx, vw), tok),
                  pl.BlockSpec((1, n_ctx, LANES), full3),
                  pl.BlockSpec((2, LANES, LANES), lambda i, p: (0, 0, p)),
                  pl.BlockSpec((2, 1, LANES), lambda i, p: (0, 0, p)),
                  pl.BlockSpec((1, GLA_DV), lambda i, p: (0, 0)),
                  pl.BlockSpec(cmats.shape, lambda i, p: (0, 0, 0))],
        out_specs=pl.BlockSpec((1, n, vw), tok),
        scratch_shapes=[pltpu.VMEM((n, vw), F32), pltpu.VMEM((n, vw), F32)],
        compiler_params=_cparams(("arbitrary", "arbitrary")),
        name="gla",
    )(q, k, v, ad, g, ck, cv, cad, u, abias, gnorm, cmats)


def _gla_decay_up(a_up):
    r = GLA_GATE_RANK
    pad = jnp.zeros((2, 2 * r, GLA_KEY_W), F32)
    pad = pad.at[0, :r].set(a_up[0]).at[1, r:].set(a_up[1])
    hi, lo = _split_bf16(pad)
    return jnp.concatenate([hi, hi, lo, jnp.zeros_like(hi)], axis=1)


def _outproj_kernel(ona_ref, ogla_ref, x_ref, w1_ref, w2_ref, gm_ref, shf_ref, scf_ref, npost_ref,
                    nfpre_ref, rt_ref, xnew_ref, hf_ref, afft_ref):
    for r0 in range(0, x_ref.shape[1], TOK_TILE):
        rs = slice(r0, r0 + TOK_TILE)
        mix = _dot(ona_ref[0, rs], w1_ref[...]) + _dot(ogla_ref[0, rs], w2_ref[...])
        xn = x_ref[0, rs] + gm_ref[0] * _rms(mix, npost_ref[...])
        xnew_ref[0, rs] = xn
        h = _rms(xn, nfpre_ref[...]) * (1.0 + scf_ref[0]) + shf_ref[0]
        h_hi, h_lo = _split_bf16(h)
        hf_ref[0, rs] = _pack_bf16_pairs(h_hi)
        res = _dot(h_hi, rt_ref[...])
        logits = res[:, :LANES] + res[:, LANES:] + _dot(h_lo, rt_ref[:, :LANES])
        lane = lax.broadcasted_iota(jnp.int32, logits.shape, 1)
        logits = jnp.where(lane < N_EXPERTS, logits, NEG_BIG)
        p = jnp.exp(logits - jnp.max(logits, axis=-1, keepdims=True))
        aff = p / jnp.sum(p, axis=-1, keepdims=True)
        afft_ref[0, :, rs] = aff.T[:N_EXPERTS, :]


def _outproj(o_na, o_gla, x, w1, w2, gm, shf, scf, npost, nfpre, router_cat):
    b, n, d = x.shape
    tm = OUTPROJ_SUBTILES * TOK_TILE
    tokmap = lambda i, j: (i, j, 0)
    smp = lambda i, j: (i, 0, 0)
    cst = lambda i, j: (0, 0)
    return pl.pallas_call(
        _outproj_kernel,
        out_shape=[jax.ShapeDtypeStruct((b, n, d), F32),
                   jax.ShapeDtypeStruct((b, n, d // 2), jnp.int32),
                   jax.ShapeDtypeStruct((b, N_EXPERTS, n), F32)],
        grid=(b, n // tm),
        in_specs=[pl.BlockSpec((1, tm, o_na.shape[2]), tokmap),
                  pl.BlockSpec((1, tm, o_gla.shape[2]), tokmap),
                  pl.BlockSpec((1, tm, d), tokmap),
                  pl.BlockSpec(w1.shape, cst),
                  pl.BlockSpec(w2.shape, cst),
                  pl.BlockSpec((1, 1, d), smp),
                  pl.BlockSpec((1, 1, d), smp),
                  pl.BlockSpec((1, 1, d), smp),
                  pl.BlockSpec((1, d), cst),
                  pl.BlockSpec((1, d), cst),
                  pl.BlockSpec(router_cat.shape, cst)],
        out_specs=[pl.BlockSpec((1, tm, d), tokmap),
                   pl.BlockSpec((1, tm, d // 2), tokmap),
                   pl.BlockSpec((1, N_EXPERTS, tm), lambda i, j: (i, 0, j))],
        compiler_params=_cparams(("arbitrary", "arbitrary")),
        name="outproj",
    )(o_na, o_gla, x, w1, w2, gm, shf, scf, npost.reshape(1, d), nfpre.reshape(1, d), router_cat)


def _route_kernel(afft_ref, rt_ref, *, cap):
    a = afft_ref[0]
    e, n = a.shape
    capf = jnp.float32(cap)

    def search(i, thr_bits):
        cand = thr_bits | lax.shift_left(jnp.int32(1), 30 - i)
        cnt = jnp.sum(jnp.where(a >= lax.bitcast_convert_type(cand, F32), 1.0, 0.0), axis=-1, keepdims=True)
        return jnp.where(cnt >= capf, cand, thr_bits)

    thr_bits = lax.fori_loop(0, 31, search, jnp.zeros((e, 1), jnp.int32))
    thr = lax.bitcast_convert_type(thr_bits, F32)
    need = capf - jnp.sum(jnp.where(a > thr, 1.0, 0.0), axis=-1, keepdims=True)
    r_i = lax.broadcasted_iota(jnp.int32, (LANES, LANES), 0)
    c_i = lax.broadcasted_iota(jnp.int32, (LANES, LANES), 1)
    incl = jnp.where(r_i <= c_i, 1.0, 0.0).astype(BF16)
    off_eq = jnp.zeros((e, 1), F32)
    off_sel = jnp.zeros((e, 1), F32)
    for j in range(n // LANES):
        sl = slice(j * LANES, (j + 1) * LANES)
        a_b = a[:, sl]
        eq_b = jnp.where(a_b == thr, 1.0, 0.0)
        tie_rank = _dot(eq_b.astype(BF16), incl) - eq_b + off_eq
        off_eq = off_eq + jnp.sum(eq_b, axis=-1, keepdims=True)
        sel_b = jnp.where(a_b > thr, 1.0, jnp.where(tie_rank < need, eq_b, 0.0))
        sel = sel_b > 0.5
        rank = _dot(sel_b.astype(BF16), incl) - sel_b + off_sel
        off_sel = off_sel + jnp.sum(sel_b, axis=-1, keepdims=True)
        rsel = jnp.where(sel, rank, -1.0)
        rt_ref[0, :, sl] = rsel.astype(jnp.int32)


def _route(afft, cap):
    b, e, n = afft.shape
    return pl.pallas_call(
        functools.partial(_route_kernel, cap=cap),
        out_shape=jax.ShapeDtypeStruct((b, e, n), jnp.int32),
        grid=(b,),
        in_specs=[pl.BlockSpec((1, e, n), lambda i: (i, 0, 0))],
        out_specs=pl.BlockSpec((1, e, n), lambda i: (i, 0, 0)),
        compiler_params=_cparams(("arbitrary",)),
        name="route",
    )(afft)


def _sc_gather(rsel_t, hf2, cap):
    b, e, n = rsel_t.shape
    width = hf2.shape[1]
    info = plsc.get_sparse_core_info()
    nc, lanes = info.num_cores, info.num_lanes
    workers = nc * info.num_subcores
    items = b * e
    assert items % workers == 0 and n % lanes == 0 and cap % SC_GATHER_ROWS == 0
    per_worker = items // workers
    mesh = plsc.VectorSubcoreMesh(core_axis_name="c", subcore_axis_name="s")

    def body(rank_hbm, hf_hbm, out_hbm, rank_v, idx_v, rows_a, rows_b, sem_a, sem_b):
        bufs = ((rows_a, sem_a), (rows_b, sem_b))
        wid = lax.axis_index("s") * nc + lax.axis_index("c")
        for k in range(per_worker):
            item = wid * per_worker + k
            base_tok = (item // e) * n
            pltpu.sync_copy(rank_hbm.at[item], rank_v)

            @pl.loop(0, n // lanes)
            def _(j):
                r = rank_v[pl.ds(j * lanes, lanes)]
                tok = lax.iota(jnp.int32, lanes) + (j * lanes + base_tok)
                plsc.store_scatter(idx_v, [r], tok, mask=r >= 0)

            def gather(c):
                buf, sem = bufs[c % 2]
                rows = pl.ds(c * SC_GATHER_ROWS, SC_GATHER_ROWS)
                return pltpu.async_copy(hf_hbm.at[idx_v.at[rows]], buf, sem)

            n_chunks = cap // SC_GATHER_ROWS
            pending = gather(0)
            for c in range(n_chunks):
                nxt = gather(c + 1) if c + 1 < n_chunks else None
                pending.wait()
                pltpu.sync_copy(bufs[c % 2][0],
                                out_hbm.at[pl.ds(item * cap + c * SC_GATHER_ROWS, SC_GATHER_ROWS)])
                pending = nxt

    return pl.kernel(
        body, out_type=jax.ShapeDtypeStruct((items * cap, width), hf2.dtype), mesh=mesh,
        scratch_types=[pltpu.VMEM((n,), jnp.int32), pltpu.VMEM((cap,), jnp.int32),
                       pltpu.VMEM((SC_GATHER_ROWS, width), hf2.dtype),
                       pltpu.VMEM((SC_GATHER_ROWS, width), hf2.dtype),
                       pltpu.SemaphoreType.DMA, pltpu.SemaphoreType.DMA],
        compiler_params=pltpu.CompilerParams(needs_layout_passes=False),
        name="scgather",
    )(rsel_t.reshape(items, n), hf2)


def _ffn_kernel(x_ref, wg_ref, wu_ref, wd_ref, o_ref, acc_ref, xb_ref):
    f = pl.program_id(1)
    b = x_ref.shape[0]
    wg = wg_ref[0].astype(BF16)
    wu = wu_ref[0].astype(BF16)
    wd = wd_ref[0].astype(BF16)

    @pl.when(f == 0)
    def _():
        acc_ref[...] = jnp.zeros_like(acc_ref)
        for i in range(b):
            xb_ref[i] = _unpack_bf16_pairs(x_ref[i, 0])

    for i in range(b):
        x = xb_ref[i]
        g = _dot(x, wg)
        u = _dot(x, wu)
        hid = (g / (1.0 + jnp.exp(-g)) * u).astype(BF16)
        acc_ref[i] += _dot(hid, wd)

    @pl.when(f == pl.num_programs(1) - 1)
    def _():
        for i in range(b):
            o_ref[i, 0] = acc_ref[i].astype(o_ref.dtype)


def _ffn(xs, w_gate, w_up, w_down):
    b, e, cap, dp = xs.shape
    d = 2 * dp
    dff = w_gate.shape[2]
    tf = FFN_TILE
    return pl.pallas_call(
        _ffn_kernel,
        out_shape=jax.ShapeDtypeStruct((b, e, cap, d), BF16),
        grid=(e, dff // tf),
        in_specs=[pl.BlockSpec((b, 1, cap, dp), lambda i, f: (0, i, 0, 0)),
                  pl.BlockSpec((1, d, tf), lambda i, f: (i, 0, f)),
                  pl.BlockSpec((1, d, tf), lambda i, f: (i, 0, f)),
                  pl.BlockSpec((1, tf, d), lambda i, f: (i, f, 0))],
        out_specs=pl.BlockSpec((b, 1, cap, d), lambda i, f: (0, i, 0, 0)),
        scratch_shapes=[pltpu.VMEM((b, cap, d), F32), pltpu.VMEM((b, cap, d), BF16)],
        compiler_params=_cparams(("arbitrary", "arbitrary")),
        name="ffn",
    )(xs, w_gate, w_up, w_down)


def _combine_kernel(off_ref, ys_ref, rt_ref, afft_ref, xn_ref, gf_ref, npost_ref, o_ref, acc_ref):
    bi, tt = pl.program_id(0), pl.program_id(1)
    tm = rt_ref.shape[2]
    n_experts, cap = ys_ref.shape[1], ys_ref.shape[2]
    blocks = tm // LANES
    slot = lax.broadcasted_iota(jnp.int32, (COMBINE_WIN, tm), 0)

    def window(e, w0, j):
        nominal = w0 + j * COMBINE_WIN
        start = pl.multiple_of(jnp.minimum(nominal, cap - COMBINE_WIN), BF16_ROWS)
        rank = rt_ref[0, e:e + 1, :]
        hit = ((rank - start) == slot) & (rank >= nominal)
        weights = jnp.where(hit, afft_ref[0, e:e + 1, :], 0.0).astype(BF16)
        return weights, ys_ref[0, e, pl.ds(start, COMBINE_WIN), :]

    first, extra = [], []
    for e in range(n_experts):
        r0 = off_ref[bi, e, tt * blocks]
        r1 = off_ref[bi, e, (tt + 1) * blocks]
        w0 = (r0 // BF16_ROWS) * BF16_ROWS
        first.append(w0)
        extra.append(jnp.maximum((r1 - w0 + COMBINE_WIN - 1) // COMBINE_WIN - 1, 0))
    terms = []
    for e in range(0, n_experts, 2):
        (wa, ya), (wb, yb) = window(e, first[e], 0), window(e + 1, first[e + 1], 0)
        terms.append(_dot(jnp.concatenate([wa, wb], axis=0).T, jnp.concatenate([ya, yb], axis=0)))
    acc_ref[...] = functools.reduce(lambda a, c: a + c, terms)

    @pl.when(functools.reduce(lambda a, c: a + c, extra) > 0)
    def _():
        for e in range(n_experts):
            def more(j, carry, e=e):
                w, y = window(e, first[e], j)
                acc_ref[...] += _dot(w.T, y)
                return carry
            lax.fori_loop(1, extra[e] + 1, more, 0)

    o_ref[0] = xn_ref[0] + gf_ref[0] * _rms(acc_ref[...], npost_ref[...])


def _combine(ys, slot_off, rsel_t, aff_t, x_new, gf, npost):
    b, e, cap, d = ys.shape
    n = x_new.shape[1]
    tm = TOK_TILE
    tokmap = lambda i, j, off: (i, j, 0)
    return pl.pallas_call(
        _combine_kernel,
        out_shape=jax.ShapeDtypeStruct((b, n, d), F32),
        grid_spec=pltpu.PrefetchScalarGridSpec(
            num_scalar_prefetch=1,
            grid=(b, n // tm),
            in_specs=[pl.BlockSpec((1, e, cap, d), lambda i, j, off: (i, 0, 0, 0)),
                      pl.BlockSpec((1, e, tm), lambda i, j, off: (i, 0, j)),
                      pl.BlockSpec((1, e, tm), lambda i, j, off: (i, 0, j)),
                      pl.BlockSpec((1, tm, d), tokmap),
                      pl.BlockSpec((1, 1, d), lambda i, j, off: (i, 0, 0)),
                      pl.BlockSpec((1, d), lambda i, j, off: (0, 0))],
            out_specs=pl.BlockSpec((1, tm, d), tokmap),
            scratch_shapes=[pltpu.VMEM((tm, d), F32)]),
        compiler_params=_cparams(("arbitrary", "arbitrary")),
        name="combine",
    )(slot_off, ys, rsel_t, aff_t, x_new, gf, npost.reshape(1, d))


def kernel(x, c, ctx, c_ctx, w_mod, b_mod, norm_mix_pre, norm_mix_post, norm_ffn_pre, norm_ffn_post,
           w_in, na_rpb, gla_a_up, gla_a_bias, gla_norm, w_out, router, w_gate, w_up, w_down):
    b, n, d = x.shape
    assert w_mod.shape[0] == 1 and d == D_MODEL and n % (GRID_W * NA_QROWS) == 0 and n % GLA_T == 0
    assert ctx.shape[1] == GLA_T
    rows = n // GRID_W
    cap = EC_CAPACITY_FACTOR * n // N_EXPERTS

    c8 = jnp.concatenate([c, c_ctx[None, :], jnp.zeros((8 - b - 1, d), F32)], axis=0)
    mod = _mod(c8, w_mod[0], b_mod[0])
    sh_m, sc_m, g_m, sh_f, sc_f, g_f = [m[:b, None, :] for m in jnp.split(mod, 6, axis=-1)]
    sh_c, sc_c = mod[b:b + 1, None, :d], mod[b:b + 1, None, d:2 * d]

    wb = w_in[0].astype(BF16)
    cuts = np.cumsum([0, NA_W, NA_W, GLA_KEY_W, GLA_VAL_W, 2 * GLA_GATE_RANK, NA_W, GLA_KEY_W, GLA_VAL_W])
    w_nak, w_nav, w_gk, w_gv, w_ad, w_naq, w_gq, w_gg = [wb[:, cuts[i]:cuts[i + 1]] for i in range(8)]
    w_ad3 = jnp.concatenate([w_ad, w_ad, w_ad, jnp.zeros_like(w_ad)], axis=1)
    rope = _rope_tables(n)
    na_q, na_k, na_v, gq, gk, gv, ad, gg = _inproj(
        x, sh_m, sc_m, norm_mix_pre[0],
        [w_naq, w_nak, w_nav, w_gq, w_gk, w_gv, w_ad3, w_gg],
        ["plain", "plain", "plain", "rope", "rope", "plain", "split", "plain"],
        [1.0, 1.0, 1.0, GLA_DK ** -0.5, 1.0, 1.0, 1.0, 1.0],
        [BF16] * 8, rope=rope)
    c_nak, c_nav, c_gk, c_gv, c_ad = _inproj(
        ctx, sh_c, sc_c, norm_mix_pre[0],
        [w_nak, w_nav, w_gk, w_gv, w_ad3],
        ["plain", "plain", "plain", "plain", "split"], [1.0] * 5, [BF16] * 5)

    o_na = _na(na_q, na_k, na_v, c_nak, c_nav, *_na_bias_tables(na_rpb[0], rows))

    cmats = jnp.asarray(_gla_prefix_matrices(GLA_T), BF16)
    o_gla = _gla(gq, gk, gv, ad, gg, c_gk, c_gv, c_ad, _gla_decay_up(gla_a_up[0]),
                 gla_a_bias[0][:, None, :], gla_norm[0][None, :], cmats)

    wo = w_out[0].astype(BF16)
    router_pad = jnp.zeros((d, LANES), F32).at[:, :N_EXPERTS].set(router[0])
    x_new, hf, aff_t = _outproj(o_na, o_gla, x, wo[:NA_W], wo[NA_W:], g_m, sh_f, sc_f,
                                     norm_mix_post[0], norm_ffn_pre[0],
                                     jnp.concatenate(_split_bf16(router_pad), axis=1))

    rsel_t = _route(aff_t, cap)
    xs = _sc_gather(rsel_t, hf.reshape(b * n, d // 2), cap).reshape(b, N_EXPERTS, cap, d // 2)
    ys = _ffn(xs, w_gate[0], w_up[0], w_down[0])
    picked = (rsel_t >= 0).astype(jnp.int32).reshape(b, N_EXPERTS, n // LANES, LANES).sum(axis=-1)
    slot_off = jnp.concatenate([jnp.zeros((b, N_EXPERTS, 1), jnp.int32), jnp.cumsum(picked, axis=-1)], axis=-1)
    return _combine(ys, slot_off, rsel_t, aff_t, x_new, g_f, norm_ffn_post[0])
```

```python
import functools

import numpy as np
import jax
import jax.numpy as jnp
from jax import lax
from jax.experimental import pallas as pl
from jax.experimental.pallas import tpu as pltpu
from jax.experimental.pallas import tpu_sc as plsc

F32 = jnp.float32
BF16 = jnp.bfloat16
HIGHEST = lax.Precision.HIGHEST

D_MODEL = 1024
GRID_W = 64
NA_W = 512
NA_HEADS = 8
NA_HEAD_DIM = 64
NA_WIN_ROWS = 8
NA_WIN_COLS = 16
GLA_HEADS = 4
GLA_DV = 128
GLA_DK = 64
GLA_KEY_W = 256
GLA_VAL_W = 512
GLA_GATE_RANK = 16
GLA_GATE_TAU = 16.0
ROPE_BASE = 10000.0
N_EXPERTS = 16
EC_CAPACITY_FACTOR = 2
NORM_EPS = 1e-6
NEG_BIG = -1e30

LANES = 128
BF16_ROWS = 16
VMEM_LIMIT = 56 * 1024 * 1024

TOK_TILE = 512
NA_QROWS = 4
NA_KROWS = NA_QROWS + NA_WIN_ROWS
NA_TAB_PAD = NA_QROWS
NA_TAB_BLOCKS = NA_TAB_PAD + 2 * NA_WIN_ROWS - 1 + NA_QROWS + 1
GLA_T = 256
GLA_LEVELS = (128, 64, 32, 16)
GLA_DIAG = 16
GLA_UNROLL = 2
OUTPROJ_SUBTILES = 2
FFN_TILE = 256
COMBINE_WIN = 128
SC_GATHER_ROWS = 64


def _cparams(sem):
    return pltpu.CompilerParams(dimension_semantics=sem, vmem_limit_bytes=VMEM_LIMIT)


def _rms(v, g):
    return v * lax.rsqrt(jnp.mean(v * v, axis=-1, keepdims=True) + NORM_EPS) * g


def _dot(a, b):
    return jnp.dot(a, b, preferred_element_type=F32)


def _dot_nt(a, b):
    return lax.dot_general(a, b, (((1,), (1,)), ((), ())), preferred_element_type=F32)


def _split_bf16(v):
    hi = v.astype(BF16)
    return hi, (v - hi.astype(F32)).astype(BF16)


_HIGH_HALF = -65536


def _pack_bf16_pairs(hb):
    bits = lax.bitcast_convert_type(hb.astype(F32), jnp.int32)
    half = hb.shape[1] // 2
    return lax.shift_right_logical(bits[:, :half], 16) | (bits[:, half:] & _HIGH_HALF)


def _unpack_bf16_pairs(w):
    lo = lax.bitcast_convert_type(lax.shift_left(w, 16), F32).astype(BF16)
    hi = lax.bitcast_convert_type(w & _HIGH_HALF, F32).astype(BF16)
    return jnp.concatenate([lo, hi], axis=1)


def _mod_kernel(c_ref, w_ref, b_ref, o_ref):
    c = c_ref[...]
    s = c / (1.0 + jnp.exp(-c))
    o_ref[...] = _dot(s.astype(BF16), w_ref[...].astype(BF16)) + b_ref[...]


def _mod(c8, w_mod, b_mod):
    d, n = w_mod.shape
    tn = 1536
    return pl.pallas_call(
        _mod_kernel,
        out_shape=jax.ShapeDtypeStruct((8, n), F32),
        grid=(n // tn,),
        in_specs=[pl.BlockSpec((8, d), lambda j: (0, 0)),
                  pl.BlockSpec((d, tn), lambda j: (0, j)),
                  pl.BlockSpec((1, tn), lambda j: (0, j))],
        out_specs=pl.BlockSpec((8, tn), lambda j: (0, j)),
        compiler_params=_cparams(("arbitrary",)),
        name="mod",
    )(c8, w_mod, b_mod.reshape(1, n))


def _inproj_kernel(x_ref, sh_ref, sc_ref, g_ref, *refs, modes, scales):
    n_rope = 2 if "rope" in modes else 0
    rope_refs, refs = refs[:n_rope], refs[n_rope:]
    n_w = len(modes)
    w_refs, o_refs = refs[:n_w], refs[n_w:]
    x = x_ref[0]
    h = _rms(x, g_ref[...]) * (1.0 + sc_ref[0]) + sh_ref[0]
    hb = h.astype(BF16)
    for w_ref, o_ref, mode, scale in zip(w_refs, o_refs, modes, scales):
        y = _dot(hb, w_ref[...])
        if mode == "rope":
            cols = y.shape[1]
            quarter = GLA_DK // 4
            lane = lax.broadcasted_iota(jnp.int32, y.shape, 1)
            partner = jnp.where((lane & (2 * quarter - 1)) < quarter,
                                pltpu.roll(y, cols - quarter, axis=1), pltpu.roll(y, quarter, axis=1))
            y = (y * rope_refs[0][...] + partner * rope_refs[1][...]) * scale
        elif mode == "split":
            hi, lo = _split_bf16(y)
            lane = lax.broadcasted_iota(jnp.int32, y.shape, 1)
            rank2 = 2 * GLA_GATE_RANK
            y = jnp.where((lane >= rank2) & (lane < 2 * rank2), lo, hi)
        o_ref[0] = y.astype(o_ref.dtype)


def _inproj(x, shift, scale, g, weights, modes, scales, out_dtypes, rope=None):
    b, n, d = x.shape
    tm = min(TOK_TILE, n)
    per_sample = shift.shape[0] == b
    mod_map = (lambda i, j: (i, 0, 0)) if per_sample else (lambda i, j: (0, 0, 0))
    in_specs = [pl.BlockSpec((1, tm, d), lambda i, j: (i, j, 0)),
                pl.BlockSpec((1, 1, d), mod_map),
                pl.BlockSpec((1, 1, d), mod_map),
                pl.BlockSpec((1, d), lambda i, j: (0, 0))]
    args = [x, shift, scale, g.reshape(1, d)]
    if rope is not None:
        in_specs += [pl.BlockSpec((tm, rope[0].shape[1]), lambda i, j: (j, 0))] * 2
        args += list(rope)
    out_shapes, out_specs = [], []
    for w, mode, dt in zip(weights, modes, out_dtypes):
        in_specs.append(pl.BlockSpec(w.shape, lambda i, j: (0, 0)))
        cols = w.shape[1]
        out_shapes.append(jax.ShapeDtypeStruct((b, n, cols), dt))
        out_specs.append(pl.BlockSpec((1, tm, cols), lambda i, j: (i, j, 0)))
    return pl.pallas_call(
        functools.partial(_inproj_kernel, modes=tuple(modes), scales=tuple(scales)),
        out_shape=out_shapes,
        grid=(b, n // tm),
        in_specs=in_specs,
        out_specs=out_specs,
        compiler_params=_cparams(("arbitrary", "arbitrary")),
        name="inproj",
    )(*args, *weights)


def _rope_tables(n):
    t = np.arange(n)
    pos_row, pos_col = (t // GRID_W).astype(np.float32), (t % GRID_W).astype(np.float32)
    quarter = GLA_DK // 4
    freqs = np.float32(ROPE_BASE) ** (-np.arange(quarter, dtype=np.float32) / quarter)
    ang_r = pos_row[:, None] * freqs
    ang_c = pos_col[:, None] * freqs
    cos = np.concatenate([np.cos(ang_r), np.cos(ang_r), np.cos(ang_c), np.cos(ang_c)], axis=-1)
    sin = np.concatenate([-np.sin(ang_r), np.sin(ang_r), -np.sin(ang_c), np.sin(ang_c)], axis=-1)
    return jnp.asarray(np.tile(cos, (1, GLA_HEADS))), jnp.asarray(np.tile(sin, (1, GLA_HEADS)))


def _na_patterns(rows):
    kr = min(NA_WIN_ROWS, rows)
    n_blocks = rows // NA_QROWS
    pats = []
    for blk in (0, 1, n_blocks - 1):
        r0 = blk * NA_QROWS
        k0 = int(np.clip(r0 - kr // 2, 0, rows - NA_KROWS))
        strips = []
        for a in range(NA_QROWS):
            r_start = int(np.clip(r0 + a - kr // 2, 0, rows - kr))
            start = k0 - (r0 + a) + NA_WIN_ROWS - 1 + NA_TAB_PAD
            assert 0 <= start and start + NA_KROWS <= NA_TAB_BLOCKS
            strips.append((start, [r_start <= k0 + c < r_start + kr for c in range(NA_KROWS)]))
        pats.append(strips)
    return pats


def _na_bias_tables(rpb, rows):
    heads = rpb.shape[0]
    col = np.arange(GRID_W)
    c_start = np.clip(col - NA_WIN_COLS // 2, 0, GRID_W - NA_WIN_COLS)
    col_ok = (col[None, :] >= c_start[:, None]) & (col[None, :] < c_start[:, None] + NA_WIN_COLS)
    dc = np.clip(col[None, :] - col[:, None] + NA_WIN_COLS - 1, 0, 2 * NA_WIN_COLS - 2)
    sel_c = (np.arange(2 * NA_WIN_COLS - 1)[:, None, None] == dc[None]) & col_ok[None]
    t = jnp.einsum("hrd,dqk->hqrk", rpb, jnp.asarray(sel_c, F32), precision=HIGHEST)
    t = jnp.where(jnp.asarray(col_ok)[None, :, None, :], t, NEG_BIG)
    n_dr = 2 * NA_WIN_ROWS - 1
    t = t.reshape(heads, GRID_W, n_dr * GRID_W)
    back = NA_TAB_BLOCKS + 1 - NA_TAB_PAD - n_dr
    t = jnp.pad(t, ((0, 0), (0, 0), (NA_TAB_PAD * GRID_W, back * GRID_W)), constant_values=NEG_BIG)
    width = NA_TAB_BLOCKS * GRID_W
    tab = jnp.stack([t[:, :, :width], t[:, :, GRID_W:GRID_W + width]], axis=1)
    row_mask = np.zeros((3, NA_QROWS, 1, NA_KROWS * GRID_W), np.float32)
    for pat, strips in enumerate(_na_patterns(rows)):
        for a, (_, valid) in enumerate(strips):
            row_mask[pat, a, 0] = np.repeat(np.where(valid, 0.0, NEG_BIG), GRID_W)
    return tab, jnp.asarray(row_mask)


def _na_kernel(q_ref, k_ref, v_ref, kc_ref, vc_ref, tab_ref, rmask_ref, o_ref, bias_ref,
               sw_ref, sc_ref, pw_ref, pc_ref, *, rows):
    nq, nk = NA_QROWS * GRID_W, NA_KROWS * GRID_W
    n_blocks = rows // NA_QROWS
    kr = min(NA_WIN_ROWS, rows)
    scale = NA_HEAD_DIM ** -0.5
    lane = lax.broadcasted_iota(jnp.int32, (nq, LANES), 1)
    first_head = lane < NA_HEAD_DIM
    kc = kc_ref[0]
    vc = vc_ref[0]
    lane_w = lax.broadcasted_iota(jnp.int32, (nk, LANES), 1)
    lane_c = lax.broadcasted_iota(jnp.int32, vc.shape, 1)
    for pat, strips in enumerate(_na_patterns(rows)):
        for a, (start, _) in enumerate(strips):
            parity = start % 2
            off = (start - parity) * GRID_W
            for h in range(2):
                bias_ref[0, pat, h, a * GRID_W:(a + 1) * GRID_W, :] = (
                    tab_ref[h, parity, :, off:off + nk] + rmask_ref[pat, a])

    def key_start(i):
        return pl.multiple_of(jnp.clip(i * NA_QROWS - kr // 2, 0, rows - NA_KROWS) * GRID_W, GRID_W)

    def scores(i, slot):
        pat = jnp.where(i == 0, 0, jnp.where(i == n_blocks - 1, 2, 1))
        q = q_ref[0, pl.ds(pl.multiple_of(i * nq, nq), nq), :] * scale
        kw = k_ref[0, pl.ds(key_start(i), nk), :]
        for h in range(2):
            qh = jnp.where(first_head if h == 0 else jnp.logical_not(first_head), q, jnp.zeros_like(q))
            sw_ref[slot, h] = _dot_nt(qh, kw) + bias_ref[0, pat, h]
            sc_ref[slot, h] = _dot_nt(qh, kc)

    def softmax(slot):
        for h in range(2):
            s_w = sw_ref[slot, h]
            s_c = sc_ref[slot, h]
            m = jnp.maximum(jnp.max(s_w, axis=-1, keepdims=True), jnp.max(s_c, axis=-1, keepdims=True))
            pw_ref[slot, h] = jnp.exp((s_w - m).astype(BF16))
            pc_ref[slot, h] = jnp.exp((s_c - m).astype(BF16))

    def values(i, slot):
        vw = v_ref[0, pl.ds(key_start(i), nk), :]
        outs = []
        for h in range(2):
            sum_lane = NA_HEAD_DIM * (1 - h)
            vw_h = jnp.where(lane_w == sum_lane, jnp.ones_like(vw), vw)
            vc_h = jnp.where(lane_c == sum_lane, jnp.ones_like(vc), vc)
            o = _dot(pw_ref[slot, h], vw_h) + _dot(pc_ref[slot, h], vc_h)
            outs.append(o * (1.0 / o[:, sum_lane:sum_lane + 1]))
        o = jnp.where(first_head, outs[0], outs[1])
        o_ref[0, pl.ds(pl.multiple_of(i * nq, nq), nq), :] = o.astype(o_ref.dtype)

    assert n_blocks % 2 == 0 and n_blocks >= 4
    scores(0, 0)
    softmax(0)
    scores(1, 1)

    def trip(j, carry):
        i = 2 * j
        values(i - 2, 0)
        softmax(1)
        scores(i, 0)
        values(i - 1, 1)
        softmax(0)
        scores(i + 1, 1)
        return carry

    lax.fori_loop(1, n_blocks // 2, trip, 0)
    values(n_blocks - 2, 0)
    softmax(1)
    values(n_blocks - 1, 1)


def _na(q, k, v, kc, vc, tab, row_mask):
    b, n, w = q.shape
    n_ctx = kc.shape[1]
    pairs = w // LANES
    rows = n // GRID_W
    nq, nk = NA_QROWS * GRID_W, NA_KROWS * GRID_W
    tok = lambda i, p: (i, 0, p)
    return pl.pallas_call(
        functools.partial(_na_kernel, rows=rows),
        out_shape=jax.ShapeDtypeStruct((b, n, w), BF16),
        grid=(b, pairs),
        in_specs=[pl.BlockSpec((1, n, LANES), tok),
                  pl.BlockSpec((1, n, LANES), tok),
                  pl.BlockSpec((1, n, LANES), tok),
                  pl.BlockSpec((1, n_ctx, LANES), tok),
                  pl.BlockSpec((1, n_ctx, LANES), tok),
                  pl.BlockSpec((2,) + tab.shape[1:], lambda i, p: (p, 0, 0, 0)),
                  pl.BlockSpec(row_mask.shape, lambda i, p: (0, 0, 0, 0))],
        out_specs=pl.BlockSpec((1, n, LANES), tok),
        scratch_shapes=[pltpu.VMEM((1, 3, 2, nq, nk), F32),
                        pltpu.VMEM((2, 2, nq, nk), F32), pltpu.VMEM((2, 2, nq, n_ctx), F32),
                        pltpu.VMEM((2, 2, nq, nk), BF16), pltpu.VMEM((2, 2, nq, n_ctx), BF16)],
        compiler_params=_cparams(("arbitrary", "arbitrary")),
        name="na",
    )(q, k, v, kc, vc, tab, row_mask)


def _gla_prefix_matrices(t):
    i = np.arange(t)
    return np.stack([i[:, None] >= i[None, :], i[:, None] <= i[None, :]]).astype(np.float32)


def _gla_kernel(q_ref, k_ref, v_ref, ad_ref, g_ref, ck_ref, cv_ref, cad_ref,
                u_ref, ab_ref, gn_ref, cm_ref, o_ref, accf_ref, accb_ref, *, n_tok):
    t = GLA_T
    n_chunks = n_tok // t
    row = lax.broadcasted_iota(jnp.int32, (t, LANES), 0)
    hb = t // 2
    row_b = lax.broadcasted_iota(jnp.int32, (hb, LANES), 0)
    head0_b = lax.broadcasted_iota(jnp.int32, (hb, LANES), 1) < GLA_DK
    row2 = lax.broadcasted_iota(jnp.int32, (hb, 2 * hb), 0)
    col2 = lax.broadcasted_iota(jnp.int32, (hb, 2 * hb), 1) & (hb - 1)
    vrow = lax.broadcasted_iota(jnp.int32, (2 * t, 2 * GLA_DV), 0)
    vlane = lax.broadcasted_iota(jnp.int32, (2 * t, 2 * GLA_DV), 1)
    v_head_match = (vrow >= t) == (vlane >= GLA_DV)
    srow = lax.broadcasted_iota(jnp.int32, (2 * GLA_DV, LANES), 0)
    slane = lax.broadcasted_iota(jnp.int32, (2 * GLA_DV, LANES), 1)
    s_blockdiag = (srow >= GLA_DV) == (slane >= GLA_DK)
    blk_mask = {half: jnp.where((row2 & ~(2 * half - 1)) == (col2 & ~(2 * half - 1)), 1.0, 0.0)
                for half in GLA_LEVELS if 2 * half < hb}
    diag_blk = (row2 & ~(GLA_DIAG - 1)) == (col2 & ~(GLA_DIAG - 1))
    diag_mask = (jnp.where(diag_blk & (row2 >= col2), 1.0, 0.0), jnp.where(diag_blk & (row2 <= col2), 1.0, 0.0))

    def prefix_sums(ad, dirn):
        z = _dot(ad, u_ref[dirn]) + ab_ref[dirn]
        loga = (jnp.minimum(z, 0.0) - jnp.log(1.0 + jnp.exp(-jnp.abs(z)))) * (1.0 / GLA_GATE_TAU)
        hi, lo = _split_bf16(loga)
        p2 = _dot(cm_ref[dirn], jnp.concatenate([hi, lo], axis=-1))
        return p2[:, :LANES] + p2[:, LANES:]

    def chunk_end(p, dirn):
        return p[t - 1:t, :] if dirn == 0 else p[0:1, :]

    def level_sums(p, half, dirn):
        blk = 2 * half
        p3 = p.reshape(t // blk, blk, LANES)
        edge = half - 1 if dirn == 0 else half
        ref = jnp.broadcast_to(p3[:, edge:edge + 1, :], p3.shape).reshape(t, LANES)
        later = (row & half) != 0
        return jnp.where(later == (dirn == 0), p - ref, ref - p)

    def state_update(s, k, vt, p, dirn):
        kh = (k * jnp.exp(chunk_end(p, dirn) - p)).astype(BF16)
        return s * jnp.exp(chunk_end(p, dirn)) + jnp.where(s_blockdiag, _dot(vt, kh), 0.0)

    def chunk(tok0, s, dirn):
        q = q_ref[0, pl.ds(tok0, t), :].astype(F32)
        k = k_ref[0, pl.ds(tok0, t), :].astype(F32)
        v = v_ref[0, pl.ds(tok0, t), :]
        vt = v.T
        p = prefix_sums(ad_ref[0, pl.ds(tok0, t), :], dirn)
        qh = (q * jnp.exp(p)).astype(BF16)
        o = _dot_nt(qh, s.astype(BF16))
        def rows(x, b):
            return x[b * hb:(b + 1) * hb]

        def scores(qt, kt):
            kcat = jnp.concatenate([jnp.where(head0_b, kt, 0.0), jnp.where(head0_b, 0.0, kt)], axis=0)
            return _dot_nt(qt.astype(BF16), kcat.astype(BF16))

        assert GLA_LEVELS[0] == hb
        w = jnp.exp(level_sums(p, hb, dirn))
        qb, kb = (1, 0) if dirn == 0 else (0, 1)
        wide = scores(rows(q, qb) * rows(w, qb), rows(k, kb) * rows(w, kb))
        fine_w = [jnp.exp(level_sums(p, half, dirn)) for half in GLA_LEVELS[1:]]
        e_d = level_sums(p, GLA_DIAG // 2, dirn)
        w_d, wi_d = jnp.exp(e_d), jnp.exp(-e_d)
        fine = []
        for b in range(2):
            qs, ks = rows(q, b), rows(k, b)
            acc = None
            for half, w in zip(GLA_LEVELS[1:], fine_w):
                later = (row_b & half) != 0
                q_side = later if dirn == 0 else jnp.logical_not(later)
                part = scores(jnp.where(q_side, qs * rows(w, b), 0.0), jnp.where(q_side, 0.0, ks * rows(w, b)))
                if half in blk_mask:
                    part = part * blk_mask[half]
                acc = part if acc is None else acc + part
            later = (row_b & (GLA_DIAG // 2)) != 0
            shrink_q = later if dirn == 0 else jnp.logical_not(later)
            part = scores(qs * jnp.where(shrink_q, rows(w_d, b), rows(wi_d, b)),
                          ks * jnp.where(shrink_q, rows(wi_d, b), rows(w_d, b)))
            fine.append(acc + jnp.where(diag_mask[dirn] > 0.5, part, 0.0))
        zero = jnp.zeros((hb, hb), F32)
        h0, h1 = slice(0, hb), slice(hb, 2 * hb)
        if dirn == 0:
            top = [fine[0][:, h0], zero, fine[0][:, h1], zero]
            bot = [wide[:, h0], fine[1][:, h0], wide[:, h1], fine[1][:, h1]]
        else:
            top = [fine[0][:, h0], wide[:, h0], fine[0][:, h1], wide[:, h1]]
            bot = [zero, fine[1][:, h0], zero, fine[1][:, h1]]
        a = jnp.concatenate([jnp.concatenate(top, axis=1), jnp.concatenate(bot, axis=1)], axis=0)
        vcat = jnp.concatenate([v, v], axis=0)
        vcat = jnp.where(v_head_match, vcat, jnp.zeros_like(vcat))
        o = o + _dot(a.astype(BF16), vcat)
        return o, state_update(s, k, vt, p, dirn)

    def ctx_state(dirn):
        p = prefix_sums(cad_ref[0], dirn)
        s0 = jnp.zeros((2 * GLA_DV, LANES), F32)
        return state_update(s0, ck_ref[0].astype(F32), cv_ref[0].T, p, dirn)

    def body(i, carry):
        s_f, s_b = carry
        for u in range(GLA_UNROLL):
            c = i * GLA_UNROLL + u
            tok_f = pl.multiple_of(c * t, t)
            tok_b = pl.multiple_of((n_chunks - 1 - c) * t, t)
            o_f, s_f = chunk(tok_f, s_f, 0)
            o_b, s_b = chunk(tok_b, s_b, 1)
            accf_ref[pl.ds(tok_f, t), :] = o_f
            accb_ref[pl.ds(tok_b, t), :] = o_b
        return s_f, s_b

    assert n_chunks % GLA_UNROLL == 0
    lax.fori_loop(0, n_chunks // GLA_UNROLL, body, (ctx_state(0), ctx_state(1)))

    def finish(c, carry):
        tok0 = pl.multiple_of(c * t, t)
        o = accf_ref[pl.ds(tok0, t), :] + accb_ref[pl.ds(tok0, t), :]
        g = g_ref[0, pl.ds(tok0, t), :].astype(F32)
        gate = g / (1.0 + jnp.exp(-g))
        halves = [_rms(o[:, h * GLA_DV:(h + 1) * GLA_DV], gn_ref[...]) for h in range(2)]
        o_ref[0, pl.ds(tok0, t), :] = (jnp.concatenate(halves, axis=-1) * gate).astype(o_ref.dtype)
        return carry

    lax.fori_loop(0, n_chunks, finish, 0)


def _gla(q, k, v, ad, g, ck, cv, cad, u, abias, gnorm, cmats):
    b, n, kw = q.shape
    n_ctx = ck.shape[1]
    pairs = kw // LANES
    vw = 2 * GLA_DV
    tok = lambda i, p: (i, 0, p)
    full3 = lambda i, p: (i, 0, 0)
    return pl.pallas_call(
        functools.partial(_gla_kernel, n_tok=n),
        out_shape=jax.ShapeDtypeStruct((b, n, v.shape[2]), BF16),
        grid=(b, pairs),
        in_specs=[pl.BlockSpec((1, n, LANES), tok),
                  pl.BlockSpec((1, n, LANES), tok),
                  pl.BlockSpec((1, n, vw), tok),
                  pl.BlockSpec((1, n, LANES), full3),
                  pl.BlockSpec((1, n, vw), tok),
                  pl.BlockSpec((1, n_ctx, LANES), tok),
                  pl.BlockSpec((1, n_ctx, vw), tok),
                  pl.BlockSpec((1, n_ctx, LANES), full3),
                  pl.BlockSpec((2, LANES, LANES), lambda i, p: (0, 0, p)),
                  pl.BlockSpec((2, 1, LANES), lambda i, p: (0, 0, p)),
                  pl.BlockSpec((1, GLA_DV), lambda i, p: (0, 0)),
                  pl.BlockSpec(cmats.shape, lambda i, p: (0, 0, 0))],
        out_specs=pl.BlockSpec((1, n, vw), tok),
        scratch_shapes=[pltpu.VMEM((n, vw), F32), pltpu.VMEM((n, vw), F32)],
        compiler_params=_cparams(("arbitrary", "arbitrary")),
        name="gla",
    )(q, k, v, ad, g, ck, cv, cad, u, abias, gnorm, cmats)


def _gla_decay_up(a_up):
    r = GLA_GATE_RANK
    pad = jnp.zeros((2, 2 * r, GLA_KEY_W), F32)
    pad = pad.at[0, :r].set(a_up[0]).at[1, r:].set(a_up[1])
    hi, lo = _split_bf16(pad)
    return jnp.concatenate([hi, hi, lo, jnp.zeros_like(hi)], axis=1)


def _outproj_kernel(ona_ref, ogla_ref, x_ref, w1_ref, w2_ref, gm_ref, shf_ref, scf_ref, npost_ref,
                    nfpre_ref, rt_ref, xnew_ref, hf_ref, afft_ref):
    for r0 in range(0, x_ref.shape[1], TOK_TILE):
        rs = slice(r0, r0 + TOK_TILE)
        mix = _dot(ona_ref[0, rs], w1_ref[...]) + _dot(ogla_ref[0, rs], w2_ref[...])
        xn = x_ref[0, rs] + gm_ref[0] * _rms(mix, npost_ref[...])
        xnew_ref[0, rs] = xn
        h = _rms(xn, nfpre_ref[...]) * (1.0 + scf_ref[0]) + shf_ref[0]
        h_hi, h_lo = _split_bf16(h)
        hf_ref[0, rs] = _pack_bf16_pairs(h_hi)
        res = _dot(h_hi, rt_ref[...])
        logits = res[:, :LANES] + res[:, LANES:] + _dot(h_lo, rt_ref[:, :LANES])
        lane = lax.broadcasted_iota(jnp.int32, logits.shape, 1)
        logits = jnp.where(lane < N_EXPERTS, logits, NEG_BIG)
        p = jnp.exp(logits - jnp.max(logits, axis=-1, keepdims=True))
        aff = p / jnp.sum(p, axis=-1, keepdims=True)
        afft_ref[0, :, rs] = aff.T[:N_EXPERTS, :]


def _outproj(o_na, o_gla, x, w1, w2, gm, shf, scf, npost, nfpre, router_cat):
    b, n, d = x.shape
    tm = OUTPROJ_SUBTILES * TOK_TILE
    tokmap = lambda i, j: (i, j, 0)
    smp = lambda i, j: (i, 0, 0)
    cst = lambda i, j: (0, 0)
    return pl.pallas_call(
        _outproj_kernel,
        out_shape=[jax.ShapeDtypeStruct((b, n, d), F32),
                   jax.ShapeDtypeStruct((b, n, d // 2), jnp.int32),
                   jax.ShapeDtypeStruct((b, N_EXPERTS, n), F32)],
        grid=(b, n // tm),
        in_specs=[pl.BlockSpec((1, tm, o_na.shape[2]), tokmap),
                  pl.BlockSpec((1, tm, o_gla.shape[2]), tokmap),
                  pl.BlockSpec((1, tm, d), tokmap),
                  pl.BlockSpec(w1.shape, cst),
                  pl.BlockSpec(w2.shape, cst),
                  pl.BlockSpec((1, 1, d), smp),
                  pl.BlockSpec((1, 1, d), smp),
                  pl.BlockSpec((1, 1, d), smp),
                  pl.BlockSpec((1, d), cst),
                  pl.BlockSpec((1, d), cst),
                  pl.BlockSpec(router_cat.shape, cst)],
        out_specs=[pl.BlockSpec((1, tm, d), tokmap),
                   pl.BlockSpec((1, tm, d // 2), tokmap),
                   pl.BlockSpec((1, N_EXPERTS, tm), lambda i, j: (i, 0, j))],
        compiler_params=_cparams(("arbitrary", "arbitrary")),
        name="outproj",
    )(o_na, o_gla, x, w1, w2, gm, shf, scf, npost.reshape(1, d), nfpre.reshape(1, d), router_cat)


def _route_kernel(afft_ref, rt_ref, *, cap):
    a = afft_ref[0]
    e, n = a.shape
    capf = jnp.float32(cap)

    def search(i, thr_bits):
        cand = thr_bits | lax.shift_left(jnp.int32(1), 30 - i)
        cnt = jnp.sum(jnp.where(a >= lax.bitcast_convert_type(cand, F32), 1.0, 0.0), axis=-1, keepdims=True)
        return jnp.where(cnt >= capf, cand, thr_bits)

    thr_bits = lax.fori_loop(0, 31, search, jnp.zeros((e, 1), jnp.int32))
    thr = lax.bitcast_convert_type(thr_bits, F32)
    need = capf - jnp.sum(jnp.where(a > thr, 1.0, 0.0), axis=-1, keepdims=True)
    r_i = lax.broadcasted_iota(jnp.int32, (LANES, LANES), 0)
    c_i = lax.broadcasted_iota(jnp.int32, (LANES, LANES), 1)
    incl = jnp.where(r_i <= c_i, 1.0, 0.0).astype(BF16)
    off_eq = jnp.zeros((e, 1), F32)
    off_sel = jnp.zeros((e, 1), F32)
    for j in range(n // LANES):
        sl = slice(j * LANES, (j + 1) * LANES)
        a_b = a[:, sl]
        eq_b = jnp.where(a_b == thr, 1.0, 0.0)
        tie_rank = _dot(eq_b.astype(BF16), incl) - eq_b + off_eq
        off_eq = off_eq + jnp.sum(eq_b, axis=-1, keepdims=True)
        sel_b = jnp.where(a_b > thr, 1.0, jnp.where(tie_rank < need, eq_b, 0.0))
        sel = sel_b > 0.5
        rank = _dot(sel_b.astype(BF16), incl) - sel_b + off_sel
        off_sel = off_sel + jnp.sum(sel_b, axis=-1, keepdims=True)
        rsel = jnp.where(sel, rank, -1.0)
        rt_ref[0, :, sl] = rsel.astype(jnp.int32)


def _route(afft, cap):
    b, e, n = afft.shape
    return pl.pallas_call(
        functools.partial(_route_kernel, cap=cap),
        out_shape=jax.ShapeDtypeStruct((b, e, n), jnp.int32),
        grid=(b,),
        in_specs=[pl.BlockSpec((1, e, n), lambda i: (i, 0, 0))],
        out_specs=pl.BlockSpec((1, e, n), lambda i: (i, 0, 0)),
        compiler_params=_cparams(("arbitrary",)),
        name="route",
    )(afft)


def _sc_gather(rsel_t, hf2, cap):
    b, e, n = rsel_t.shape
    width = hf2.shape[1]
    info = plsc.get_sparse_core_info()
    nc, lanes = info.num_cores, info.num_lanes
    workers = nc * info.num_subcores
    items = b * e
    assert items % workers == 0 and n % lanes == 0 and cap % SC_GATHER_ROWS == 0
    per_worker = items // workers
    mesh = plsc.VectorSubcoreMesh(core_axis_name="c", subcore_axis_name="s")

    def body(rank_hbm, hf_hbm, out_hbm, rank_v, idx_v, rows_a, rows_b, sem_a, sem_b):
        bufs = ((rows_a, sem_a), (rows_b, sem_b))
        wid = lax.axis_index("s") * nc + lax.axis_index("c")
        for k in range(per_worker):
            item = wid * per_worker + k
            base_tok = (item // e) * n
            pltpu.sync_copy(rank_hbm.at[item], rank_v)

            @pl.loop(0, n // lanes)
            def _(j):
                r = rank_v[pl.ds(j * lanes, lanes)]
                tok = lax.iota(jnp.int32, lanes) + (j * lanes + base_tok)
                plsc.store_scatter(idx_v, [r], tok, mask=r >= 0)

            def gather(c):
                buf, sem = bufs[c % 2]
                rows = pl.ds(c * SC_GATHER_ROWS, SC_GATHER_ROWS)
                return pltpu.async_copy(hf_hbm.at[idx_v.at[rows]], buf, sem)

            n_chunks = cap // SC_GATHER_ROWS
            pending = gather(0)
            for c in range(n_chunks):
                nxt = gather(c + 1) if c + 1 < n_chunks else None
                pending.wait()
                pltpu.sync_copy(bufs[c % 2][0],
                                out_hbm.at[pl.ds(item * cap + c * SC_GATHER_ROWS, SC_GATHER_ROWS)])
                pending = nxt

    return pl.kernel(
        body, out_type=jax.ShapeDtypeStruct((items * cap, width), hf2.dtype), mesh=mesh,
        scratch_types=[pltpu.VMEM((n,), jnp.int32), pltpu.VMEM((cap,), jnp.int32),
                       pltpu.VMEM((SC_GATHER_ROWS, width), hf2.dtype),
                       pltpu.VMEM((SC_GATHER_ROWS, width), hf2.dtype),
                       pltpu.SemaphoreType.DMA, pltpu.SemaphoreType.DMA],
        compiler_params=pltpu.CompilerParams(needs_layout_passes=False),
        name="scgather",
    )(rsel_t.reshape(items, n), hf2)


def _ffn_kernel(x_ref, wg_ref, wu_ref, wd_ref, o_ref, acc_ref, xb_ref):
    f = pl.program_id(1)
    b = x_ref.shape[0]
    wg = wg_ref[0].astype(BF16)
    wu = wu_ref[0].astype(BF16)
    wd = wd_ref[0].astype(BF16)

    @pl.when(f == 0)
    def _():
        acc_ref[...] = jnp.zeros_like(acc_ref)
        for i in range(b):
            xb_ref[i] = _unpack_bf16_pairs(x_ref[i, 0])

    for i in range(b):
        x = xb_ref[i]
        g = _dot(x, wg)
        u = _dot(x, wu)
        hid = (g / (1.0 + jnp.exp(-g)) * u).astype(BF16)
        acc_ref[i] += _dot(hid, wd)

    @pl.when(f == pl.num_programs(1) - 1)
    def _():
        for i in range(b):
            o_ref[i, 0] = acc_ref[i].astype(o_ref.dtype)


def _ffn(xs, w_gate, w_up, w_down):
    b, e, cap, dp = xs.shape
    d = 2 * dp
    dff = w_gate.shape[2]
    tf = FFN_TILE
    return pl.pallas_call(
        _ffn_kernel,
        out_shape=jax.ShapeDtypeStruct((b, e, cap, d), BF16),
        grid=(e, dff // tf),
        in_specs=[pl.BlockSpec((b, 1, cap, dp), lambda i, f: (0, i, 0, 0)),
                  pl.BlockSpec((1, d, tf), lambda i, f: (i, 0, f)),
                  pl.BlockSpec((1, d, tf), lambda i, f: (i, 0, f)),
                  pl.BlockSpec((1, tf, d), lambda i, f: (i, f, 0))],
        out_specs=pl.BlockSpec((b, 1, cap, d), lambda i, f: (0, i, 0, 0)),
        scratch_shapes=[pltpu.VMEM((b, cap, d), F32), pltpu.VMEM((b, cap, d), BF16)],
        compiler_params=_cparams(("arbitrary", "arbitrary")),
        name="ffn",
    )(xs, w_gate, w_up, w_down)


def _combine_kernel(off_ref, ys_ref, rt_ref, afft_ref, xn_ref, gf_ref, npost_ref, o_ref, acc_ref):
    bi, tt = pl.program_id(0), pl.program_id(1)
    tm = rt_ref.shape[2]
    n_experts, cap = ys_ref.shape[1], ys_ref.shape[2]
    blocks = tm // LANES
    slot = lax.broadcasted_iota(jnp.int32, (COMBINE_WIN, tm), 0)

    def window(e, w0, j):
        nominal = w0 + j * COMBINE_WIN
        start = pl.multiple_of(jnp.minimum(nominal, cap - COMBINE_WIN), BF16_ROWS)
        rank = rt_ref[0, e:e + 1, :]
        hit = ((rank - start) == slot) & (rank >= nominal)
        weights = jnp.where(hit, afft_ref[0, e:e + 1, :], 0.0).astype(BF16)
        return weights, ys_ref[0, e, pl.ds(start, COMBINE_WIN), :]

    first, extra = [], []
    for e in range(n_experts):
        r0 = off_ref[bi, e, tt * blocks]
        r1 = off_ref[bi, e, (tt + 1) * blocks]
        w0 = (r0 // BF16_ROWS) * BF16_ROWS
        first.append(w0)
        extra.append(jnp.maximum((r1 - w0 + COMBINE_WIN - 1) // COMBINE_WIN - 1, 0))
    terms = []
    for e in range(0, n_experts, 2):
        (wa, ya), (wb, yb) = window(e, first[e], 0), window(e + 1, first[e + 1], 0)
        terms.append(_dot(jnp.concatenate([wa, wb], axis=0).T, jnp.concatenate([ya, yb], axis=0)))
    acc_ref[...] = functools.reduce(lambda a, c: a + c, terms)

    @pl.when(functools.reduce(lambda a, c: a + c, extra) > 0)
    def _():
        for e in range(n_experts):
            def more(j, carry, e=e):
                w, y = window(e, first[e], j)
                acc_ref[...] += _dot(w.T, y)
                return carry
            lax.fori_loop(1, extra[e] + 1, more, 0)

    o_ref[0] = xn_ref[0] + gf_ref[0] * _rms(acc_ref[...], npost_ref[...])


def _combine(ys, slot_off, rsel_t, aff_t, x_new, gf, npost):
    b, e, cap, d = ys.shape
    n = x_new.shape[1]
    tm = TOK_TILE
    tokmap = lambda i, j, off: (i, j, 0)
    return pl.pallas_call(
        _combine_kernel,
        out_shape=jax.ShapeDtypeStruct((b, n, d), F32),
        grid_spec=pltpu.PrefetchScalarGridSpec(
            num_scalar_prefetch=1,
            grid=(b, n // tm),
            in_specs=[pl.BlockSpec((1, e, cap, d), lambda i, j, off: (i, 0, 0, 0)),
                      pl.BlockSpec((1, e, tm), lambda i, j, off: (i, 0, j)),
                      pl.BlockSpec((1, e, tm), lambda i, j, off: (i, 0, j)),
                      pl.BlockSpec((1, tm, d), tokmap),
                      pl.BlockSpec((1, 1, d), lambda i, j, off: (i, 0, 0)),
                      pl.BlockSpec((1, d), lambda i, j, off: (0, 0))],
            out_specs=pl.BlockSpec((1, tm, d), tokmap),
            scratch_shapes=[pltpu.VMEM((tm, d), F32)]),
        compiler_params=_cparams(("arbitrary", "arbitrary")),
        name="combine",
    )(slot_off, ys, rsel_t, aff_t, x_new, gf, npost.reshape(1, d))


def kernel(x, c, ctx, c_ctx, w_mod, b_mod, norm_mix_pre, norm_mix_post, norm_ffn_pre, norm_ffn_post,
           w_in, na_rpb, gla_a_up, gla_a_bias, gla_norm, w_out, router, w_gate, w_up, w_down):
    b, n, d = x.shape
    assert w_mod.shape[0] == 1 and d == D_MODEL and n % (GRID_W * NA_QROWS) == 0 and n % GLA_T == 0
    assert ctx.shape[1] == GLA_T
    rows = n // GRID_W
    cap = EC_CAPACITY_FACTOR * n // N_EXPERTS

    c8 = jnp.concatenate([c, c_ctx[None, :], jnp.zeros((8 - b - 1, d), F32)], axis=0)
    mod = _mod(c8, w_mod[0], b_mod[0])
    sh_m, sc_m, g_m, sh_f, sc_f, g_f = [m[:b, None, :] for m in jnp.split(mod, 6, axis=-1)]
    sh_c, sc_c = mod[b:b + 1, None, :d], mod[b:b + 1, None, d:2 * d]

    wb = w_in[0].astype(BF16)
    cuts = np.cumsum([0, NA_W, NA_W, GLA_KEY_W, GLA_VAL_W, 2 * GLA_GATE_RANK, NA_W, GLA_KEY_W, GLA_VAL_W])
    w_nak, w_nav, w_gk, w_gv, w_ad, w_naq, w_gq, w_gg = [wb[:, cuts[i]:cuts[i + 1]] for i in range(8)]
    w_ad3 = jnp.concatenate([w_ad, w_ad, w_ad, jnp.zeros_like(w_ad)], axis=1)
    rope = _rope_tables(n)
    na_q, na_k, na_v, gq, gk, gv, ad, gg = _inproj(
        x, sh_m, sc_m, norm_mix_pre[0],
        [w_naq, w_nak, w_nav, w_gq, w_gk, w_gv, w_ad3, w_gg],
        ["plain", "plain", "plain", "rope", "rope", "plain", "split", "plain"],
        [1.0, 1.0, 1.0, GLA_DK ** -0.5, 1.0, 1.0, 1.0, 1.0],
        [BF16] * 8, rope=rope)
    c_nak, c_nav, c_gk, c_gv, c_ad = _inproj(
        ctx, sh_c, sc_c, norm_mix_pre[0],
        [w_nak, w_nav, w_gk, w_gv, w_ad3],
        ["plain", "plain", "plain", "plain", "split"], [1.0] * 5, [BF16] * 5)

    o_na = _na(na_q, na_k, na_v, c_nak, c_nav, *_na_bias_tables(na_rpb[0], rows))

    cmats = jnp.asarray(_gla_prefix_matrices(GLA_T), BF16)
    o_gla = _gla(gq, gk, gv, ad, gg, c_gk, c_gv, c_ad, _gla_decay_up(gla_a_up[0]),
                 gla_a_bias[0][:, None, :], gla_norm[0][None, :], cmats)

    wo = w_out[0].astype(BF16)
    router_pad = jnp.zeros((d, LANES), F32).at[:, :N_EXPERTS].set(router[0])
    x_new, hf, aff_t = _outproj(o_na, o_gla, x, wo[:NA_W], wo[NA_W:], g_m, sh_f, sc_f,
                                     norm_mix_post[0], norm_ffn_pre[0],
                                     jnp.concatenate(_split_bf16(router_pad), axis=1))

    rsel_t = _route(aff_t, cap)
    xs = _sc_gather(rsel_t, hf.reshape(b * n, d // 2), cap).reshape(b, N_EXPERTS, cap, d // 2)
    ys = _ffn(xs, w_gate[0], w_up[0], w_down[0])
    picked = (rsel_t >= 0).astype(jnp.int32).reshape(b, N_EXPERTS, n // LANES, LANES).sum(axis=-1)
    slot_off = jnp.concatenate([jnp.zeros((b, N_EXPERTS, 1), jnp.int32), jnp.cumsum(picked, axis=-1)], axis=-1)
    return _combine(ys, slot_off, rsel_t, aff_t, x_new, g_f, norm_ffn_post[0])
```

```python
import functools

import numpy as np
import jax
import jax.numpy as jnp
from jax import lax
from jax.experimental import pallas as pl
from jax.experimental.pallas import tpu as pltpu
from jax.experimental.pallas import tpu_sc as plsc

F32 = jnp.float32
BF16 = jnp.bfloat16
HIGHEST = lax.Precision.HIGHEST

D_MODEL = 1024
GRID_W = 64
NA_W = 512
NA_HEADS = 8
NA_HEAD_DIM = 64
NA_WIN_ROWS = 8
NA_WIN_COLS = 16
GLA_HEADS = 4
GLA_DV = 128
GLA_DK = 64
GLA_KEY_W = 256
GLA_VAL_W = 512
GLA_GATE_RANK = 16
GLA_GATE_TAU = 16.0
ROPE_BASE = 10000.0
N_EXPERTS = 16
EC_CAPACITY_FACTOR = 2
NORM_EPS = 1e-6
NEG_BIG = -1e30

LANES = 128
BF16_ROWS = 16
VMEM_LIMIT = 56 * 1024 * 1024

TOK_TILE = 512
NA_QROWS = 4
NA_KROWS = NA_QROWS + NA_WIN_ROWS
NA_TAB_PAD = NA_QROWS
NA_TAB_BLOCKS = NA_TAB_PAD + 2 * NA_WIN_ROWS - 1 + NA_QROWS + 1
GLA_T = 256
GLA_LEVELS = (128, 64, 32, 16)
GLA_DIAG = 16
GLA_UNROLL = 2
OUTPROJ_SUBTILES = 2
FFN_TILE = 256
COMBINE_WIN = 128
SC_GATHER_ROWS = 64


def _cparams(sem):
    return pltpu.CompilerParams(dimension_semantics=sem, vmem_limit_bytes=VMEM_LIMIT)


def _rms(v, g):
    return v * lax.rsqrt(jnp.mean(v * v, axis=-1, keepdims=True) + NORM_EPS) * g


def _dot(a, b):
    return jnp.dot(a, b, preferred_element_type=F32)


def _dot_nt(a, b):
    return lax.dot_general(a, b, (((1,), (1,)), ((), ())), preferred_element_type=F32)


def _split_bf16(v):
    hi = v.astype(BF16)
    return hi, (v - hi.astype(F32)).astype(BF16)


_HIGH_HALF = -65536


def _pack_bf16_pairs(hb):
    bits = lax.bitcast_convert_type(hb.astype(F32), jnp.int32)
    half = hb.shape[1] // 2
    return lax.shift_right_logical(bits[:, :half], 16) | (bits[:, half:] & _HIGH_HALF)


def _unpack_bf16_pairs(w):
    lo = lax.bitcast_convert_type(lax.shift_left(w, 16), F32).astype(BF16)
    hi = lax.bitcast_convert_type(w & _HIGH_HALF, F32).astype(BF16)
    return jnp.concatenate([lo, hi], axis=1)


def _mod_kernel(c_ref, w_ref, b_ref, o_ref):
    c = c_ref[...]
    s = c / (1.0 + jnp.exp(-c))
    o_ref[...] = _dot(s.astype(BF16), w_ref[...].astype(BF16)) + b_ref[...]


def _mod(c8, w_mod, b_mod):
    d, n = w_mod.shape
    tn = 1536
    return pl.pallas_call(
        _mod_kernel,
        out_shape=jax.ShapeDtypeStruct((8, n), F32),
        grid=(n // tn,),
        in_specs=[pl.BlockSpec((8, d), lambda j: (0, 0)),
                  pl.BlockSpec((d, tn), lambda j: (0, j)),
                  pl.BlockSpec((1, tn), lambda j: (0, j))],
        out_specs=pl.BlockSpec((8, tn), lambda j: (0, j)),
        compiler_params=_cparams(("arbitrary",)),
        name="mod",
    )(c8, w_mod, b_mod.reshape(1, n))


def _inproj_kernel(x_ref, sh_ref, sc_ref, g_ref, *refs, modes, scales):
    n_rope = 2 if "rope" in modes else 0
    rope_refs, refs = refs[:n_rope], refs[n_rope:]
    n_w = len(modes)
    w_refs, o_refs, wb_refs = refs[:n_w], refs[n_w:2 * n_w], refs[2 * n_w:]

    @pl.when((pl.program_id(0) == 0) & (pl.program_id(1) == 0))
    def _():
        for w_ref, wb_ref in zip(w_refs, wb_refs):
            wb_ref[...] = w_ref[...].astype(BF16)

    x = x_ref[0]
    h = _rms(x, g_ref[...]) * (1.0 + sc_ref[0]) + sh_ref[0]
    hb = h.astype(BF16)
    for w_ref, o_ref, mode, scale in zip(wb_refs, o_refs, modes, scales):
        y = _dot(hb, w_ref[...])
        if mode == "rope":
            cols = y.shape[1]
            quarter = GLA_DK // 4
            lane = lax.broadcasted_iota(jnp.int32, y.shape, 1)
            partner = jnp.where((lane & (2 * quarter - 1)) < quarter,
                                pltpu.roll(y, cols - quarter, axis=1), pltpu.roll(y, quarter, axis=1))
            y = (y * rope_refs[0][...] + partner * rope_refs[1][...]) * scale
        elif mode == "split":
            hi, lo = _split_bf16(y)
            lane = lax.broadcasted_iota(jnp.int32, y.shape, 1)
            rank2 = 2 * GLA_GATE_RANK
            y = jnp.where((lane >= rank2) & (lane < 2 * rank2), lo, hi)
        o_ref[0] = y.astype(o_ref.dtype)


def _inproj(x, shift, scale, g, weights, modes, scales, out_dtypes, rope=None):
    b, n, d = x.shape
    tm = min(TOK_TILE, n)
    per_sample = shift.shape[0] == b
    mod_map = (lambda i, j: (i, 0, 0)) if per_sample else (lambda i, j: (0, 0, 0))
    in_specs = [pl.BlockSpec((1, tm, d), lambda i, j: (i, j, 0)),
                pl.BlockSpec((1, 1, d), mod_map),
                pl.BlockSpec((1, 1, d), mod_map),
                pl.BlockSpec((1, d), lambda i, j: (0, 0))]
    args = [x, shift, scale, g.reshape(1, d)]
    if rope is not None:
        in_specs += [pl.BlockSpec((tm, rope[0].shape[1]), lambda i, j: (j, 0))] * 2
        args += list(rope)
    out_shapes, out_specs = [], []
    for w, mode, dt in zip(weights, modes, out_dtypes):
        in_specs.append(pl.BlockSpec(w.shape, lambda i, j: (0, 0), pipeline_mode=pl.Buffered(1)))
        cols = w.shape[1]
        out_shapes.append(jax.ShapeDtypeStruct((b, n, cols), dt))
        out_specs.append(pl.BlockSpec((1, tm, cols), lambda i, j: (i, j, 0)))
    return pl.pallas_call(
        functools.partial(_inproj_kernel, modes=tuple(modes), scales=tuple(scales)),
        out_shape=out_shapes,
        grid=(b, n // tm),
        in_specs=in_specs,
        out_specs=out_specs,
        scratch_shapes=[pltpu.VMEM(w.shape, BF16) for w in weights],
        compiler_params=_cparams(("arbitrary", "arbitrary")),
        name="inproj",
    )(*args, *weights)


def _rope_tables(n):
    t = np.arange(n)
    pos_row, pos_col = (t // GRID_W).astype(np.float32), (t % GRID_W).astype(np.float32)
    quarter = GLA_DK // 4
    freqs = np.float32(ROPE_BASE) ** (-np.arange(quarter, dtype=np.float32) / quarter)
    ang_r = pos_row[:, None] * freqs
    ang_c = pos_col[:, None] * freqs
    cos = np.concatenate([np.cos(ang_r), np.cos(ang_r), np.cos(ang_c), np.cos(ang_c)], axis=-1)
    sin = np.concatenate([-np.sin(ang_r), np.sin(ang_r), -np.sin(ang_c), np.sin(ang_c)], axis=-1)
    return jnp.asarray(np.tile(cos, (1, GLA_HEADS))), jnp.asarray(np.tile(sin, (1, GLA_HEADS)))


def _na_patterns(rows):
    kr = min(NA_WIN_ROWS, rows)
    n_blocks = rows // NA_QROWS
    pats = []
    for blk in (0, 1, n_blocks - 1):
        r0 = blk * NA_QROWS
        k0 = int(np.clip(r0 - kr // 2, 0, rows - NA_KROWS))
        strips = []
        for a in range(NA_QROWS):
            r_start = int(np.clip(r0 + a - kr // 2, 0, rows - kr))
            start = k0 - (r0 + a) + NA_WIN_ROWS - 1 + NA_TAB_PAD
            assert 0 <= start and start + NA_KROWS <= NA_TAB_BLOCKS
            strips.append((start, [r_start <= k0 + c < r_start + kr for c in range(NA_KROWS)]))
        pats.append(strips)
    return pats


def _na_bias_tables(rpb, rows):
    heads = rpb.shape[0]
    col = np.arange(GRID_W)
    c_start = np.clip(col - NA_WIN_COLS // 2, 0, GRID_W - NA_WIN_COLS)
    col_ok = (col[None, :] >= c_start[:, None]) & (col[None, :] < c_start[:, None] + NA_WIN_COLS)
    dc = np.clip(col[None, :] - col[:, None] + NA_WIN_COLS - 1, 0, 2 * NA_WIN_COLS - 2)
    sel_c = (np.arange(2 * NA_WIN_COLS - 1)[:, None, None] == dc[None]) & col_ok[None]
    t = jnp.einsum("hrd,dqk->hqrk", rpb, jnp.asarray(sel_c, F32), precision=HIGHEST)
    t = jnp.where(jnp.asarray(col_ok)[None, :, None, :], t, NEG_BIG)
    n_dr = 2 * NA_WIN_ROWS - 1
    t = t.reshape(heads, GRID_W, n_dr * GRID_W)
    back = NA_TAB_BLOCKS + 1 - NA_TAB_PAD - n_dr
    t = jnp.pad(t, ((0, 0), (0, 0), (NA_TAB_PAD * GRID_W, back * GRID_W)), constant_values=NEG_BIG)
    width = NA_TAB_BLOCKS * GRID_W
    tab = jnp.stack([t[:, :, :width], t[:, :, GRID_W:GRID_W + width]], axis=1)
    row_mask = np.zeros((3, NA_QROWS, 1, NA_KROWS * GRID_W), np.float32)
    for pat, strips in enumerate(_na_patterns(rows)):
        for a, (_, valid) in enumerate(strips):
            row_mask[pat, a, 0] = np.repeat(np.where(valid, 0.0, NEG_BIG), GRID_W)
    return tab, jnp.asarray(row_mask)


def _na_kernel(q_ref, k_ref, v_ref, kc_ref, vc_ref, tab_ref, rmask_ref, o_ref, bias_ref,
               sw_ref, sc_ref, pw_ref, pc_ref, *, rows):
    nq, nk = NA_QROWS * GRID_W, NA_KROWS * GRID_W
    n_blocks = rows // NA_QROWS
    kr = min(NA_WIN_ROWS, rows)
    scale = NA_HEAD_DIM ** -0.5
    lane = lax.broadcasted_iota(jnp.int32, (nq, LANES), 1)
    first_head = lane < NA_HEAD_DIM
    kc = kc_ref[0]
    vc = vc_ref[0]
    lane_w = lax.broadcasted_iota(jnp.int32, (nk, LANES), 1)
    lane_c = lax.broadcasted_iota(jnp.int32, vc.shape, 1)
    for pat, strips in enumerate(_na_patterns(rows)):
        for a, (start, _) in enumerate(strips):
            parity = start % 2
            off = (start - parity) * GRID_W
            for h in range(2):
                bias_ref[0, pat, h, a * GRID_W:(a + 1) * GRID_W, :] = (
                    tab_ref[h, parity, :, off:off + nk] + rmask_ref[pat, a])

    def key_start(i):
        return pl.multiple_of(jnp.clip(i * NA_QROWS - kr // 2, 0, rows - NA_KROWS) * GRID_W, GRID_W)

    def scores(i, slot):
        pat = jnp.where(i == 0, 0, jnp.where(i == n_blocks - 1, 2, 1))
        q = q_ref[0, pl.ds(pl.multiple_of(i * nq, nq), nq), :] * scale
        kw = k_ref[0, pl.ds(key_start(i), nk), :]
        for h in range(2):
            qh = jnp.where(first_head if h == 0 else jnp.logical_not(first_head), q, jnp.zeros_like(q))
            sw_ref[slot, h] = _dot_nt(qh, kw) + bias_ref[0, pat, h]
            sc_ref[slot, h] = _dot_nt(qh, kc)

    def softmax(slot):
        for h in range(2):
            s_w = sw_ref[slot, h]
            s_c = sc_ref[slot, h]
            m = jnp.maximum(jnp.max(s_w, axis=-1, keepdims=True), jnp.max(s_c, axis=-1, keepdims=True))
            pw_ref[slot, h] = jnp.exp((s_w - m).astype(BF16))
            pc_ref[slot, h] = jnp.exp((s_c - m).astype(BF16))

    def values(i, slot):
        vw = v_ref[0, pl.ds(key_start(i), nk), :]
        outs = []
        for h in range(2):
            sum_lane = NA_HEAD_DIM * (1 - h)
            vw_h = jnp.where(lane_w == sum_lane, jnp.ones_like(vw), vw)
            vc_h = jnp.where(lane_c == sum_lane, jnp.ones_like(vc), vc)
            o = _dot(pw_ref[slot, h], vw_h) + _dot(pc_ref[slot, h], vc_h)
            outs.append(o * (1.0 / o[:, sum_lane:sum_lane + 1]))
        o = jnp.where(first_head, outs[0], outs[1])
        o_ref[0, pl.ds(pl.multiple_of(i * nq, nq), nq), :] = o.astype(o_ref.dtype)

    assert n_blocks % 2 == 0 and n_blocks >= 4
    scores(0, 0)
    softmax(0)
    scores(1, 1)

    def trip(j, carry):
        i = 2 * j
        values(i - 2, 0)
        softmax(1)
        scores(i, 0)
        values(i - 1, 1)
        softmax(0)
        scores(i + 1, 1)
        return carry

    lax.fori_loop(1, n_blocks // 2, trip, 0)
    values(n_blocks - 2, 0)
    softmax(1)
    values(n_blocks - 1, 1)


def _na(q, k, v, kc, vc, tab, row_mask):
    b, n, w = q.shape
    n_ctx = kc.shape[1]
    pairs = w // LANES
    rows = n // GRID_W
    nq, nk = NA_QROWS * GRID_W, NA_KROWS * GRID_W
    tok = lambda i, p: (i, 0, p)
    return pl.pallas_call(
        functools.partial(_na_kernel, rows=rows),
        out_shape=jax.ShapeDtypeStruct((b, n, w), BF16),
        grid=(b, pairs),
        in_specs=[pl.BlockSpec((1, n, LANES), tok),
                  pl.BlockSpec((1, n, LANES), tok),
                  pl.BlockSpec((1, n, LANES), tok),
                  pl.BlockSpec((1, n_ctx, LANES), tok),
                  pl.BlockSpec((1, n_ctx, LANES), tok),
                  pl.BlockSpec((2,) + tab.shape[1:], lambda i, p: (p, 0, 0, 0)),
                  pl.BlockSpec(row_mask.shape, lambda i, p: (0, 0, 0, 0))],
        out_specs=pl.BlockSpec((1, n, LANES), tok),
        scratch_shapes=[pltpu.VMEM((1, 3, 2, nq, nk), F32),
                        pltpu.VMEM((2, 2, nq, nk), F32), pltpu.VMEM((2, 2, nq, n_ctx), F32),
                        pltpu.VMEM((2, 2, nq, nk), BF16), pltpu.VMEM((2, 2, nq, n_ctx), BF16)],
        compiler_params=_cparams(("arbitrary", "arbitrary")),
        name="na",
    )(q, k, v, kc, vc, tab, row_mask)


def _gla_prefix_matrices(t):
    i = np.arange(t)
    return np.stack([i[:, None] >= i[None, :], i[:, None] <= i[None, :]]).astype(np.float32)


def _gla_kernel(q_ref, k_ref, v_ref, ad_ref, g_ref, ck_ref, cv_ref, cad_ref,
                u_ref, ab_ref, gn_ref, cm_ref, o_ref, accf_ref, accb_ref, *, n_tok):
    t = GLA_T
    n_chunks = n_tok // t
    row = lax.broadcasted_iota(jnp.int32, (t, LANES), 0)
    hb = t // 2
    row_b = lax.broadcasted_iota(jnp.int32, (hb, LANES), 0)
    head0_b = lax.broadcasted_iota(jnp.int32, (hb, LANES), 1) < GLA_DK
    row2 = lax.broadcasted_iota(jnp.int32, (hb, 2 * hb), 0)
    col2 = lax.broadcasted_iota(jnp.int32, (hb, 2 * hb), 1) & (hb - 1)
    vrow = lax.broadcasted_iota(jnp.int32, (2 * t, 2 * GLA_DV), 0)
    vlane = lax.broadcasted_iota(jnp.int32, (2 * t, 2 * GLA_DV), 1)
    v_head_match = (vrow >= t) == (vlane >= GLA_DV)
    srow = lax.broadcasted_iota(jnp.int32, (2 * GLA_DV, LANES), 0)
    slane = lax.broadcasted_iota(jnp.int32, (2 * GLA_DV, LANES), 1)
    s_blockdiag = (srow >= GLA_DV) == (slane >= GLA_DK)
    blk_mask = {half: jnp.where((row2 & ~(2 * half - 1)) == (col2 & ~(2 * half - 1)), 1.0, 0.0)
                for half in GLA_LEVELS if 2 * half < hb}
    diag_blk = (row2 & ~(GLA_DIAG - 1)) == (col2 & ~(GLA_DIAG - 1))
    diag_mask = (jnp.where(diag_blk & (row2 >= col2), 1.0, 0.0), jnp.where(diag_blk & (row2 <= col2), 1.0, 0.0))

    def prefix_sums(ad, dirn):
        z = _dot(ad, u_ref[dirn]) + ab_ref[dirn]
        loga = (jnp.minimum(z, 0.0) - jnp.log(1.0 + jnp.exp(-jnp.abs(z)))) * (1.0 / GLA_GATE_TAU)
        hi, lo = _split_bf16(loga)
        p2 = _dot(cm_ref[dirn], jnp.concatenate([hi, lo], axis=-1))
        return p2[:, :LANES] + p2[:, LANES:]

    def chunk_end(p, dirn):
        return p[t - 1:t, :] if dirn == 0 else p[0:1, :]

    def level_sums(p, half, dirn):
        blk = 2 * half
        p3 = p.reshape(t // blk, blk, LANES)
        edge = half - 1 if dirn == 0 else half
        ref = jnp.broadcast_to(p3[:, edge:edge + 1, :], p3.shape).reshape(t, LANES)
        later = (row & half) != 0
        return jnp.where(later == (dirn == 0), p - ref, ref - p)

    def state_update(s, k, vt, p, dirn):
        kh = (k * jnp.exp(chunk_end(p, dirn) - p)).astype(BF16)
        return s * jnp.exp(chunk_end(p, dirn)) + jnp.where(s_blockdiag, _dot(vt, kh), 0.0)

    def chunk(tok0, s, dirn):
        q = q_ref[0, pl.ds(tok0, t), :].astype(F32)
        k = k_ref[0, pl.ds(tok0, t), :].astype(F32)
        v = v_ref[0, pl.ds(tok0, t), :]
        vt = v.T
        p = prefix_sums(ad_ref[0, pl.ds(tok0, t), :], dirn)
        qh = (q * jnp.exp(p)).astype(BF16)
        o = _dot_nt(qh, s.astype(BF16))
        def rows(x, b):
            return x[b * hb:(b + 1) * hb]

        def scores(qt, kt):
            kcat = jnp.concatenate([jnp.where(head0_b, kt, 0.0), jnp.where(head0_b, 0.0, kt)], axis=0)
            return _dot_nt(qt.astype(BF16), kcat.astype(BF16))

        assert GLA_LEVELS[0] == hb
        w = jnp.exp(level_sums(p, hb, dirn))
        qb, kb = (1, 0) if dirn == 0 else (0, 1)
        wide = scores(rows(q, qb) * rows(w, qb), rows(k, kb) * rows(w, kb))
        fine_w = [jnp.exp(level_sums(p, half, dirn)) for half in GLA_LEVELS[1:]]
        e_d = level_sums(p, GLA_DIAG // 2, dirn)
        w_d, wi_d = jnp.exp(e_d), jnp.exp(-e_d)
        fine = []
        for b in range(2):
            qs, ks = rows(q, b), rows(k, b)
            acc = None
            for half, w in zip(GLA_LEVELS[1:], fine_w):
                later = (row_b & half) != 0
                q_side = later if dirn == 0 else jnp.logical_not(later)
                part = scores(jnp.where(q_side, qs * rows(w, b), 0.0), jnp.where(q_side, 0.0, ks * rows(w, b)))
                if half in blk_mask:
                    part = part * blk_mask[half]
                acc = part if acc is None else acc + part
            later = (row_b & (GLA_DIAG // 2)) != 0
            shrink_q = later if dirn == 0 else jnp.logical_not(later)
            part = scores(qs * jnp.where(shrink_q, rows(w_d, b), rows(wi_d, b)),
                          ks * jnp.where(shrink_q, rows(wi_d, b), rows(w_d, b)))
            fine.append(acc + jnp.where(diag_mask[dirn] > 0.5, part, 0.0))
        zero = jnp.zeros((hb, hb), F32)
        h0, h1 = slice(0, hb), slice(hb, 2 * hb)
        if dirn == 0:
            top = [fine[0][:, h0], zero, fine[0][:, h1], zero]
            bot = [wide[:, h0], fine[1][:, h0], wide[:, h1], fine[1][:, h1]]
        else:
            top = [fine[0][:, h0], wide[:, h0], fine[0][:, h1], wide[:, h1]]
            bot = [zero, fine[1][:, h0], zero, fine[1][:, h1]]
        a = jnp.concatenate([jnp.concatenate(top, axis=1), jnp.concatenate(bot, axis=1)], axis=0)
        vcat = jnp.concatenate([v, v], axis=0)
        vcat = jnp.where(v_head_match, vcat, jnp.zeros_like(vcat))
        o = o + _dot(a.astype(BF16), vcat)
        return o, state_update(s, k, vt, p, dirn)

    def ctx_state(dirn):
        p = prefix_sums(cad_ref[0], dirn)
        s0 = jnp.zeros((2 * GLA_DV, LANES), F32)
        return state_update(s0, ck_ref[0].astype(F32), cv_ref[0].T, p, dirn)

    def finish(tok0, o):
        g = g_ref[0, pl.ds(tok0, t), :].astype(F32)
        gate = g / (1.0 + jnp.exp(-g))
        halves = [_rms(o[:, h * GLA_DV:(h + 1) * GLA_DV], gn_ref[...]) for h in range(2)]
        o_ref[0, pl.ds(tok0, t), :] = (jnp.concatenate(halves, axis=-1) * gate).astype(o_ref.dtype)

    def body(i, carry, second_half):
        s_f, s_b = carry
        for u in range(GLA_UNROLL):
            c = i * GLA_UNROLL + u
            tok_f = pl.multiple_of(c * t, t)
            tok_b = pl.multiple_of((n_chunks - 1 - c) * t, t)
            o_f, s_f = chunk(tok_f, s_f, 0)
            o_b, s_b = chunk(tok_b, s_b, 1)
            if second_half:
                finish(tok_f, o_f + accb_ref[pl.ds(tok_f, t), :])
                finish(tok_b, o_b + accf_ref[pl.ds(tok_b, t), :])
            else:
                accf_ref[pl.ds(tok_f, t), :] = o_f
                accb_ref[pl.ds(tok_b, t), :] = o_b
        return s_f, s_b

    trips = n_chunks // GLA_UNROLL
    assert n_chunks % (2 * GLA_UNROLL) == 0
    carry = lax.fori_loop(0, trips // 2, functools.partial(body, second_half=False), (ctx_state(0), ctx_state(1)))
    lax.fori_loop(trips // 2, trips, functools.partial(body, second_half=True), carry)


def _gla(q, k, v, ad, g, ck, cv, cad, u, abias, gnorm, cmats):
    b, n, kw = q.shape
    n_ctx = ck.shape[1]
    pairs = kw // LANES
    vw = 2 * GLA_DV
    tok = lambda i, p: (i, 0, p)
    full3 = lambda i, p: (i, 0, 0)
    return pl.pallas_call(
        functools.partial(_gla_kernel, n_tok=n),
        out_shape=jax.ShapeDtypeStruct((b, n, v.shape[2]), BF16),
        grid=(b, pairs),
        in_specs=[pl.BlockSpec((1, n, LANES), tok),
                  pl.BlockSpec((1, n, LANES), tok),
                  pl.BlockSpec((1, n, vw), tok),
                  pl.BlockSpec((1, n, LANES), full3),
                  pl.BlockSpec((1, n, vw), tok),
                  pl.BlockSpec((1, n_ctx, LANES), tok),
                  pl.BlockSpec((1, n_ctx, vw), tok),
                  pl.BlockSpec((1, n_ctx, LANES), full3),
                  pl.BlockSpec((2, LANES, LANES), lambda i, p: (0, 0, p)),
                  pl.BlockSpec((2, 1, LANES), lambda i, p: (0, 0, p)),
                  pl.BlockSpec((1, GLA_DV), lambda i, p: (0, 0)),
                  pl.BlockSpec(cmats.shape, lambda i, p: (0, 0, 0))],
        out_specs=pl.BlockSpec((1, n, vw), tok),
        scratch_shapes=[pltpu.VMEM((n, vw), F32), pltpu.VMEM((n, vw), F32)],
        compiler_params=_cparams(("arbitrary", "arbitrary")),
        name="gla",
    )(q, k, v, ad, g, ck, cv, cad, u, abias, gnorm, cmats)


def _gla_decay_up(a_up):
    r = GLA_GATE_RANK
    pad = jnp.zeros((2, 2 * r, GLA_KEY_W), F32)
    pad = pad.at[0, :r].set(a_up[0]).at[1, r:].set(a_up[1])
    hi, lo = _split_bf16(pad)
    return jnp.concatenate([hi, hi, lo, jnp.zeros_like(hi)], axis=1)


def _outproj_kernel(ona_ref, ogla_ref, x_ref, w1_ref, w2_ref, gm_ref, shf_ref, scf_ref, npost_ref,
                    nfpre_ref, rt_ref, xnew_ref, hf_ref, afft_ref):
    for r0 in range(0, x_ref.shape[1], TOK_TILE):
        rs = slice(r0, r0 + TOK_TILE)
        mix = _dot(ona_ref[0, rs], w1_ref[...]) + _dot(ogla_ref[0, rs], w2_ref[...])
        xn = x_ref[0, rs] + gm_ref[0] * _rms(mix, npost_ref[...])
        xnew_ref[0, rs] = xn
        h = _rms(xn, nfpre_ref[...]) * (1.0 + scf_ref[0]) + shf_ref[0]
        h_hi, h_lo = _split_bf16(h)
        hf_ref[0, rs] = _pack_bf16_pairs(h_hi)
        res = _dot(h_hi, rt_ref[...])
        logits = res[:, :LANES] + res[:, LANES:] + _dot(h_lo, rt_ref[:, :LANES])
        lane = lax.broadcasted_iota(jnp.int32, logits.shape, 1)
        logits = jnp.where(lane < N_EXPERTS, logits, NEG_BIG)
        p = jnp.exp(logits - jnp.max(logits, axis=-1, keepdims=True))
        aff = p / jnp.sum(p, axis=-1, keepdims=True)
        afft_ref[0, :, rs] = aff.T[:N_EXPERTS, :]


def _outproj(o_na, o_gla, x, w1, w2, gm, shf, scf, npost, nfpre, router_cat):
    b, n, d = x.shape
    tm = OUTPROJ_SUBTILES * TOK_TILE
    tokmap = lambda i, j: (i, j, 0)
    smp = lambda i, j: (i, 0, 0)
    cst = lambda i, j: (0, 0)
    return pl.pallas_call(
        _outproj_kernel,
        out_shape=[jax.ShapeDtypeStruct((b, n, d), F32),
                   jax.ShapeDtypeStruct((b, n, d // 2), jnp.int32),
                   jax.ShapeDtypeStruct((b, N_EXPERTS, n), F32)],
        grid=(b, n // tm),
        in_specs=[pl.BlockSpec((1, tm, o_na.shape[2]), tokmap),
                  pl.BlockSpec((1, tm, o_gla.shape[2]), tokmap),
                  pl.BlockSpec((1, tm, d), tokmap),
                  pl.BlockSpec(w1.shape, cst),
                  pl.BlockSpec(w2.shape, cst),
                  pl.BlockSpec((1, 1, d), smp),
                  pl.BlockSpec((1, 1, d), smp),
                  pl.BlockSpec((1, 1, d), smp),
                  pl.BlockSpec((1, d), cst),
                  pl.BlockSpec((1, d), cst),
                  pl.BlockSpec(router_cat.shape, cst)],
        out_specs=[pl.BlockSpec((1, tm, d), tokmap),
                   pl.BlockSpec((1, tm, d // 2), tokmap),
                   pl.BlockSpec((1, N_EXPERTS, tm), lambda i, j: (i, 0, j))],
        compiler_params=_cparams(("arbitrary", "arbitrary")),
        name="outproj",
    )(o_na, o_gla, x, w1, w2, gm, shf, scf, npost.reshape(1, d), nfpre.reshape(1, d), router_cat)


def _route_kernel(afft_ref, rt_ref, *, cap):
    a = afft_ref[0]
    e, n = a.shape
    capf = jnp.float32(cap)

    def search(i, thr_bits):
        cand = thr_bits | lax.shift_left(jnp.int32(1), 30 - i)
        cnt = jnp.sum(jnp.where(a >= lax.bitcast_convert_type(cand, F32), 1.0, 0.0), axis=-1, keepdims=True)
        return jnp.where(cnt >= capf, cand, thr_bits)

    thr_bits = lax.fori_loop(0, 31, search, jnp.zeros((e, 1), jnp.int32))
    thr = lax.bitcast_convert_type(thr_bits, F32)
    need = capf - jnp.sum(jnp.where(a > thr, 1.0, 0.0), axis=-1, keepdims=True)
    r_i = lax.broadcasted_iota(jnp.int32, (LANES, LANES), 0)
    c_i = lax.broadcasted_iota(jnp.int32, (LANES, LANES), 1)
    incl = jnp.where(r_i <= c_i, 1.0, 0.0).astype(BF16)
    off_eq = jnp.zeros((e, 1), F32)
    off_sel = jnp.zeros((e, 1), F32)
    for j in range(n // LANES):
        sl = slice(j * LANES, (j + 1) * LANES)
        a_b = a[:, sl]
        eq_b = jnp.where(a_b == thr, 1.0, 0.0)
        tie_rank = _dot(eq_b.astype(BF16), incl) - eq_b + off_eq
        off_eq = off_eq + jnp.sum(eq_b, axis=-1, keepdims=True)
        sel_b = jnp.where(a_b > thr, 1.0, jnp.where(tie_rank < need, eq_b, 0.0))
        sel = sel_b > 0.5
        rank = _dot(sel_b.astype(BF16), incl) - sel_b + off_sel
        off_sel = off_sel + jnp.sum(sel_b, axis=-1, keepdims=True)
        rsel = jnp.where(sel, rank, -1.0)
        rt_ref[0, :, sl] = rsel.astype(jnp.int32)


def _route(afft, cap):
    b, e, n = afft.shape
    return pl.pallas_call(
        functools.partial(_route_kernel, cap=cap),
        out_shape=jax.ShapeDtypeStruct((b, e, n), jnp.int32),
        grid=(b,),
        in_specs=[pl.BlockSpec((1, e, n), lambda i: (i, 0, 0))],
        out_specs=pl.BlockSpec((1, e, n), lambda i: (i, 0, 0)),
        compiler_params=_cparams(("arbitrary",)),
        name="route",
    )(afft)


def _sc_gather(rsel_t, hf2, cap):
    b, e, n = rsel_t.shape
    width = hf2.shape[1]
    info = plsc.get_sparse_core_info()
    nc, lanes = info.num_cores, info.num_lanes
    workers = nc * info.num_subcores
    items = b * e
    assert items % workers == 0 and n % lanes == 0 and cap % SC_GATHER_ROWS == 0
    per_worker = items // workers
    mesh = plsc.VectorSubcoreMesh(core_axis_name="c", subcore_axis_name="s")

    def body(rank_hbm, hf_hbm, out_hbm, rank_v, idx_v, rows_a, rows_b, sem_a, sem_b):
        bufs = ((rows_a, sem_a), (rows_b, sem_b))
        wid = lax.axis_index("s") * nc + lax.axis_index("c")
        for k in range(per_worker):
            item = wid * per_worker + k
            base_tok = (item // e) * n
            pltpu.sync_copy(rank_hbm.at[item], rank_v)

            @pl.loop(0, n // lanes)
            def _(j):
                r = rank_v[pl.ds(j * lanes, lanes)]
                tok = lax.iota(jnp.int32, lanes) + (j * lanes + base_tok)
                plsc.store_scatter(idx_v, [r], tok, mask=r >= 0)

            def gather(c):
                buf, sem = bufs[c % 2]
                rows = pl.ds(c * SC_GATHER_ROWS, SC_GATHER_ROWS)
                return pltpu.async_copy(hf_hbm.at[idx_v.at[rows]], buf, sem)

            n_chunks = cap // SC_GATHER_ROWS
            pending = gather(0)
            for c in range(n_chunks):
                nxt = gather(c + 1) if c + 1 < n_chunks else None
                pending.wait()
                pltpu.sync_copy(bufs[c % 2][0],
                                out_hbm.at[pl.ds(item * cap + c * SC_GATHER_ROWS, SC_GATHER_ROWS)])
                pending = nxt

    return pl.kernel(
        body, out_type=jax.ShapeDtypeStruct((items * cap, width), hf2.dtype), mesh=mesh,
        scratch_types=[pltpu.VMEM((n,), jnp.int32), pltpu.VMEM((cap,), jnp.int32),
                       pltpu.VMEM((SC_GATHER_ROWS, width), hf2.dtype),
                       pltpu.VMEM((SC_GATHER_ROWS, width), hf2.dtype),
                       pltpu.SemaphoreType.DMA, pltpu.SemaphoreType.DMA],
        compiler_params=pltpu.CompilerParams(needs_layout_passes=False),
        name="scgather",
    )(rsel_t.reshape(items, n), hf2)


def _ffn_kernel(x_ref, wg_ref, wu_ref, wd_ref, o_ref, acc_ref, xb_ref):
    f = pl.program_id(1)
    b = x_ref.shape[0]
    wg = wg_ref[0].astype(BF16)
    wu = wu_ref[0].astype(BF16)
    wd = wd_ref[0].astype(BF16)

    @pl.when(f == 0)
    def _():
        acc_ref[...] = jnp.zeros_like(acc_ref)
        for i in range(b):
            xb_ref[i] = _unpack_bf16_pairs(x_ref[i, 0])

    for i in range(b):
        x = xb_ref[i]
        g = _dot(x, wg)
        u = _dot(x, wu)
        hid = (g / (1.0 + jnp.exp(-g)) * u).astype(BF16)
        acc_ref[i] += _dot(hid, wd)

    @pl.when(f == pl.num_programs(1) - 1)
    def _():
        for i in range(b):
            o_ref[i, 0] = acc_ref[i].astype(o_ref.dtype)


def _ffn(xs, w_gate, w_up, w_down):
    b, e, cap, dp = xs.shape
    d = 2 * dp
    dff = w_gate.shape[2]
    tf = FFN_TILE
    return pl.pallas_call(
        _ffn_kernel,
        out_shape=jax.ShapeDtypeStruct((b, e, cap, d), BF16),
        grid=(e, dff // tf),
        in_specs=[pl.BlockSpec((b, 1, cap, dp), lambda i, f: (0, i, 0, 0)),
                  pl.BlockSpec((1, d, tf), lambda i, f: (i, 0, f)),
                  pl.BlockSpec((1, d, tf), lambda i, f: (i, 0, f)),
                  pl.BlockSpec((1, tf, d), lambda i, f: (i, f, 0))],
        out_specs=pl.BlockSpec((b, 1, cap, d), lambda i, f: (0, i, 0, 0)),
        scratch_shapes=[pltpu.VMEM((b, cap, d), F32), pltpu.VMEM((b, cap, d), BF16)],
        compiler_params=_cparams(("arbitrary", "arbitrary")),
        name="ffn",
    )(xs, w_gate, w_up, w_down)


def _combine_kernel(off_ref, ys_ref, rt_ref, afft_ref, xn_ref, gf_ref, npost_ref, o_ref, acc_ref):
    bi, tt = pl.program_id(0), pl.program_id(1)
    tm = rt_ref.shape[2]
    n_experts, cap = ys_ref.shape[1], ys_ref.shape[2]
    blocks = tm // LANES
    slot = lax.broadcasted_iota(jnp.int32, (COMBINE_WIN, tm), 0)

    def window(e, w0, j):
        nominal = w0 + j * COMBINE_WIN
        start = pl.multiple_of(jnp.minimum(nominal, cap - COMBINE_WIN), BF16_ROWS)
        rank = rt_ref[0, e:e + 1, :]
        hit = ((rank - start) == slot) & (rank >= nominal)
        weights = jnp.where(hit, afft_ref[0, e:e + 1, :], 0.0).astype(BF16)
        return weights, ys_ref[0, e, pl.ds(start, COMBINE_WIN), :]

    first, extra = [], []
    for e in range(n_experts):
        r0 = off_ref[bi, e, tt * blocks]
        r1 = off_ref[bi, e, (tt + 1) * blocks]
        w0 = (r0 // BF16_ROWS) * BF16_ROWS
        first.append(w0)
        extra.append(jnp.maximum((r1 - w0 + COMBINE_WIN - 1) // COMBINE_WIN - 1, 0))
    terms = []
    for e in range(0, n_experts, 2):
        (wa, ya), (wb, yb) = window(e, first[e], 0), window(e + 1, first[e + 1], 0)
        terms.append(_dot(jnp.concatenate([wa, wb], axis=0).T, jnp.concatenate([ya, yb], axis=0)))
    acc_ref[...] = functools.reduce(lambda a, c: a + c, terms)

    @pl.when(functools.reduce(lambda a, c: a + c, extra) > 0)
    def _():
        for e in range(n_experts):
            def more(j, carry, e=e):
                w, y = window(e, first[e], j)
                acc_ref[...] += _dot(w.T, y)
                return carry
            lax.fori_loop(1, extra[e] + 1, more, 0)

    o_ref[0] = xn_ref[0] + gf_ref[0] * _rms(acc_ref[...], npost_ref[...])


def _combine(ys, slot_off, rsel_t, aff_t, x_new, gf, npost):
    b, e, cap, d = ys.shape
    n = x_new.shape[1]
    tm = TOK_TILE
    tokmap = lambda i, j, off: (i, j, 0)
    return pl.pallas_call(
        _combine_kernel,
        out_shape=jax.ShapeDtypeStruct((b, n, d), F32),
        grid_spec=pltpu.PrefetchScalarGridSpec(
            num_scalar_prefetch=1,
            grid=(b, n // tm),
            in_specs=[pl.BlockSpec((1, e, cap, d), lambda i, j, off: (i, 0, 0, 0)),
                      pl.BlockSpec((1, e, tm), lambda i, j, off: (i, 0, j)),
                      pl.BlockSpec((1, e, tm), lambda i, j, off: (i, 0, j)),
                      pl.BlockSpec((1, tm, d), tokmap),
                      pl.BlockSpec((1, 1, d), lambda i, j, off: (i, 0, 0)),
                      pl.BlockSpec((1, d), lambda i, j, off: (0, 0))],
            out_specs=pl.BlockSpec((1, tm, d), tokmap),
            scratch_shapes=[pltpu.VMEM((tm, d), F32)]),
        compiler_params=_cparams(("arbitrary", "arbitrary")),
        name="combine",
    )(slot_off, ys, rsel_t, aff_t, x_new, gf, npost.reshape(1, d))


def kernel(x, c, ctx, c_ctx, w_mod, b_mod, norm_mix_pre, norm_mix_post, norm_ffn_pre, norm_ffn_post,
           w_in, na_rpb, gla_a_up, gla_a_bias, gla_norm, w_out, router, w_gate, w_up, w_down):
    b, n, d = x.shape
    assert w_mod.shape[0] == 1 and d == D_MODEL and n % (GRID_W * NA_QROWS) == 0 and n % GLA_T == 0
    assert ctx.shape[1] == GLA_T
    rows = n // GRID_W
    cap = EC_CAPACITY_FACTOR * n // N_EXPERTS

    c8 = jnp.concatenate([c, c_ctx[None, :], jnp.zeros((8 - b - 1, d), F32)], axis=0)
    mod = _mod(c8, w_mod[0], b_mod[0])
    sh_m, sc_m, g_m, sh_f, sc_f, g_f = [m[:b, None, :] for m in jnp.split(mod, 6, axis=-1)]
    sh_c, sc_c = mod[b:b + 1, None, :d], mod[b:b + 1, None, d:2 * d]

    wb = w_in[0]
    cuts = np.cumsum([0, NA_W, NA_W, GLA_KEY_W, GLA_VAL_W, 2 * GLA_GATE_RANK, NA_W, GLA_KEY_W, GLA_VAL_W])
    w_nak, w_nav, w_gk, w_gv, w_ad, w_naq, w_gq, w_gg = [wb[:, cuts[i]:cuts[i + 1]] for i in range(8)]
    w_ad3 = jnp.concatenate([w_ad, w_ad, w_ad, jnp.zeros_like(w_ad)], axis=1)
    rope = _rope_tables(n)
    na_q, na_k, na_v, gq, gk, gv, ad, gg = _inproj(
        x, sh_m, sc_m, norm_mix_pre[0],
        [w_naq, w_nak, w_nav, w_gq, w_gk, w_gv, w_ad3, w_gg],
        ["plain", "plain", "plain", "rope", "rope", "plain", "split", "plain"],
        [1.0, 1.0, 1.0, GLA_DK ** -0.5, 1.0, 1.0, 1.0, 1.0],
        [BF16] * 8, rope=rope)
    c_nak, c_nav, c_gk, c_gv, c_ad = _inproj(
        ctx, sh_c, sc_c, norm_mix_pre[0],
        [w_nak, w_nav, w_gk, w_gv, w_ad3],
        ["plain", "plain", "plain", "plain", "split"], [1.0] * 5, [BF16] * 5)

    o_na = _na(na_q, na_k, na_v, c_nak, c_nav, *_na_bias_tables(na_rpb[0], rows))

    cmats = jnp.asarray(_gla_prefix_matrices(GLA_T), BF16)
    o_gla = _gla(gq, gk, gv, ad, gg, c_gk, c_gv, c_ad, _gla_decay_up(gla_a_up[0]),
                 gla_a_bias[0][:, None, :], gla_norm[0][None, :], cmats)

    wo = w_out[0].astype(BF16)
    router_pad = jnp.zeros((d, LANES), F32).at[:, :N_EXPERTS].set(router[0])
    x_new, hf, aff_t = _outproj(o_na, o_gla, x, wo[:NA_W], wo[NA_W:], g_m, sh_f, sc_f,
                                     norm_mix_post[0], norm_ffn_pre[0],
                                     jnp.concatenate(_split_bf16(router_pad), axis=1))

    rsel_t = _route(aff_t, cap)
    xs = _sc_gather(rsel_t, hf.reshape(b * n, d // 2), cap).reshape(b, N_EXPERTS, cap, d // 2)
    ys = _ffn(xs, w_gate[0], w_up[0], w_down[0])
    picked = (rsel_t >= 0).astype(jnp.int32).reshape(b, N_EXPERTS, n // LANES, LANES).sum(axis=-1)
    slot_off = jnp.concatenate([jnp.zeros((b, N_EXPERTS, 1), jnp.int32), jnp.cumsum(picked, axis=-1)], axis=-1)
    return _combine(ys, slot_off, rsel_t, aff_t, x_new, g_f, norm_ffn_post[0])
```

```python
import functools

import numpy as np
import jax
import jax.numpy as jnp
from jax import lax
from jax.experimental import pallas as pl
from jax.experimental.pallas import tpu as pltpu
from jax.experimental.pallas import tpu_sc as plsc

F32 = jnp.float32
BF16 = jnp.bfloat16
HIGHEST = lax.Precision.HIGHEST

D_MODEL = 1024
GRID_W = 64
NA_W = 512
NA_HEADS = 8
NA_HEAD_DIM = 64
NA_WIN_ROWS = 8
NA_WIN_COLS = 16
GLA_HEADS = 4
GLA_DV = 128
GLA_DK = 64
GLA_KEY_W = 256
GLA_VAL_W = 512
GLA_GATE_RANK = 16
GLA_GATE_TAU = 16.0
ROPE_BASE = 10000.0
N_EXPERTS = 16
EC_CAPACITY_FACTOR = 2
NORM_EPS = 1e-6
NEG_BIG = -1e30

LANES = 128
BF16_ROWS = 16
VMEM_LIMIT = 56 * 1024 * 1024

TOK_TILE = 512
NA_QROWS = 4
NA_KROWS = NA_QROWS + NA_WIN_ROWS
NA_TAB_PAD = NA_QROWS
NA_TAB_BLOCKS = NA_TAB_PAD + 2 * NA_WIN_ROWS - 1 + NA_QROWS + 1
GLA_T = 256
GLA_LEVELS = (128, 64, 32, 16)
GLA_DIAG = 16
GLA_UNROLL = 2
OUTPROJ_SUBTILES = 2
FFN_TILE = 256
COMBINE_WIN = 128
SC_GATHER_ROWS = 64


def _cparams(sem):
    return pltpu.CompilerParams(dimension_semantics=sem, vmem_limit_bytes=VMEM_LIMIT)


def _rms(v, g):
    return v * lax.rsqrt(jnp.mean(v * v, axis=-1, keepdims=True) + NORM_EPS) * g


def _dot(a, b):
    return jnp.dot(a, b, preferred_element_type=F32)


def _dot_nt(a, b):
    return lax.dot_general(a, b, (((1,), (1,)), ((), ())), preferred_element_type=F32)


def _split_bf16(v):
    hi = v.astype(BF16)
    return hi, (v - hi.astype(F32)).astype(BF16)


_HIGH_HALF = -65536


def _pack_bf16_pairs(hb):
    bits = lax.bitcast_convert_type(hb.astype(F32), jnp.int32)
    half = hb.shape[1] // 2
    return lax.shift_right_logical(bits[:, :half], 16) | (bits[:, half:] & _HIGH_HALF)


def _unpack_bf16_pairs(w):
    lo = lax.bitcast_convert_type(lax.shift_left(w, 16), F32).astype(BF16)
    hi = lax.bitcast_convert_type(w & _HIGH_HALF, F32).astype(BF16)
    return jnp.concatenate([lo, hi], axis=1)


def _mod_kernel(c_ref, w_ref, b_ref, o_ref):
    c = c_ref[...]
    s = c / (1.0 + jnp.exp(-c))
    o_ref[...] = _dot(s.astype(BF16), w_ref[...].astype(BF16)) + b_ref[...]


def _mod(c8, w_mod, b_mod):
    d, n = w_mod.shape
    tn = 1536
    return pl.pallas_call(
        _mod_kernel,
        out_shape=jax.ShapeDtypeStruct((8, n), F32),
        grid=(n // tn,),
        in_specs=[pl.BlockSpec((8, d), lambda j: (0, 0)),
                  pl.BlockSpec((d, tn), lambda j: (0, j)),
                  pl.BlockSpec((1, tn), lambda j: (0, j))],
        out_specs=pl.BlockSpec((8, tn), lambda j: (0, j)),
        compiler_params=_cparams(("arbitrary",)),
        name="mod",
    )(c8, w_mod, b_mod.reshape(1, n))


def _inproj_kernel(x_ref, sh_ref, sc_ref, g_ref, *refs, modes, scales):
    n_rope = 2 if "rope" in modes else 0
    rope_refs, refs = refs[:n_rope], refs[n_rope:]
    n_w = len(modes)
    w_refs, o_refs, wb_refs = refs[:n_w], refs[n_w:2 * n_w], refs[2 * n_w:]

    @pl.when((pl.program_id(0) == 0) & (pl.program_id(1) == 0))
    def _():
        for w_ref, wb_ref in zip(w_refs, wb_refs):
            wb_ref[...] = w_ref[...].astype(BF16)

    x = x_ref[0]
    h = _rms(x, g_ref[...]) * (1.0 + sc_ref[0]) + sh_ref[0]
    hb = h.astype(BF16)
    for w_ref, o_ref, mode, scale in zip(wb_refs, o_refs, modes, scales):
        y = _dot(hb, w_ref[...])
        if mode == "rope":
            cols = y.shape[1]
            quarter = GLA_DK // 4
            lane = lax.broadcasted_iota(jnp.int32, y.shape, 1)
            partner = jnp.where((lane & (2 * quarter - 1)) < quarter,
                                pltpu.roll(y, cols - quarter, axis=1), pltpu.roll(y, quarter, axis=1))
            y = (y * rope_refs[0][...] + partner * rope_refs[1][...]) * scale
        elif mode == "split":
            hi, lo = _split_bf16(y)
            lane = lax.broadcasted_iota(jnp.int32, y.shape, 1)
            rank2 = 2 * GLA_GATE_RANK
            y = jnp.where((lane >= rank2) & (lane < 2 * rank2), lo, hi)
        o_ref[0] = y.astype(o_ref.dtype)


def _inproj(x, shift, scale, g, weights, modes, scales, out_dtypes, rope=None):
    b, n, d = x.shape
    tm = min(TOK_TILE, n)
    per_sample = shift.shape[0] == b
    mod_map = (lambda i, j: (i, 0, 0)) if per_sample else (lambda i, j: (0, 0, 0))
    in_specs = [pl.BlockSpec((1, tm, d), lambda i, j: (i, j, 0)),
                pl.BlockSpec((1, 1, d), mod_map),
                pl.BlockSpec((1, 1, d), mod_map),
                pl.BlockSpec((1, d), lambda i, j: (0, 0))]
    args = [x, shift, scale, g.reshape(1, d)]
    if rope is not None:
        in_specs += [pl.BlockSpec((tm, rope[0].shape[1]), lambda i, j: (j, 0))] * 2
        args += list(rope)
    out_shapes, out_specs = [], []
    for w, mode, dt in zip(weights, modes, out_dtypes):
        in_specs.append(pl.BlockSpec(w.shape, lambda i, j: (0, 0), pipeline_mode=pl.Buffered(1)))
        cols = w.shape[1]
        out_shapes.append(jax.ShapeDtypeStruct((b, n, cols), dt))
        out_specs.append(pl.BlockSpec((1, tm, cols), lambda i, j: (i, j, 0)))
    return pl.pallas_call(
        functools.partial(_inproj_kernel, modes=tuple(modes), scales=tuple(scales)),
        out_shape=out_shapes,
        grid=(b, n // tm),
        in_specs=in_specs,
        out_specs=out_specs,
        scratch_shapes=[pltpu.VMEM(w.shape, BF16) for w in weights],
        compiler_params=_cparams(("arbitrary", "arbitrary")),
        name="inproj",
    )(*args, *weights)


def _rope_tables(n):
    t = np.arange(n)
    pos_row, pos_col = (t // GRID_W).astype(np.float32), (t % GRID_W).astype(np.float32)
    quarter = GLA_DK // 4
    freqs = np.float32(ROPE_BASE) ** (-np.arange(quarter, dtype=np.float32) / quarter)
    ang_r = pos_row[:, None] * freqs
    ang_c = pos_col[:, None] * freqs
    cos = np.concatenate([np.cos(ang_r), np.cos(ang_r), np.cos(ang_c), np.cos(ang_c)], axis=-1)
    sin = np.concatenate([-np.sin(ang_r), np.sin(ang_r), -np.sin(ang_c), np.sin(ang_c)], axis=-1)
    return jnp.asarray(np.tile(cos, (1, GLA_HEADS))), jnp.asarray(np.tile(sin, (1, GLA_HEADS)))


def _na_patterns(rows):
    kr = min(NA_WIN_ROWS, rows)
    n_blocks = rows // NA_QROWS
    pats = []
    for blk in (0, 1, n_blocks - 1):
        r0 = blk * NA_QROWS
        k0 = int(np.clip(r0 - kr // 2, 0, rows - NA_KROWS))
        strips = []
        for a in range(NA_QROWS):
            r_start = int(np.clip(r0 + a - kr // 2, 0, rows - kr))
            start = k0 - (r0 + a) + NA_WIN_ROWS - 1 + NA_TAB_PAD
            assert 0 <= start and start + NA_KROWS <= NA_TAB_BLOCKS
            strips.append((start, [r_start <= k0 + c < r_start + kr for c in range(NA_KROWS)]))
        pats.append(strips)
    return pats


def _na_bias_tables(rpb, rows):
    heads = rpb.shape[0]
    col = np.arange(GRID_W)
    c_start = np.clip(col - NA_WIN_COLS // 2, 0, GRID_W - NA_WIN_COLS)
    col_ok = (col[None, :] >= c_start[:, None]) & (col[None, :] < c_start[:, None] + NA_WIN_COLS)
    dc = np.clip(col[None, :] - col[:, None] + NA_WIN_COLS - 1, 0, 2 * NA_WIN_COLS - 2)
    sel_c = (np.arange(2 * NA_WIN_COLS - 1)[:, None, None] == dc[None]) & col_ok[None]
    t = jnp.einsum("hrd,dqk->hqrk", rpb, jnp.asarray(sel_c, F32), precision=HIGHEST)
    t = jnp.where(jnp.asarray(col_ok)[None, :, None, :], t, NEG_BIG)
    n_dr = 2 * NA_WIN_ROWS - 1
    t = t.reshape(heads, GRID_W, n_dr * GRID_W)
    back = NA_TAB_BLOCKS + 1 - NA_TAB_PAD - n_dr
    t = jnp.pad(t, ((0, 0), (0, 0), (NA_TAB_PAD * GRID_W, back * GRID_W)), constant_values=NEG_BIG)
    width = NA_TAB_BLOCKS * GRID_W
    tab = jnp.stack([t[:, :, :width], t[:, :, GRID_W:GRID_W + width]], axis=1)
    row_mask = np.zeros((3, NA_QROWS, 1, NA_KROWS * GRID_W), np.float32)
    for pat, strips in enumerate(_na_patterns(rows)):
        for a, (_, valid) in enumerate(strips):
            row_mask[pat, a, 0] = np.repeat(np.where(valid, 0.0, NEG_BIG), GRID_W)
    return tab, jnp.asarray(row_mask)


def _na_kernel(q_ref, k_ref, v_ref, kc_ref, vc_ref, tab_ref, rmask_ref, o_ref, bias_ref,
               sw_ref, sc_ref, pw_ref, pc_ref, *, rows):
    nq, nk = NA_QROWS * GRID_W, NA_KROWS * GRID_W
    n_blocks = rows // NA_QROWS
    kr = min(NA_WIN_ROWS, rows)
    scale = NA_HEAD_DIM ** -0.5
    lane = lax.broadcasted_iota(jnp.int32, (nq, LANES), 1)
    first_head = lane < NA_HEAD_DIM
    kc = kc_ref[0]
    vc = vc_ref[0]
    lane_w = lax.broadcasted_iota(jnp.int32, (nk, LANES), 1)
    lane_c = lax.broadcasted_iota(jnp.int32, vc.shape, 1)
    for pat, strips in enumerate(_na_patterns(rows)):
        for a, (start, _) in enumerate(strips):
            parity = start % 2
            off = (start - parity) * GRID_W
            for h in range(2):
                bias_ref[0, pat, h, a * GRID_W:(a + 1) * GRID_W, :] = (
                    tab_ref[h, parity, :, off:off + nk] + rmask_ref[pat, a])

    def key_start(i):
        return pl.multiple_of(jnp.clip(i * NA_QROWS - kr // 2, 0, rows - NA_KROWS) * GRID_W, GRID_W)

    def scores(i, slot):
        pat = jnp.where(i == 0, 0, jnp.where(i == n_blocks - 1, 2, 1))
        q = q_ref[0, pl.ds(pl.multiple_of(i * nq, nq), nq), :] * scale
        kw = k_ref[0, pl.ds(key_start(i), nk), :]
        for h in range(2):
            qh = jnp.where(first_head if h == 0 else jnp.logical_not(first_head), q, jnp.zeros_like(q))
            sw_ref[slot, h] = _dot_nt(qh, kw) + bias_ref[0, pat, h]
            sc_ref[slot, h] = _dot_nt(qh, kc)

    def softmax(slot):
        for h in range(2):
            s_w = sw_ref[slot, h]
            s_c = sc_ref[slot, h]
            m = jnp.maximum(jnp.max(s_w, axis=-1, keepdims=True), jnp.max(s_c, axis=-1, keepdims=True))
            pw_ref[slot, h] = jnp.exp((s_w - m).astype(BF16))
            pc_ref[slot, h] = jnp.exp((s_c - m).astype(BF16))

    def values(i, slot):
        vw = v_ref[0, pl.ds(key_start(i), nk), :]
        outs = []
        for h in range(2):
            sum_lane = NA_HEAD_DIM * (1 - h)
            vw_h = jnp.where(lane_w == sum_lane, jnp.ones_like(vw), vw)
            vc_h = jnp.where(lane_c == sum_lane, jnp.ones_like(vc), vc)
            o = _dot(pw_ref[slot, h], vw_h) + _dot(pc_ref[slot, h], vc_h)
            outs.append(o * (1.0 / o[:, sum_lane:sum_lane + 1]))
        o = jnp.where(first_head, outs[0], outs[1])
        o_ref[0, pl.ds(pl.multiple_of(i * nq, nq), nq), :] = o.astype(o_ref.dtype)

    assert n_blocks % 2 == 0 and n_blocks >= 4
    scores(0, 0)
    softmax(0)
    scores(1, 1)

    def trip(j, carry):
        i = 2 * j
        values(i - 2, 0)
        softmax(1)
        scores(i, 0)
        values(i - 1, 1)
        softmax(0)
        scores(i + 1, 1)
        return carry

    lax.fori_loop(1, n_blocks // 2, trip, 0)
    values(n_blocks - 2, 0)
    softmax(1)
    values(n_blocks - 1, 1)


def _na(q, k, v, kc, vc, tab, row_mask):
    b, n, w = q.shape
    n_ctx = kc.shape[1]
    pairs = w // LANES
    rows = n // GRID_W
    nq, nk = NA_QROWS * GRID_W, NA_KROWS * GRID_W
    tok = lambda i, p: (i, 0, p)
    return pl.pallas_call(
        functools.partial(_na_kernel, rows=rows),
        out_shape=jax.ShapeDtypeStruct((b, n, w), BF16),
        grid=(b, pairs),
        in_specs=[pl.BlockSpec((1, n, LANES), tok),
                  pl.BlockSpec((1, n, LANES), tok),
                  pl.BlockSpec((1, n, LANES), tok),
                  pl.BlockSpec((1, n_ctx, LANES), tok),
                  pl.BlockSpec((1, n_ctx, LANES), tok),
                  pl.BlockSpec((2,) + tab.shape[1:], lambda i, p: (p, 0, 0, 0)),
                  pl.BlockSpec(row_mask.shape, lambda i, p: (0, 0, 0, 0))],
        out_specs=pl.BlockSpec((1, n, LANES), tok),
        scratch_shapes=[pltpu.VMEM((1, 3, 2, nq, nk), F32),
                        pltpu.VMEM((2, 2, nq, nk), F32), pltpu.VMEM((2, 2, nq, n_ctx), F32),
                        pltpu.VMEM((2, 2, nq, nk), BF16), pltpu.VMEM((2, 2, nq, n_ctx), BF16)],
        compiler_params=_cparams(("arbitrary", "arbitrary")),
        name="na",
    )(q, k, v, kc, vc, tab, row_mask)


def _gla_prefix_matrices(t):
    i = np.arange(t)
    return np.stack([i[:, None] >= i[None, :], i[:, None] <= i[None, :]]).astype(np.float32)


def _gla_kernel(q_ref, k_ref, v_ref, ad_ref, g_ref, ck_ref, cv_ref, cad_ref,
                u_ref, ab_ref, gn_ref, cm_ref, o_ref, accf_ref, accb_ref, *, n_tok):
    t = GLA_T
    n_chunks = n_tok // t
    row = lax.broadcasted_iota(jnp.int32, (t, LANES), 0)
    hb = t // 2
    row_b = lax.broadcasted_iota(jnp.int32, (hb, LANES), 0)
    head0_b = lax.broadcasted_iota(jnp.int32, (hb, LANES), 1) < GLA_DK
    row2 = lax.broadcasted_iota(jnp.int32, (hb, 2 * hb), 0)
    col2 = lax.broadcasted_iota(jnp.int32, (hb, 2 * hb), 1) & (hb - 1)
    vrow = lax.broadcasted_iota(jnp.int32, (2 * t, 2 * GLA_DV), 0)
    vlane = lax.broadcasted_iota(jnp.int32, (2 * t, 2 * GLA_DV), 1)
    v_head_match = (vrow >= t) == (vlane >= GLA_DV)
    srow = lax.broadcasted_iota(jnp.int32, (2 * GLA_DV, LANES), 0)
    slane = lax.broadcasted_iota(jnp.int32, (2 * GLA_DV, LANES), 1)
    s_blockdiag = (srow >= GLA_DV) == (slane >= GLA_DK)
    blk_mask = {half: jnp.where((row2 & ~(2 * half - 1)) == (col2 & ~(2 * half - 1)), 1.0, 0.0)
                for half in GLA_LEVELS if 2 * half < hb}
    diag_blk = (row2 & ~(GLA_DIAG - 1)) == (col2 & ~(GLA_DIAG - 1))
    diag_mask = (jnp.where(diag_blk & (row2 >= col2), 1.0, 0.0), jnp.where(diag_blk & (row2 <= col2), 1.0, 0.0))

    def prefix_sums(ad, dirn):
        z = _dot(ad, u_ref[dirn]) + ab_ref[dirn]
        loga = (jnp.minimum(z, 0.0) - jnp.log(1.0 + jnp.exp(-jnp.abs(z)))) * (1.0 / GLA_GATE_TAU)
        hi, lo = _split_bf16(loga)
        p2 = _dot(cm_ref[dirn], jnp.concatenate([hi, lo], axis=-1))
        return p2[:, :LANES] + p2[:, LANES:]

    def chunk_end(p, dirn):
        return p[t - 1:t, :] if dirn == 0 else p[0:1, :]

    def level_sums(p, half, dirn):
        blk = 2 * half
        p3 = p.reshape(t // blk, blk, LANES)
        edge = half - 1 if dirn == 0 else half
        ref = jnp.broadcast_to(p3[:, edge:edge + 1, :], p3.shape).reshape(t, LANES)
        later = (row & half) != 0
        return jnp.where(later == (dirn == 0), p - ref, ref - p)

    def state_update(s, k, vt, p, dirn):
        kh = (k * jnp.exp(chunk_end(p, dirn) - p)).astype(BF16)
        return s * jnp.exp(chunk_end(p, dirn)) + jnp.where(s_blockdiag, _dot(vt, kh), 0.0)

    def chunk(tok0, s, dirn):
        q = q_ref[0, pl.ds(tok0, t), :].astype(F32)
        k = k_ref[0, pl.ds(tok0, t), :].astype(F32)
        v = v_ref[0, pl.ds(tok0, t), :]
        vt = v.T
        p = prefix_sums(ad_ref[0, pl.ds(tok0, t), :], dirn)
        qh = (q * jnp.exp(p)).astype(BF16)
        o = _dot_nt(qh, s.astype(BF16))
        def rows(x, b):
            return x[b * hb:(b + 1) * hb]

        def scores(qt, kt):
            kcat = jnp.concatenate([jnp.where(head0_b, kt, 0.0), jnp.where(head0_b, 0.0, kt)], axis=0)
            return _dot_nt(qt.astype(BF16), kcat.astype(BF16))

        assert GLA_LEVELS[0] == hb
        w = jnp.exp(level_sums(p, hb, dirn))
        qb, kb = (1, 0) if dirn == 0 else (0, 1)
        wide = scores(rows(q, qb) * rows(w, qb), rows(k, kb) * rows(w, kb))
        fine_w = [jnp.exp(level_sums(p, half, dirn)) for half in GLA_LEVELS[1:]]
        e_d = level_sums(p, GLA_DIAG // 2, dirn)
        w_d, wi_d = jnp.exp(e_d), jnp.exp(-e_d)
        fine = []
        for b in range(2):
            qs, ks = rows(q, b), rows(k, b)
            acc = None
            for half, w in zip(GLA_LEVELS[1:], fine_w):
                later = (row_b & half) != 0
                q_side = later if dirn == 0 else jnp.logical_not(later)
                part = scores(jnp.where(q_side, qs * rows(w, b), 0.0), jnp.where(q_side, 0.0, ks * rows(w, b)))
                if half in blk_mask:
                    part = part * blk_mask[half]
                acc = part if acc is None else acc + part
            later = (row_b & (GLA_DIAG // 2)) != 0
            shrink_q = later if dirn == 0 else jnp.logical_not(later)
            part = scores(qs * jnp.where(shrink_q, rows(w_d, b), rows(wi_d, b)),
                          ks * jnp.where(shrink_q, rows(wi_d, b), rows(w_d, b)))
            fine.append(acc + jnp.where(diag_mask[dirn] > 0.5, part, 0.0))
        zero = jnp.zeros((hb, hb), F32)
        h0, h1 = slice(0, hb), slice(hb, 2 * hb)
        if dirn == 0:
            top = [fine[0][:, h0], zero, fine[0][:, h1], zero]
            bot = [wide[:, h0], fine[1][:, h0], wide[:, h1], fine[1][:, h1]]
        else:
            top = [fine[0][:, h0], wide[:, h0], fine[0][:, h1], wide[:, h1]]
            bot = [zero, fine[1][:, h0], zero, fine[1][:, h1]]
        a = jnp.concatenate([jnp.concatenate(top, axis=1), jnp.concatenate(bot, axis=1)], axis=0)
        vcat = jnp.concatenate([v, v], axis=0)
        vcat = jnp.where(v_head_match, vcat, jnp.zeros_like(vcat))
        o = o + _dot(a.astype(BF16), vcat)
        return o, state_update(s, k, vt, p, dirn)

    def ctx_state(dirn):
        p = prefix_sums(cad_ref[0], dirn)
        s0 = jnp.zeros((2 * GLA_DV, LANES), F32)
        return state_update(s0, ck_ref[0].astype(F32), cv_ref[0].T, p, dirn)

    def finish(tok0, o):
        g = g_ref[0, pl.ds(tok0, t), :].astype(F32)
        gate = g / (1.0 + jnp.exp(-g))
        halves = [_rms(o[:, h * GLA_DV:(h + 1) * GLA_DV], gn_ref[...]) for h in range(2)]
        o_ref[0, pl.ds(tok0, t), :] = (jnp.concatenate(halves, axis=-1) * gate).astype(o_ref.dtype)

    def body(i, carry, second_half):
        s_f, s_b = carry
        for u in range(GLA_UNROLL):
            c = i * GLA_UNROLL + u
            tok_f = pl.multiple_of(c * t, t)
            tok_b = pl.multiple_of((n_chunks - 1 - c) * t, t)
            o_f, s_f = chunk(tok_f, s_f, 0)
            o_b, s_b = chunk(tok_b, s_b, 1)
            if second_half:
                finish(tok_f, o_f + accb_ref[pl.ds(tok_f, t), :])
                finish(tok_b, o_b + accf_ref[pl.ds(tok_b, t), :])
            else:
                accf_ref[pl.ds(tok_f, t), :] = o_f
                accb_ref[pl.ds(tok_b, t), :] = o_b
        return s_f, s_b

    trips = n_chunks // GLA_UNROLL
    assert n_chunks % (2 * GLA_UNROLL) == 0
    carry = lax.fori_loop(0, trips // 2, functools.partial(body, second_half=False), (ctx_state(0), ctx_state(1)))
    lax.fori_loop(trips // 2, trips, functools.partial(body, second_half=True), carry)


def _gla(q, k, v, ad, g, ck, cv, cad, u, abias, gnorm, cmats):
    b, n, kw = q.shape
    n_ctx = ck.shape[1]
    pairs = kw // LANES
    vw = 2 * GLA_DV
    tok = lambda i, p: (i, 0, p)
    full3 = lambda i, p: (i, 0, 0)
    return pl.pallas_call(
        functools.partial(_gla_kernel, n_tok=n),
        out_shape=jax.ShapeDtypeStruct((b, n, v.shape[2]), BF16),
        grid=(b, pairs),
        in_specs=[pl.BlockSpec((1, n, LANES), tok),
                  pl.BlockSpec((1, n, LANES), tok),
                  pl.BlockSpec((1, n, vw), tok),
                  pl.BlockSpec((1, n, LANES), full3),
                  pl.BlockSpec((1, n, vw), tok),
                  pl.BlockSpec((1, n_ctx, LANES), tok),
                  pl.BlockSpec((1, n_ctx, vw), tok),
                  pl.BlockSpec((1, n_ctx, LANES), full3),
                  pl.BlockSpec((2, LANES, LANES), lambda i, p: (0, 0, p)),
                  pl.BlockSpec((2, 1, LANES), lambda i, p: (0, 0, p)),
                  pl.BlockSpec((1, GLA_DV), lambda i, p: (0, 0)),
                  pl.BlockSpec(cmats.shape, lambda i, p: (0, 0, 0))],
        out_specs=pl.BlockSpec((1, n, vw), tok),
        scratch_shapes=[pltpu.VMEM((n, vw), F32), pltpu.VMEM((n, vw), F32)],
        compiler_params=_cparams(("arbitrary", "arbitrary")),
        name="gla",
    )(q, k, v, ad, g, ck, cv, cad, u, abias, gnorm, cmats)


def _gla_decay_up(a_up):
    r = GLA_GATE_RANK
    pad = jnp.zeros((2, 2 * r, GLA_KEY_W), F32)
    pad = pad.at[0, :r].set(a_up[0]).at[1, r:].set(a_up[1])
    hi, lo = _split_bf16(pad)
    return jnp.concatenate([hi, hi, lo, jnp.zeros_like(hi)], axis=1)


def _outproj_kernel(ona_ref, ogla_ref, x_ref, w1_ref, w2_ref, gm_ref, shf_ref, scf_ref, npost_ref,
                    nfpre_ref, rt_ref, xnew_ref, hf_ref, afft_ref):
    for r0 in range(0, x_ref.shape[1], TOK_TILE):
        rs = slice(r0, r0 + TOK_TILE)
        mix = _dot(ona_ref[0, rs], w1_ref[...]) + _dot(ogla_ref[0, rs], w2_ref[...])
        xn = x_ref[0, rs] + gm_ref[0] * _rms(mix, npost_ref[...])
        xnew_ref[0, rs] = xn
        h = _rms(xn, nfpre_ref[...]) * (1.0 + scf_ref[0]) + shf_ref[0]
        h_hi, h_lo = _split_bf16(h)
        hf_ref[0, rs] = _pack_bf16_pairs(h_hi)
        res = _dot(h_hi, rt_ref[...])
        logits = res[:, :LANES] + res[:, LANES:] + _dot(h_lo, rt_ref[:, :LANES])
        lane = lax.broadcasted_iota(jnp.int32, logits.shape, 1)
        logits = jnp.where(lane < N_EXPERTS, logits, NEG_BIG)
        p = jnp.exp(logits - jnp.max(logits, axis=-1, keepdims=True))
        aff = p / jnp.sum(p, axis=-1, keepdims=True)
        afft_ref[0, :, rs] = aff.T[:N_EXPERTS, :]


def _outproj(o_na, o_gla, x, w1, w2, gm, shf, scf, npost, nfpre, router_cat):
    b, n, d = x.shape
    tm = OUTPROJ_SUBTILES * TOK_TILE
    tokmap = lambda i, j: (i, j, 0)
    smp = lambda i, j: (i, 0, 0)
    cst = lambda i, j: (0, 0)
    return pl.pallas_call(
        _outproj_kernel,
        out_shape=[jax.ShapeDtypeStruct((b, n, d), F32),
                   jax.ShapeDtypeStruct((b, n, d // 2), jnp.int32),
                   jax.ShapeDtypeStruct((b, N_EXPERTS, n), F32)],
        grid=(b, n // tm),
        in_specs=[pl.BlockSpec((1, tm, o_na.shape[2]), tokmap),
                  pl.BlockSpec((1, tm, o_gla.shape[2]), tokmap),
                  pl.BlockSpec((1, tm, d), tokmap),
                  pl.BlockSpec(w1.shape, cst),
                  pl.BlockSpec(w2.shape, cst),
                  pl.BlockSpec((1, 1, d), smp),
                  pl.BlockSpec((1, 1, d), smp),
                  pl.BlockSpec((1, 1, d), smp),
                  pl.BlockSpec((1, d), cst),
                  pl.BlockSpec((1, d), cst),
                  pl.BlockSpec(router_cat.shape, cst)],
        out_specs=[pl.BlockSpec((1, tm, d), tokmap),
                   pl.BlockSpec((1, tm, d // 2), tokmap),
                   pl.BlockSpec((1, N_EXPERTS, tm), lambda i, j: (i, 0, j))],
        compiler_params=_cparams(("arbitrary", "arbitrary")),
        name="outproj",
    )(o_na, o_gla, x, w1, w2, gm, shf, scf, npost.reshape(1, d), nfpre.reshape(1, d), router_cat)


def _route_kernel(afft_ref, rt_ref, *, cap):
    a = afft_ref[0]
    e, n = a.shape
    capf = jnp.float32(cap)

    def search(i, thr_bits):
        cand = thr_bits | lax.shift_left(jnp.int32(1), 30 - i)
        cnt = jnp.sum(jnp.where(a >= lax.bitcast_convert_type(cand, F32), 1.0, 0.0), axis=-1, keepdims=True)
        return jnp.where(cnt >= capf, cand, thr_bits)

    thr_bits = lax.fori_loop(0, 31, search, jnp.zeros((e, 1), jnp.int32))
    thr = lax.bitcast_convert_type(thr_bits, F32)
    need = capf - jnp.sum(jnp.where(a > thr, 1.0, 0.0), axis=-1, keepdims=True)
    r_i = lax.broadcasted_iota(jnp.int32, (LANES, LANES), 0)
    c_i = lax.broadcasted_iota(jnp.int32, (LANES, LANES), 1)
    incl = jnp.where(r_i <= c_i, 1.0, 0.0).astype(BF16)
    off_eq = jnp.zeros((e, 1), F32)
    off_sel = jnp.zeros((e, 1), F32)
    for j in range(n // LANES):
        sl = slice(j * LANES, (j + 1) * LANES)
        a_b = a[:, sl]
        eq_b = jnp.where(a_b == thr, 1.0, 0.0)
        tie_rank = _dot(eq_b.astype(BF16), incl) - eq_b + off_eq
        off_eq = off_eq + jnp.sum(eq_b, axis=-1, keepdims=True)
        sel_b = jnp.where(a_b > thr, 1.0, jnp.where(tie_rank < need, eq_b, 0.0))
        sel = sel_b > 0.5
        rank = _dot(sel_b.astype(BF16), incl) - sel_b + off_sel
        off_sel = off_sel + jnp.sum(sel_b, axis=-1, keepdims=True)
        rsel = jnp.where(sel, rank, -1.0)
        rt_ref[0, :, sl] = rsel.astype(jnp.int32)


def _route(afft, cap):
    b, e, n = afft.shape
    return pl.pallas_call(
        functools.partial(_route_kernel, cap=cap),
        out_shape=jax.ShapeDtypeStruct((b, e, n), jnp.int32),
        grid=(b,),
        in_specs=[pl.BlockSpec((1, e, n), lambda i: (i, 0, 0))],
        out_specs=pl.BlockSpec((1, e, n), lambda i: (i, 0, 0)),
        compiler_params=_cparams(("arbitrary",)),
        name="route",
    )(afft)


def _sc_gather(rsel_t, hf2, cap):
    b, e, n = rsel_t.shape
    width = hf2.shape[1]
    info = plsc.get_sparse_core_info()
    nc, lanes = info.num_cores, info.num_lanes
    workers = nc * info.num_subcores
    items = b * e
    assert items % workers == 0 and n % lanes == 0 and cap % SC_GATHER_ROWS == 0
    per_worker = items // workers
    mesh = plsc.VectorSubcoreMesh(core_axis_name="c", subcore_axis_name="s")

    def body(rank_hbm, hf_hbm, out_hbm, rank_v, idx_v, rows_a, rows_b, sem_a, sem_b):
        bufs = ((rows_a, sem_a), (rows_b, sem_b))
        wid = lax.axis_index("s") * nc + lax.axis_index("c")
        for k in range(per_worker):
            item = wid * per_worker + k
            base_tok = (item // e) * n
            pltpu.sync_copy(rank_hbm.at[item], rank_v)

            @pl.loop(0, n // lanes)
            def _(j):
                r = rank_v[pl.ds(j * lanes, lanes)]
                tok = lax.iota(jnp.int32, lanes) + (j * lanes + base_tok)
                plsc.store_scatter(idx_v, [r], tok, mask=r >= 0)

            def gather(c):
                buf, sem = bufs[c % 2]
                rows = pl.ds(c * SC_GATHER_ROWS, SC_GATHER_ROWS)
                return pltpu.async_copy(hf_hbm.at[idx_v.at[rows]], buf, sem)

            n_chunks = cap // SC_GATHER_ROWS
            pending = gather(0)
            for c in range(n_chunks):
                nxt = gather(c + 1) if c + 1 < n_chunks else None
                pending.wait()
                pltpu.sync_copy(bufs[c % 2][0],
                                out_hbm.at[pl.ds(item * cap + c * SC_GATHER_ROWS, SC_GATHER_ROWS)])
                pending = nxt

    return pl.kernel(
        body, out_type=jax.ShapeDtypeStruct((items * cap, width), hf2.dtype), mesh=mesh,
        scratch_types=[pltpu.VMEM((n,), jnp.int32), pltpu.VMEM((cap,), jnp.int32),
                       pltpu.VMEM((SC_GATHER_ROWS, width), hf2.dtype),
                       pltpu.VMEM((SC_GATHER_ROWS, width), hf2.dtype),
                       pltpu.SemaphoreType.DMA, pltpu.SemaphoreType.DMA],
        compiler_params=pltpu.CompilerParams(needs_layout_passes=False),
        name="scgather",
    )(rsel_t.reshape(items, n), hf2)


def _ffn_kernel(x_ref, wg_ref, wu_ref, wd_ref, o_ref, acc_ref, xb_ref):
    f = pl.program_id(1)
    b = x_ref.shape[0]
    last = pl.num_programs(1) - 1

    def tile(first, final):
        wg = wg_ref[0].astype(BF16)
        wu = wu_ref[0].astype(BF16)
        wd = wd_ref[0].astype(BF16)
        for i in range(b):
            if first:
                xb_ref[i] = _unpack_bf16_pairs(x_ref[i, 0])
            x = xb_ref[i]
            g = _dot(x, wg)
            u = _dot(x, wu)
            hid = (g / (1.0 + jnp.exp(-g)) * u).astype(BF16)
            y = _dot(hid, wd)
            if not first:
                y = acc_ref[i] + y
            if final:
                o_ref[i, 0] = y.astype(o_ref.dtype)
            else:
                acc_ref[i] = y

    @pl.when(f == 0)
    def _():
        tile(True, False)

    @pl.when((f > 0) & (f < last))
    def _():
        tile(False, False)

    @pl.when(f == last)
    def _():
        tile(False, True)


def _ffn(xs, w_gate, w_up, w_down):
    b, e, cap, dp = xs.shape
    d = 2 * dp
    dff = w_gate.shape[2]
    tf = FFN_TILE
    assert dff // tf >= 2
    return pl.pallas_call(
        _ffn_kernel,
        out_shape=jax.ShapeDtypeStruct((b, e, cap, d), BF16),
        grid=(e, dff // tf),
        in_specs=[pl.BlockSpec((b, 1, cap, dp), lambda i, f: (0, i, 0, 0)),
                  pl.BlockSpec((1, d, tf), lambda i, f: (i, 0, f)),
                  pl.BlockSpec((1, d, tf), lambda i, f: (i, 0, f)),
                  pl.BlockSpec((1, tf, d), lambda i, f: (i, f, 0))],
        out_specs=pl.BlockSpec((b, 1, cap, d), lambda i, f: (0, i, 0, 0)),
        scratch_shapes=[pltpu.VMEM((b, cap, d), F32), pltpu.VMEM((b, cap, d), BF16)],
        compiler_params=_cparams(("arbitrary", "arbitrary")),
        name="ffn",
    )(xs, w_gate, w_up, w_down)


def _combine_kernel(off_ref, ys_ref, rt_ref, afft_ref, xn_ref, gf_ref, npost_ref, o_ref, acc_ref):
    bi, tt = pl.program_id(0), pl.program_id(1)
    tm = rt_ref.shape[2]
    n_experts, cap = ys_ref.shape[1], ys_ref.shape[2]
    blocks = tm // LANES
    slot = lax.broadcasted_iota(jnp.int32, (COMBINE_WIN, tm), 0)

    def window(e, w0, j):
        nominal = w0 + j * COMBINE_WIN
        start = pl.multiple_of(jnp.minimum(nominal, cap - COMBINE_WIN), BF16_ROWS)
        rank = rt_ref[0, e:e + 1, :]
        hit = ((rank - start) == slot) & (rank >= nominal)
        weights = jnp.where(hit, afft_ref[0, e:e + 1, :], 0.0).astype(BF16)
        return weights, ys_ref[0, e, pl.ds(start, COMBINE_WIN), :]

    first, extra = [], []
    for e in range(n_experts):
        r0 = off_ref[bi, e, tt * blocks]
        r1 = off_ref[bi, e, (tt + 1) * blocks]
        w0 = (r0 // BF16_ROWS) * BF16_ROWS
        first.append(w0)
        extra.append(jnp.maximum((r1 - w0 + COMBINE_WIN - 1) // COMBINE_WIN - 1, 0))
    terms = []
    for e in range(0, n_experts, 2):
        (wa, ya), (wb, yb) = window(e, first[e], 0), window(e + 1, first[e + 1], 0)
        terms.append(_dot(jnp.concatenate([wa, wb], axis=0).T, jnp.concatenate([ya, yb], axis=0)))
    acc_ref[...] = functools.reduce(lambda a, c: a + c, terms)

    @pl.when(functools.reduce(lambda a, c: a + c, extra) > 0)
    def _():
        for e in range(n_experts):
            def more(j, carry, e=e):
                w, y = window(e, first[e], j)
                acc_ref[...] += _dot(w.T, y)
                return carry
            lax.fori_loop(1, extra[e] + 1, more, 0)

    o_ref[0] = xn_ref[0] + gf_ref[0] * _rms(acc_ref[...], npost_ref[...])


def _combine(ys, slot_off, rsel_t, aff_t, x_new, gf, npost):
    b, e, cap, d = ys.shape
    n = x_new.shape[1]
    tm = TOK_TILE
    tokmap = lambda i, j, off: (i, j, 0)
    return pl.pallas_call(
        _combine_kernel,
        out_shape=jax.ShapeDtypeStruct((b, n, d), F32),
        grid_spec=pltpu.PrefetchScalarGridSpec(
            num_scalar_prefetch=1,
            grid=(b, n // tm),
            in_specs=[pl.BlockSpec((1, e, cap, d), lambda i, j, off: (i, 0, 0, 0)),
                      pl.BlockSpec((1, e, tm), lambda i, j, off: (i, 0, j)),
                      pl.BlockSpec((1, e, tm), lambda i, j, off: (i, 0, j)),
                      pl.BlockSpec((1, tm, d), tokmap),
                      pl.BlockSpec((1, 1, d), lambda i, j, off: (i, 0, 0)),
                      pl.BlockSpec((1, d), lambda i, j, off: (0, 0))],
            out_specs=pl.BlockSpec((1, tm, d), tokmap),
            scratch_shapes=[pltpu.VMEM((tm, d), F32)]),
        compiler_params=_cparams(("arbitrary", "arbitrary")),
        name="combine",
    )(slot_off, ys, rsel_t, aff_t, x_new, gf, npost.reshape(1, d))


def kernel(x, c, ctx, c_ctx, w_mod, b_mod, norm_mix_pre, norm_mix_post, norm_ffn_pre, norm_ffn_post,
           w_in, na_rpb, gla_a_up, gla_a_bias, gla_norm, w_out, router, w_gate, w_up, w_down):
    b, n, d = x.shape
    assert w_mod.shape[0] == 1 and d == D_MODEL and n % (GRID_W * NA_QROWS) == 0 and n % GLA_T == 0
    assert ctx.shape[1] == GLA_T
    rows = n // GRID_W
    cap = EC_CAPACITY_FACTOR * n // N_EXPERTS

    c8 = jnp.concatenate([c, c_ctx[None, :], jnp.zeros((8 - b - 1, d), F32)], axis=0)
    mod = _mod(c8, w_mod[0], b_mod[0])
    sh_m, sc_m, g_m, sh_f, sc_f, g_f = [m[:b, None, :] for m in jnp.split(mod, 6, axis=-1)]
    sh_c, sc_c = mod[b:b + 1, None, :d], mod[b:b + 1, None, d:2 * d]

    wb = w_in[0]
    cuts = np.cumsum([0, NA_W, NA_W, GLA_KEY_W, GLA_VAL_W, 2 * GLA_GATE_RANK, NA_W, GLA_KEY_W, GLA_VAL_W])
    w_nak, w_nav, w_gk, w_gv, w_ad, w_naq, w_gq, w_gg = [wb[:, cuts[i]:cuts[i + 1]] for i in range(8)]
    w_ad3 = jnp.concatenate([w_ad, w_ad, w_ad, jnp.zeros_like(w_ad)], axis=1)
    rope = _rope_tables(n)
    na_q, na_k, na_v, gq, gk, gv, ad, gg = _inproj(
        x, sh_m, sc_m, norm_mix_pre[0],
        [w_naq, w_nak, w_nav, w_gq, w_gk, w_gv, w_ad3, w_gg],
        ["plain", "plain", "plain", "rope", "rope", "plain", "split", "plain"],
        [1.0, 1.0, 1.0, GLA_DK ** -0.5, 1.0, 1.0, 1.0, 1.0],
        [BF16] * 8, rope=rope)
    c_nak, c_nav, c_gk, c_gv, c_ad = _inproj(
        ctx, sh_c, sc_c, norm_mix_pre[0],
        [w_nak, w_nav, w_gk, w_gv, w_ad3],
        ["plain", "plain", "plain", "plain", "split"], [1.0] * 5, [BF16] * 5)

    o_na = _na(na_q, na_k, na_v, c_nak, c_nav, *_na_bias_tables(na_rpb[0], rows))

    cmats = jnp.asarray(_gla_prefix_matrices(GLA_T), BF16)
    o_gla = _gla(gq, gk, gv, ad, gg, c_gk, c_gv, c_ad, _gla_decay_up(gla_a_up[0]),
                 gla_a_bias[0][:, None, :], gla_norm[0][None, :], cmats)

    wo = w_out[0].astype(BF16)
    router_pad = jnp.zeros((d, LANES), F32).at[:, :N_EXPERTS].set(router[0])
    x_new, hf, aff_t = _outproj(o_na, o_gla, x, wo[:NA_W], wo[NA_W:], g_m, sh_f, sc_f,
                                     norm_mix_post[0], norm_ffn_pre[0],
                                     jnp.concatenate(_split_bf16(router_pad), axis=1))

    rsel_t = _route(aff_t, cap)
    xs = _sc_gather(rsel_t, hf.reshape(b * n, d // 2), cap).reshape(b, N_EXPERTS, cap, d // 2)
    ys = _ffn(xs, w_gate[0], w_up[0], w_down[0])
    picked = (rsel_t >= 0).astype(jnp.int32).reshape(b, N_EXPERTS, n // LANES, LANES).sum(axis=-1)
    slot_off = jnp.concatenate([jnp.zeros((b, N_EXPERTS, 1), jnp.int32), jnp.cumsum(picked, axis=-1)], axis=-1)
    return _combine(ys, slot_off, rsel_t, aff_t, x_new, g_f, norm_ffn_post[0])
```

```python
import functools

import numpy as np
import jax
import jax.numpy as jnp
from jax import lax
from jax.experimental import pallas as pl
from jax.experimental.pallas import tpu as pltpu
from jax.experimental.pallas import tpu_sc as plsc

F32 = jnp.float32
BF16 = jnp.bfloat16
HIGHEST = lax.Precision.HIGHEST

D_MODEL = 1024
GRID_W = 64
NA_W = 512
NA_HEADS = 8
NA_HEAD_DIM = 64
NA_WIN_ROWS = 8
NA_WIN_COLS = 16
GLA_HEADS = 4
GLA_DV = 128
GLA_DK = 64
GLA_KEY_W = 256
GLA_VAL_W = 512
GLA_GATE_RANK = 16
GLA_GATE_TAU = 16.0
ROPE_BASE = 10000.0
N_EXPERTS = 16
EC_CAPACITY_FACTOR = 2
NORM_EPS = 1e-6
NEG_BIG = -1e30

LANES = 128
BF16_ROWS = 16
VMEM_LIMIT = 56 * 1024 * 1024

TOK_TILE = 512
NA_QROWS = 4
NA_KROWS = NA_QROWS + NA_WIN_ROWS
NA_TAB_PAD = NA_QROWS
NA_TAB_BLOCKS = NA_TAB_PAD + 2 * NA_WIN_ROWS - 1 + NA_QROWS + 1
GLA_T = 256
GLA_LEVELS = (128, 64, 32, 16)
GLA_DIAG = 16
GLA_UNROLL = 2
OUTPROJ_SUBTILES = 2
FFN_TILE = 256
COMBINE_TILE = 256
COMBINE_WIN = 64
COMBINE_STACK = 4
SC_GATHER_ROWS = 64


def _cparams(sem):
    return pltpu.CompilerParams(dimension_semantics=sem, vmem_limit_bytes=VMEM_LIMIT)


def _rms(v, g):
    return v * lax.rsqrt(jnp.mean(v * v, axis=-1, keepdims=True) + NORM_EPS) * g


def _dot(a, b):
    return jnp.dot(a, b, preferred_element_type=F32)


def _dot_nt(a, b):
    return lax.dot_general(a, b, (((1,), (1,)), ((), ())), preferred_element_type=F32)


def _split_bf16(v):
    hi = v.astype(BF16)
    return hi, (v - hi.astype(F32)).astype(BF16)


_HIGH_HALF = -65536


def _pack_bf16_pairs(hb):
    bits = lax.bitcast_convert_type(hb.astype(F32), jnp.int32)
    half = hb.shape[1] // 2
    return lax.shift_right_logical(bits[:, :half], 16) | (bits[:, half:] & _HIGH_HALF)


def _unpack_bf16_pairs(w):
    lo = lax.bitcast_convert_type(lax.shift_left(w, 16), F32).astype(BF16)
    hi = lax.bitcast_convert_type(w & _HIGH_HALF, F32).astype(BF16)
    return jnp.concatenate([lo, hi], axis=1)


def _mod_kernel(c_ref, w_ref, b_ref, o_ref):
    c = c_ref[...]
    s = c / (1.0 + jnp.exp(-c))
    o_ref[...] = _dot(s.astype(BF16), w_ref[...].astype(BF16)) + b_ref[...]


def _mod(c8, w_mod, b_mod):
    d, n = w_mod.shape
    tn = 1536
    return pl.pallas_call(
        _mod_kernel,
        out_shape=jax.ShapeDtypeStruct((8, n), F32),
        grid=(n // tn,),
        in_specs=[pl.BlockSpec((8, d), lambda j: (0, 0)),
                  pl.BlockSpec((d, tn), lambda j: (0, j)),
                  pl.BlockSpec((1, tn), lambda j: (0, j))],
        out_specs=pl.BlockSpec((8, tn), lambda j: (0, j)),
        compiler_params=_cparams(("arbitrary",)),
        name="mod",
    )(c8, w_mod, b_mod.reshape(1, n))


def _inproj_kernel(x_ref, sh_ref, sc_ref, g_ref, *refs, modes, scales):
    n_rope = 2 if "rope" in modes else 0
    rope_refs, refs = refs[:n_rope], refs[n_rope:]
    n_w = len(modes)
    w_refs, o_refs, wb_refs = refs[:n_w], refs[n_w:2 * n_w], refs[2 * n_w:]

    @pl.when((pl.program_id(0) == 0) & (pl.program_id(1) == 0))
    def _():
        for w_ref, wb_ref in zip(w_refs, wb_refs):
            wb_ref[...] = w_ref[...].astype(BF16)

    x = x_ref[0]
    h = _rms(x, g_ref[...]) * (1.0 + sc_ref[0]) + sh_ref[0]
    hb = h.astype(BF16)
    for w_ref, o_ref, mode, scale in zip(wb_refs, o_refs, modes, scales):
        y = _dot(hb, w_ref[...])
        if mode == "rope":
            cols = y.shape[1]
            quarter = GLA_DK // 4
            lane = lax.broadcasted_iota(jnp.int32, y.shape, 1)
            partner = jnp.where((lane & (2 * quarter - 1)) < quarter,
                                pltpu.roll(y, cols - quarter, axis=1), pltpu.roll(y, quarter, axis=1))
            y = (y * rope_refs[0][...] + partner * rope_refs[1][...]) * scale
        elif mode == "split":
            hi, lo = _split_bf16(y)
            lane = lax.broadcasted_iota(jnp.int32, y.shape, 1)
            rank2 = 2 * GLA_GATE_RANK
            y = jnp.where((lane >= rank2) & (lane < 2 * rank2), lo, hi)
        o_ref[0] = y.astype(o_ref.dtype)


def _inproj(x, shift, scale, g, weights, modes, scales, out_dtypes, rope=None):
    b, n, d = x.shape
    tm = min(TOK_TILE, n)
    per_sample = shift.shape[0] == b
    mod_map = (lambda i, j: (i, 0, 0)) if per_sample else (lambda i, j: (0, 0, 0))
    in_specs = [pl.BlockSpec((1, tm, d), lambda i, j: (i, j, 0)),
                pl.BlockSpec((1, 1, d), mod_map),
                pl.BlockSpec((1, 1, d), mod_map),
                pl.BlockSpec((1, d), lambda i, j: (0, 0))]
    args = [x, shift, scale, g.reshape(1, d)]
    if rope is not None:
        in_specs += [pl.BlockSpec((tm, rope[0].shape[1]), lambda i, j: (j, 0))] * 2
        args += list(rope)
    out_shapes, out_specs = [], []
    for w, mode, dt in zip(weights, modes, out_dtypes):
        in_specs.append(pl.BlockSpec(w.shape, lambda i, j: (0, 0), pipeline_mode=pl.Buffered(1)))
        cols = w.shape[1]
        out_shapes.append(jax.ShapeDtypeStruct((b, n, cols), dt))
        out_specs.append(pl.BlockSpec((1, tm, cols), lambda i, j: (i, j, 0)))
    return pl.pallas_call(
        functools.partial(_inproj_kernel, modes=tuple(modes), scales=tuple(scales)),
        out_shape=out_shapes,
        grid=(b, n // tm),
        in_specs=in_specs,
        out_specs=out_specs,
        scratch_shapes=[pltpu.VMEM(w.shape, BF16) for w in weights],
        compiler_params=_cparams(("arbitrary", "arbitrary")),
        name="inproj",
    )(*args, *weights)


def _rope_tables(n):
    t = np.arange(n)
    pos_row, pos_col = (t // GRID_W).astype(np.float32), (t % GRID_W).astype(np.float32)
    quarter = GLA_DK // 4
    freqs = np.float32(ROPE_BASE) ** (-np.arange(quarter, dtype=np.float32) / quarter)
    ang_r = pos_row[:, None] * freqs
    ang_c = pos_col[:, None] * freqs
    cos = np.concatenate([np.cos(ang_r), np.cos(ang_r), np.cos(ang_c), np.cos(ang_c)], axis=-1)
    sin = np.concatenate([-np.sin(ang_r), np.sin(ang_r), -np.sin(ang_c), np.sin(ang_c)], axis=-1)
    return jnp.asarray(np.tile(cos, (1, GLA_HEADS))), jnp.asarray(np.tile(sin, (1, GLA_HEADS)))


def _na_patterns(rows):
    kr = min(NA_WIN_ROWS, rows)
    n_blocks = rows // NA_QROWS
    pats = []
    for blk in (0, 1, n_blocks - 1):
        r0 = blk * NA_QROWS
        k0 = int(np.clip(r0 - kr // 2, 0, rows - NA_KROWS))
        strips = []
        for a in range(NA_QROWS):
            r_start = int(np.clip(r0 + a - kr // 2, 0, rows - kr))
            start = k0 - (r0 + a) + NA_WIN_ROWS - 1 + NA_TAB_PAD
            assert 0 <= start and start + NA_KROWS <= NA_TAB_BLOCKS
            strips.append((start, [r_start <= k0 + c < r_start + kr for c in range(NA_KROWS)]))
        pats.append(strips)
    return pats


def _na_bias_tables(rpb, rows):
    heads = rpb.shape[0]
    col = np.arange(GRID_W)
    c_start = np.clip(col - NA_WIN_COLS // 2, 0, GRID_W - NA_WIN_COLS)
    col_ok = (col[None, :] >= c_start[:, None]) & (col[None, :] < c_start[:, None] + NA_WIN_COLS)
    dc = np.clip(col[None, :] - col[:, None] + NA_WIN_COLS - 1, 0, 2 * NA_WIN_COLS - 2)
    sel_c = (np.arange(2 * NA_WIN_COLS - 1)[:, None, None] == dc[None]) & col_ok[None]
    t = jnp.einsum("hrd,dqk->hqrk", rpb, jnp.asarray(sel_c, F32), precision=HIGHEST)
    t = jnp.where(jnp.asarray(col_ok)[None, :, None, :], t, NEG_BIG)
    n_dr = 2 * NA_WIN_ROWS - 1
    t = t.reshape(heads, GRID_W, n_dr * GRID_W)
    back = NA_TAB_BLOCKS + 1 - NA_TAB_PAD - n_dr
    t = jnp.pad(t, ((0, 0), (0, 0), (NA_TAB_PAD * GRID_W, back * GRID_W)), constant_values=NEG_BIG)
    width = NA_TAB_BLOCKS * GRID_W
    tab = jnp.stack([t[:, :, :width], t[:, :, GRID_W:GRID_W + width]], axis=1)
    row_mask = np.zeros((3, NA_QROWS, 1, NA_KROWS * GRID_W), np.float32)
    for pat, strips in enumerate(_na_patterns(rows)):
        for a, (_, valid) in enumerate(strips):
            row_mask[pat, a, 0] = np.repeat(np.where(valid, 0.0, NEG_BIG), GRID_W)
    return tab, jnp.asarray(row_mask)


def _na_kernel(q_ref, k_ref, v_ref, kc_ref, vc_ref, tab_ref, rmask_ref, o_ref, bias_ref,
               sw_ref, sc_ref, pw_ref, pc_ref, *, rows):
    nq, nk = NA_QROWS * GRID_W, NA_KROWS * GRID_W
    n_blocks = rows // NA_QROWS
    kr = min(NA_WIN_ROWS, rows)
    scale = NA_HEAD_DIM ** -0.5
    lane = lax.broadcasted_iota(jnp.int32, (nq, LANES), 1)
    first_head = lane < NA_HEAD_DIM
    kc = kc_ref[0]
    vc = vc_ref[0]
    lane_w = lax.broadcasted_iota(jnp.int32, (nk, LANES), 1)
    lane_c = lax.broadcasted_iota(jnp.int32, vc.shape, 1)
    for pat, strips in enumerate(_na_patterns(rows)):
        for a, (start, _) in enumerate(strips):
            parity = start % 2
            off = (start - parity) * GRID_W
            for h in range(2):
                bias_ref[0, pat, h, a * GRID_W:(a + 1) * GRID_W, :] = (
                    tab_ref[h, parity, :, off:off + nk] + rmask_ref[pat, a])

    def key_start(i):
        return pl.multiple_of(jnp.clip(i * NA_QROWS - kr // 2, 0, rows - NA_KROWS) * GRID_W, GRID_W)

    def scores(i, slot):
        pat = jnp.where(i == 0, 0, jnp.where(i == n_blocks - 1, 2, 1))
        q = q_ref[0, pl.ds(pl.multiple_of(i * nq, nq), nq), :] * scale
        kw = k_ref[0, pl.ds(key_start(i), nk), :]
        for h in range(2):
            qh = jnp.where(first_head if h == 0 else jnp.logical_not(first_head), q, jnp.zeros_like(q))
            sw_ref[slot, h] = _dot_nt(qh, kw) + bias_ref[0, pat, h]
            sc_ref[slot, h] = _dot_nt(qh, kc)

    def softmax(slot):
        for h in range(2):
            s_w = sw_ref[slot, h]
            s_c = sc_ref[slot, h]
            m = jnp.maximum(jnp.max(s_w, axis=-1, keepdims=True), jnp.max(s_c, axis=-1, keepdims=True))
            pw_ref[slot, h] = jnp.exp((s_w - m).astype(BF16))
            pc_ref[slot, h] = jnp.exp((s_c - m).astype(BF16))

    def values(i, slot):
        vw = v_ref[0, pl.ds(key_start(i), nk), :]
        outs = []
        for h in range(2):
            sum_lane = NA_HEAD_DIM * (1 - h)
            vw_h = jnp.where(lane_w == sum_lane, jnp.ones_like(vw), vw)
            vc_h = jnp.where(lane_c == sum_lane, jnp.ones_like(vc), vc)
            o = _dot(pw_ref[slot, h], vw_h) + _dot(pc_ref[slot, h], vc_h)
            outs.append(o * (1.0 / o[:, sum_lane:sum_lane + 1]))
        o = jnp.where(first_head, outs[0], outs[1])
        o_ref[0, pl.ds(pl.multiple_of(i * nq, nq), nq), :] = o.astype(o_ref.dtype)

    assert n_blocks % 2 == 0 and n_blocks >= 4
    scores(0, 0)
    softmax(0)
    scores(1, 1)

    def trip(j, carry):
        i = 2 * j
        values(i - 2, 0)
        softmax(1)
        scores(i, 0)
        values(i - 1, 1)
        softmax(0)
        scores(i + 1, 1)
        return carry

    lax.fori_loop(1, n_blocks // 2, trip, 0)
    values(n_blocks - 2, 0)
    softmax(1)
    values(n_blocks - 1, 1)


def _na(q, k, v, kc, vc, tab, row_mask):
    b, n, w = q.shape
    n_ctx = kc.shape[1]
    pairs = w // LANES
    rows = n // GRID_W
    nq, nk = NA_QROWS * GRID_W, NA_KROWS * GRID_W
    tok = lambda i, p: (i, 0, p)
    return pl.pallas_call(
        functools.partial(_na_kernel, rows=rows),
        out_shape=jax.ShapeDtypeStruct((b, n, w), BF16),
        grid=(b, pairs),
        in_specs=[pl.BlockSpec((1, n, LANES), tok),
                  pl.BlockSpec((1, n, LANES), tok),
                  pl.BlockSpec((1, n, LANES), tok),
                  pl.BlockSpec((1, n_ctx, LANES), tok),
                  pl.BlockSpec((1, n_ctx, LANES), tok),
                  pl.BlockSpec((2,) + tab.shape[1:], lambda i, p: (p, 0, 0, 0)),
                  pl.BlockSpec(row_mask.shape, lambda i, p: (0, 0, 0, 0))],
        out_specs=pl.BlockSpec((1, n, LANES), tok),
        scratch_shapes=[pltpu.VMEM((1, 3, 2, nq, nk), F32),
                        pltpu.VMEM((2, 2, nq, nk), F32), pltpu.VMEM((2, 2, nq, n_ctx), F32),
                        pltpu.VMEM((2, 2, nq, nk), BF16), pltpu.VMEM((2, 2, nq, n_ctx), BF16)],
        compiler_params=_cparams(("arbitrary", "arbitrary")),
        name="na",
    )(q, k, v, kc, vc, tab, row_mask)


def _gla_prefix_matrices(t):
    i = np.arange(t)
    return np.stack([i[:, None] >= i[None, :], i[:, None] <= i[None, :]]).astype(np.float32)


def _gla_kernel(q_ref, k_ref, v_ref, ad_ref, g_ref, ck_ref, cv_ref, cad_ref,
                u_ref, ab_ref, gn_ref, cm_ref, o_ref, accf_ref, accb_ref, *, n_tok):
    t = GLA_T
    n_chunks = n_tok // t
    row = lax.broadcasted_iota(jnp.int32, (t, LANES), 0)
    hb = t // 2
    row_b = lax.broadcasted_iota(jnp.int32, (hb, LANES), 0)
    head0_b = lax.broadcasted_iota(jnp.int32, (hb, LANES), 1) < GLA_DK
    row2 = lax.broadcasted_iota(jnp.int32, (hb, 2 * hb), 0)
    col2 = lax.broadcasted_iota(jnp.int32, (hb, 2 * hb), 1) & (hb - 1)
    vrow = lax.broadcasted_iota(jnp.int32, (2 * t, 2 * GLA_DV), 0)
    vlane = lax.broadcasted_iota(jnp.int32, (2 * t, 2 * GLA_DV), 1)
    v_head_match = (vrow >= t) == (vlane >= GLA_DV)
    srow = lax.broadcasted_iota(jnp.int32, (2 * GLA_DV, LANES), 0)
    slane = lax.broadcasted_iota(jnp.int32, (2 * GLA_DV, LANES), 1)
    s_blockdiag = (srow >= GLA_DV) == (slane >= GLA_DK)
    blk_mask = {half: jnp.where((row2 & ~(2 * half - 1)) == (col2 & ~(2 * half - 1)), 1.0, 0.0)
                for half in GLA_LEVELS if 2 * half < hb}
    diag_blk = (row2 & ~(GLA_DIAG - 1)) == (col2 & ~(GLA_DIAG - 1))
    diag_mask = (jnp.where(diag_blk & (row2 >= col2), 1.0, 0.0), jnp.where(diag_blk & (row2 <= col2), 1.0, 0.0))

    def prefix_sums(ad, dirn):
        z = _dot(ad, u_ref[dirn]) + ab_ref[dirn]
        loga = (jnp.minimum(z, 0.0) - jnp.log(1.0 + jnp.exp(-jnp.abs(z)))) * (1.0 / GLA_GATE_TAU)
        hi, lo = _split_bf16(loga)
        p2 = _dot(cm_ref[dirn], jnp.concatenate([hi, lo], axis=-1))
        return p2[:, :LANES] + p2[:, LANES:]

    def chunk_end(p, dirn):
        return p[t - 1:t, :] if dirn == 0 else p[0:1, :]

    def level_sums(p, half, dirn):
        blk = 2 * half
        p3 = p.reshape(t // blk, blk, LANES)
        edge = half - 1 if dirn == 0 else half
        ref = jnp.broadcast_to(p3[:, edge:edge + 1, :], p3.shape).reshape(t, LANES)
        later = (row & half) != 0
        return jnp.where(later == (dirn == 0), p - ref, ref - p)

    def state_update(s, k, vt, p, dirn):
        kh = (k * jnp.exp(chunk_end(p, dirn) - p)).astype(BF16)
        return s * jnp.exp(chunk_end(p, dirn)) + jnp.where(s_blockdiag, _dot(vt, kh), 0.0)

    def chunk(tok0, s, dirn):
        q = q_ref[0, pl.ds(tok0, t), :].astype(F32)
        k = k_ref[0, pl.ds(tok0, t), :].astype(F32)
        v = v_ref[0, pl.ds(tok0, t), :]
        vt = v.T
        p = prefix_sums(ad_ref[0, pl.ds(tok0, t), :], dirn)
        qh = (q * jnp.exp(p)).astype(BF16)
        o = _dot_nt(qh, s.astype(BF16))
        def rows(x, b):
            return x[b * hb:(b + 1) * hb]

        def scores(qt, kt):
            kcat = jnp.concatenate([jnp.where(head0_b, kt, 0.0), jnp.where(head0_b, 0.0, kt)], axis=0)
            return _dot_nt(qt.astype(BF16), kcat.astype(BF16))

        assert GLA_LEVELS[0] == hb
        w = jnp.exp(level_sums(p, hb, dirn))
        qb, kb = (1, 0) if dirn == 0 else (0, 1)
        wide = scores(rows(q, qb) * rows(w, qb), rows(k, kb) * rows(w, kb))
        fine_w = [jnp.exp(level_sums(p, half, dirn)) for half in GLA_LEVELS[1:]]
        e_d = level_sums(p, GLA_DIAG // 2, dirn)
        w_d, wi_d = jnp.exp(e_d), jnp.exp(-e_d)
        fine = []
        for b in range(2):
            qs, ks = rows(q, b), rows(k, b)
            acc = None
            for half, w in zip(GLA_LEVELS[1:], fine_w):
                later = (row_b & half) != 0
                q_side = later if dirn == 0 else jnp.logical_not(later)
                part = scores(jnp.where(q_side, qs * rows(w, b), 0.0), jnp.where(q_side, 0.0, ks * rows(w, b)))
                if half in blk_mask:
                    part = part * blk_mask[half]
                acc = part if acc is None else acc + part
            later = (row_b & (GLA_DIAG // 2)) != 0
            shrink_q = later if dirn == 0 else jnp.logical_not(later)
            part = scores(qs * jnp.where(shrink_q, rows(w_d, b), rows(wi_d, b)),
                          ks * jnp.where(shrink_q, rows(wi_d, b), rows(w_d, b)))
            fine.append(acc + jnp.where(diag_mask[dirn] > 0.5, part, 0.0))
        zero = jnp.zeros((hb, hb), F32)
        h0, h1 = slice(0, hb), slice(hb, 2 * hb)
        if dirn == 0:
            top = [fine[0][:, h0], zero, fine[0][:, h1], zero]
            bot = [wide[:, h0], fine[1][:, h0], wide[:, h1], fine[1][:, h1]]
        else:
            top = [fine[0][:, h0], wide[:, h0], fine[0][:, h1], wide[:, h1]]
            bot = [zero, fine[1][:, h0], zero, fine[1][:, h1]]
        a = jnp.concatenate([jnp.concatenate(top, axis=1), jnp.concatenate(bot, axis=1)], axis=0)
        vcat = jnp.concatenate([v, v], axis=0)
        vcat = jnp.where(v_head_match, vcat, jnp.zeros_like(vcat))
        o = o + _dot(a.astype(BF16), vcat)
        return o, state_update(s, k, vt, p, dirn)

    def ctx_state(dirn):
        p = prefix_sums(cad_ref[0], dirn)
        s0 = jnp.zeros((2 * GLA_DV, LANES), F32)
        return state_update(s0, ck_ref[0].astype(F32), cv_ref[0].T, p, dirn)

    def finish(tok0, o):
        g = g_ref[0, pl.ds(tok0, t), :].astype(F32)
        gate = g / (1.0 + jnp.exp(-g))
        halves = [_rms(o[:, h * GLA_DV:(h + 1) * GLA_DV], gn_ref[...]) for h in range(2)]
        o_ref[0, pl.ds(tok0, t), :] = (jnp.concatenate(halves, axis=-1) * gate).astype(o_ref.dtype)

    def body(i, carry, second_half):
        s_f, s_b = carry
        for u in range(GLA_UNROLL):
            c = i * GLA_UNROLL + u
            tok_f = pl.multiple_of(c * t, t)
            tok_b = pl.multiple_of((n_chunks - 1 - c) * t, t)
            o_f, s_f = chunk(tok_f, s_f, 0)
            o_b, s_b = chunk(tok_b, s_b, 1)
            if second_half:
                finish(tok_f, o_f + accb_ref[pl.ds(tok_f, t), :])
                finish(tok_b, o_b + accf_ref[pl.ds(tok_b, t), :])
            else:
                accf_ref[pl.ds(tok_f, t), :] = o_f
                accb_ref[pl.ds(tok_b, t), :] = o_b
        return s_f, s_b

    trips = n_chunks // GLA_UNROLL
    assert n_chunks % (2 * GLA_UNROLL) == 0
    carry = lax.fori_loop(0, trips // 2, functools.partial(body, second_half=False), (ctx_state(0), ctx_state(1)))
    lax.fori_loop(trips // 2, trips, functools.partial(body, second_half=True), carry)


def _gla(q, k, v, ad, g, ck, cv, cad, u, abias, gnorm, cmats):
    b, n, kw = q.shape
    n_ctx = ck.shape[1]
    pairs = kw // LANES
    vw = 2 * GLA_DV
    tok = lambda i, p: (i, 0, p)
    full3 = lambda i, p: (i, 0, 0)
    return pl.pallas_call(
        functools.partial(_gla_kernel, n_tok=n),
        out_shape=jax.ShapeDtypeStruct((b, n, v.shape[2]), BF16),
        grid=(b, pairs),
        in_specs=[pl.BlockSpec((1, n, LANES), tok),
                  pl.BlockSpec((1, n, LANES), tok),
                  pl.BlockSpec((1, n, vw), tok),
                  pl.BlockSpec((1, n, LANES), full3),
                  pl.BlockSpec((1, n, vw), tok),
                  pl.BlockSpec((1, n_ctx, LANES), tok),
                  pl.BlockSpec((1, n_ctx, vw), tok),
                  pl.BlockSpec((1, n_ctx, LANES), full3),
                  pl.BlockSpec((2, LANES, LANES), lambda i, p: (0, 0, p)),
                  pl.BlockSpec((2, 1, LANES), lambda i, p: (0, 0, p)),
                  pl.BlockSpec((1, GLA_DV), lambda i, p: (0, 0)),
                  pl.BlockSpec(cmats.shape, lambda i, p: (0, 0, 0))],
        out_specs=pl.BlockSpec((1, n, vw), tok),
        scratch_shapes=[pltpu.VMEM((n, vw), F32), pltpu.VMEM((n, vw), F32)],
        compiler_params=_cparams(("arbitrary", "arbitrary")),
        name="gla",
    )(q, k, v, ad, g, ck, cv, cad, u, abias, gnorm, cmats)


def _gla_decay_up(a_up):
    r = GLA_GATE_RANK
    pad = jnp.zeros((2, 2 * r, GLA_KEY_W), F32)
    pad = pad.at[0, :r].set(a_up[0]).at[1, r:].set(a_up[1])
    hi, lo = _split_bf16(pad)
    return jnp.concatenate([hi, hi, lo, jnp.zeros_like(hi)], axis=1)


def _outproj_kernel(ona_ref, ogla_ref, x_ref, w1_ref, w2_ref, gm_ref, shf_ref, scf_ref, npost_ref,
                    nfpre_ref, rt_ref, xnew_ref, hf_ref, afft_ref):
    for r0 in range(0, x_ref.shape[1], TOK_TILE):
        rs = slice(r0, r0 + TOK_TILE)
        mix = _dot(ona_ref[0, rs], w1_ref[...]) + _dot(ogla_ref[0, rs], w2_ref[...])
        xn = x_ref[0, rs] + gm_ref[0] * _rms(mix, npost_ref[...])
        xnew_ref[0, rs] = xn
        h = _rms(xn, nfpre_ref[...]) * (1.0 + scf_ref[0]) + shf_ref[0]
        h_hi, h_lo = _split_bf16(h)
        hf_ref[0, rs] = _pack_bf16_pairs(h_hi)
        res = _dot(h_hi, rt_ref[...])
        logits = res[:, :LANES] + res[:, LANES:] + _dot(h_lo, rt_ref[:, :LANES])
        lane = lax.broadcasted_iota(jnp.int32, logits.shape, 1)
        logits = jnp.where(lane < N_EXPERTS, logits, NEG_BIG)
        p = jnp.exp(logits - jnp.max(logits, axis=-1, keepdims=True))
        aff = p / jnp.sum(p, axis=-1, keepdims=True)
        afft_ref[0, :, rs] = aff.T[:N_EXPERTS, :]


def _outproj(o_na, o_gla, x, w1, w2, gm, shf, scf, npost, nfpre, router_cat):
    b, n, d = x.shape
    tm = OUTPROJ_SUBTILES * TOK_TILE
    tokmap = lambda i, j: (i, j, 0)
    smp = lambda i, j: (i, 0, 0)
    cst = lambda i, j: (0, 0)
    return pl.pallas_call(
        _outproj_kernel,
        out_shape=[jax.ShapeDtypeStruct((b, n, d), F32),
                   jax.ShapeDtypeStruct((b, n, d // 2), jnp.int32),
                   jax.ShapeDtypeStruct((b, N_EXPERTS, n), F32)],
        grid=(b, n // tm),
        in_specs=[pl.BlockSpec((1, tm, o_na.shape[2]), tokmap),
                  pl.BlockSpec((1, tm, o_gla.shape[2]), tokmap),
                  pl.BlockSpec((1, tm, d), tokmap),
                  pl.BlockSpec(w1.shape, cst),
                  pl.BlockSpec(w2.shape, cst),
                  pl.BlockSpec((1, 1, d), smp),
                  pl.BlockSpec((1, 1, d), smp),
                  pl.BlockSpec((1, 1, d), smp),
                  pl.BlockSpec((1, d), cst),
                  pl.BlockSpec((1, d), cst),
                  pl.BlockSpec(router_cat.shape, cst)],
        out_specs=[pl.BlockSpec((1, tm, d), tokmap),
                   pl.BlockSpec((1, tm, d // 2), tokmap),
                   pl.BlockSpec((1, N_EXPERTS, tm), lambda i, j: (i, 0, j))],
        compiler_params=_cparams(("arbitrary", "arbitrary")),
        name="outproj",
    )(o_na, o_gla, x, w1, w2, gm, shf, scf, npost.reshape(1, d), nfpre.reshape(1, d), router_cat)


def _route_kernel(afft_ref, rt_ref, *, cap):
    a = afft_ref[0]
    e, n = a.shape
    capf = jnp.float32(cap)

    def search(i, thr_bits):
        cand = thr_bits | lax.shift_left(jnp.int32(1), 30 - i)
        cnt = jnp.sum(jnp.where(a >= lax.bitcast_convert_type(cand, F32), 1.0, 0.0), axis=-1, keepdims=True)
        return jnp.where(cnt >= capf, cand, thr_bits)

    thr_bits = lax.fori_loop(0, 31, search, jnp.zeros((e, 1), jnp.int32))
    thr = lax.bitcast_convert_type(thr_bits, F32)
    need = capf - jnp.sum(jnp.where(a > thr, 1.0, 0.0), axis=-1, keepdims=True)
    r_i = lax.broadcasted_iota(jnp.int32, (LANES, LANES), 0)
    c_i = lax.broadcasted_iota(jnp.int32, (LANES, LANES), 1)
    incl = jnp.where(r_i <= c_i, 1.0, 0.0).astype(BF16)
    off_eq = jnp.zeros((e, 1), F32)
    off_sel = jnp.zeros((e, 1), F32)
    for j in range(n // LANES):
        sl = slice(j * LANES, (j + 1) * LANES)
        a_b = a[:, sl]
        eq_b = jnp.where(a_b == thr, 1.0, 0.0)
        tie_rank = _dot(eq_b.astype(BF16), incl) - eq_b + off_eq
        off_eq = off_eq + jnp.sum(eq_b, axis=-1, keepdims=True)
        sel_b = jnp.where(a_b > thr, 1.0, jnp.where(tie_rank < need, eq_b, 0.0))
        sel = sel_b > 0.5
        rank = _dot(sel_b.astype(BF16), incl) - sel_b + off_sel
        off_sel = off_sel + jnp.sum(sel_b, axis=-1, keepdims=True)
        rsel = jnp.where(sel, rank, -1.0)
        rt_ref[0, :, sl] = rsel.astype(jnp.int32)


def _route(afft, cap):
    b, e, n = afft.shape
    return pl.pallas_call(
        functools.partial(_route_kernel, cap=cap),
        out_shape=jax.ShapeDtypeStruct((b, e, n), jnp.int32),
        grid=(b,),
        in_specs=[pl.BlockSpec((1, e, n), lambda i: (i, 0, 0))],
        out_specs=pl.BlockSpec((1, e, n), lambda i: (i, 0, 0)),
        compiler_params=_cparams(("arbitrary",)),
        name="route",
    )(afft)


def _sc_gather(rsel_t, hf2, cap):
    b, e, n = rsel_t.shape
    width = hf2.shape[1]
    info = plsc.get_sparse_core_info()
    nc, lanes = info.num_cores, info.num_lanes
    workers = nc * info.num_subcores
    items = b * e
    assert items % workers == 0 and n % lanes == 0 and cap % SC_GATHER_ROWS == 0
    per_worker = items // workers
    mesh = plsc.VectorSubcoreMesh(core_axis_name="c", subcore_axis_name="s")

    def body(rank_hbm, hf_hbm, out_hbm, rank_v, idx_v, rows_a, rows_b, sem_a, sem_b):
        bufs = ((rows_a, sem_a), (rows_b, sem_b))
        wid = lax.axis_index("s") * nc + lax.axis_index("c")
        for k in range(per_worker):
            item = wid * per_worker + k
            base_tok = (item // e) * n
            pltpu.sync_copy(rank_hbm.at[item], rank_v)

            @pl.loop(0, n // lanes)
            def _(j):
                r = rank_v[pl.ds(j * lanes, lanes)]
                tok = lax.iota(jnp.int32, lanes) + (j * lanes + base_tok)
                plsc.store_scatter(idx_v, [r], tok, mask=r >= 0)

            def gather(c):
                buf, sem = bufs[c % 2]
                rows = pl.ds(c * SC_GATHER_ROWS, SC_GATHER_ROWS)
                return pltpu.async_copy(hf_hbm.at[idx_v.at[rows]], buf, sem)

            n_chunks = cap // SC_GATHER_ROWS
            pending = gather(0)
            for c in range(n_chunks):
                nxt = gather(c + 1) if c + 1 < n_chunks else None
                pending.wait()
                pltpu.sync_copy(bufs[c % 2][0],
                                out_hbm.at[pl.ds(item * cap + c * SC_GATHER_ROWS, SC_GATHER_ROWS)])
                pending = nxt

    return pl.kernel(
        body, out_type=jax.ShapeDtypeStruct((items * cap, width), hf2.dtype), mesh=mesh,
        scratch_types=[pltpu.VMEM((n,), jnp.int32), pltpu.VMEM((cap,), jnp.int32),
                       pltpu.VMEM((SC_GATHER_ROWS, width), hf2.dtype),
                       pltpu.VMEM((SC_GATHER_ROWS, width), hf2.dtype),
                       pltpu.SemaphoreType.DMA, pltpu.SemaphoreType.DMA],
        compiler_params=pltpu.CompilerParams(needs_layout_passes=False),
        name="scgather",
    )(rsel_t.reshape(items, n), hf2)


def _ffn_kernel(x_ref, wg_ref, wu_ref, wd_ref, o_ref, acc_ref, xb_ref):
    f = pl.program_id(1)
    b = x_ref.shape[0]
    last = pl.num_programs(1) - 1

    def tile(first, final):
        wg = wg_ref[0].astype(BF16)
        wu = wu_ref[0].astype(BF16)
        wd = wd_ref[0].astype(BF16)
        for i in range(b):
            if first:
                xb_ref[i] = _unpack_bf16_pairs(x_ref[i, 0])
            x = xb_ref[i]
            g = _dot(x, wg)
            u = _dot(x, wu)
            hid = (g / (1.0 + jnp.exp(-g)) * u).astype(BF16)
            y = _dot(hid, wd)
            if not first:
                y = acc_ref[i] + y
            if final:
                o_ref[i, 0] = y.astype(o_ref.dtype)
            else:
                acc_ref[i] = y

    @pl.when(f == 0)
    def _():
        tile(True, False)

    @pl.when((f > 0) & (f < last))
    def _():
        tile(False, False)

    @pl.when(f == last)
    def _():
        tile(False, True)


def _ffn(xs, w_gate, w_up, w_down):
    b, e, cap, dp = xs.shape
    d = 2 * dp
    dff = w_gate.shape[2]
    tf = FFN_TILE
    assert dff // tf >= 2
    return pl.pallas_call(
        _ffn_kernel,
        out_shape=jax.ShapeDtypeStruct((b, e, cap, d), BF16),
        grid=(e, dff // tf),
        in_specs=[pl.BlockSpec((b, 1, cap, dp), lambda i, f: (0, i, 0, 0)),
                  pl.BlockSpec((1, d, tf), lambda i, f: (i, 0, f)),
                  pl.BlockSpec((1, d, tf), lambda i, f: (i, 0, f)),
                  pl.BlockSpec((1, tf, d), lambda i, f: (i, f, 0))],
        out_specs=pl.BlockSpec((b, 1, cap, d), lambda i, f: (0, i, 0, 0)),
        scratch_shapes=[pltpu.VMEM((b, cap, d), F32), pltpu.VMEM((b, cap, d), BF16)],
        compiler_params=_cparams(("arbitrary", "arbitrary")),
        name="ffn",
    )(xs, w_gate, w_up, w_down)


def _combine_kernel(off_ref, ys_ref, rt_ref, afft_ref, xn_ref, gf_ref, npost_ref, o_ref, acc_ref):
    bi, tt = pl.program_id(0), pl.program_id(1)
    tm = rt_ref.shape[2]
    n_experts, cap = ys_ref.shape[1], ys_ref.shape[2]
    blocks = tm // LANES
    slot = lax.broadcasted_iota(jnp.int32, (COMBINE_WIN, tm), 0)

    def window(e, w0, j):
        nominal = w0 + j * COMBINE_WIN
        start = pl.multiple_of(jnp.minimum(nominal, cap - COMBINE_WIN), BF16_ROWS)
        rank = rt_ref[0, e:e + 1, :]
        hit = ((rank - start) == slot) & (rank >= nominal)
        weights = jnp.where(hit, afft_ref[0, e:e + 1, :], 0.0).astype(BF16)
        return weights, ys_ref[0, e, pl.ds(start, COMBINE_WIN), :]

    first, extra = [], []
    for e in range(n_experts):
        r0 = off_ref[bi, e, tt * blocks]
        r1 = off_ref[bi, e, (tt + 1) * blocks]
        w0 = (r0 // BF16_ROWS) * BF16_ROWS
        first.append(w0)
        extra.append(jnp.maximum((r1 - w0 + COMBINE_WIN - 1) // COMBINE_WIN - 1, 0))
    terms = []
    for e in range(0, n_experts, COMBINE_STACK):
        ws, ys = zip(*[window(e + k, first[e + k], 0) for k in range(COMBINE_STACK)])
        terms.append(_dot(jnp.concatenate(ws, axis=0).T, jnp.concatenate(ys, axis=0)))
    acc_ref[...] = functools.reduce(lambda a, c: a + c, terms)

    @pl.when(functools.reduce(lambda a, c: a + c, extra) > 0)
    def _():
        for e in range(n_experts):
            def more(j, carry, e=e):
                w, y = window(e, first[e], j)
                acc_ref[...] += _dot(w.T, y)
                return carry
            lax.fori_loop(1, extra[e] + 1, more, 0)

    o_ref[0] = xn_ref[0] + gf_ref[0] * _rms(acc_ref[...], npost_ref[...])


def _combine(ys, slot_off, rsel_t, aff_t, x_new, gf, npost):
    b, e, cap, d = ys.shape
    n = x_new.shape[1]
    tm = COMBINE_TILE
    tokmap = lambda i, j, off: (i, j, 0)
    return pl.pallas_call(
        _combine_kernel,
        out_shape=jax.ShapeDtypeStruct((b, n, d), F32),
        grid_spec=pltpu.PrefetchScalarGridSpec(
            num_scalar_prefetch=1,
            grid=(b, n // tm),
            in_specs=[pl.BlockSpec((1, e, cap, d), lambda i, j, off: (i, 0, 0, 0)),
                      pl.BlockSpec((1, e, tm), lambda i, j, off: (i, 0, j)),
                      pl.BlockSpec((1, e, tm), lambda i, j, off: (i, 0, j)),
                      pl.BlockSpec((1, tm, d), tokmap),
                      pl.BlockSpec((1, 1, d), lambda i, j, off: (i, 0, 0)),
                      pl.BlockSpec((1, d), lambda i, j, off: (0, 0))],
            out_specs=pl.BlockSpec((1, tm, d), tokmap),
            scratch_shapes=[pltpu.VMEM((tm, d), F32)]),
        compiler_params=_cparams(("arbitrary", "arbitrary")),
        name="combine",
    )(slot_off, ys, rsel_t, aff_t, x_new, gf, npost.reshape(1, d))


def kernel(x, c, ctx, c_ctx, w_mod, b_mod, norm_mix_pre, norm_mix_post, norm_ffn_pre, norm_ffn_post,
           w_in, na_rpb, gla_a_up, gla_a_bias, gla_norm, w_out, router, w_gate, w_up, w_down):
    b, n, d = x.shape
    assert w_mod.shape[0] == 1 and d == D_MODEL and n % (GRID_W * NA_QROWS) == 0 and n % GLA_T == 0
    assert ctx.shape[1] == GLA_T
    rows = n // GRID_W
    cap = EC_CAPACITY_FACTOR * n // N_EXPERTS

    c8 = jnp.concatenate([c, c_ctx[None, :], jnp.zeros((8 - b - 1, d), F32)], axis=0)
    mod = _mod(c8, w_mod[0], b_mod[0])
    sh_m, sc_m, g_m, sh_f, sc_f, g_f = [m[:b, None, :] for m in jnp.split(mod, 6, axis=-1)]
    sh_c, sc_c = mod[b:b + 1, None, :d], mod[b:b + 1, None, d:2 * d]

    wb = w_in[0]
    cuts = np.cumsum([0, NA_W, NA_W, GLA_KEY_W, GLA_VAL_W, 2 * GLA_GATE_RANK, NA_W, GLA_KEY_W, GLA_VAL_W])
    w_nak, w_nav, w_gk, w_gv, w_ad, w_naq, w_gq, w_gg = [wb[:, cuts[i]:cuts[i + 1]] for i in range(8)]
    w_ad3 = jnp.concatenate([w_ad, w_ad, w_ad, jnp.zeros_like(w_ad)], axis=1)
    rope = _rope_tables(n)
    na_q, na_k, na_v, gq, gk, gv, ad, gg = _inproj(
        x, sh_m, sc_m, norm_mix_pre[0],
        [w_naq, w_nak, w_nav, w_gq, w_gk, w_gv, w_ad3, w_gg],
        ["plain", "plain", "plain", "rope", "rope", "plain", "split", "plain"],
        [1.0, 1.0, 1.0, GLA_DK ** -0.5, 1.0, 1.0, 1.0, 1.0],
        [BF16] * 8, rope=rope)
    c_nak, c_nav, c_gk, c_gv, c_ad = _inproj(
        ctx, sh_c, sc_c, norm_mix_pre[0],
        [w_nak, w_nav, w_gk, w_gv, w_ad3],
        ["plain", "plain", "plain", "plain", "split"], [1.0] * 5, [BF16] * 5)

    o_na = _na(na_q, na_k, na_v, c_nak, c_nav, *_na_bias_tables(na_rpb[0], rows))

    cmats = jnp.asarray(_gla_prefix_matrices(GLA_T), BF16)
    o_gla = _gla(gq, gk, gv, ad, gg, c_gk, c_gv, c_ad, _gla_decay_up(gla_a_up[0]),
                 gla_a_bias[0][:, None, :], gla_norm[0][None, :], cmats)

    wo = w_out[0].astype(BF16)
    router_pad = jnp.zeros((d, LANES), F32).at[:, :N_EXPERTS].set(router[0])
    x_new, hf, aff_t = _outproj(o_na, o_gla, x, wo[:NA_W], wo[NA_W:], g_m, sh_f, sc_f,
                                     norm_mix_post[0], norm_ffn_pre[0],
                                     jnp.concatenate(_split_bf16(router_pad), axis=1))

    rsel_t = _route(aff_t, cap)
    xs = _sc_gather(rsel_t, hf.reshape(b * n, d // 2), cap).reshape(b, N_EXPERTS, cap, d // 2)
    ys = _ffn(xs, w_gate[0], w_up[0], w_down[0])
    picked = (rsel_t >= 0).astype(jnp.int32).reshape(b, N_EXPERTS, n // LANES, LANES).sum(axis=-1)
    slot_off = jnp.concatenate([jnp.zeros((b, N_EXPERTS, 1), jnp.int32), jnp.cumsum(picked, axis=-1)], axis=-1)
    return _combine(ys, slot_off, rsel_t, aff_t, x_new, g_f, norm_ffn_post[0])
```

```python
import functools

import numpy as np
import jax
import jax.numpy as jnp
from jax import lax
from jax.experimental import pallas as pl
from jax.experimental.pallas import tpu as pltpu
from jax.experimental.pallas import tpu_sc as plsc

F32 = jnp.float32
BF16 = jnp.bfloat16
HIGHEST = lax.Precision.HIGHEST

D_MODEL = 1024
GRID_W = 64
NA_W = 512
NA_HEADS = 8
NA_HEAD_DIM = 64
NA_WIN_ROWS = 8
NA_WIN_COLS = 16
GLA_HEADS = 4
GLA_DV = 128
GLA_DK = 64
GLA_KEY_W = 256
GLA_VAL_W = 512
GLA_GATE_RANK = 16
GLA_GATE_TAU = 16.0
ROPE_BASE = 10000.0
N_EXPERTS = 16
EC_CAPACITY_FACTOR = 2
NORM_EPS = 1e-6
NEG_BIG = -1e30

LANES = 128
BF16_ROWS = 16
VMEM_LIMIT = 56 * 1024 * 1024

TOK_TILE = 512
NA_QROWS = 4
NA_KROWS = NA_QROWS + NA_WIN_ROWS
NA_TAB_PAD = NA_QROWS
NA_TAB_BLOCKS = NA_TAB_PAD + 2 * NA_WIN_ROWS - 1 + NA_QROWS + 1
GLA_T = 256
GLA_LEVELS = (128, 64, 32, 16)
GLA_DIAG = 16
GLA_UNROLL = 2
OUTPROJ_SUBTILES = 2
FFN_TILE = 256
COMBINE_TILE = 256
COMBINE_SUBTILES = 2
COMBINE_WIN = 64
COMBINE_STACK = 4
SC_GATHER_ROWS = 64


def _cparams(sem):
    return pltpu.CompilerParams(dimension_semantics=sem, vmem_limit_bytes=VMEM_LIMIT)


def _rms(v, g):
    return v * lax.rsqrt(jnp.mean(v * v, axis=-1, keepdims=True) + NORM_EPS) * g


def _dot(a, b):
    return jnp.dot(a, b, preferred_element_type=F32)


def _dot_nt(a, b):
    return lax.dot_general(a, b, (((1,), (1,)), ((), ())), preferred_element_type=F32)


def _split_bf16(v):
    hi = v.astype(BF16)
    return hi, (v - hi.astype(F32)).astype(BF16)


_HIGH_HALF = -65536


def _pack_bf16_pairs(hb):
    bits = lax.bitcast_convert_type(hb.astype(F32), jnp.int32)
    half = hb.shape[1] // 2
    return lax.shift_right_logical(bits[:, :half], 16) | (bits[:, half:] & _HIGH_HALF)


def _unpack_bf16_pairs(w):
    lo = lax.bitcast_convert_type(lax.shift_left(w, 16), F32).astype(BF16)
    hi = lax.bitcast_convert_type(w & _HIGH_HALF, F32).astype(BF16)
    return jnp.concatenate([lo, hi], axis=1)


def _mod_kernel(c_ref, w_ref, b_ref, o_ref):
    c = c_ref[...]
    s = c / (1.0 + jnp.exp(-c))
    o_ref[...] = _dot(s.astype(BF16), w_ref[...].astype(BF16)) + b_ref[...]


def _mod(c8, w_mod, b_mod):
    d, n = w_mod.shape
    tn = 1536
    return pl.pallas_call(
        _mod_kernel,
        out_shape=jax.ShapeDtypeStruct((8, n), F32),
        grid=(n // tn,),
        in_specs=[pl.BlockSpec((8, d), lambda j: (0, 0)),
                  pl.BlockSpec((d, tn), lambda j: (0, j)),
                  pl.BlockSpec((1, tn), lambda j: (0, j))],
        out_specs=pl.BlockSpec((8, tn), lambda j: (0, j)),
        compiler_params=_cparams(("arbitrary",)),
        name="mod",
    )(c8, w_mod, b_mod.reshape(1, n))


def _inproj_kernel(x_ref, sh_ref, sc_ref, g_ref, *refs, modes, scales):
    n_rope = 2 if "rope" in modes else 0
    rope_refs, refs = refs[:n_rope], refs[n_rope:]
    n_w = len(modes)
    w_refs, o_refs, wb_refs = refs[:n_w], refs[n_w:2 * n_w], refs[2 * n_w:]

    @pl.when((pl.program_id(0) == 0) & (pl.program_id(1) == 0))
    def _():
        for w_ref, wb_ref in zip(w_refs, wb_refs):
            wb_ref[...] = w_ref[...].astype(BF16)

    x = x_ref[0]
    h = _rms(x, g_ref[...]) * (1.0 + sc_ref[0]) + sh_ref[0]
    hb = h.astype(BF16)
    for w_ref, o_ref, mode, scale in zip(wb_refs, o_refs, modes, scales):
        y = _dot(hb, w_ref[...])
        if mode == "rope":
            cols = y.shape[1]
            quarter = GLA_DK // 4
            lane = lax.broadcasted_iota(jnp.int32, y.shape, 1)
            partner = jnp.where((lane & (2 * quarter - 1)) < quarter,
                                pltpu.roll(y, cols - quarter, axis=1), pltpu.roll(y, quarter, axis=1))
            y = (y * rope_refs[0][...] + partner * rope_refs[1][...]) * scale
        elif mode == "split":
            hi, lo = _split_bf16(y)
            lane = lax.broadcasted_iota(jnp.int32, y.shape, 1)
            rank2 = 2 * GLA_GATE_RANK
            y = jnp.where((lane >= rank2) & (lane < 2 * rank2), lo, hi)
        o_ref[0] = y.astype(o_ref.dtype)


def _inproj(x, shift, scale, g, weights, modes, scales, out_dtypes, rope=None):
    b, n, d = x.shape
    tm = min(TOK_TILE, n)
    per_sample = shift.shape[0] == b
    mod_map = (lambda i, j: (i, 0, 0)) if per_sample else (lambda i, j: (0, 0, 0))
    in_specs = [pl.BlockSpec((1, tm, d), lambda i, j: (i, j, 0)),
                pl.BlockSpec((1, 1, d), mod_map),
                pl.BlockSpec((1, 1, d), mod_map),
                pl.BlockSpec((1, d), lambda i, j: (0, 0))]
    args = [x, shift, scale, g.reshape(1, d)]
    if rope is not None:
        in_specs += [pl.BlockSpec((tm, rope[0].shape[1]), lambda i, j: (j, 0))] * 2
        args += list(rope)
    out_shapes, out_specs = [], []
    for w, mode, dt in zip(weights, modes, out_dtypes):
        in_specs.append(pl.BlockSpec(w.shape, lambda i, j: (0, 0), pipeline_mode=pl.Buffered(1)))
        cols = w.shape[1]
        out_shapes.append(jax.ShapeDtypeStruct((b, n, cols), dt))
        out_specs.append(pl.BlockSpec((1, tm, cols), lambda i, j: (i, j, 0)))
    return pl.pallas_call(
        functools.partial(_inproj_kernel, modes=tuple(modes), scales=tuple(scales)),
        out_shape=out_shapes,
        grid=(b, n // tm),
        in_specs=in_specs,
        out_specs=out_specs,
        scratch_shapes=[pltpu.VMEM(w.shape, BF16) for w in weights],
        compiler_params=_cparams(("arbitrary", "arbitrary")),
        name="inproj",
    )(*args, *weights)


def _rope_tables(n):
    t = np.arange(n)
    pos_row, pos_col = (t // GRID_W).astype(np.float32), (t % GRID_W).astype(np.float32)
    quarter = GLA_DK // 4
    freqs = np.float32(ROPE_BASE) ** (-np.arange(quarter, dtype=np.float32) / quarter)
    ang_r = pos_row[:, None] * freqs
    ang_c = pos_col[:, None] * freqs
    cos = np.concatenate([np.cos(ang_r), np.cos(ang_r), np.cos(ang_c), np.cos(ang_c)], axis=-1)
    sin = np.concatenate([-np.sin(ang_r), np.sin(ang_r), -np.sin(ang_c), np.sin(ang_c)], axis=-1)
    return jnp.asarray(np.tile(cos, (1, GLA_HEADS))), jnp.asarray(np.tile(sin, (1, GLA_HEADS)))


def _na_patterns(rows):
    kr = min(NA_WIN_ROWS, rows)
    n_blocks = rows // NA_QROWS
    pats = []
    for blk in (0, 1, n_blocks - 1):
        r0 = blk * NA_QROWS
        k0 = int(np.clip(r0 - kr // 2, 0, rows - NA_KROWS))
        strips = []
        for a in range(NA_QROWS):
            r_start = int(np.clip(r0 + a - kr // 2, 0, rows - kr))
            start = k0 - (r0 + a) + NA_WIN_ROWS - 1 + NA_TAB_PAD
            assert 0 <= start and start + NA_KROWS <= NA_TAB_BLOCKS
            strips.append((start, [r_start <= k0 + c < r_start + kr for c in range(NA_KROWS)]))
        pats.append(strips)
    return pats


def _na_bias_tables(rpb, rows):
    heads = rpb.shape[0]
    col = np.arange(GRID_W)
    c_start = np.clip(col - NA_WIN_COLS // 2, 0, GRID_W - NA_WIN_COLS)
    col_ok = (col[None, :] >= c_start[:, None]) & (col[None, :] < c_start[:, None] + NA_WIN_COLS)
    dc = np.clip(col[None, :] - col[:, None] + NA_WIN_COLS - 1, 0, 2 * NA_WIN_COLS - 2)
    sel_c = (np.arange(2 * NA_WIN_COLS - 1)[:, None, None] == dc[None]) & col_ok[None]
    t = jnp.einsum("hrd,dqk->hqrk", rpb, jnp.asarray(sel_c, F32), precision=HIGHEST)
    t = jnp.where(jnp.asarray(col_ok)[None, :, None, :], t, NEG_BIG)
    n_dr = 2 * NA_WIN_ROWS - 1
    t = t.reshape(heads, GRID_W, n_dr * GRID_W)
    back = NA_TAB_BLOCKS + 1 - NA_TAB_PAD - n_dr
    t = jnp.pad(t, ((0, 0), (0, 0), (NA_TAB_PAD * GRID_W, back * GRID_W)), constant_values=NEG_BIG)
    width = NA_TAB_BLOCKS * GRID_W
    tab = jnp.stack([t[:, :, :width], t[:, :, GRID_W:GRID_W + width]], axis=1)
    row_mask = np.zeros((3, NA_QROWS, 1, NA_KROWS * GRID_W), np.float32)
    for pat, strips in enumerate(_na_patterns(rows)):
        for a, (_, valid) in enumerate(strips):
            row_mask[pat, a, 0] = np.repeat(np.where(valid, 0.0, NEG_BIG), GRID_W)
    return tab, jnp.asarray(row_mask)


def _na_kernel(q_ref, k_ref, v_ref, kc_ref, vc_ref, tab_ref, rmask_ref, o_ref, bias_ref,
               sw_ref, sc_ref, pw_ref, pc_ref, *, rows):
    nq, nk = NA_QROWS * GRID_W, NA_KROWS * GRID_W
    n_blocks = rows // NA_QROWS
    kr = min(NA_WIN_ROWS, rows)
    scale = NA_HEAD_DIM ** -0.5
    lane = lax.broadcasted_iota(jnp.int32, (nq, LANES), 1)
    first_head = lane < NA_HEAD_DIM
    kc = kc_ref[0]
    vc = vc_ref[0]
    lane_w = lax.broadcasted_iota(jnp.int32, (nk, LANES), 1)
    lane_c = lax.broadcasted_iota(jnp.int32, vc.shape, 1)
    @pl.when(pl.program_id(1) == 0)
    def _():
        for pat, strips in enumerate(_na_patterns(rows)):
            for a, (start, _) in enumerate(strips):
                parity = start % 2
                off = (start - parity) * GRID_W
                for h in range(2):
                    bias_ref[0, pat, h, a * GRID_W:(a + 1) * GRID_W, :] = (
                        tab_ref[h, parity, :, off:off + nk] + rmask_ref[pat, a])

    def key_start(i):
        return pl.multiple_of(jnp.clip(i * NA_QROWS - kr // 2, 0, rows - NA_KROWS) * GRID_W, GRID_W)

    def scores(i, slot):
        pat = jnp.where(i == 0, 0, jnp.where(i == n_blocks - 1, 2, 1))
        q = q_ref[0, pl.ds(pl.multiple_of(i * nq, nq), nq), :] * scale
        kw = k_ref[0, pl.ds(key_start(i), nk), :]
        for h in range(2):
            qh = jnp.where(first_head if h == 0 else jnp.logical_not(first_head), q, jnp.zeros_like(q))
            sw_ref[slot, h] = _dot_nt(qh, kw) + bias_ref[0, pat, h]
            sc_ref[slot, h] = _dot_nt(qh, kc)

    def softmax(slot):
        for h in range(2):
            s_w = sw_ref[slot, h]
            s_c = sc_ref[slot, h]
            m = jnp.maximum(jnp.max(s_w, axis=-1, keepdims=True), jnp.max(s_c, axis=-1, keepdims=True))
            pw_ref[slot, h] = jnp.exp((s_w - m).astype(BF16))
            pc_ref[slot, h] = jnp.exp((s_c - m).astype(BF16))

    def values(i, slot):
        vw = v_ref[0, pl.ds(key_start(i), nk), :]
        outs = []
        for h in range(2):
            sum_lane = NA_HEAD_DIM * (1 - h)
            vw_h = jnp.where(lane_w == sum_lane, jnp.ones_like(vw), vw)
            vc_h = jnp.where(lane_c == sum_lane, jnp.ones_like(vc), vc)
            o = _dot(pw_ref[slot, h], vw_h) + _dot(pc_ref[slot, h], vc_h)
            outs.append(o * (1.0 / o[:, sum_lane:sum_lane + 1]))
        o = jnp.where(first_head, outs[0], outs[1])
        o_ref[0, pl.ds(pl.multiple_of(i * nq, nq), nq), :] = o.astype(o_ref.dtype)

    assert n_blocks % 2 == 0 and n_blocks >= 4
    scores(0, 0)
    softmax(0)
    scores(1, 1)

    def trip(j, carry):
        i = 2 * j
        values(i - 2, 0)
        softmax(1)
        scores(i, 0)
        values(i - 1, 1)
        softmax(0)
        scores(i + 1, 1)
        return carry

    lax.fori_loop(1, n_blocks // 2, trip, 0)
    values(n_blocks - 2, 0)
    softmax(1)
    values(n_blocks - 1, 1)


def _na(q, k, v, kc, vc, tab, row_mask):
    b, n, w = q.shape
    n_ctx = kc.shape[1]
    pairs = w // LANES
    rows = n // GRID_W
    nq, nk = NA_QROWS * GRID_W, NA_KROWS * GRID_W
    tok = lambda p, i: (i, 0, p)
    return pl.pallas_call(
        functools.partial(_na_kernel, rows=rows),
        out_shape=jax.ShapeDtypeStruct((b, n, w), BF16),
        grid=(pairs, b),
        in_specs=[pl.BlockSpec((1, n, LANES), tok),
                  pl.BlockSpec((1, n, LANES), tok),
                  pl.BlockSpec((1, n, LANES), tok),
                  pl.BlockSpec((1, n_ctx, LANES), tok),
                  pl.BlockSpec((1, n_ctx, LANES), tok),
                  pl.BlockSpec((2,) + tab.shape[1:], lambda p, i: (p, 0, 0, 0)),
                  pl.BlockSpec(row_mask.shape, lambda p, i: (0, 0, 0, 0))],
        out_specs=pl.BlockSpec((1, n, LANES), tok),
        scratch_shapes=[pltpu.VMEM((1, 3, 2, nq, nk), F32),
                        pltpu.VMEM((2, 2, nq, nk), F32), pltpu.VMEM((2, 2, nq, n_ctx), F32),
                        pltpu.VMEM((2, 2, nq, nk), BF16), pltpu.VMEM((2, 2, nq, n_ctx), BF16)],
        compiler_params=_cparams(("arbitrary", "arbitrary")),
        name="na",
    )(q, k, v, kc, vc, tab, row_mask)


def _gla_prefix_matrices(t):
    i = np.arange(t)
    return np.stack([i[:, None] >= i[None, :], i[:, None] <= i[None, :]]).astype(np.float32)


def _gla_kernel(q_ref, k_ref, v_ref, ad_ref, g_ref, ck_ref, cv_ref, cad_ref,
                u_ref, ab_ref, gn_ref, cm_ref, o_ref, accf_ref, accb_ref, *, n_tok):
    t = GLA_T
    n_chunks = n_tok // t
    row = lax.broadcasted_iota(jnp.int32, (t, LANES), 0)
    hb = t // 2
    row_b = lax.broadcasted_iota(jnp.int32, (hb, LANES), 0)
    head0_b = lax.broadcasted_iota(jnp.int32, (hb, LANES), 1) < GLA_DK
    row2 = lax.broadcasted_iota(jnp.int32, (hb, 2 * hb), 0)
    col2 = lax.broadcasted_iota(jnp.int32, (hb, 2 * hb), 1) & (hb - 1)
    vrow = lax.broadcasted_iota(jnp.int32, (2 * t, 2 * GLA_DV), 0)
    vlane = lax.broadcasted_iota(jnp.int32, (2 * t, 2 * GLA_DV), 1)
    v_head_match = (vrow >= t) == (vlane >= GLA_DV)
    srow = lax.broadcasted_iota(jnp.int32, (2 * GLA_DV, LANES), 0)
    slane = lax.broadcasted_iota(jnp.int32, (2 * GLA_DV, LANES), 1)
    s_blockdiag = (srow >= GLA_DV) == (slane >= GLA_DK)
    blk_mask = {half: jnp.where((row2 & ~(2 * half - 1)) == (col2 & ~(2 * half - 1)), 1.0, 0.0)
                for half in GLA_LEVELS if 2 * half < hb}
    diag_blk = (row2 & ~(GLA_DIAG - 1)) == (col2 & ~(GLA_DIAG - 1))
    diag_mask = (jnp.where(diag_blk & (row2 >= col2), 1.0, 0.0), jnp.where(diag_blk & (row2 <= col2), 1.0, 0.0))

    def prefix_sums(ad, dirn):
        z = _dot(ad, u_ref[dirn]) + ab_ref[dirn]
        loga = (jnp.minimum(z, 0.0) - jnp.log(1.0 + jnp.exp(-jnp.abs(z)))) * (1.0 / GLA_GATE_TAU)
        hi, lo = _split_bf16(loga)
        p2 = _dot(cm_ref[dirn], jnp.concatenate([hi, lo], axis=-1))
        return p2[:, :LANES] + p2[:, LANES:]

    def chunk_end(p, dirn):
        return p[t - 1:t, :] if dirn == 0 else p[0:1, :]

    def level_sums(p, half, dirn):
        blk = 2 * half
        p3 = p.reshape(t // blk, blk, LANES)
        edge = half - 1 if dirn == 0 else half
        ref = jnp.broadcast_to(p3[:, edge:edge + 1, :], p3.shape).reshape(t, LANES)
        later = (row & half) != 0
        return jnp.where(later == (dirn == 0), p - ref, ref - p)

    def state_update(s, k, vt, p, dirn):
        kh = (k * jnp.exp(chunk_end(p, dirn) - p)).astype(BF16)
        return s * jnp.exp(chunk_end(p, dirn)) + jnp.where(s_blockdiag, _dot(vt, kh), 0.0)

    def chunk(tok0, s, dirn):
        q = q_ref[0, pl.ds(tok0, t), :].astype(F32)
        k = k_ref[0, pl.ds(tok0, t), :].astype(F32)
        v = v_ref[0, pl.ds(tok0, t), :]
        vt = v.T
        p = prefix_sums(ad_ref[0, pl.ds(tok0, t), :], dirn)
        qh = (q * jnp.exp(p)).astype(BF16)
        o = _dot_nt(qh, s.astype(BF16))
        def rows(x, b):
            return x[b * hb:(b + 1) * hb]

        def scores(qt, kt):
            kcat = jnp.concatenate([jnp.where(head0_b, kt, 0.0), jnp.where(head0_b, 0.0, kt)], axis=0)
            return _dot_nt(qt.astype(BF16), kcat.astype(BF16))

        assert GLA_LEVELS[0] == hb
        w = jnp.exp(level_sums(p, hb, dirn))
        qb, kb = (1, 0) if dirn == 0 else (0, 1)
        wide = scores(rows(q, qb) * rows(w, qb), rows(k, kb) * rows(w, kb))
        fine_w = [jnp.exp(level_sums(p, half, dirn)) for half in GLA_LEVELS[1:]]
        e_d = level_sums(p, GLA_DIAG // 2, dirn)
        w_d, wi_d = jnp.exp(e_d), jnp.exp(-e_d)
        fine = []
        for b in range(2):
            qs, ks = rows(q, b), rows(k, b)
            acc = None
            for half, w in zip(GLA_LEVELS[1:], fine_w):
                later = (row_b & half) != 0
                q_side = later if dirn == 0 else jnp.logical_not(later)
                part = scores(jnp.where(q_side, qs * rows(w, b), 0.0), jnp.where(q_side, 0.0, ks * rows(w, b)))
                if half in blk_mask:
                    part = part * blk_mask[half]
                acc = part if acc is None else acc + part
            later = (row_b & (GLA_DIAG // 2)) != 0
            shrink_q = later if dirn == 0 else jnp.logical_not(later)
            part = scores(qs * jnp.where(shrink_q, rows(w_d, b), rows(wi_d, b)),
                          ks * jnp.where(shrink_q, rows(wi_d, b), rows(w_d, b)))
            fine.append(acc + jnp.where(diag_mask[dirn] > 0.5, part, 0.0))
        zero = jnp.zeros((hb, hb), F32)
        h0, h1 = slice(0, hb), slice(hb, 2 * hb)
        if dirn == 0:
            top = [fine[0][:, h0], zero, fine[0][:, h1], zero]
            bot = [wide[:, h0], fine[1][:, h0], wide[:, h1], fine[1][:, h1]]
        else:
            top = [fine[0][:, h0], wide[:, h0], fine[0][:, h1], wide[:, h1]]
            bot = [zero, fine[1][:, h0], zero, fine[1][:, h1]]
        a = jnp.concatenate([jnp.concatenate(top, axis=1), jnp.concatenate(bot, axis=1)], axis=0)
        vcat = jnp.concatenate([v, v], axis=0)
        vcat = jnp.where(v_head_match, vcat, jnp.zeros_like(vcat))
        o = o + _dot(a.astype(BF16), vcat)
        return o, state_update(s, k, vt, p, dirn)

    def ctx_state(dirn):
        p = prefix_sums(cad_ref[0], dirn)
        s0 = jnp.zeros((2 * GLA_DV, LANES), F32)
        return state_update(s0, ck_ref[0].astype(F32), cv_ref[0].T, p, dirn)

    def finish(tok0, o):
        g = g_ref[0, pl.ds(tok0, t), :].astype(F32)
        gate = g / (1.0 + jnp.exp(-g))
        halves = [_rms(o[:, h * GLA_DV:(h + 1) * GLA_DV], gn_ref[...]) for h in range(2)]
        o_ref[0, pl.ds(tok0, t), :] = (jnp.concatenate(halves, axis=-1) * gate).astype(o_ref.dtype)

    def body(i, carry, second_half):
        s_f, s_b = carry
        for u in range(GLA_UNROLL):
            c = i * GLA_UNROLL + u
            tok_f = pl.multiple_of(c * t, t)
            tok_b = pl.multiple_of((n_chunks - 1 - c) * t, t)
            o_f, s_f = chunk(tok_f, s_f, 0)
            o_b, s_b = chunk(tok_b, s_b, 1)
            if second_half:
                finish(tok_f, o_f + accb_ref[pl.ds(tok_f, t), :])
                finish(tok_b, o_b + accf_ref[pl.ds(tok_b, t), :])
            else:
                accf_ref[pl.ds(tok_f, t), :] = o_f
                accb_ref[pl.ds(tok_b, t), :] = o_b
        return s_f, s_b

    trips = n_chunks // GLA_UNROLL
    assert n_chunks % (2 * GLA_UNROLL) == 0
    carry = lax.fori_loop(0, trips // 2, functools.partial(body, second_half=False), (ctx_state(0), ctx_state(1)))
    lax.fori_loop(trips // 2, trips, functools.partial(body, second_half=True), carry)


def _gla(q, k, v, ad, g, ck, cv, cad, u, abias, gnorm, cmats):
    b, n, kw = q.shape
    n_ctx = ck.shape[1]
    pairs = kw // LANES
    vw = 2 * GLA_DV
    tok = lambda i, p: (i, 0, p)
    full3 = lambda i, p: (i, 0, 0)
    return pl.pallas_call(
        functools.partial(_gla_kernel, n_tok=n),
        out_shape=jax.ShapeDtypeStruct((b, n, v.shape[2]), BF16),
        grid=(b, pairs),
        in_specs=[pl.BlockSpec((1, n, LANES), tok),
                  pl.BlockSpec((1, n, LANES), tok),
                  pl.BlockSpec((1, n, vw), tok),
                  pl.BlockSpec((1, n, LANES), full3),
                  pl.BlockSpec((1, n, vw), tok),
                  pl.BlockSpec((1, n_ctx, LANES), tok),
                  pl.BlockSpec((1, n_ctx, vw), tok),
                  pl.BlockSpec((1, n_ctx, LANES), full3),
                  pl.BlockSpec((2, LANES, LANES), lambda i, p: (0, 0, p)),
                  pl.BlockSpec((2, 1, LANES), lambda i, p: (0, 0, p)),
                  pl.BlockSpec((1, GLA_DV), lambda i, p: (0, 0)),
                  pl.BlockSpec(cmats.shape, lambda i, p: (0, 0, 0))],
        out_specs=pl.BlockSpec((1, n, vw), tok),
        scratch_shapes=[pltpu.VMEM((n, vw), F32), pltpu.VMEM((n, vw), F32)],
        compiler_params=_cparams(("arbitrary", "arbitrary")),
        name="gla",
    )(q, k, v, ad, g, ck, cv, cad, u, abias, gnorm, cmats)


def _gla_decay_up(a_up):
    r = GLA_GATE_RANK
    pad = jnp.zeros((2, 2 * r, GLA_KEY_W), F32)
    pad = pad.at[0, :r].set(a_up[0]).at[1, r:].set(a_up[1])
    hi, lo = _split_bf16(pad)
    return jnp.concatenate([hi, hi, lo, jnp.zeros_like(hi)], axis=1)


def _outproj_kernel(ona_ref, ogla_ref, x_ref, w1_ref, w2_ref, gm_ref, shf_ref, scf_ref, npost_ref,
                    nfpre_ref, rt_ref, xnew_ref, hf_ref, afft_ref):
    for r0 in range(0, x_ref.shape[1], TOK_TILE):
        rs = slice(r0, r0 + TOK_TILE)
        mix = _dot(ona_ref[0, rs], w1_ref[...]) + _dot(ogla_ref[0, rs], w2_ref[...])
        xn = x_ref[0, rs] + gm_ref[0] * _rms(mix, npost_ref[...])
        xnew_ref[0, rs] = xn
        h = _rms(xn, nfpre_ref[...]) * (1.0 + scf_ref[0]) + shf_ref[0]
        h_hi, h_lo = _split_bf16(h)
        hf_ref[0, rs] = _pack_bf16_pairs(h_hi)
        res = _dot(h_hi, rt_ref[...])
        logits = res[:, :LANES] + res[:, LANES:] + _dot(h_lo, rt_ref[:, :LANES])
        lane = lax.broadcasted_iota(jnp.int32, logits.shape, 1)
        logits = jnp.where(lane < N_EXPERTS, logits, NEG_BIG)
        p = jnp.exp(logits - jnp.max(logits, axis=-1, keepdims=True))
        aff = p / jnp.sum(p, axis=-1, keepdims=True)
        afft_ref[0, :, rs] = aff.T[:N_EXPERTS, :]


def _outproj(o_na, o_gla, x, w1, w2, gm, shf, scf, npost, nfpre, router_cat):
    b, n, d = x.shape
    tm = OUTPROJ_SUBTILES * TOK_TILE
    tokmap = lambda i, j: (i, j, 0)
    smp = lambda i, j: (i, 0, 0)
    cst = lambda i, j: (0, 0)
    return pl.pallas_call(
        _outproj_kernel,
        out_shape=[jax.ShapeDtypeStruct((b, n, d), F32),
                   jax.ShapeDtypeStruct((b, n, d // 2), jnp.int32),
                   jax.ShapeDtypeStruct((b, N_EXPERTS, n), F32)],
        grid=(b, n // tm),
        in_specs=[pl.BlockSpec((1, tm, o_na.shape[2]), tokmap),
                  pl.BlockSpec((1, tm, o_gla.shape[2]), tokmap),
                  pl.BlockSpec((1, tm, d), tokmap),
                  pl.BlockSpec(w1.shape, cst),
                  pl.BlockSpec(w2.shape, cst),
                  pl.BlockSpec((1, 1, d), smp),
                  pl.BlockSpec((1, 1, d), smp),
                  pl.BlockSpec((1, 1, d), smp),
                  pl.BlockSpec((1, d), cst),
                  pl.BlockSpec((1, d), cst),
                  pl.BlockSpec(router_cat.shape, cst)],
        out_specs=[pl.BlockSpec((1, tm, d), tokmap),
                   pl.BlockSpec((1, tm, d // 2), tokmap),
                   pl.BlockSpec((1, N_EXPERTS, tm), lambda i, j: (i, 0, j))],
        compiler_params=_cparams(("arbitrary", "arbitrary")),
        name="outproj",
    )(o_na, o_gla, x, w1, w2, gm, shf, scf, npost.reshape(1, d), nfpre.reshape(1, d), router_cat)


def _route_kernel(afft_ref, rt_ref, *, cap):
    a = afft_ref[0]
    e, n = a.shape
    capf = jnp.float32(cap)

    def search(i, thr_bits):
        cand = thr_bits | lax.shift_left(jnp.int32(1), 30 - i)
        cnt = jnp.sum(jnp.where(a >= lax.bitcast_convert_type(cand, F32), 1.0, 0.0), axis=-1, keepdims=True)
        return jnp.where(cnt >= capf, cand, thr_bits)

    thr_bits = lax.fori_loop(0, 31, search, jnp.zeros((e, 1), jnp.int32))
    thr = lax.bitcast_convert_type(thr_bits, F32)
    need = capf - jnp.sum(jnp.where(a > thr, 1.0, 0.0), axis=-1, keepdims=True)
    r_i = lax.broadcasted_iota(jnp.int32, (LANES, LANES), 0)
    c_i = lax.broadcasted_iota(jnp.int32, (LANES, LANES), 1)
    incl = jnp.where(r_i <= c_i, 1.0, 0.0).astype(BF16)
    off_eq = jnp.zeros((e, 1), F32)
    off_sel = jnp.zeros((e, 1), F32)
    for j in range(n // LANES):
        sl = slice(j * LANES, (j + 1) * LANES)
        a_b = a[:, sl]
        eq_b = jnp.where(a_b == thr, 1.0, 0.0)
        tie_rank = _dot(eq_b.astype(BF16), incl) - eq_b + off_eq
        off_eq = off_eq + jnp.sum(eq_b, axis=-1, keepdims=True)
        sel_b = jnp.where(a_b > thr, 1.0, jnp.where(tie_rank < need, eq_b, 0.0))
        sel = sel_b > 0.5
        rank = _dot(sel_b.astype(BF16), incl) - sel_b + off_sel
        off_sel = off_sel + jnp.sum(sel_b, axis=-1, keepdims=True)
        rsel = jnp.where(sel, rank, -1.0)
        rt_ref[0, :, sl] = rsel.astype(jnp.int32)


def _route(afft, cap):
    b, e, n = afft.shape
    return pl.pallas_call(
        functools.partial(_route_kernel, cap=cap),
        out_shape=jax.ShapeDtypeStruct((b, e, n), jnp.int32),
        grid=(b,),
        in_specs=[pl.BlockSpec((1, e, n), lambda i: (i, 0, 0))],
        out_specs=pl.BlockSpec((1, e, n), lambda i: (i, 0, 0)),
        compiler_params=_cparams(("arbitrary",)),
        name="route",
    )(afft)


def _sc_gather(rsel_t, hf2, cap):
    b, e, n = rsel_t.shape
    width = hf2.shape[1]
    info = plsc.get_sparse_core_info()
    nc, lanes = info.num_cores, info.num_lanes
    workers = nc * info.num_subcores
    items = b * e
    assert items % workers == 0 and n % lanes == 0 and cap % SC_GATHER_ROWS == 0
    per_worker = items // workers
    mesh = plsc.VectorSubcoreMesh(core_axis_name="c", subcore_axis_name="s")

    def body(rank_hbm, hf_hbm, out_hbm, rank_v, idx_v, rows_a, rows_b, sem_a, sem_b):
        bufs = ((rows_a, sem_a), (rows_b, sem_b))
        wid = lax.axis_index("s") * nc + lax.axis_index("c")
        for k in range(per_worker):
            item = wid * per_worker + k
            base_tok = (item // e) * n
            pltpu.sync_copy(rank_hbm.at[item], rank_v)

            @pl.loop(0, n // lanes)
            def _(j):
                r = rank_v[pl.ds(j * lanes, lanes)]
                tok = lax.iota(jnp.int32, lanes) + (j * lanes + base_tok)
                plsc.store_scatter(idx_v, [r], tok, mask=r >= 0)

            def gather(c):
                buf, sem = bufs[c % 2]
                rows = pl.ds(c * SC_GATHER_ROWS, SC_GATHER_ROWS)
                return pltpu.async_copy(hf_hbm.at[idx_v.at[rows]], buf, sem)

            n_chunks = cap // SC_GATHER_ROWS
            pending = gather(0)
            for c in range(n_chunks):
                nxt = gather(c + 1) if c + 1 < n_chunks else None
                pending.wait()
                pltpu.sync_copy(bufs[c % 2][0],
                                out_hbm.at[pl.ds(item * cap + c * SC_GATHER_ROWS, SC_GATHER_ROWS)])
                pending = nxt

    return pl.kernel(
        body, out_type=jax.ShapeDtypeStruct((items * cap, width), hf2.dtype), mesh=mesh,
        scratch_types=[pltpu.VMEM((n,), jnp.int32), pltpu.VMEM((cap,), jnp.int32),
                       pltpu.VMEM((SC_GATHER_ROWS, width), hf2.dtype),
                       pltpu.VMEM((SC_GATHER_ROWS, width), hf2.dtype),
                       pltpu.SemaphoreType.DMA, pltpu.SemaphoreType.DMA],
        compiler_params=pltpu.CompilerParams(needs_layout_passes=False),
        name="scgather",
    )(rsel_t.reshape(items, n), hf2)


def _ffn_kernel(x_ref, wg_ref, wu_ref, wd_ref, o_ref, acc_ref, xb_ref):
    f = pl.program_id(1)
    b = x_ref.shape[0]
    last = pl.num_programs(1) - 1

    def tile(first, final):
        wg = wg_ref[0].astype(BF16)
        wu = wu_ref[0].astype(BF16)
        wd = wd_ref[0].astype(BF16)
        for i in range(b):
            if first:
                xb_ref[i] = _unpack_bf16_pairs(x_ref[i, 0])
            x = xb_ref[i]
            g = _dot(x, wg)
            u = _dot(x, wu)
            hid = (g / (1.0 + jnp.exp(-g)) * u).astype(BF16)
            y = _dot(hid, wd)
            if not first:
                y = acc_ref[i] + y
            if final:
                o_ref[i, 0] = y.astype(o_ref.dtype)
            else:
                acc_ref[i] = y

    @pl.when(f == 0)
    def _():
        tile(True, False)

    @pl.when((f > 0) & (f < last))
    def _():
        tile(False, False)

    @pl.when(f == last)
    def _():
        tile(False, True)


def _ffn(xs, w_gate, w_up, w_down):
    b, e, cap, dp = xs.shape
    d = 2 * dp
    dff = w_gate.shape[2]
    tf = FFN_TILE
    assert dff // tf >= 2
    return pl.pallas_call(
        _ffn_kernel,
        out_shape=jax.ShapeDtypeStruct((b, e, cap, d), BF16),
        grid=(e, dff // tf),
        in_specs=[pl.BlockSpec((b, 1, cap, dp), lambda i, f: (0, i, 0, 0)),
                  pl.BlockSpec((1, d, tf), lambda i, f: (i, 0, f)),
                  pl.BlockSpec((1, d, tf), lambda i, f: (i, 0, f)),
                  pl.BlockSpec((1, tf, d), lambda i, f: (i, f, 0))],
        out_specs=pl.BlockSpec((b, 1, cap, d), lambda i, f: (0, i, 0, 0)),
        scratch_shapes=[pltpu.VMEM((b, cap, d), F32), pltpu.VMEM((b, cap, d), BF16)],
        compiler_params=_cparams(("arbitrary", "arbitrary")),
        name="ffn",
    )(xs, w_gate, w_up, w_down)


def _combine_kernel(off_ref, ys_ref, rt_ref, afft_ref, xn_ref, gf_ref, npost_ref, o_ref, acc_ref):
    bi, tt = pl.program_id(0), pl.program_id(1)
    ts = COMBINE_TILE
    n_sub = rt_ref.shape[2] // ts
    n_experts, cap = ys_ref.shape[1], ys_ref.shape[2]
    blocks = ts // LANES
    slot = lax.broadcasted_iota(jnp.int32, (COMBINE_WIN, ts), 0)

    for sub in range(n_sub):
        toks = slice(sub * ts, (sub + 1) * ts)
        blk0 = (tt * n_sub + sub) * blocks

        def window(e, w0, j, toks=toks):
            nominal = w0 + j * COMBINE_WIN
            start = pl.multiple_of(jnp.minimum(nominal, cap - COMBINE_WIN), BF16_ROWS)
            rank = rt_ref[0, e:e + 1, toks]
            hit = ((rank - start) == slot) & (rank >= nominal)
            weights = jnp.where(hit, afft_ref[0, e:e + 1, toks], 0.0).astype(BF16)
            return weights, ys_ref[0, e, pl.ds(start, COMBINE_WIN), :]

        first, extra = [], []
        for e in range(n_experts):
            r0 = off_ref[bi, e, blk0]
            r1 = off_ref[bi, e, blk0 + blocks]
            w0 = (r0 // BF16_ROWS) * BF16_ROWS
            first.append(w0)
            extra.append(jnp.maximum((r1 - w0 + COMBINE_WIN - 1) // COMBINE_WIN - 1, 0))
        terms = []
        for e in range(0, n_experts, COMBINE_STACK):
            ws, ys = zip(*[window(e + k, first[e + k], 0) for k in range(COMBINE_STACK)])
            terms.append(_dot(jnp.concatenate(ws, axis=0).T, jnp.concatenate(ys, axis=0)))
        acc_ref[toks] = functools.reduce(lambda a, c: a + c, terms)

        @pl.when(functools.reduce(lambda a, c: a + c, extra) > 0)
        def _(window=window, first=first, extra=extra, toks=toks):
            for e in range(n_experts):
                def more(j, carry, e=e):
                    w, y = window(e, first[e], j)
                    acc_ref[toks] += _dot(w.T, y)
                    return carry
                lax.fori_loop(1, extra[e] + 1, more, 0)

    o_ref[0] = xn_ref[0] + gf_ref[0] * _rms(acc_ref[...], npost_ref[...])


def _combine(ys, slot_off, rsel_t, aff_t, x_new, gf, npost):
    b, e, cap, d = ys.shape
    n = x_new.shape[1]
    tm = COMBINE_SUBTILES * COMBINE_TILE
    tokmap = lambda i, j, off: (i, j, 0)
    return pl.pallas_call(
        _combine_kernel,
        out_shape=jax.ShapeDtypeStruct((b, n, d), F32),
        grid_spec=pltpu.PrefetchScalarGridSpec(
            num_scalar_prefetch=1,
            grid=(b, n // tm),
            in_specs=[pl.BlockSpec((1, e, cap, d), lambda i, j, off: (i, 0, 0, 0)),
                      pl.BlockSpec((1, e, tm), lambda i, j, off: (i, 0, j)),
                      pl.BlockSpec((1, e, tm), lambda i, j, off: (i, 0, j)),
                      pl.BlockSpec((1, tm, d), tokmap),
                      pl.BlockSpec((1, 1, d), lambda i, j, off: (i, 0, 0)),
                      pl.BlockSpec((1, d), lambda i, j, off: (0, 0))],
            out_specs=pl.BlockSpec((1, tm, d), tokmap),
            scratch_shapes=[pltpu.VMEM((tm, d), F32)]),
        compiler_params=_cparams(("arbitrary", "arbitrary")),
        name="combine",
    )(slot_off, ys, rsel_t, aff_t, x_new, gf, npost.reshape(1, d))


def kernel(x, c, ctx, c_ctx, w_mod, b_mod, norm_mix_pre, norm_mix_post, norm_ffn_pre, norm_ffn_post,
           w_in, na_rpb, gla_a_up, gla_a_bias, gla_norm, w_out, router, w_gate, w_up, w_down):
    b, n, d = x.shape
    assert w_mod.shape[0] == 1 and d == D_MODEL and n % (GRID_W * NA_QROWS) == 0 and n % GLA_T == 0
    assert ctx.shape[1] == GLA_T
    rows = n // GRID_W
    cap = EC_CAPACITY_FACTOR * n // N_EXPERTS

    c8 = jnp.concatenate([c, c_ctx[None, :], jnp.zeros((8 - b - 1, d), F32)], axis=0)
    mod = _mod(c8, w_mod[0], b_mod[0])
    sh_m, sc_m, g_m, sh_f, sc_f, g_f = [m[:b, None, :] for m in jnp.split(mod, 6, axis=-1)]
    sh_c, sc_c = mod[b:b + 1, None, :d], mod[b:b + 1, None, d:2 * d]

    wb = w_in[0]
    cuts = np.cumsum([0, NA_W, NA_W, GLA_KEY_W, GLA_VAL_W, 2 * GLA_GATE_RANK, NA_W, GLA_KEY_W, GLA_VAL_W])
    w_nak, w_nav, w_gk, w_gv, w_ad, w_naq, w_gq, w_gg = [wb[:, cuts[i]:cuts[i + 1]] for i in range(8)]
    w_ad3 = jnp.concatenate([w_ad, w_ad, w_ad, jnp.zeros_like(w_ad)], axis=1)
    rope = _rope_tables(n)
    na_q, na_k, na_v, gq, gk, gv, ad, gg = _inproj(
        x, sh_m, sc_m, norm_mix_pre[0],
        [w_naq, w_nak, w_nav, w_gq, w_gk, w_gv, w_ad3, w_gg],
        ["plain", "plain", "plain", "rope", "rope", "plain", "split", "plain"],
        [1.0, 1.0, 1.0, GLA_DK ** -0.5, 1.0, 1.0, 1.0, 1.0],
        [BF16] * 8, rope=rope)
    c_nak, c_nav, c_gk, c_gv, c_ad = _inproj(
        ctx, sh_c, sc_c, norm_mix_pre[0],
        [w_nak, w_nav, w_gk, w_gv, w_ad3],
        ["plain", "plain", "plain", "plain", "split"], [1.0] * 5, [BF16] * 5)

    o_na = _na(na_q, na_k, na_v, c_nak, c_nav, *_na_bias_tables(na_rpb[0], rows))

    cmats = jnp.asarray(_gla_prefix_matrices(GLA_T), BF16)
    o_gla = _gla(gq, gk, gv, ad, gg, c_gk, c_gv, c_ad, _gla_decay_up(gla_a_up[0]),
                 gla_a_bias[0][:, None, :], gla_norm[0][None, :], cmats)

    wo = w_out[0].astype(BF16)
    router_pad = jnp.zeros((d, LANES), F32).at[:, :N_EXPERTS].set(router[0])
    x_new, hf, aff_t = _outproj(o_na, o_gla, x, wo[:NA_W], wo[NA_W:], g_m, sh_f, sc_f,
                                     norm_mix_post[0], norm_ffn_pre[0],
                                     jnp.concatenate(_split_bf16(router_pad), axis=1))

    rsel_t = _route(aff_t, cap)
    xs = _sc_gather(rsel_t, hf.reshape(b * n, d // 2), cap).reshape(b, N_EXPERTS, cap, d // 2)
    ys = _ffn(xs, w_gate[0], w_up[0], w_down[0])
    picked = (rsel_t >= 0).astype(jnp.int32).reshape(b, N_EXPERTS, n // LANES, LANES).sum(axis=-1)
    slot_off = jnp.concatenate([jnp.zeros((b, N_EXPERTS, 1), jnp.int32), jnp.cumsum(picked, axis=-1)], axis=-1)
    return _combine(ys, slot_off, rsel_t, aff_t, x_new, g_f, norm_ffn_post[0])
```

```python
import functools

import numpy as np
import jax
import jax.numpy as jnp
from jax import lax
from jax.experimental import pallas as pl
from jax.experimental.pallas import tpu as pltpu
from jax.experimental.pallas import tpu_sc as plsc

F32 = jnp.float32
BF16 = jnp.bfloat16
HIGHEST = lax.Precision.HIGHEST

D_MODEL = 1024
GRID_W = 64
NA_W = 512
NA_HEADS = 8
NA_HEAD_DIM = 64
NA_WIN_ROWS = 8
NA_WIN_COLS = 16
GLA_HEADS = 4
GLA_DV = 128
GLA_DK = 64
GLA_KEY_W = 256
GLA_VAL_W = 512
GLA_GATE_RANK = 16
GLA_GATE_TAU = 16.0
ROPE_BASE = 10000.0
N_EXPERTS = 16
EC_CAPACITY_FACTOR = 2
NORM_EPS = 1e-6
NEG_BIG = -1e30

LANES = 128
BF16_ROWS = 16
VMEM_LIMIT = 56 * 1024 * 1024

TOK_TILE = 512
NA_QROWS = 4
NA_KROWS = NA_QROWS + NA_WIN_ROWS
NA_TAB_PAD = NA_QROWS
NA_TAB_BLOCKS = NA_TAB_PAD + 2 * NA_WIN_ROWS - 1 + NA_QROWS + 1
GLA_T = 256
GLA_LEVELS = (128, 64, 32, 16)
GLA_DIAG = 16
GLA_UNROLL = 2
OUTPROJ_SUBTILES = 2
FFN_TILE = 256
COMBINE_TILE = 256
COMBINE_SUBTILES = 2
COMBINE_WIN = 64
COMBINE_STACK = 4
SC_GATHER_ROWS = 64


def _cparams(sem):
    return pltpu.CompilerParams(dimension_semantics=sem, vmem_limit_bytes=VMEM_LIMIT)


def _rms(v, g):
    return v * lax.rsqrt(jnp.mean(v * v, axis=-1, keepdims=True) + NORM_EPS) * g


def _dot(a, b):
    return jnp.dot(a, b, preferred_element_type=F32)


def _dot_nt(a, b):
    return lax.dot_general(a, b, (((1,), (1,)), ((), ())), preferred_element_type=F32)


def _split_bf16(v):
    hi = v.astype(BF16)
    return hi, (v - hi.astype(F32)).astype(BF16)


_HIGH_HALF = -65536


def _pack_bf16_pairs(hb):
    bits = lax.bitcast_convert_type(hb.astype(F32), jnp.int32)
    half = hb.shape[1] // 2
    return lax.shift_right_logical(bits[:, :half], 16) | (bits[:, half:] & _HIGH_HALF)


def _unpack_bf16_pairs(w):
    lo = lax.bitcast_convert_type(lax.shift_left(w, 16), F32).astype(BF16)
    hi = lax.bitcast_convert_type(w & _HIGH_HALF, F32).astype(BF16)
    return jnp.concatenate([lo, hi], axis=1)


def _mod_kernel(c_ref, w_ref, b_ref, o_ref):
    c = c_ref[...]
    s = c / (1.0 + jnp.exp(-c))
    o_ref[...] = _dot(s.astype(BF16), w_ref[...].astype(BF16)) + b_ref[...]


def _mod(c8, w_mod, b_mod):
    d, n = w_mod.shape
    tn = 1536
    return pl.pallas_call(
        _mod_kernel,
        out_shape=jax.ShapeDtypeStruct((8, n), F32),
        grid=(n // tn,),
        in_specs=[pl.BlockSpec((8, d), lambda j: (0, 0)),
                  pl.BlockSpec((d, tn), lambda j: (0, j)),
                  pl.BlockSpec((1, tn), lambda j: (0, j))],
        out_specs=pl.BlockSpec((8, tn), lambda j: (0, j)),
        compiler_params=_cparams(("arbitrary",)),
        name="mod",
    )(c8, w_mod, b_mod.reshape(1, n))


def _inproj_kernel(x_ref, sh_ref, sc_ref, g_ref, *refs, modes, scales, col_segments):
    n_rope = 2 if "rope" in modes else 0
    rope_refs, refs = refs[:n_rope], refs[n_rope:]
    n_w = len(modes)
    w_ref, o_refs, wb_refs = refs[0], refs[1:1 + n_w], refs[1 + n_w:]

    @pl.when((pl.program_id(0) == 0) & (pl.program_id(1) == 0))
    def _():
        for segments, wb_ref in zip(col_segments, wb_refs):
            off = 0
            for lo, hi in segments:
                wb_ref[:, off:off + hi - lo] = w_ref[:, lo:hi].astype(BF16)
                off += hi - lo
            if off < wb_ref.shape[1]:
                wb_ref[:, off:] = jnp.zeros((wb_ref.shape[0], wb_ref.shape[1] - off), BF16)

    x = x_ref[0]
    h = _rms(x, g_ref[...]) * (1.0 + sc_ref[0]) + sh_ref[0]
    hb = h.astype(BF16)
    for w_ref, o_ref, mode, scale in zip(wb_refs, o_refs, modes, scales):
        y = _dot(hb, w_ref[...])
        if mode == "rope":
            cols = y.shape[1]
            quarter = GLA_DK // 4
            lane = lax.broadcasted_iota(jnp.int32, y.shape, 1)
            partner = jnp.where((lane & (2 * quarter - 1)) < quarter,
                                pltpu.roll(y, cols - quarter, axis=1), pltpu.roll(y, quarter, axis=1))
            y = (y * rope_refs[0][...] + partner * rope_refs[1][...]) * scale
        elif mode == "split":
            hi, lo = _split_bf16(y)
            lane = lax.broadcasted_iota(jnp.int32, y.shape, 1)
            rank2 = 2 * GLA_GATE_RANK
            y = jnp.where((lane >= rank2) & (lane < 2 * rank2), lo, hi)
        o_ref[0] = y.astype(o_ref.dtype)


def _inproj(x, shift, scale, g, w_all, col_segments, widths, modes, scales, out_dtypes, rope=None):
    b, n, d = x.shape
    tm = min(TOK_TILE, n)
    per_sample = shift.shape[0] == b
    mod_map = (lambda i, j: (i, 0, 0)) if per_sample else (lambda i, j: (0, 0, 0))
    in_specs = [pl.BlockSpec((1, tm, d), lambda i, j: (i, j, 0)),
                pl.BlockSpec((1, 1, d), mod_map),
                pl.BlockSpec((1, 1, d), mod_map),
                pl.BlockSpec((1, d), lambda i, j: (0, 0))]
    args = [x, shift, scale, g.reshape(1, d)]
    if rope is not None:
        in_specs += [pl.BlockSpec((tm, rope[0].shape[1]), lambda i, j: (j, 0))] * 2
        args += list(rope)
    in_specs.append(pl.BlockSpec(w_all.shape, lambda i, j: (0, 0), pipeline_mode=pl.Buffered(1)))
    out_shapes, out_specs = [], []
    for cols, dt in zip(widths, out_dtypes):
        out_shapes.append(jax.ShapeDtypeStruct((b, n, cols), dt))
        out_specs.append(pl.BlockSpec((1, tm, cols), lambda i, j: (i, j, 0)))
    return pl.pallas_call(
        functools.partial(_inproj_kernel, modes=tuple(modes), scales=tuple(scales),
                          col_segments=tuple(tuple(seg) for seg in col_segments)),
        out_shape=out_shapes,
        grid=(b, n // tm),
        in_specs=in_specs,
        out_specs=out_specs,
        scratch_shapes=[pltpu.VMEM((d, cols), BF16) for cols in widths],
        compiler_params=_cparams(("arbitrary", "arbitrary")),
        name="inproj",
    )(*args, w_all)


def _rope_tables(n):
    t = np.arange(n)
    pos_row, pos_col = (t // GRID_W).astype(np.float32), (t % GRID_W).astype(np.float32)
    quarter = GLA_DK // 4
    freqs = np.float32(ROPE_BASE) ** (-np.arange(quarter, dtype=np.float32) / quarter)
    ang_r = pos_row[:, None] * freqs
    ang_c = pos_col[:, None] * freqs
    cos = np.concatenate([np.cos(ang_r), np.cos(ang_r), np.cos(ang_c), np.cos(ang_c)], axis=-1)
    sin = np.concatenate([-np.sin(ang_r), np.sin(ang_r), -np.sin(ang_c), np.sin(ang_c)], axis=-1)
    return jnp.asarray(np.tile(cos, (1, GLA_HEADS))), jnp.asarray(np.tile(sin, (1, GLA_HEADS)))


def _na_patterns(rows):
    kr = min(NA_WIN_ROWS, rows)
    n_blocks = rows // NA_QROWS
    pats = []
    for blk in (0, 1, n_blocks - 1):
        r0 = blk * NA_QROWS
        k0 = int(np.clip(r0 - kr // 2, 0, rows - NA_KROWS))
        strips = []
        for a in range(NA_QROWS):
            r_start = int(np.clip(r0 + a - kr // 2, 0, rows - kr))
            start = k0 - (r0 + a) + NA_WIN_ROWS - 1 + NA_TAB_PAD
            assert 0 <= start and start + NA_KROWS <= NA_TAB_BLOCKS
            strips.append((start, [r_start <= k0 + c < r_start + kr for c in range(NA_KROWS)]))
        pats.append(strips)
    return pats


def _na_bias_tables(rpb, rows):
    heads = rpb.shape[0]
    col = np.arange(GRID_W)
    c_start = np.clip(col - NA_WIN_COLS // 2, 0, GRID_W - NA_WIN_COLS)
    col_ok = (col[None, :] >= c_start[:, None]) & (col[None, :] < c_start[:, None] + NA_WIN_COLS)
    dc = np.clip(col[None, :] - col[:, None] + NA_WIN_COLS - 1, 0, 2 * NA_WIN_COLS - 2)
    sel_c = (np.arange(2 * NA_WIN_COLS - 1)[:, None, None] == dc[None]) & col_ok[None]
    t = jnp.einsum("hrd,dqk->hqrk", rpb, jnp.asarray(sel_c, F32), precision=HIGHEST)
    t = jnp.where(jnp.asarray(col_ok)[None, :, None, :], t, NEG_BIG)
    n_dr = 2 * NA_WIN_ROWS - 1
    t = t.reshape(heads, GRID_W, n_dr * GRID_W)
    back = NA_TAB_BLOCKS + 1 - NA_TAB_PAD - n_dr
    t = jnp.pad(t, ((0, 0), (0, 0), (NA_TAB_PAD * GRID_W, back * GRID_W)), constant_values=NEG_BIG)
    width = NA_TAB_BLOCKS * GRID_W
    tab = jnp.stack([t[:, :, :width], t[:, :, GRID_W:GRID_W + width]], axis=1)
    row_mask = np.zeros((3, NA_QROWS, 1, NA_KROWS * GRID_W), np.float32)
    for pat, strips in enumerate(_na_patterns(rows)):
        for a, (_, valid) in enumerate(strips):
            row_mask[pat, a, 0] = np.repeat(np.where(valid, 0.0, NEG_BIG), GRID_W)
    return tab, jnp.asarray(row_mask)


def _na_kernel(q_ref, k_ref, v_ref, kc_ref, vc_ref, tab_ref, rmask_ref, o_ref, bias_ref,
               sw_ref, sc_ref, pw_ref, pc_ref, *, rows):
    nq, nk = NA_QROWS * GRID_W, NA_KROWS * GRID_W
    n_blocks = rows // NA_QROWS
    kr = min(NA_WIN_ROWS, rows)
    scale = NA_HEAD_DIM ** -0.5
    lane = lax.broadcasted_iota(jnp.int32, (nq, LANES), 1)
    first_head = lane < NA_HEAD_DIM
    kc = kc_ref[0]
    vc = vc_ref[0]
    lane_w = lax.broadcasted_iota(jnp.int32, (nk, LANES), 1)
    lane_c = lax.broadcasted_iota(jnp.int32, vc.shape, 1)
    @pl.when(pl.program_id(1) == 0)
    def _():
        for pat, strips in enumerate(_na_patterns(rows)):
            for a, (start, _) in enumerate(strips):
                parity = start % 2
                off = (start - parity) * GRID_W
                for h in range(2):
                    bias_ref[0, pat, h, a * GRID_W:(a + 1) * GRID_W, :] = (
                        tab_ref[h, parity, :, off:off + nk] + rmask_ref[pat, a])

    def key_start(i):
        return pl.multiple_of(jnp.clip(i * NA_QROWS - kr // 2, 0, rows - NA_KROWS) * GRID_W, GRID_W)

    def scores(i, slot):
        pat = jnp.where(i == 0, 0, jnp.where(i == n_blocks - 1, 2, 1))
        q = q_ref[0, pl.ds(pl.multiple_of(i * nq, nq), nq), :] * scale
        kw = k_ref[0, pl.ds(key_start(i), nk), :]
        for h in range(2):
            qh = jnp.where(first_head if h == 0 else jnp.logical_not(first_head), q, jnp.zeros_like(q))
            sw_ref[slot, h] = _dot_nt(qh, kw) + bias_ref[0, pat, h]
            sc_ref[slot, h] = _dot_nt(qh, kc)

    def softmax(slot):
        for h in range(2):
            s_w = sw_ref[slot, h]
            s_c = sc_ref[slot, h]
            m = jnp.maximum(jnp.max(s_w, axis=-1, keepdims=True), jnp.max(s_c, axis=-1, keepdims=True))
            pw_ref[slot, h] = jnp.exp((s_w - m).astype(BF16))
            pc_ref[slot, h] = jnp.exp((s_c - m).astype(BF16))

    def values(i, slot):
        vw = v_ref[0, pl.ds(key_start(i), nk), :]
        outs = []
        for h in range(2):
            sum_lane = NA_HEAD_DIM * (1 - h)
            vw_h = jnp.where(lane_w == sum_lane, jnp.ones_like(vw), vw)
            vc_h = jnp.where(lane_c == sum_lane, jnp.ones_like(vc), vc)
            o = _dot(pw_ref[slot, h], vw_h) + _dot(pc_ref[slot, h], vc_h)
            outs.append(o * (1.0 / o[:, sum_lane:sum_lane + 1]))
        o = jnp.where(first_head, outs[0], outs[1])
        o_ref[0, pl.ds(pl.multiple_of(i * nq, nq), nq), :] = o.astype(o_ref.dtype)

    assert n_blocks % 2 == 0 and n_blocks >= 4
    scores(0, 0)
    softmax(0)
    scores(1, 1)

    def trip(j, carry):
        i = 2 * j
        values(i - 2, 0)
        softmax(1)
        scores(i, 0)
        values(i - 1, 1)
        softmax(0)
        scores(i + 1, 1)
        return carry

    lax.fori_loop(1, n_blocks // 2, trip, 0)
    values(n_blocks - 2, 0)
    softmax(1)
    values(n_blocks - 1, 1)


def _na(q, k, v, kc, vc, tab, row_mask):
    b, n, w = q.shape
    n_ctx = kc.shape[1]
    pairs = w // LANES
    rows = n // GRID_W
    nq, nk = NA_QROWS * GRID_W, NA_KROWS * GRID_W
    tok = lambda p, i: (i, 0, p)
    return pl.pallas_call(
        functools.partial(_na_kernel, rows=rows),
        out_shape=jax.ShapeDtypeStruct((b, n, w), BF16),
        grid=(pairs, b),
        in_specs=[pl.BlockSpec((1, n, LANES), tok),
                  pl.BlockSpec((1, n, LANES), tok),
                  pl.BlockSpec((1, n, LANES), tok),
                  pl.BlockSpec((1, n_ctx, LANES), tok),
                  pl.BlockSpec((1, n_ctx, LANES), tok),
                  pl.BlockSpec((2,) + tab.shape[1:], lambda p, i: (p, 0, 0, 0)),
                  pl.BlockSpec(row_mask.shape, lambda p, i: (0, 0, 0, 0))],
        out_specs=pl.BlockSpec((1, n, LANES), tok),
        scratch_shapes=[pltpu.VMEM((1, 3, 2, nq, nk), F32),
                        pltpu.VMEM((2, 2, nq, nk), F32), pltpu.VMEM((2, 2, nq, n_ctx), F32),
                        pltpu.VMEM((2, 2, nq, nk), BF16), pltpu.VMEM((2, 2, nq, n_ctx), BF16)],
        compiler_params=_cparams(("arbitrary", "arbitrary")),
        name="na",
    )(q, k, v, kc, vc, tab, row_mask)


def _gla_prefix_matrices(t):
    i = np.arange(t)
    return np.stack([i[:, None] >= i[None, :], i[:, None] <= i[None, :]]).astype(np.float32)


def _gla_kernel(q_ref, k_ref, v_ref, ad_ref, g_ref, ck_ref, cv_ref, cad_ref,
                u_ref, ab_ref, gn_ref, cm_ref, o_ref, accf_ref, accb_ref, *, n_tok):
    t = GLA_T
    n_chunks = n_tok // t
    row = lax.broadcasted_iota(jnp.int32, (t, LANES), 0)
    hb = t // 2
    row_b = lax.broadcasted_iota(jnp.int32, (hb, LANES), 0)
    head0_b = lax.broadcasted_iota(jnp.int32, (hb, LANES), 1) < GLA_DK
    row2 = lax.broadcasted_iota(jnp.int32, (hb, 2 * hb), 0)
    col2 = lax.broadcasted_iota(jnp.int32, (hb, 2 * hb), 1) & (hb - 1)
    vrow = lax.broadcasted_iota(jnp.int32, (2 * t, 2 * GLA_DV), 0)
    vlane = lax.broadcasted_iota(jnp.int32, (2 * t, 2 * GLA_DV), 1)
    v_head_match = (vrow >= t) == (vlane >= GLA_DV)
    srow = lax.broadcasted_iota(jnp.int32, (2 * GLA_DV, LANES), 0)
    slane = lax.broadcasted_iota(jnp.int32, (2 * GLA_DV, LANES), 1)
    s_blockdiag = (srow >= GLA_DV) == (slane >= GLA_DK)
    blk_mask = {half: jnp.where((row2 & ~(2 * half - 1)) == (col2 & ~(2 * half - 1)), 1.0, 0.0)
                for half in GLA_LEVELS if 2 * half < hb}
    diag_blk = (row2 & ~(GLA_DIAG - 1)) == (col2 & ~(GLA_DIAG - 1))
    diag_mask = (jnp.where(diag_blk & (row2 >= col2), 1.0, 0.0), jnp.where(diag_blk & (row2 <= col2), 1.0, 0.0))

    def prefix_sums(ad, dirn):
        z = _dot(ad, u_ref[dirn]) + ab_ref[dirn]
        loga = (jnp.minimum(z, 0.0) - jnp.log(1.0 + jnp.exp(-jnp.abs(z)))) * (1.0 / GLA_GATE_TAU)
        hi, lo = _split_bf16(loga)
        p2 = _dot(cm_ref[dirn], jnp.concatenate([hi, lo], axis=-1))
        return p2[:, :LANES] + p2[:, LANES:]

    def chunk_end(p, dirn):
        return p[t - 1:t, :] if dirn == 0 else p[0:1, :]

    def level_sums(p, half, dirn):
        blk = 2 * half
        p3 = p.reshape(t // blk, blk, LANES)
        edge = half - 1 if dirn == 0 else half
        ref = jnp.broadcast_to(p3[:, edge:edge + 1, :], p3.shape).reshape(t, LANES)
        later = (row & half) != 0
        return jnp.where(later == (dirn == 0), p - ref, ref - p)

    def state_update(s, k, vt, p, dirn):
        kh = (k * jnp.exp(chunk_end(p, dirn) - p)).astype(BF16)
        return s * jnp.exp(chunk_end(p, dirn)) + jnp.where(s_blockdiag, _dot(vt, kh), 0.0)

    def chunk(tok0, s, dirn):
        q = q_ref[0, pl.ds(tok0, t), :].astype(F32)
        k = k_ref[0, pl.ds(tok0, t), :].astype(F32)
        v = v_ref[0, pl.ds(tok0, t), :]
        vt = v.T
        p = prefix_sums(ad_ref[0, pl.ds(tok0, t), :], dirn)
        qh = (q * jnp.exp(p)).astype(BF16)
        o = _dot_nt(qh, s.astype(BF16))
        def rows(x, b):
            return x[b * hb:(b + 1) * hb]

        def scores(qt, kt):
            kcat = jnp.concatenate([jnp.where(head0_b, kt, 0.0), jnp.where(head0_b, 0.0, kt)], axis=0)
            return _dot_nt(qt.astype(BF16), kcat.astype(BF16))

        assert GLA_LEVELS[0] == hb
        w = jnp.exp(level_sums(p, hb, dirn))
        qb, kb = (1, 0) if dirn == 0 else (0, 1)
        wide = scores(rows(q, qb) * rows(w, qb), rows(k, kb) * rows(w, kb))
        fine_w = [jnp.exp(level_sums(p, half, dirn)) for half in GLA_LEVELS[1:]]
        e_d = level_sums(p, GLA_DIAG // 2, dirn)
        w_d, wi_d = jnp.exp(e_d), jnp.exp(-e_d)
        fine = []
        for b in range(2):
            qs, ks = rows(q, b), rows(k, b)
            acc = None
            for half, w in zip(GLA_LEVELS[1:], fine_w):
                later = (row_b & half) != 0
                q_side = later if dirn == 0 else jnp.logical_not(later)
                part = scores(jnp.where(q_side, qs * rows(w, b), 0.0), jnp.where(q_side, 0.0, ks * rows(w, b)))
                if half in blk_mask:
                    part = part * blk_mask[half]
                acc = part if acc is None else acc + part
            later = (row_b & (GLA_DIAG // 2)) != 0
            shrink_q = later if dirn == 0 else jnp.logical_not(later)
            part = scores(qs * jnp.where(shrink_q, rows(w_d, b), rows(wi_d, b)),
                          ks * jnp.where(shrink_q, rows(wi_d, b), rows(w_d, b)))
            fine.append(acc + jnp.where(diag_mask[dirn] > 0.5, part, 0.0))
        zero = jnp.zeros((hb, hb), F32)
        h0, h1 = slice(0, hb), slice(hb, 2 * hb)
        if dirn == 0:
            top = [fine[0][:, h0], zero, fine[0][:, h1], zero]
            bot = [wide[:, h0], fine[1][:, h0], wide[:, h1], fine[1][:, h1]]
        else:
            top = [fine[0][:, h0], wide[:, h0], fine[0][:, h1], wide[:, h1]]
            bot = [zero, fine[1][:, h0], zero, fine[1][:, h1]]
        a = jnp.concatenate([jnp.concatenate(top, axis=1), jnp.concatenate(bot, axis=1)], axis=0)
        vcat = jnp.concatenate([v, v], axis=0)
        vcat = jnp.where(v_head_match, vcat, jnp.zeros_like(vcat))
        o = o + _dot(a.astype(BF16), vcat)
        return o, state_update(s, k, vt, p, dirn)

    def ctx_state(dirn):
        p = prefix_sums(cad_ref[0], dirn)
        s0 = jnp.zeros((2 * GLA_DV, LANES), F32)
        return state_update(s0, ck_ref[0].astype(F32), cv_ref[0].T, p, dirn)

    def finish(tok0, o):
        g = g_ref[0, pl.ds(tok0, t), :].astype(F32)
        gate = g / (1.0 + jnp.exp(-g))
        halves = [_rms(o[:, h * GLA_DV:(h + 1) * GLA_DV], gn_ref[...]) for h in range(2)]
        o_ref[0, pl.ds(tok0, t), :] = (jnp.concatenate(halves, axis=-1) * gate).astype(o_ref.dtype)

    def body(i, carry, second_half):
        s_f, s_b = carry
        for u in range(GLA_UNROLL):
            c = i * GLA_UNROLL + u
            tok_f = pl.multiple_of(c * t, t)
            tok_b = pl.multiple_of((n_chunks - 1 - c) * t, t)
            o_f, s_f = chunk(tok_f, s_f, 0)
            o_b, s_b = chunk(tok_b, s_b, 1)
            if second_half:
                finish(tok_f, o_f + accb_ref[pl.ds(tok_f, t), :])
                finish(tok_b, o_b + accf_ref[pl.ds(tok_b, t), :])
            else:
                accf_ref[pl.ds(tok_f, t), :] = o_f
                accb_ref[pl.ds(tok_b, t), :] = o_b
        return s_f, s_b

    trips = n_chunks // GLA_UNROLL
    assert n_chunks % (2 * GLA_UNROLL) == 0
    carry = lax.fori_loop(0, trips // 2, functools.partial(body, second_half=False), (ctx_state(0), ctx_state(1)))
    lax.fori_loop(trips // 2, trips, functools.partial(body, second_half=True), carry)


def _gla(q, k, v, ad, g, ck, cv, cad, u, abias, gnorm, cmats):
    b, n, kw = q.shape
    n_ctx = ck.shape[1]
    pairs = kw // LANES
    vw = 2 * GLA_DV
    tok = lambda i, p: (i, 0, p)
    full3 = lambda i, p: (i, 0, 0)
    return pl.pallas_call(
        functools.partial(_gla_kernel, n_tok=n),
        out_shape=jax.ShapeDtypeStruct((b, n, v.shape[2]), BF16),
        grid=(b, pairs),
        in_specs=[pl.BlockSpec((1, n, LANES), tok),
                  pl.BlockSpec((1, n, LANES), tok),
                  pl.BlockSpec((1, n, vw), tok),
                  pl.BlockSpec((1, n, LANES), full3),
                  pl.BlockSpec((1, n, vw), tok),
                  pl.BlockSpec((1, n_ctx, LANES), tok),
                  pl.BlockSpec((1, n_ctx, vw), tok),
                  pl.BlockSpec((1, n_ctx, LANES), full3),
                  pl.BlockSpec((2, LANES, LANES), lambda i, p: (0, 0, p)),
                  pl.BlockSpec((2, 1, LANES), lambda i, p: (0, 0, p)),
                  pl.BlockSpec((1, GLA_DV), lambda i, p: (0, 0)),
                  pl.BlockSpec(cmats.shape, lambda i, p: (0, 0, 0))],
        out_specs=pl.BlockSpec((1, n, vw), tok),
        scratch_shapes=[pltpu.VMEM((n, vw), F32), pltpu.VMEM((n, vw), F32)],
        compiler_params=_cparams(("arbitrary", "arbitrary")),
        name="gla",
    )(q, k, v, ad, g, ck, cv, cad, u, abias, gnorm, cmats)


def _gla_decay_up(a_up):
    r = GLA_GATE_RANK
    pad = jnp.zeros((2, 2 * r, GLA_KEY_W), F32)
    pad = pad.at[0, :r].set(a_up[0]).at[1, r:].set(a_up[1])
    hi, lo = _split_bf16(pad)
    return jnp.concatenate([hi, hi, lo, jnp.zeros_like(hi)], axis=1)


def _outproj_kernel(ona_ref, ogla_ref, x_ref, w1_ref, w2_ref, gm_ref, shf_ref, scf_ref, npost_ref,
                    nfpre_ref, rt_ref, xnew_ref, hf_ref, afft_ref):
    for r0 in range(0, x_ref.shape[1], TOK_TILE):
        rs = slice(r0, r0 + TOK_TILE)
        mix = _dot(ona_ref[0, rs], w1_ref[...]) + _dot(ogla_ref[0, rs], w2_ref[...])
        xn = x_ref[0, rs] + gm_ref[0] * _rms(mix, npost_ref[...])
        xnew_ref[0, rs] = xn
        h = _rms(xn, nfpre_ref[...]) * (1.0 + scf_ref[0]) + shf_ref[0]
        h_hi, h_lo = _split_bf16(h)
        hf_ref[0, rs] = _pack_bf16_pairs(h_hi)
        res = _dot(h_hi, rt_ref[...])
        logits = res[:, :LANES] + res[:, LANES:] + _dot(h_lo, rt_ref[:, :LANES])
        lane = lax.broadcasted_iota(jnp.int32, logits.shape, 1)
        logits = jnp.where(lane < N_EXPERTS, logits, NEG_BIG)
        p = jnp.exp(logits - jnp.max(logits, axis=-1, keepdims=True))
        aff = p / jnp.sum(p, axis=-1, keepdims=True)
        afft_ref[0, :, rs] = aff.T[:N_EXPERTS, :]


def _outproj(o_na, o_gla, x, w1, w2, gm, shf, scf, npost, nfpre, router_cat):
    b, n, d = x.shape
    tm = OUTPROJ_SUBTILES * TOK_TILE
    tokmap = lambda i, j: (i, j, 0)
    smp = lambda i, j: (i, 0, 0)
    cst = lambda i, j: (0, 0)
    return pl.pallas_call(
        _outproj_kernel,
        out_shape=[jax.ShapeDtypeStruct((b, n, d), F32),
                   jax.ShapeDtypeStruct((b, n, d // 2), jnp.int32),
                   jax.ShapeDtypeStruct((b, N_EXPERTS, n), F32)],
        grid=(b, n // tm),
        in_specs=[pl.BlockSpec((1, tm, o_na.shape[2]), tokmap),
                  pl.BlockSpec((1, tm, o_gla.shape[2]), tokmap),
                  pl.BlockSpec((1, tm, d), tokmap),
                  pl.BlockSpec(w1.shape, cst),
                  pl.BlockSpec(w2.shape, cst),
                  pl.BlockSpec((1, 1, d), smp),
                  pl.BlockSpec((1, 1, d), smp),
                  pl.BlockSpec((1, 1, d), smp),
                  pl.BlockSpec((1, d), cst),
                  pl.BlockSpec((1, d), cst),
                  pl.BlockSpec(router_cat.shape, cst)],
        out_specs=[pl.BlockSpec((1, tm, d), tokmap),
                   pl.BlockSpec((1, tm, d // 2), tokmap),
                   pl.BlockSpec((1, N_EXPERTS, tm), lambda i, j: (i, 0, j))],
        compiler_params=_cparams(("arbitrary", "arbitrary")),
        name="outproj",
    )(o_na, o_gla, x, w1, w2, gm, shf, scf, npost.reshape(1, d), nfpre.reshape(1, d), router_cat)


def _route_kernel(afft_ref, rt_ref, *, cap):
    a = afft_ref[...]
    e, n = a.shape
    capf = jnp.float32(cap)

    def search(i, thr_bits):
        cand = thr_bits | lax.shift_left(jnp.int32(1), 30 - i)
        cnt = jnp.sum(jnp.where(a >= lax.bitcast_convert_type(cand, F32), 1.0, 0.0), axis=-1, keepdims=True)
        return jnp.where(cnt >= capf, cand, thr_bits)

    thr_bits = lax.fori_loop(0, 31, search, jnp.zeros((e, 1), jnp.int32))
    thr = lax.bitcast_convert_type(thr_bits, F32)
    need = capf - jnp.sum(jnp.where(a > thr, 1.0, 0.0), axis=-1, keepdims=True)
    r_i = lax.broadcasted_iota(jnp.int32, (LANES, LANES), 0)
    c_i = lax.broadcasted_iota(jnp.int32, (LANES, LANES), 1)
    incl = jnp.where(r_i <= c_i, 1.0, 0.0).astype(BF16)
    off_eq = jnp.zeros((e, 1), F32)
    off_sel = jnp.zeros((e, 1), F32)
    for j in range(n // LANES):
        sl = slice(j * LANES, (j + 1) * LANES)
        a_b = a[:, sl]
        eq_b = jnp.where(a_b == thr, 1.0, 0.0)
        tie_rank = _dot(eq_b.astype(BF16), incl) - eq_b + off_eq
        off_eq = off_eq + jnp.sum(eq_b, axis=-1, keepdims=True)
        sel_b = jnp.where(a_b > thr, 1.0, jnp.where(tie_rank < need, eq_b, 0.0))
        sel = sel_b > 0.5
        rank = _dot(sel_b.astype(BF16), incl) - sel_b + off_sel
        off_sel = off_sel + jnp.sum(sel_b, axis=-1, keepdims=True)
        rsel = jnp.where(sel, rank, -1.0)
        rt_ref[:, sl] = rsel.astype(jnp.int32)


def _route(afft, cap):
    b, e, n = afft.shape
    return pl.pallas_call(
        functools.partial(_route_kernel, cap=cap),
        out_shape=jax.ShapeDtypeStruct((b * e, n), jnp.int32),
        grid=(1,),
        in_specs=[pl.BlockSpec((b * e, n), lambda i: (0, 0))],
        out_specs=pl.BlockSpec((b * e, n), lambda i: (0, 0)),
        compiler_params=_cparams(("arbitrary",)),
        name="route",
    )(afft.reshape(b * e, n)).reshape(b, e, n)


def _sc_gather(rsel_t, hf2, cap):
    b, e, n = rsel_t.shape
    width = hf2.shape[1]
    info = plsc.get_sparse_core_info()
    nc, lanes = info.num_cores, info.num_lanes
    workers = nc * info.num_subcores
    items = b * e
    assert items % workers == 0 and n % lanes == 0 and cap % SC_GATHER_ROWS == 0
    per_worker = items // workers
    mesh = plsc.VectorSubcoreMesh(core_axis_name="c", subcore_axis_name="s")

    def body(rank_hbm, hf_hbm, out_hbm, rank_v, idx_v, rows_a, rows_b, sem_a, sem_b):
        bufs = ((rows_a, sem_a), (rows_b, sem_b))
        wid = lax.axis_index("s") * nc + lax.axis_index("c")
        for k in range(per_worker):
            item = wid * per_worker + k
            base_tok = (item // e) * n
            pltpu.sync_copy(rank_hbm.at[item], rank_v)

            @pl.loop(0, n // lanes)
            def _(j):
                r = rank_v[pl.ds(j * lanes, lanes)]
                tok = lax.iota(jnp.int32, lanes) + (j * lanes + base_tok)
                plsc.store_scatter(idx_v, [r], tok, mask=r >= 0)

            def gather(c):
                buf, sem = bufs[c % 2]
                rows = pl.ds(c * SC_GATHER_ROWS, SC_GATHER_ROWS)
                return pltpu.async_copy(hf_hbm.at[idx_v.at[rows]], buf, sem)

            n_chunks = cap // SC_GATHER_ROWS
            pending = gather(0)
            for c in range(n_chunks):
                nxt = gather(c + 1) if c + 1 < n_chunks else None
                pending.wait()
                pltpu.sync_copy(bufs[c % 2][0],
                                out_hbm.at[pl.ds(item * cap + c * SC_GATHER_ROWS, SC_GATHER_ROWS)])
                pending = nxt

    return pl.kernel(
        body, out_type=jax.ShapeDtypeStruct((items * cap, width), hf2.dtype), mesh=mesh,
        scratch_types=[pltpu.VMEM((n,), jnp.int32), pltpu.VMEM((cap,), jnp.int32),
                       pltpu.VMEM((SC_GATHER_ROWS, width), hf2.dtype),
                       pltpu.VMEM((SC_GATHER_ROWS, width), hf2.dtype),
                       pltpu.SemaphoreType.DMA, pltpu.SemaphoreType.DMA],
        compiler_params=pltpu.CompilerParams(needs_layout_passes=False),
        name="scgather",
    )(rsel_t.reshape(items, n), hf2)


def _ffn_kernel(x_ref, wg_ref, wu_ref, wd_ref, o_ref, acc_ref, xb_ref):
    f = pl.program_id(1)
    b = x_ref.shape[0]
    last = pl.num_programs(1) - 1

    def tile(first, final):
        wg = wg_ref[0].astype(BF16)
        wu = wu_ref[0].astype(BF16)
        wd = wd_ref[0].astype(BF16)
        for i in range(b):
            if first:
                xb_ref[i] = _unpack_bf16_pairs(x_ref[i, 0])
            x = xb_ref[i]
            g = _dot(x, wg)
            u = _dot(x, wu)
            hid = (g / (1.0 + jnp.exp(-g)) * u).astype(BF16)
            y = _dot(hid, wd)
            if not first:
                y = acc_ref[i] + y
            if final:
                o_ref[i, 0] = y.astype(o_ref.dtype)
            else:
                acc_ref[i] = y

    @pl.when(f == 0)
    def _():
        tile(True, False)

    @pl.when((f > 0) & (f < last))
    def _():
        tile(False, False)

    @pl.when(f == last)
    def _():
        tile(False, True)


def _ffn(xs, w_gate, w_up, w_down):
    b, e, cap, dp = xs.shape
    d = 2 * dp
    dff = w_gate.shape[2]
    tf = FFN_TILE
    assert dff // tf >= 2
    return pl.pallas_call(
        _ffn_kernel,
        out_shape=jax.ShapeDtypeStruct((b, e, cap, d), BF16),
        grid=(e, dff // tf),
        in_specs=[pl.BlockSpec((b, 1, cap, dp), lambda i, f: (0, i, 0, 0)),
                  pl.BlockSpec((1, d, tf), lambda i, f: (i, 0, f)),
                  pl.BlockSpec((1, d, tf), lambda i, f: (i, 0, f)),
                  pl.BlockSpec((1, tf, d), lambda i, f: (i, f, 0))],
        out_specs=pl.BlockSpec((b, 1, cap, d), lambda i, f: (0, i, 0, 0)),
        scratch_shapes=[pltpu.VMEM((b, cap, d), F32), pltpu.VMEM((b, cap, d), BF16)],
        compiler_params=_cparams(("arbitrary", "arbitrary")),
        name="ffn",
    )(xs, w_gate, w_up, w_down)


def _combine_kernel(off_ref, ys_ref, rt_ref, afft_ref, xn_ref, gf_ref, npost_ref, o_ref, acc_ref):
    bi, tt = pl.program_id(0), pl.program_id(1)
    ts = COMBINE_TILE
    n_sub = rt_ref.shape[2] // ts
    n_experts, cap = ys_ref.shape[1], ys_ref.shape[2]
    blocks = ts // LANES
    slot = lax.broadcasted_iota(jnp.int32, (COMBINE_WIN, ts), 0)

    for sub in range(n_sub):
        toks = slice(sub * ts, (sub + 1) * ts)
        blk0 = (tt * n_sub + sub) * blocks

        def window(e, w0, j, toks=toks):
            nominal = w0 + j * COMBINE_WIN
            start = pl.multiple_of(jnp.minimum(nominal, cap - COMBINE_WIN), BF16_ROWS)
            rank = rt_ref[0, e:e + 1, toks]
            hit = ((rank - start) == slot) & (rank >= nominal)
            weights = jnp.where(hit, afft_ref[0, e:e + 1, toks], 0.0).astype(BF16)
            return weights, ys_ref[0, e, pl.ds(start, COMBINE_WIN), :]

        first, extra = [], []
        for e in range(n_experts):
            r0 = off_ref[bi, e, blk0]
            r1 = off_ref[bi, e, blk0 + blocks]
            w0 = (r0 // BF16_ROWS) * BF16_ROWS
            first.append(w0)
            extra.append(jnp.maximum((r1 - w0 + COMBINE_WIN - 1) // COMBINE_WIN - 1, 0))
        terms = []
        for e in range(0, n_experts, COMBINE_STACK):
            ws, ys = zip(*[window(e + k, first[e + k], 0) for k in range(COMBINE_STACK)])
            terms.append(_dot(jnp.concatenate(ws, axis=0).T, jnp.concatenate(ys, axis=0)))
        acc_ref[toks] = functools.reduce(lambda a, c: a + c, terms)

        @pl.when(functools.reduce(lambda a, c: a + c, extra) > 0)
        def _(window=window, first=first, extra=extra, toks=toks):
            for e in range(n_experts):
                def more(j, carry, e=e):
                    w, y = window(e, first[e], j)
                    acc_ref[toks] += _dot(w.T, y)
                    return carry
                lax.fori_loop(1, extra[e] + 1, more, 0)

    o_ref[0] = xn_ref[0] + gf_ref[0] * _rms(acc_ref[...], npost_ref[...])


def _combine(ys, slot_off, rsel_t, aff_t, x_new, gf, npost):
    b, e, cap, d = ys.shape
    n = x_new.shape[1]
    tm = COMBINE_SUBTILES * COMBINE_TILE
    tokmap = lambda i, j, off: (i, j, 0)
    return pl.pallas_call(
        _combine_kernel,
        out_shape=jax.ShapeDtypeStruct((b, n, d), F32),
        grid_spec=pltpu.PrefetchScalarGridSpec(
            num_scalar_prefetch=1,
            grid=(b, n // tm),
            in_specs=[pl.BlockSpec((1, e, cap, d), lambda i, j, off: (i, 0, 0, 0)),
                      pl.BlockSpec((1, e, tm), lambda i, j, off: (i, 0, j)),
                      pl.BlockSpec((1, e, tm), lambda i, j, off: (i, 0, j)),
                      pl.BlockSpec((1, tm, d), tokmap),
                      pl.BlockSpec((1, 1, d), lambda i, j, off: (i, 0, 0)),
                      pl.BlockSpec((1, d), lambda i, j, off: (0, 0))],
            out_specs=pl.BlockSpec((1, tm, d), tokmap),
            scratch_shapes=[pltpu.VMEM((tm, d), F32)]),
        compiler_params=_cparams(("arbitrary", "arbitrary")),
        name="combine",
    )(slot_off, ys, rsel_t, aff_t, x_new, gf, npost.reshape(1, d))


def kernel(x, c, ctx, c_ctx, w_mod, b_mod, norm_mix_pre, norm_mix_post, norm_ffn_pre, norm_ffn_post,
           w_in, na_rpb, gla_a_up, gla_a_bias, gla_norm, w_out, router, w_gate, w_up, w_down):
    b, n, d = x.shape
    assert w_mod.shape[0] == 1 and d == D_MODEL and n % (GRID_W * NA_QROWS) == 0 and n % GLA_T == 0
    assert ctx.shape[1] == GLA_T
    rows = n // GRID_W
    cap = EC_CAPACITY_FACTOR * n // N_EXPERTS

    c8 = jnp.concatenate([c, c_ctx[None, :], jnp.zeros((8 - b - 1, d), F32)], axis=0)
    mod = _mod(c8, w_mod[0], b_mod[0])
    sh_m, sc_m, g_m, sh_f, sc_f, g_f = [m[:b, None, :] for m in jnp.split(mod, 6, axis=-1)]
    sh_c, sc_c = mod[b:b + 1, None, :d], mod[b:b + 1, None, d:2 * d]

    cuts = [int(c) for c in np.cumsum([0, NA_W, NA_W, GLA_KEY_W, GLA_VAL_W, 2 * GLA_GATE_RANK, NA_W, GLA_KEY_W, GLA_VAL_W])]
    c_nak, c_nav, c_gk, c_gv, c_ad, c_naq, c_gq, c_gg = [[(cuts[i], cuts[i + 1])] for i in range(8)]
    c_ad3 = c_ad * 3
    rope = _rope_tables(n)
    na_q, na_k, na_v, gq, gk, gv, ad, gg = _inproj(
        x, sh_m, sc_m, norm_mix_pre[0], w_in[0],
        [c_naq, c_nak, c_nav, c_gq, c_gk, c_gv, c_ad3, c_gg],
        [NA_W, NA_W, NA_W, GLA_KEY_W, GLA_KEY_W, GLA_VAL_W, LANES, GLA_VAL_W],
        ["plain", "plain", "plain", "rope", "rope", "plain", "split", "plain"],
        [1.0, 1.0, 1.0, GLA_DK ** -0.5, 1.0, 1.0, 1.0, 1.0],
        [BF16] * 8, rope=rope)
    c_nak, c_nav, c_gk, c_gv, c_ad = _inproj(
        ctx, sh_c, sc_c, norm_mix_pre[0], w_in[0],
        [c_nak, c_nav, c_gk, c_gv, c_ad3],
        [NA_W, NA_W, GLA_KEY_W, GLA_VAL_W, LANES],
        ["plain", "plain", "plain", "plain", "split"], [1.0] * 5, [BF16] * 5)

    o_na = _na(na_q, na_k, na_v, c_nak, c_nav, *_na_bias_tables(na_rpb[0], rows))

    cmats = jnp.asarray(_gla_prefix_matrices(GLA_T), BF16)
    o_gla = _gla(gq, gk, gv, ad, gg, c_gk, c_gv, c_ad, _gla_decay_up(gla_a_up[0]),
                 gla_a_bias[0][:, None, :], gla_norm[0][None, :], cmats)

    wo = w_out[0].astype(BF16)
    router_pad = jnp.zeros((d, LANES), F32).at[:, :N_EXPERTS].set(router[0])
    x_new, hf, aff_t = _outproj(o_na, o_gla, x, wo[:NA_W], wo[NA_W:], g_m, sh_f, sc_f,
                                     norm_mix_post[0], norm_ffn_pre[0],
                                     jnp.concatenate(_split_bf16(router_pad), axis=1))

    rsel_t = _route(aff_t, cap)
    xs = _sc_gather(rsel_t, hf.reshape(b * n, d // 2), cap).reshape(b, N_EXPERTS, cap, d // 2)
    ys = _ffn(xs, w_gate[0], w_up[0], w_down[0])
    picked = (rsel_t >= 0).astype(jnp.int32).reshape(b, N_EXPERTS, n // LANES, LANES).sum(axis=-1)
    slot_off = jnp.concatenate([jnp.zeros((b, N_EXPERTS, 1), jnp.int32), jnp.cumsum(picked, axis=-1)], axis=-1)
    return _combine(ys, slot_off, rsel_t, aff_t, x_new, g_f, norm_ffn_post[0])
```

```python
import functools

import numpy as np
import jax
import jax.numpy as jnp
from jax import lax
from jax.experimental import pallas as pl
from jax.experimental.pallas import tpu as pltpu
from jax.experimental.pallas import tpu_sc as plsc

F32 = jnp.float32
BF16 = jnp.bfloat16
HIGHEST = lax.Precision.HIGHEST

D_MODEL = 1024
GRID_W = 64
NA_W = 512
NA_HEADS = 8
NA_HEAD_DIM = 64
NA_WIN_ROWS = 8
NA_WIN_COLS = 16
GLA_HEADS = 4
GLA_DV = 128
GLA_DK = 64
GLA_KEY_W = 256
GLA_VAL_W = 512
GLA_GATE_RANK = 16
GLA_GATE_TAU = 16.0
ROPE_BASE = 10000.0
N_EXPERTS = 16
EC_CAPACITY_FACTOR = 2
NORM_EPS = 1e-6
NEG_BIG = -1e30

LANES = 128
BF16_ROWS = 16
VMEM_LIMIT = 56 * 1024 * 1024

TOK_TILE = 512
NA_QROWS = 4
NA_KROWS = NA_QROWS + NA_WIN_ROWS
NA_TAB_PAD = NA_QROWS
NA_TAB_BLOCKS = NA_TAB_PAD + 2 * NA_WIN_ROWS - 1 + NA_QROWS + 1
GLA_T = 256
GLA_LEVELS = (128, 64, 32, 16)
GLA_DIAG = 16
GLA_UNROLL = 4
OUTPROJ_SUBTILES = 2
FFN_TILE = 256
COMBINE_TILE = 256
COMBINE_SUBTILES = 4
COMBINE_WIN = 64
COMBINE_STACK = 4
SC_GATHER_ROWS = 64


def _cparams(sem):
    return pltpu.CompilerParams(dimension_semantics=sem, vmem_limit_bytes=VMEM_LIMIT)


def _rms(v, g):
    return v * lax.rsqrt(jnp.mean(v * v, axis=-1, keepdims=True) + NORM_EPS) * g


def _dot(a, b):
    return jnp.dot(a, b, preferred_element_type=F32)


def _dot_nt(a, b):
    return lax.dot_general(a, b, (((1,), (1,)), ((), ())), preferred_element_type=F32)


def _split_bf16(v):
    hi = v.astype(BF16)
    return hi, (v - hi.astype(F32)).astype(BF16)


_HIGH_HALF = -65536


def _pack_bf16_pairs(hb):
    bits = lax.bitcast_convert_type(hb.astype(F32), jnp.int32)
    half = hb.shape[1] // 2
    return lax.shift_right_logical(bits[:, :half], 16) | (bits[:, half:] & _HIGH_HALF)


def _unpack_bf16_pairs(w):
    lo = lax.bitcast_convert_type(lax.shift_left(w, 16), F32).astype(BF16)
    hi = lax.bitcast_convert_type(w & _HIGH_HALF, F32).astype(BF16)
    return jnp.concatenate([lo, hi], axis=1)


def _mod_kernel(c_ref, w_ref, b_ref, o_ref):
    c = c_ref[...]
    s = c / (1.0 + jnp.exp(-c))
    o_ref[...] = _dot(s.astype(BF16), w_ref[...].astype(BF16)) + b_ref[...]


def _mod(c8, w_mod, b_mod):
    d, n = w_mod.shape
    tn = 1536
    return pl.pallas_call(
        _mod_kernel,
        out_shape=jax.ShapeDtypeStruct((8, n), F32),
        grid=(n // tn,),
        in_specs=[pl.BlockSpec((8, d), lambda j: (0, 0)),
                  pl.BlockSpec((d, tn), lambda j: (0, j)),
                  pl.BlockSpec((1, tn), lambda j: (0, j))],
        out_specs=pl.BlockSpec((8, tn), lambda j: (0, j)),
        compiler_params=_cparams(("arbitrary",)),
        name="mod",
    )(c8, w_mod, b_mod.reshape(1, n))


def _inproj_kernel(x_ref, sh_ref, sc_ref, g_ref, *refs, modes, scales, col_segments):
    n_rope = 2 if "rope" in modes else 0
    rope_refs, refs = refs[:n_rope], refs[n_rope:]
    n_w = len(modes)
    w_ref, o_refs, wb_refs = refs[0], refs[1:1 + n_w], refs[1 + n_w:]

    @pl.when((pl.program_id(0) == 0) & (pl.program_id(1) == 0))
    def _():
        for segments, wb_ref in zip(col_segments, wb_refs):
            off = 0
            for lo, hi in segments:
                wb_ref[:, off:off + hi - lo] = w_ref[:, lo:hi].astype(BF16)
                off += hi - lo
            if off < wb_ref.shape[1]:
                wb_ref[:, off:] = jnp.zeros((wb_ref.shape[0], wb_ref.shape[1] - off), BF16)

    x = x_ref[0]
    h = _rms(x, g_ref[...]) * (1.0 + sc_ref[0]) + sh_ref[0]
    hb = h.astype(BF16)
    for w_ref, o_ref, mode, scale in zip(wb_refs, o_refs, modes, scales):
        y = _dot(hb, w_ref[...])
        if mode == "rope":
            cols = y.shape[1]
            quarter = GLA_DK // 4
            lane = lax.broadcasted_iota(jnp.int32, y.shape, 1)
            partner = jnp.where((lane & (2 * quarter - 1)) < quarter,
                                pltpu.roll(y, cols - quarter, axis=1), pltpu.roll(y, quarter, axis=1))
            y = (y * rope_refs[0][...] + partner * rope_refs[1][...]) * scale
        elif mode == "split":
            hi, lo = _split_bf16(y)
            lane = lax.broadcasted_iota(jnp.int32, y.shape, 1)
            rank2 = 2 * GLA_GATE_RANK
            y = jnp.where((lane >= rank2) & (lane < 2 * rank2), lo, hi)
        o_ref[0] = y.astype(o_ref.dtype)


def _inproj(x, shift, scale, g, w_all, col_segments, widths, modes, scales, out_dtypes, rope=None):
    b, n, d = x.shape
    tm = min(TOK_TILE, n)
    per_sample = shift.shape[0] == b
    mod_map = (lambda i, j: (i, 0, 0)) if per_sample else (lambda i, j: (0, 0, 0))
    in_specs = [pl.BlockSpec((1, tm, d), lambda i, j: (i, j, 0)),
                pl.BlockSpec((1, 1, d), mod_map),
                pl.BlockSpec((1, 1, d), mod_map),
                pl.BlockSpec((1, d), lambda i, j: (0, 0))]
    args = [x, shift, scale, g.reshape(1, d)]
    if rope is not None:
        in_specs += [pl.BlockSpec((tm, rope[0].shape[1]), lambda i, j: (j, 0))] * 2
        args += list(rope)
    in_specs.append(pl.BlockSpec(w_all.shape, lambda i, j: (0, 0), pipeline_mode=pl.Buffered(1)))
    out_shapes, out_specs = [], []
    for cols, dt in zip(widths, out_dtypes):
        out_shapes.append(jax.ShapeDtypeStruct((b, n, cols), dt))
        out_specs.append(pl.BlockSpec((1, tm, cols), lambda i, j: (i, j, 0)))
    return pl.pallas_call(
        functools.partial(_inproj_kernel, modes=tuple(modes), scales=tuple(scales),
                          col_segments=tuple(tuple(seg) for seg in col_segments)),
        out_shape=out_shapes,
        grid=(b, n // tm),
        in_specs=in_specs,
        out_specs=out_specs,
        scratch_shapes=[pltpu.VMEM((d, cols), BF16) for cols in widths],
        compiler_params=_cparams(("arbitrary", "arbitrary")),
        name="inproj",
    )(*args, w_all)


def _rope_tables(n):
    t = np.arange(n)
    pos_row, pos_col = (t // GRID_W).astype(np.float32), (t % GRID_W).astype(np.float32)
    quarter = GLA_DK // 4
    freqs = np.float32(ROPE_BASE) ** (-np.arange(quarter, dtype=np.float32) / quarter)
    ang_r = pos_row[:, None] * freqs
    ang_c = pos_col[:, None] * freqs
    cos = np.concatenate([np.cos(ang_r), np.cos(ang_r), np.cos(ang_c), np.cos(ang_c)], axis=-1)
    sin = np.concatenate([-np.sin(ang_r), np.sin(ang_r), -np.sin(ang_c), np.sin(ang_c)], axis=-1)
    return jnp.asarray(np.tile(cos, (1, GLA_HEADS))), jnp.asarray(np.tile(sin, (1, GLA_HEADS)))


def _na_patterns(rows):
    kr = min(NA_WIN_ROWS, rows)
    n_blocks = rows // NA_QROWS
    pats = []
    for blk in (0, 1, n_blocks - 1):
        r0 = blk * NA_QROWS
        k0 = int(np.clip(r0 - kr // 2, 0, rows - NA_KROWS))
        strips = []
        for a in range(NA_QROWS):
            r_start = int(np.clip(r0 + a - kr // 2, 0, rows - kr))
            start = k0 - (r0 + a) + NA_WIN_ROWS - 1 + NA_TAB_PAD
            assert 0 <= start and start + NA_KROWS <= NA_TAB_BLOCKS
            strips.append((start, [r_start <= k0 + c < r_start + kr for c in range(NA_KROWS)]))
        pats.append(strips)
    return pats


def _na_bias_tables(rpb, rows):
    heads = rpb.shape[0]
    col = np.arange(GRID_W)
    c_start = np.clip(col - NA_WIN_COLS // 2, 0, GRID_W - NA_WIN_COLS)
    col_ok = (col[None, :] >= c_start[:, None]) & (col[None, :] < c_start[:, None] + NA_WIN_COLS)
    dc = np.clip(col[None, :] - col[:, None] + NA_WIN_COLS - 1, 0, 2 * NA_WIN_COLS - 2)
    sel_c = (np.arange(2 * NA_WIN_COLS - 1)[:, None, None] == dc[None]) & col_ok[None]
    t = jnp.einsum("hrd,dqk->hqrk", rpb, jnp.asarray(sel_c, F32), precision=HIGHEST)
    t = jnp.where(jnp.asarray(col_ok)[None, :, None, :], t, NEG_BIG)
    n_dr = 2 * NA_WIN_ROWS - 1
    t = t.reshape(heads, GRID_W, n_dr * GRID_W)
    back = NA_TAB_BLOCKS + 1 - NA_TAB_PAD - n_dr
    t = jnp.pad(t, ((0, 0), (0, 0), (NA_TAB_PAD * GRID_W, back * GRID_W)), constant_values=NEG_BIG)
    width = NA_TAB_BLOCKS * GRID_W
    tab = jnp.stack([t[:, :, :width], t[:, :, GRID_W:GRID_W + width]], axis=1)
    row_mask = np.zeros((3, NA_QROWS, 1, NA_KROWS * GRID_W), np.float32)
    for pat, strips in enumerate(_na_patterns(rows)):
        for a, (_, valid) in enumerate(strips):
            row_mask[pat, a, 0] = np.repeat(np.where(valid, 0.0, NEG_BIG), GRID_W)
    return tab, jnp.asarray(row_mask)


def _na_kernel(q_ref, k_ref, v_ref, kc_ref, vc_ref, tab_ref, rmask_ref, o_ref, bias_ref,
               sw_ref, sc_ref, pw_ref, pc_ref, *, rows):
    nq, nk = NA_QROWS * GRID_W, NA_KROWS * GRID_W
    n_blocks = rows // NA_QROWS
    kr = min(NA_WIN_ROWS, rows)
    scale = NA_HEAD_DIM ** -0.5
    lane = lax.broadcasted_iota(jnp.int32, (nq, LANES), 1)
    first_head = lane < NA_HEAD_DIM
    kc = kc_ref[0]
    vc = vc_ref[0]
    lane_w = lax.broadcasted_iota(jnp.int32, (nk, LANES), 1)
    lane_c = lax.broadcasted_iota(jnp.int32, vc.shape, 1)
    @pl.when(pl.program_id(1) == 0)
    def _():
        for pat, strips in enumerate(_na_patterns(rows)):
            for a, (start, _) in enumerate(strips):
                parity = start % 2
                off = (start - parity) * GRID_W
                for h in range(2):
                    bias_ref[0, pat, h, a * GRID_W:(a + 1) * GRID_W, :] = (
                        tab_ref[h, parity, :, off:off + nk] + rmask_ref[pat, a])

    def key_start(i):
        return pl.multiple_of(jnp.clip(i * NA_QROWS - kr // 2, 0, rows - NA_KROWS) * GRID_W, GRID_W)

    def scores(i, slot):
        pat = jnp.where(i == 0, 0, jnp.where(i == n_blocks - 1, 2, 1))
        q = q_ref[0, pl.ds(pl.multiple_of(i * nq, nq), nq), :] * scale
        kw = k_ref[0, pl.ds(key_start(i), nk), :]
        for h in range(2):
            qh = jnp.where(first_head if h == 0 else jnp.logical_not(first_head), q, jnp.zeros_like(q))
            sw_ref[slot, h] = _dot_nt(qh, kw) + bias_ref[0, pat, h]
            sc_ref[slot, h] = _dot_nt(qh, kc)

    def softmax(slot):
        for h in range(2):
            s_w = sw_ref[slot, h]
            s_c = sc_ref[slot, h]
            m = jnp.maximum(jnp.max(s_w, axis=-1, keepdims=True), jnp.max(s_c, axis=-1, keepdims=True))
            pw_ref[slot, h] = jnp.exp((s_w - m).astype(BF16))
            pc_ref[slot, h] = jnp.exp((s_c - m).astype(BF16))

    def values(i, slot):
        vw = v_ref[0, pl.ds(key_start(i), nk), :]
        outs = []
        for h in range(2):
            sum_lane = NA_HEAD_DIM * (1 - h)
            vw_h = jnp.where(lane_w == sum_lane, jnp.ones_like(vw), vw)
            vc_h = jnp.where(lane_c == sum_lane, jnp.ones_like(vc), vc)
            o = _dot(pw_ref[slot, h], vw_h) + _dot(pc_ref[slot, h], vc_h)
            outs.append(o * (1.0 / o[:, sum_lane:sum_lane + 1]))
        o = jnp.where(first_head, outs[0], outs[1])
        o_ref[0, pl.ds(pl.multiple_of(i * nq, nq), nq), :] = o.astype(o_ref.dtype)

    assert n_blocks % 2 == 0 and n_blocks >= 4
    scores(0, 0)
    softmax(0)
    scores(1, 1)

    def trip(j, carry):
        i = 2 * j
        values(i - 2, 0)
        softmax(1)
        scores(i, 0)
        values(i - 1, 1)
        softmax(0)
        scores(i + 1, 1)
        return carry

    lax.fori_loop(1, n_blocks // 2, trip, 0)
    values(n_blocks - 2, 0)
    softmax(1)
    values(n_blocks - 1, 1)


def _na(q, k, v, kc, vc, tab, row_mask):
    b, n, w = q.shape
    n_ctx = kc.shape[1]
    pairs = w // LANES
    rows = n // GRID_W
    nq, nk = NA_QROWS * GRID_W, NA_KROWS * GRID_W
    tok = lambda p, i: (i, 0, p)
    return pl.pallas_call(
        functools.partial(_na_kernel, rows=rows),
        out_shape=jax.ShapeDtypeStruct((b, n, w), BF16),
        grid=(pairs, b),
        in_specs=[pl.BlockSpec((1, n, LANES), tok),
                  pl.BlockSpec((1, n, LANES), tok),
                  pl.BlockSpec((1, n, LANES), tok),
                  pl.BlockSpec((1, n_ctx, LANES), tok),
                  pl.BlockSpec((1, n_ctx, LANES), tok),
                  pl.BlockSpec((2,) + tab.shape[1:], lambda p, i: (p, 0, 0, 0)),
                  pl.BlockSpec(row_mask.shape, lambda p, i: (0, 0, 0, 0))],
        out_specs=pl.BlockSpec((1, n, LANES), tok),
        scratch_shapes=[pltpu.VMEM((1, 3, 2, nq, nk), F32),
                        pltpu.VMEM((2, 2, nq, nk), F32), pltpu.VMEM((2, 2, nq, n_ctx), F32),
                        pltpu.VMEM((2, 2, nq, nk), BF16), pltpu.VMEM((2, 2, nq, n_ctx), BF16)],
        compiler_params=_cparams(("arbitrary", "arbitrary")),
        name="na",
    )(q, k, v, kc, vc, tab, row_mask)


def _gla_prefix_matrices(t):
    i = np.arange(t)
    return np.stack([i[:, None] >= i[None, :], i[:, None] <= i[None, :]]).astype(np.float32)


def _gla_kernel(q_ref, k_ref, v_ref, ad_ref, g_ref, ck_ref, cv_ref, cad_ref,
                u_ref, ab_ref, gn_ref, cm_ref, o_ref, accf_ref, accb_ref, *, n_tok):
    t = GLA_T
    n_chunks = n_tok // t
    row = lax.broadcasted_iota(jnp.int32, (t, LANES), 0)
    hb = t // 2
    row_b = lax.broadcasted_iota(jnp.int32, (hb, LANES), 0)
    head0_b = lax.broadcasted_iota(jnp.int32, (hb, LANES), 1) < GLA_DK
    row2 = lax.broadcasted_iota(jnp.int32, (hb, 2 * hb), 0)
    col2 = lax.broadcasted_iota(jnp.int32, (hb, 2 * hb), 1) & (hb - 1)
    vrow = lax.broadcasted_iota(jnp.int32, (2 * t, 2 * GLA_DV), 0)
    vlane = lax.broadcasted_iota(jnp.int32, (2 * t, 2 * GLA_DV), 1)
    v_head_match = (vrow >= t) == (vlane >= GLA_DV)
    srow = lax.broadcasted_iota(jnp.int32, (2 * GLA_DV, LANES), 0)
    slane = lax.broadcasted_iota(jnp.int32, (2 * GLA_DV, LANES), 1)
    s_blockdiag = (srow >= GLA_DV) == (slane >= GLA_DK)
    blk_mask = {half: jnp.where((row2 & ~(2 * half - 1)) == (col2 & ~(2 * half - 1)), 1.0, 0.0)
                for half in GLA_LEVELS if 2 * half < hb}
    diag_blk = (row2 & ~(GLA_DIAG - 1)) == (col2 & ~(GLA_DIAG - 1))
    diag_mask = (jnp.where(diag_blk & (row2 >= col2), 1.0, 0.0), jnp.where(diag_blk & (row2 <= col2), 1.0, 0.0))

    def prefix_sums(ad, dirn):
        z = _dot(ad, u_ref[dirn]) + ab_ref[dirn]
        loga = (jnp.minimum(z, 0.0) - jnp.log(1.0 + jnp.exp(-jnp.abs(z)))) * (1.0 / GLA_GATE_TAU)
        hi, lo = _split_bf16(loga)
        p2 = _dot(cm_ref[dirn], jnp.concatenate([hi, lo], axis=-1))
        return p2[:, :LANES] + p2[:, LANES:]

    def chunk_end(p, dirn):
        return p[t - 1:t, :] if dirn == 0 else p[0:1, :]

    def level_sums(p, half, dirn):
        blk = 2 * half
        p3 = p.reshape(t // blk, blk, LANES)
        edge = half - 1 if dirn == 0 else half
        ref = jnp.broadcast_to(p3[:, edge:edge + 1, :], p3.shape).reshape(t, LANES)
        later = (row & half) != 0
        return jnp.where(later == (dirn == 0), p - ref, ref - p)

    def state_update(s, k, vt, p, dirn):
        kh = (k * jnp.exp(chunk_end(p, dirn) - p)).astype(BF16)
        return s * jnp.exp(chunk_end(p, dirn)) + jnp.where(s_blockdiag, _dot(vt, kh), 0.0)

    def chunk(tok0, s, dirn):
        q = q_ref[0, pl.ds(tok0, t), :].astype(F32)
        k = k_ref[0, pl.ds(tok0, t), :].astype(F32)
        v = v_ref[0, pl.ds(tok0, t), :]
        vt = v.T
        p = prefix_sums(ad_ref[0, pl.ds(tok0, t), :], dirn)
        qh = (q * jnp.exp(p)).astype(BF16)
        o = _dot_nt(qh, s.astype(BF16))
        def rows(x, b):
            return x[b * hb:(b + 1) * hb]

        def scores(qt, kt):
            kcat = jnp.concatenate([jnp.where(head0_b, kt, 0.0), jnp.where(head0_b, 0.0, kt)], axis=0)
            return _dot_nt(qt.astype(BF16), kcat.astype(BF16))

        assert GLA_LEVELS[0] == hb
        w = jnp.exp(level_sums(p, hb, dirn))
        qb, kb = (1, 0) if dirn == 0 else (0, 1)
        wide = scores(rows(q, qb) * rows(w, qb), rows(k, kb) * rows(w, kb))
        fine_w = [jnp.exp(level_sums(p, half, dirn)) for half in GLA_LEVELS[1:]]
        e_d = level_sums(p, GLA_DIAG // 2, dirn)
        w_d, wi_d = jnp.exp(e_d), jnp.exp(-e_d)
        fine = []
        for b in range(2):
            qs, ks = rows(q, b), rows(k, b)
            acc = None
            for half, w in zip(GLA_LEVELS[1:], fine_w):
                later = (row_b & half) != 0
                q_side = later if dirn == 0 else jnp.logical_not(later)
                part = scores(jnp.where(q_side, qs * rows(w, b), 0.0), jnp.where(q_side, 0.0, ks * rows(w, b)))
                if half in blk_mask:
                    part = part * blk_mask[half]
                acc = part if acc is None else acc + part
            later = (row_b & (GLA_DIAG // 2)) != 0
            shrink_q = later if dirn == 0 else jnp.logical_not(later)
            part = scores(qs * jnp.where(shrink_q, rows(w_d, b), rows(wi_d, b)),
                          ks * jnp.where(shrink_q, rows(wi_d, b), rows(w_d, b)))
            fine.append(acc + jnp.where(diag_mask[dirn] > 0.5, part, 0.0))
        zero = jnp.zeros((hb, hb), F32)
        h0, h1 = slice(0, hb), slice(hb, 2 * hb)
        if dirn == 0:
            top = [fine[0][:, h0], zero, fine[0][:, h1], zero]
            bot = [wide[:, h0], fine[1][:, h0], wide[:, h1], fine[1][:, h1]]
        else:
            top = [fine[0][:, h0], wide[:, h0], fine[0][:, h1], wide[:, h1]]
            bot = [zero, fine[1][:, h0], zero, fine[1][:, h1]]
        a = jnp.concatenate([jnp.concatenate(top, axis=1), jnp.concatenate(bot, axis=1)], axis=0)
        vcat = jnp.concatenate([v, v], axis=0)
        vcat = jnp.where(v_head_match, vcat, jnp.zeros_like(vcat))
        o = o + _dot(a.astype(BF16), vcat)
        return o, state_update(s, k, vt, p, dirn)

    def ctx_state(dirn):
        p = prefix_sums(cad_ref[0], dirn)
        s0 = jnp.zeros((2 * GLA_DV, LANES), F32)
        return state_update(s0, ck_ref[0].astype(F32), cv_ref[0].T, p, dirn)

    def finish(tok0, o):
        g = g_ref[0, pl.ds(tok0, t), :].astype(F32)
        gate = g / (1.0 + jnp.exp(-g))
        halves = [_rms(o[:, h * GLA_DV:(h + 1) * GLA_DV], gn_ref[...]) for h in range(2)]
        o_ref[0, pl.ds(tok0, t), :] = (jnp.concatenate(halves, axis=-1) * gate).astype(o_ref.dtype)

    def body(i, carry, second_half):
        s_f, s_b = carry
        for u in range(GLA_UNROLL):
            c = i * GLA_UNROLL + u
            tok_f = pl.multiple_of(c * t, t)
            tok_b = pl.multiple_of((n_chunks - 1 - c) * t, t)
            o_f, s_f = chunk(tok_f, s_f, 0)
            o_b, s_b = chunk(tok_b, s_b, 1)
            if second_half:
                finish(tok_f, o_f + accb_ref[pl.ds(tok_f, t), :])
                finish(tok_b, o_b + accf_ref[pl.ds(tok_b, t), :])
            else:
                accf_ref[pl.ds(tok_f, t), :] = o_f
                accb_ref[pl.ds(tok_b, t), :] = o_b
        return s_f, s_b

    trips = n_chunks // GLA_UNROLL
    assert n_chunks % (2 * GLA_UNROLL) == 0
    carry = lax.fori_loop(0, trips // 2, functools.partial(body, second_half=False), (ctx_state(0), ctx_state(1)))
    lax.fori_loop(trips // 2, trips, functools.partial(body, second_half=True), carry)


def _gla(q, k, v, ad, g, ck, cv, cad, u, abias, gnorm, cmats):
    b, n, kw = q.shape
    n_ctx = ck.shape[1]
    pairs = kw // LANES
    vw = 2 * GLA_DV
    tok = lambda i, p: (i, 0, p)
    full3 = lambda i, p: (i, 0, 0)
    return pl.pallas_call(
        functools.partial(_gla_kernel, n_tok=n),
        out_shape=jax.ShapeDtypeStruct((b, n, v.shape[2]), BF16),
        grid=(b, pairs),
        in_specs=[pl.BlockSpec((1, n, LANES), tok),
                  pl.BlockSpec((1, n, LANES), tok),
                  pl.BlockSpec((1, n, vw), tok),
                  pl.BlockSpec((1, n, LANES), full3),
                  pl.BlockSpec((1, n, vw), tok),
                  pl.BlockSpec((1, n_ctx, LANES), tok),
                  pl.BlockSpec((1, n_ctx, vw), tok),
                  pl.BlockSpec((1, n_ctx, LANES), full3),
                  pl.BlockSpec((2, LANES, LANES), lambda i, p: (0, 0, p)),
                  pl.BlockSpec((2, 1, LANES), lambda i, p: (0, 0, p)),
                  pl.BlockSpec((1, GLA_DV), lambda i, p: (0, 0)),
                  pl.BlockSpec(cmats.shape, lambda i, p: (0, 0, 0))],
        out_specs=pl.BlockSpec((1, n, vw), tok),
        scratch_shapes=[pltpu.VMEM((n, vw), F32), pltpu.VMEM((n, vw), F32)],
        compiler_params=_cparams(("arbitrary", "arbitrary")),
        name="gla",
    )(q, k, v, ad, g, ck, cv, cad, u, abias, gnorm, cmats)


def _gla_decay_up(a_up):
    r = GLA_GATE_RANK
    pad = jnp.zeros((2, 2 * r, GLA_KEY_W), F32)
    pad = pad.at[0, :r].set(a_up[0]).at[1, r:].set(a_up[1])
    hi, lo = _split_bf16(pad)
    return jnp.concatenate([hi, hi, lo, jnp.zeros_like(hi)], axis=1)


def _outproj_kernel(ona_ref, ogla_ref, x_ref, w1_ref, w2_ref, gm_ref, shf_ref, scf_ref, npost_ref,
                    nfpre_ref, rt_ref, xnew_ref, hf_ref, afft_ref):
    for r0 in range(0, x_ref.shape[1], TOK_TILE):
        rs = slice(r0, r0 + TOK_TILE)
        mix = _dot(ona_ref[0, rs], w1_ref[...]) + _dot(ogla_ref[0, rs], w2_ref[...])
        xn = x_ref[0, rs] + gm_ref[0] * _rms(mix, npost_ref[...])
        xnew_ref[0, rs] = xn
        h = _rms(xn, nfpre_ref[...]) * (1.0 + scf_ref[0]) + shf_ref[0]
        h_hi, h_lo = _split_bf16(h)
        hf_ref[0, rs] = _pack_bf16_pairs(h_hi)
        res = _dot(h_hi, rt_ref[...])
        logits = res[:, :LANES] + res[:, LANES:] + _dot(h_lo, rt_ref[:, :LANES])
        lane = lax.broadcasted_iota(jnp.int32, logits.shape, 1)
        logits = jnp.where(lane < N_EXPERTS, logits, NEG_BIG)
        p = jnp.exp(logits - jnp.max(logits, axis=-1, keepdims=True))
        aff = p / jnp.sum(p, axis=-1, keepdims=True)
        afft_ref[0, :, rs] = aff.T[:N_EXPERTS, :]


def _outproj(o_na, o_gla, x, w1, w2, gm, shf, scf, npost, nfpre, router_cat):
    b, n, d = x.shape
    tm = OUTPROJ_SUBTILES * TOK_TILE
    tokmap = lambda i, j: (i, j, 0)
    smp = lambda i, j: (i, 0, 0)
    cst = lambda i, j: (0, 0)
    return pl.pallas_call(
        _outproj_kernel,
        out_shape=[jax.ShapeDtypeStruct((b, n, d), F32),
                   jax.ShapeDtypeStruct((b, n, d // 2), jnp.int32),
                   jax.ShapeDtypeStruct((b, N_EXPERTS, n), F32)],
        grid=(b, n // tm),
        in_specs=[pl.BlockSpec((1, tm, o_na.shape[2]), tokmap),
                  pl.BlockSpec((1, tm, o_gla.shape[2]), tokmap),
                  pl.BlockSpec((1, tm, d), tokmap),
                  pl.BlockSpec(w1.shape, cst),
                  pl.BlockSpec(w2.shape, cst),
                  pl.BlockSpec((1, 1, d), smp),
                  pl.BlockSpec((1, 1, d), smp),
                  pl.BlockSpec((1, 1, d), smp),
                  pl.BlockSpec((1, d), cst),
                  pl.BlockSpec((1, d), cst),
                  pl.BlockSpec(router_cat.shape, cst)],
        out_specs=[pl.BlockSpec((1, tm, d), tokmap),
                   pl.BlockSpec((1, tm, d // 2), tokmap),
                   pl.BlockSpec((1, N_EXPERTS, tm), lambda i, j: (i, 0, j))],
        compiler_params=_cparams(("arbitrary", "arbitrary")),
        name="outproj",
    )(o_na, o_gla, x, w1, w2, gm, shf, scf, npost.reshape(1, d), nfpre.reshape(1, d), router_cat)


def _route_kernel(afft_ref, rt_ref, *, cap):
    a = afft_ref[...]
    e, n = a.shape
    capf = jnp.float32(cap)

    def search(i, thr_bits):
        cand = thr_bits | lax.shift_left(jnp.int32(1), 30 - i)
        cnt = jnp.sum(jnp.where(a >= lax.bitcast_convert_type(cand, F32), 1.0, 0.0), axis=-1, keepdims=True)
        return jnp.where(cnt >= capf, cand, thr_bits)

    thr_bits = lax.fori_loop(0, 31, search, jnp.zeros((e, 1), jnp.int32))
    thr = lax.bitcast_convert_type(thr_bits, F32)
    need = capf - jnp.sum(jnp.where(a > thr, 1.0, 0.0), axis=-1, keepdims=True)
    r_i = lax.broadcasted_iota(jnp.int32, (LANES, LANES), 0)
    c_i = lax.broadcasted_iota(jnp.int32, (LANES, LANES), 1)
    incl = jnp.where(r_i <= c_i, 1.0, 0.0).astype(BF16)
    off_eq = jnp.zeros((e, 1), F32)
    off_sel = jnp.zeros((e, 1), F32)
    for j in range(n // LANES):
        sl = slice(j * LANES, (j + 1) * LANES)
        a_b = a[:, sl]
        eq_b = jnp.where(a_b == thr, 1.0, 0.0)
        tie_rank = _dot(eq_b.astype(BF16), incl) - eq_b + off_eq
        off_eq = off_eq + jnp.sum(eq_b, axis=-1, keepdims=True)
        sel_b = jnp.where(a_b > thr, 1.0, jnp.where(tie_rank < need, eq_b, 0.0))
        sel = sel_b > 0.5
        rank = _dot(sel_b.astype(BF16), incl) - sel_b + off_sel
        off_sel = off_sel + jnp.sum(sel_b, axis=-1, keepdims=True)
        rsel = jnp.where(sel, rank, -1.0)
        rt_ref[:, sl] = rsel.astype(jnp.int32)


def _route(afft, cap):
    b, e, n = afft.shape
    return pl.pallas_call(
        functools.partial(_route_kernel, cap=cap),
        out_shape=jax.ShapeDtypeStruct((b * e, n), jnp.int32),
        grid=(1,),
        in_specs=[pl.BlockSpec((b * e, n), lambda i: (0, 0))],
        out_specs=pl.BlockSpec((b * e, n), lambda i: (0, 0)),
        compiler_params=_cparams(("arbitrary",)),
        name="route",
    )(afft.reshape(b * e, n)).reshape(b, e, n)


def _sc_gather(rsel_t, hf2, cap):
    b, e, n = rsel_t.shape
    width = hf2.shape[1]
    info = plsc.get_sparse_core_info()
    nc, lanes = info.num_cores, info.num_lanes
    workers = nc * info.num_subcores
    items = b * e
    assert items % workers == 0 and n % lanes == 0 and cap % SC_GATHER_ROWS == 0
    per_worker = items // workers
    mesh = plsc.VectorSubcoreMesh(core_axis_name="c", subcore_axis_name="s")

    def body(rank_hbm, hf_hbm, out_hbm, rank_v, idx_v, rows_a, rows_b, sem_a, sem_b):
        bufs = ((rows_a, sem_a), (rows_b, sem_b))
        wid = lax.axis_index("s") * nc + lax.axis_index("c")
        for k in range(per_worker):
            item = wid * per_worker + k
            base_tok = (item // e) * n
            pltpu.sync_copy(rank_hbm.at[item], rank_v)

            @pl.loop(0, n // lanes)
            def _(j):
                r = rank_v[pl.ds(j * lanes, lanes)]
                tok = lax.iota(jnp.int32, lanes) + (j * lanes + base_tok)
                plsc.store_scatter(idx_v, [r], tok, mask=r >= 0)

            def gather(c):
                buf, sem = bufs[c % 2]
                rows = pl.ds(c * SC_GATHER_ROWS, SC_GATHER_ROWS)
                return pltpu.async_copy(hf_hbm.at[idx_v.at[rows]], buf, sem)

            n_chunks = cap // SC_GATHER_ROWS
            pending = gather(0)
            for c in range(n_chunks):
                nxt = gather(c + 1) if c + 1 < n_chunks else None
                pending.wait()
                pltpu.sync_copy(bufs[c % 2][0],
                                out_hbm.at[pl.ds(item * cap + c * SC_GATHER_ROWS, SC_GATHER_ROWS)])
                pending = nxt

    return pl.kernel(
        body, out_type=jax.ShapeDtypeStruct((items * cap, width), hf2.dtype), mesh=mesh,
        scratch_types=[pltpu.VMEM((n,), jnp.int32), pltpu.VMEM((cap,), jnp.int32),
                       pltpu.VMEM((SC_GATHER_ROWS, width), hf2.dtype),
                       pltpu.VMEM((SC_GATHER_ROWS, width), hf2.dtype),
                       pltpu.SemaphoreType.DMA, pltpu.SemaphoreType.DMA],
        compiler_params=pltpu.CompilerParams(needs_layout_passes=False),
        name="scgather",
    )(rsel_t.reshape(items, n), hf2)


def _ffn_kernel(x_ref, wg_ref, wu_ref, wd_ref, o_ref, acc_ref, xb_ref):
    f = pl.program_id(1)
    b = x_ref.shape[0]
    last = pl.num_programs(1) - 1

    def tile(first, final):
        wg = wg_ref[0].astype(BF16)
        wu = wu_ref[0].astype(BF16)
        wd = wd_ref[0].astype(BF16)
        for i in range(b):
            if first:
                xb_ref[i] = _unpack_bf16_pairs(x_ref[i, 0])
            x = xb_ref[i]
            g = _dot(x, wg)
            u = _dot(x, wu)
            hid = (g / (1.0 + jnp.exp(-g)) * u).astype(BF16)
            y = _dot(hid, wd)
            if not first:
                y = acc_ref[i] + y
            if final:
                o_ref[i, 0] = y.astype(o_ref.dtype)
            else:
                acc_ref[i] = y

    @pl.when(f == 0)
    def _():
        tile(True, False)

    @pl.when((f > 0) & (f < last))
    def _():
        tile(False, False)

    @pl.when(f == last)
    def _():
        tile(False, True)


def _ffn(xs, w_gate, w_up, w_down):
    b, e, cap, dp = xs.shape
    d = 2 * dp
    dff = w_gate.shape[2]
    tf = FFN_TILE
    assert dff // tf >= 2
    return pl.pallas_call(
        _ffn_kernel,
        out_shape=jax.ShapeDtypeStruct((b, e, cap, d), BF16),
        grid=(e, dff // tf),
        in_specs=[pl.BlockSpec((b, 1, cap, dp), lambda i, f: (0, i, 0, 0)),
                  pl.BlockSpec((1, d, tf), lambda i, f: (i, 0, f)),
                  pl.BlockSpec((1, d, tf), lambda i, f: (i, 0, f)),
                  pl.BlockSpec((1, tf, d), lambda i, f: (i, f, 0))],
        out_specs=pl.BlockSpec((b, 1, cap, d), lambda i, f: (0, i, 0, 0)),
        scratch_shapes=[pltpu.VMEM((b, cap, d), F32), pltpu.VMEM((b, cap, d), BF16)],
        compiler_params=_cparams(("arbitrary", "arbitrary")),
        name="ffn",
    )(xs, w_gate, w_up, w_down)


def _combine_kernel(off_ref, ys_ref, rt_ref, afft_ref, xn_ref, gf_ref, npost_ref, o_ref, acc_ref):
    bi, tt = pl.program_id(0), pl.program_id(1)
    ts = COMBINE_TILE
    n_sub = rt_ref.shape[2] // ts
    n_experts, cap = ys_ref.shape[1], ys_ref.shape[2]
    blocks = ts // LANES
    slot = lax.broadcasted_iota(jnp.int32, (COMBINE_WIN, ts), 0)

    for sub in range(n_sub):
        toks = slice(sub * ts, (sub + 1) * ts)
        blk0 = (tt * n_sub + sub) * blocks

        def window(e, w0, j, toks=toks):
            nominal = w0 + j * COMBINE_WIN
            start = pl.multiple_of(jnp.minimum(nominal, cap - COMBINE_WIN), BF16_ROWS)
            rank = rt_ref[0, e:e + 1, toks]
            hit = ((rank - start) == slot) & (rank >= nominal)
            weights = jnp.where(hit, afft_ref[0, e:e + 1, toks], 0.0).astype(BF16)
            return weights, ys_ref[0, e, pl.ds(start, COMBINE_WIN), :]

        first, extra = [], []
        for e in range(n_experts):
            r0 = off_ref[bi, e, blk0]
            r1 = off_ref[bi, e, blk0 + blocks]
            w0 = (r0 // BF16_ROWS) * BF16_ROWS
            first.append(w0)
            extra.append(jnp.maximum((r1 - w0 + COMBINE_WIN - 1) // COMBINE_WIN - 1, 0))
        terms = []
        for e in range(0, n_experts, COMBINE_STACK):
            ws, ys = zip(*[window(e + k, first[e + k], 0) for k in range(COMBINE_STACK)])
            terms.append(_dot(jnp.concatenate(ws, axis=0).T, jnp.concatenate(ys, axis=0)))
        acc_ref[toks] = functools.reduce(lambda a, c: a + c, terms)

        @pl.when(functools.reduce(lambda a, c: a + c, extra) > 0)
        def _(window=window, first=first, extra=extra, toks=toks):
            for e in range(n_experts):
                def more(j, carry, e=e):
                    w, y = window(e, first[e], j)
                    acc_ref[toks] += _dot(w.T, y)
                    return carry
                lax.fori_loop(1, extra[e] + 1, more, 0)

    o_ref[0] = xn_ref[0] + gf_ref[0] * _rms(acc_ref[...], npost_ref[...])


def _combine(ys, slot_off, rsel_t, aff_t, x_new, gf, npost):
    b, e, cap, d = ys.shape
    n = x_new.shape[1]
    tm = COMBINE_SUBTILES * COMBINE_TILE
    tokmap = lambda i, j, off: (i, j, 0)
    return pl.pallas_call(
        _combine_kernel,
        out_shape=jax.ShapeDtypeStruct((b, n, d), F32),
        grid_spec=pltpu.PrefetchScalarGridSpec(
            num_scalar_prefetch=1,
            grid=(b, n // tm),
            in_specs=[pl.BlockSpec((1, e, cap, d), lambda i, j, off: (i, 0, 0, 0)),
                      pl.BlockSpec((1, e, tm), lambda i, j, off: (i, 0, j)),
                      pl.BlockSpec((1, e, tm), lambda i, j, off: (i, 0, j)),
                      pl.BlockSpec((1, tm, d), tokmap),
                      pl.BlockSpec((1, 1, d), lambda i, j, off: (i, 0, 0)),
                      pl.BlockSpec((1, d), lambda i, j, off: (0, 0))],
            out_specs=pl.BlockSpec((1, tm, d), tokmap),
            scratch_shapes=[pltpu.VMEM((tm, d), F32)]),
        compiler_params=_cparams(("arbitrary", "arbitrary")),
        name="combine",
    )(slot_off, ys, rsel_t, aff_t, x_new, gf, npost.reshape(1, d))


def kernel(x, c, ctx, c_ctx, w_mod, b_mod, norm_mix_pre, norm_mix_post, norm_ffn_pre, norm_ffn_post,
           w_in, na_rpb, gla_a_up, gla_a_bias, gla_norm, w_out, router, w_gate, w_up, w_down):
    b, n, d = x.shape
    assert w_mod.shape[0] == 1 and d == D_MODEL and n % (GRID_W * NA_QROWS) == 0 and n % GLA_T == 0
    assert ctx.shape[1] == GLA_T
    rows = n // GRID_W
    cap = EC_CAPACITY_FACTOR * n // N_EXPERTS

    c8 = jnp.concatenate([c, c_ctx[None, :], jnp.zeros((8 - b - 1, d), F32)], axis=0)
    mod = _mod(c8, w_mod[0], b_mod[0])
    sh_m, sc_m, g_m, sh_f, sc_f, g_f = [m[:b, None, :] for m in jnp.split(mod, 6, axis=-1)]
    sh_c, sc_c = mod[b:b + 1, None, :d], mod[b:b + 1, None, d:2 * d]

    cuts = [int(c) for c in np.cumsum([0, NA_W, NA_W, GLA_KEY_W, GLA_VAL_W, 2 * GLA_GATE_RANK, NA_W, GLA_KEY_W, GLA_VAL_W])]
    c_nak, c_nav, c_gk, c_gv, c_ad, c_naq, c_gq, c_gg = [[(cuts[i], cuts[i + 1])] for i in range(8)]
    c_ad3 = c_ad * 3
    rope = _rope_tables(n)
    na_q, na_k, na_v, gq, gk, gv, ad, gg = _inproj(
        x, sh_m, sc_m, norm_mix_pre[0], w_in[0],
        [c_naq, c_nak, c_nav, c_gq, c_gk, c_gv, c_ad3, c_gg],
        [NA_W, NA_W, NA_W, GLA_KEY_W, GLA_KEY_W, GLA_VAL_W, LANES, GLA_VAL_W],
        ["plain", "plain", "plain", "rope", "rope", "plain", "split", "plain"],
        [1.0, 1.0, 1.0, GLA_DK ** -0.5, 1.0, 1.0, 1.0, 1.0],
        [BF16] * 8, rope=rope)
    c_nak, c_nav, c_gk, c_gv, c_ad = _inproj(
        ctx, sh_c, sc_c, norm_mix_pre[0], w_in[0],
        [c_nak, c_nav, c_gk, c_gv, c_ad3],
        [NA_W, NA_W, GLA_KEY_W, GLA_VAL_W, LANES],
        ["plain", "plain", "plain", "plain", "split"], [1.0] * 5, [BF16] * 5)

    o_na = _na(na_q, na_k, na_v, c_nak, c_nav, *_na_bias_tables(na_rpb[0], rows))

    cmats = jnp.asarray(_gla_prefix_matrices(GLA_T), BF16)
    o_gla = _gla(gq, gk, gv, ad, gg, c_gk, c_gv, c_ad, _gla_decay_up(gla_a_up[0]),
                 gla_a_bias[0][:, None, :], gla_norm[0][None, :], cmats)

    wo = w_out[0].astype(BF16)
    router_pad = jnp.zeros((d, LANES), F32).at[:, :N_EXPERTS].set(router[0])
    x_new, hf, aff_t = _outproj(o_na, o_gla, x, wo[:NA_W], wo[NA_W:], g_m, sh_f, sc_f,
                                     norm_mix_post[0], norm_ffn_pre[0],
                                     jnp.concatenate(_split_bf16(router_pad), axis=1))

    rsel_t = _route(aff_t, cap)
    xs = _sc_gather(rsel_t, hf.reshape(b * n, d // 2), cap).reshape(b, N_EXPERTS, cap, d // 2)
    ys = _ffn(xs, w_gate[0], w_up[0], w_down[0])
    picked = (rsel_t >= 0).astype(jnp.int32).reshape(b, N_EXPERTS, n // LANES, LANES).sum(axis=-1)
    slot_off = jnp.concatenate([jnp.zeros((b, N_EXPERTS, 1), jnp.int32), jnp.cumsum(picked, axis=-1)], axis=-1)
    return _combine(ys, slot_off, rsel_t, aff_t, x_new, g_f, norm_ffn_post[0])
```

```python
import functools

import numpy as np
import jax
import jax.numpy as jnp
from jax import lax
from jax.experimental import pallas as pl
from jax.experimental.pallas import tpu as pltpu
from jax.experimental.pallas import tpu_sc as plsc

F32 = jnp.float32
BF16 = jnp.bfloat16
HIGHEST = lax.Precision.HIGHEST

D_MODEL = 1024
GRID_W = 64
NA_W = 512
NA_HEADS = 8
NA_HEAD_DIM = 64
NA_WIN_ROWS = 8
NA_WIN_COLS = 16
GLA_HEADS = 4
GLA_DV = 128
GLA_DK = 64
GLA_KEY_W = 256
GLA_VAL_W = 512
GLA_GATE_RANK = 16
GLA_GATE_TAU = 16.0
ROPE_BASE = 10000.0
N_EXPERTS = 16
EC_CAPACITY_FACTOR = 2
NORM_EPS = 1e-6
NEG_BIG = -1e30

LANES = 128
BF16_ROWS = 16
VMEM_LIMIT = 56 * 1024 * 1024

TOK_TILE = 512
NA_QROWS = 4
NA_KROWS = NA_QROWS + NA_WIN_ROWS
NA_TAB_PAD = NA_QROWS
NA_TAB_BLOCKS = NA_TAB_PAD + 2 * NA_WIN_ROWS - 1 + NA_QROWS + 1
GLA_T = 256
GLA_LEVELS = (128, 64, 32, 16)
GLA_DIAG = 16
GLA_UNROLL = 4
OUTPROJ_SUBTILES = 2
FFN_GROUPS = 2
FFN_TILE = 256
COMBINE_TILE = 256
COMBINE_SUBTILES = 4
COMBINE_WIN = 64
COMBINE_STACK = 4
SC_GATHER_ROWS = 64


def _cparams(sem):
    return pltpu.CompilerParams(dimension_semantics=sem, vmem_limit_bytes=VMEM_LIMIT)


def _rms(v, g):
    return v * lax.rsqrt(jnp.mean(v * v, axis=-1, keepdims=True) + NORM_EPS) * g


def _dot(a, b):
    return jnp.dot(a, b, preferred_element_type=F32)


def _dot_nt(a, b):
    return lax.dot_general(a, b, (((1,), (1,)), ((), ())), preferred_element_type=F32)


def _split_bf16(v):
    hi = v.astype(BF16)
    return hi, (v - hi.astype(F32)).astype(BF16)


_HIGH_HALF = -65536


def _pack_bf16_pairs(hb):
    bits = lax.bitcast_convert_type(hb.astype(F32), jnp.int32)
    half = hb.shape[1] // 2
    return lax.shift_right_logical(bits[:, :half], 16) | (bits[:, half:] & _HIGH_HALF)


def _unpack_bf16_pairs(w):
    lo = lax.bitcast_convert_type(lax.shift_left(w, 16), F32).astype(BF16)
    hi = lax.bitcast_convert_type(w & _HIGH_HALF, F32).astype(BF16)
    return jnp.concatenate([lo, hi], axis=1)


def _mod_kernel(c_ref, w_ref, b_ref, o_ref):
    c = c_ref[...]
    s = c / (1.0 + jnp.exp(-c))
    o_ref[...] = _dot(s.astype(BF16), w_ref[...].astype(BF16)) + b_ref[...]


def _mod(c8, w_mod, b_mod):
    d, n = w_mod.shape
    tn = 1536
    return pl.pallas_call(
        _mod_kernel,
        out_shape=jax.ShapeDtypeStruct((8, n), F32),
        grid=(n // tn,),
        in_specs=[pl.BlockSpec((8, d), lambda j: (0, 0)),
                  pl.BlockSpec((d, tn), lambda j: (0, j)),
                  pl.BlockSpec((1, tn), lambda j: (0, j))],
        out_specs=pl.BlockSpec((8, tn), lambda j: (0, j)),
        compiler_params=_cparams(("arbitrary",)),
        name="mod",
    )(c8, w_mod, b_mod.reshape(1, n))


def _inproj_kernel(x_ref, sh_ref, sc_ref, g_ref, *refs, modes, scales, col_segments):
    n_rope = 2 if "rope" in modes else 0
    rope_refs, refs = refs[:n_rope], refs[n_rope:]
    n_w = len(modes)
    w_ref, o_refs, wb_refs = refs[0], refs[1:1 + n_w], refs[1 + n_w:]

    @pl.when((pl.program_id(0) == 0) & (pl.program_id(1) == 0))
    def _():
        for segments, wb_ref in zip(col_segments, wb_refs):
            off = 0
            for lo, hi in segments:
                wb_ref[:, off:off + hi - lo] = w_ref[:, lo:hi].astype(BF16)
                off += hi - lo
            if off < wb_ref.shape[1]:
                wb_ref[:, off:] = jnp.zeros((wb_ref.shape[0], wb_ref.shape[1] - off), BF16)

    x = x_ref[0]
    h = _rms(x, g_ref[...]) * (1.0 + sc_ref[0]) + sh_ref[0]
    hb = h.astype(BF16)
    for w_ref, o_ref, mode, scale in zip(wb_refs, o_refs, modes, scales):
        y = _dot(hb, w_ref[...])
        if mode == "rope":
            cols = y.shape[1]
            quarter = GLA_DK // 4
            lane = lax.broadcasted_iota(jnp.int32, y.shape, 1)
            partner = jnp.where((lane & (2 * quarter - 1)) < quarter,
                                pltpu.roll(y, cols - quarter, axis=1), pltpu.roll(y, quarter, axis=1))
            y = (y * rope_refs[0][...] + partner * rope_refs[1][...]) * scale
        elif mode == "split":
            hi, lo = _split_bf16(y)
            lane = lax.broadcasted_iota(jnp.int32, y.shape, 1)
            rank2 = 2 * GLA_GATE_RANK
            y = jnp.where((lane >= rank2) & (lane < 2 * rank2), lo, hi)
        o_ref[0] = y.astype(o_ref.dtype)


def _inproj(x, shift, scale, g, w_all, col_segments, widths, modes, scales, out_dtypes, rope=None):
    b, n, d = x.shape
    tm = min(TOK_TILE, n)
    per_sample = shift.shape[0] == b
    mod_map = (lambda i, j: (i, 0, 0)) if per_sample else (lambda i, j: (0, 0, 0))
    in_specs = [pl.BlockSpec((1, tm, d), lambda i, j: (i, j, 0)),
                pl.BlockSpec((1, 1, d), mod_map),
                pl.BlockSpec((1, 1, d), mod_map),
                pl.BlockSpec((1, d), lambda i, j: (0, 0))]
    args = [x, shift, scale, g.reshape(1, d)]
    if rope is not None:
        in_specs += [pl.BlockSpec((tm, rope[0].shape[1]), lambda i, j: (j, 0))] * 2
        args += list(rope)
    in_specs.append(pl.BlockSpec(w_all.shape, lambda i, j: (0, 0), pipeline_mode=pl.Buffered(1)))
    out_shapes, out_specs = [], []
    for cols, dt in zip(widths, out_dtypes):
        out_shapes.append(jax.ShapeDtypeStruct((b, n, cols), dt))
        out_specs.append(pl.BlockSpec((1, tm, cols), lambda i, j: (i, j, 0)))
    return pl.pallas_call(
        functools.partial(_inproj_kernel, modes=tuple(modes), scales=tuple(scales),
                          col_segments=tuple(tuple(seg) for seg in col_segments)),
        out_shape=out_shapes,
        grid=(b, n // tm),
        in_specs=in_specs,
        out_specs=out_specs,
        scratch_shapes=[pltpu.VMEM((d, cols), BF16) for cols in widths],
        compiler_params=_cparams(("arbitrary", "arbitrary")),
        name="inproj",
    )(*args, w_all)


def _rope_tables(n):
    t = np.arange(n)
    pos_row, pos_col = (t // GRID_W).astype(np.float32), (t % GRID_W).astype(np.float32)
    quarter = GLA_DK // 4
    freqs = np.float32(ROPE_BASE) ** (-np.arange(quarter, dtype=np.float32) / quarter)
    ang_r = pos_row[:, None] * freqs
    ang_c = pos_col[:, None] * freqs
    cos = np.concatenate([np.cos(ang_r), np.cos(ang_r), np.cos(ang_c), np.cos(ang_c)], axis=-1)
    sin = np.concatenate([-np.sin(ang_r), np.sin(ang_r), -np.sin(ang_c), np.sin(ang_c)], axis=-1)
    return jnp.asarray(np.tile(cos, (1, GLA_HEADS))), jnp.asarray(np.tile(sin, (1, GLA_HEADS)))


def _na_patterns(rows):
    kr = min(NA_WIN_ROWS, rows)
    n_blocks = rows // NA_QROWS
    pats = []
    for blk in (0, 1, n_blocks - 1):
        r0 = blk * NA_QROWS
        k0 = int(np.clip(r0 - kr // 2, 0, rows - NA_KROWS))
        strips = []
        for a in range(NA_QROWS):
            r_start = int(np.clip(r0 + a - kr // 2, 0, rows - kr))
            start = k0 - (r0 + a) + NA_WIN_ROWS - 1 + NA_TAB_PAD
            assert 0 <= start and start + NA_KROWS <= NA_TAB_BLOCKS
            strips.append((start, [r_start <= k0 + c < r_start + kr for c in range(NA_KROWS)]))
        pats.append(strips)
    return pats


def _na_bias_tables(rpb, rows):
    heads = rpb.shape[0]
    col = np.arange(GRID_W)
    c_start = np.clip(col - NA_WIN_COLS // 2, 0, GRID_W - NA_WIN_COLS)
    col_ok = (col[None, :] >= c_start[:, None]) & (col[None, :] < c_start[:, None] + NA_WIN_COLS)
    dc = np.clip(col[None, :] - col[:, None] + NA_WIN_COLS - 1, 0, 2 * NA_WIN_COLS - 2)
    sel_c = (np.arange(2 * NA_WIN_COLS - 1)[:, None, None] == dc[None]) & col_ok[None]
    t = jnp.einsum("hrd,dqk->hqrk", rpb, jnp.asarray(sel_c, F32), precision=HIGHEST)
    t = jnp.where(jnp.asarray(col_ok)[None, :, None, :], t, NEG_BIG)
    n_dr = 2 * NA_WIN_ROWS - 1
    t = t.reshape(heads, GRID_W, n_dr * GRID_W)
    back = NA_TAB_BLOCKS + 1 - NA_TAB_PAD - n_dr
    t = jnp.pad(t, ((0, 0), (0, 0), (NA_TAB_PAD * GRID_W, back * GRID_W)), constant_values=NEG_BIG)
    width = NA_TAB_BLOCKS * GRID_W
    tab = jnp.stack([t[:, :, :width], t[:, :, GRID_W:GRID_W + width]], axis=1)
    row_mask = np.zeros((3, NA_QROWS, 1, NA_KROWS * GRID_W), np.float32)
    for pat, strips in enumerate(_na_patterns(rows)):
        for a, (_, valid) in enumerate(strips):
            row_mask[pat, a, 0] = np.repeat(np.where(valid, 0.0, NEG_BIG), GRID_W)
    return tab, jnp.asarray(row_mask)


def _na_kernel(q_ref, k_ref, v_ref, kc_ref, vc_ref, tab_ref, rmask_ref, o_ref, bias_ref,
               sw_ref, sc_ref, pw_ref, pc_ref, *, rows):
    nq, nk = NA_QROWS * GRID_W, NA_KROWS * GRID_W
    n_blocks = rows // NA_QROWS
    kr = min(NA_WIN_ROWS, rows)
    scale = NA_HEAD_DIM ** -0.5
    lane = lax.broadcasted_iota(jnp.int32, (nq, LANES), 1)
    first_head = lane < NA_HEAD_DIM
    kc = kc_ref[0]
    vc = vc_ref[0]
    lane_w = lax.broadcasted_iota(jnp.int32, (nk, LANES), 1)
    lane_c = lax.broadcasted_iota(jnp.int32, vc.shape, 1)
    @pl.when(pl.program_id(1) == 0)
    def _():
        for pat, strips in enumerate(_na_patterns(rows)):
            for a, (start, _) in enumerate(strips):
                parity = start % 2
                off = (start - parity) * GRID_W
                for h in range(2):
                    bias_ref[0, pat, h, a * GRID_W:(a + 1) * GRID_W, :] = (
                        tab_ref[h, parity, :, off:off + nk] + rmask_ref[pat, a])

    def key_start(i):
        return pl.multiple_of(jnp.clip(i * NA_QROWS - kr // 2, 0, rows - NA_KROWS) * GRID_W, GRID_W)

    def scores(i, slot):
        pat = jnp.where(i == 0, 0, jnp.where(i == n_blocks - 1, 2, 1))
        q = q_ref[0, pl.ds(pl.multiple_of(i * nq, nq), nq), :] * scale
        kw = k_ref[0, pl.ds(key_start(i), nk), :]
        for h in range(2):
            qh = jnp.where(first_head if h == 0 else jnp.logical_not(first_head), q, jnp.zeros_like(q))
            sw_ref[slot, h] = _dot_nt(qh, kw) + bias_ref[0, pat, h]
            sc_ref[slot, h] = _dot_nt(qh, kc)

    def softmax(slot):
        for h in range(2):
            s_w = sw_ref[slot, h]
            s_c = sc_ref[slot, h]
            m = jnp.maximum(jnp.max(s_w, axis=-1, keepdims=True), jnp.max(s_c, axis=-1, keepdims=True))
            pw_ref[slot, h] = jnp.exp((s_w - m).astype(BF16))
            pc_ref[slot, h] = jnp.exp((s_c - m).astype(BF16))

    def values(i, slot):
        vw = v_ref[0, pl.ds(key_start(i), nk), :]
        outs = []
        for h in range(2):
            sum_lane = NA_HEAD_DIM * (1 - h)
            vw_h = jnp.where(lane_w == sum_lane, jnp.ones_like(vw), vw)
            vc_h = jnp.where(lane_c == sum_lane, jnp.ones_like(vc), vc)
            o = _dot(pw_ref[slot, h], vw_h) + _dot(pc_ref[slot, h], vc_h)
            outs.append(o * (1.0 / o[:, sum_lane:sum_lane + 1]))
        o = jnp.where(first_head, outs[0], outs[1])
        o_ref[0, pl.ds(pl.multiple_of(i * nq, nq), nq), :] = o.astype(o_ref.dtype)

    assert n_blocks % 2 == 0 and n_blocks >= 4
    scores(0, 0)
    softmax(0)
    scores(1, 1)

    def trip(j, carry):
        i = 2 * j
        values(i - 2, 0)
        softmax(1)
        scores(i, 0)
        values(i - 1, 1)
        softmax(0)
        scores(i + 1, 1)
        return carry

    lax.fori_loop(1, n_blocks // 2, trip, 0)
    values(n_blocks - 2, 0)
    softmax(1)
    values(n_blocks - 1, 1)


def _na(q, k, v, kc, vc, tab, row_mask):
    b, n, w = q.shape
    n_ctx = kc.shape[1]
    pairs = w // LANES
    rows = n // GRID_W
    nq, nk = NA_QROWS * GRID_W, NA_KROWS * GRID_W
    tok = lambda p, i: (i, 0, p)
    return pl.pallas_call(
        functools.partial(_na_kernel, rows=rows),
        out_shape=jax.ShapeDtypeStruct((b, n, w), BF16),
        grid=(pairs, b),
        in_specs=[pl.BlockSpec((1, n, LANES), tok),
                  pl.BlockSpec((1, n, LANES), tok),
                  pl.BlockSpec((1, n, LANES), tok),
                  pl.BlockSpec((1, n_ctx, LANES), tok),
                  pl.BlockSpec((1, n_ctx, LANES), tok),
                  pl.BlockSpec((2,) + tab.shape[1:], lambda p, i: (p, 0, 0, 0)),
                  pl.BlockSpec(row_mask.shape, lambda p, i: (0, 0, 0, 0))],
        out_specs=pl.BlockSpec((1, n, LANES), tok),
        scratch_shapes=[pltpu.VMEM((1, 3, 2, nq, nk), F32),
                        pltpu.VMEM((2, 2, nq, nk), F32), pltpu.VMEM((2, 2, nq, n_ctx), F32),
                        pltpu.VMEM((2, 2, nq, nk), BF16), pltpu.VMEM((2, 2, nq, n_ctx), BF16)],
        compiler_params=_cparams(("arbitrary", "arbitrary")),
        name="na",
    )(q, k, v, kc, vc, tab, row_mask)


def _gla_prefix_matrices(t):
    i = np.arange(t)
    return np.stack([i[:, None] >= i[None, :], i[:, None] <= i[None, :]]).astype(np.float32)


def _gla_kernel(q_ref, k_ref, v_ref, ad_ref, g_ref, ck_ref, cv_ref, cad_ref,
                u_ref, ab_ref, gn_ref, cm_ref, o_ref, accf_ref, accb_ref, *, n_tok):
    t = GLA_T
    n_chunks = n_tok // t
    row = lax.broadcasted_iota(jnp.int32, (t, LANES), 0)
    hb = t // 2
    row_b = lax.broadcasted_iota(jnp.int32, (hb, LANES), 0)
    head0_b = lax.broadcasted_iota(jnp.int32, (hb, LANES), 1) < GLA_DK
    row2 = lax.broadcasted_iota(jnp.int32, (hb, 2 * hb), 0)
    col2 = lax.broadcasted_iota(jnp.int32, (hb, 2 * hb), 1) & (hb - 1)
    vrow = lax.broadcasted_iota(jnp.int32, (2 * t, 2 * GLA_DV), 0)
    vlane = lax.broadcasted_iota(jnp.int32, (2 * t, 2 * GLA_DV), 1)
    v_head_match = (vrow >= t) == (vlane >= GLA_DV)
    srow = lax.broadcasted_iota(jnp.int32, (2 * GLA_DV, LANES), 0)
    slane = lax.broadcasted_iota(jnp.int32, (2 * GLA_DV, LANES), 1)
    s_blockdiag = (srow >= GLA_DV) == (slane >= GLA_DK)
    blk_mask = {half: jnp.where((row2 & ~(2 * half - 1)) == (col2 & ~(2 * half - 1)), 1.0, 0.0)
                for half in GLA_LEVELS if 2 * half < hb}
    diag_blk = (row2 & ~(GLA_DIAG - 1)) == (col2 & ~(GLA_DIAG - 1))
    diag_mask = (jnp.where(diag_blk & (row2 >= col2), 1.0, 0.0), jnp.where(diag_blk & (row2 <= col2), 1.0, 0.0))

    def prefix_sums(ad, dirn):
        z = _dot(ad, u_ref[dirn]) + ab_ref[dirn]
        loga = (jnp.minimum(z, 0.0) - jnp.log(1.0 + jnp.exp(-jnp.abs(z)))) * (1.0 / GLA_GATE_TAU)
        hi, lo = _split_bf16(loga)
        p2 = _dot(cm_ref[dirn], jnp.concatenate([hi, lo], axis=-1))
        return p2[:, :LANES] + p2[:, LANES:]

    def chunk_end(p, dirn):
        return p[t - 1:t, :] if dirn == 0 else p[0:1, :]

    def level_sums(p, half, dirn):
        blk = 2 * half
        p3 = p.reshape(t // blk, blk, LANES)
        edge = half - 1 if dirn == 0 else half
        ref = jnp.broadcast_to(p3[:, edge:edge + 1, :], p3.shape).reshape(t, LANES)
        later = (row & half) != 0
        return jnp.where(later == (dirn == 0), p - ref, ref - p)

    def state_update(s, k, vt, p, dirn):
        kh = (k * jnp.exp(chunk_end(p, dirn) - p)).astype(BF16)
        return s * jnp.exp(chunk_end(p, dirn)) + jnp.where(s_blockdiag, _dot(vt, kh), 0.0)

    def chunk(tok0, s, dirn):
        q = q_ref[0, pl.ds(tok0, t), :].astype(F32)
        k = k_ref[0, pl.ds(tok0, t), :].astype(F32)
        v = v_ref[0, pl.ds(tok0, t), :]
        vt = v.T
        p = prefix_sums(ad_ref[0, pl.ds(tok0, t), :], dirn)
        qh = (q * jnp.exp(p)).astype(BF16)
        o = _dot_nt(qh, s.astype(BF16))
        def rows(x, b):
            return x[b * hb:(b + 1) * hb]

        def scores(qt, kt):
            kcat = jnp.concatenate([jnp.where(head0_b, kt, 0.0), jnp.where(head0_b, 0.0, kt)], axis=0)
            return _dot_nt(qt.astype(BF16), kcat.astype(BF16))

        assert GLA_LEVELS[0] == hb
        w = jnp.exp(level_sums(p, hb, dirn))
        qb, kb = (1, 0) if dirn == 0 else (0, 1)
        wide = scores(rows(q, qb) * rows(w, qb), rows(k, kb) * rows(w, kb))
        fine_w = [jnp.exp(level_sums(p, half, dirn)) for half in GLA_LEVELS[1:]]
        e_d = level_sums(p, GLA_DIAG // 2, dirn)
        w_d, wi_d = jnp.exp(e_d), jnp.exp(-e_d)
        fine = []
        for b in range(2):
            qs, ks = rows(q, b), rows(k, b)
            acc = None
            for half, w in zip(GLA_LEVELS[1:], fine_w):
                later = (row_b & half) != 0
                q_side = later if dirn == 0 else jnp.logical_not(later)
                part = scores(jnp.where(q_side, qs * rows(w, b), 0.0), jnp.where(q_side, 0.0, ks * rows(w, b)))
                if half in blk_mask:
                    part = part * blk_mask[half]
                acc = part if acc is None else acc + part
            later = (row_b & (GLA_DIAG // 2)) != 0
            shrink_q = later if dirn == 0 else jnp.logical_not(later)
            part = scores(qs * jnp.where(shrink_q, rows(w_d, b), rows(wi_d, b)),
                          ks * jnp.where(shrink_q, rows(wi_d, b), rows(w_d, b)))
            fine.append(acc + jnp.where(diag_mask[dirn] > 0.5, part, 0.0))
        zero = jnp.zeros((hb, hb), F32)
        h0, h1 = slice(0, hb), slice(hb, 2 * hb)
        if dirn == 0:
            top = [fine[0][:, h0], zero, fine[0][:, h1], zero]
            bot = [wide[:, h0], fine[1][:, h0], wide[:, h1], fine[1][:, h1]]
        else:
            top = [fine[0][:, h0], wide[:, h0], fine[0][:, h1], wide[:, h1]]
            bot = [zero, fine[1][:, h0], zero, fine[1][:, h1]]
        a = jnp.concatenate([jnp.concatenate(top, axis=1), jnp.concatenate(bot, axis=1)], axis=0)
        vcat = jnp.concatenate([v, v], axis=0)
        vcat = jnp.where(v_head_match, vcat, jnp.zeros_like(vcat))
        o = o + _dot(a.astype(BF16), vcat)
        return o, state_update(s, k, vt, p, dirn)

    def ctx_state(dirn):
        p = prefix_sums(cad_ref[0], dirn)
        s0 = jnp.zeros((2 * GLA_DV, LANES), F32)
        return state_update(s0, ck_ref[0].astype(F32), cv_ref[0].T, p, dirn)

    def finish(tok0, o):
        g = g_ref[0, pl.ds(tok0, t), :].astype(F32)
        gate = g / (1.0 + jnp.exp(-g))
        halves = [_rms(o[:, h * GLA_DV:(h + 1) * GLA_DV], gn_ref[...]) for h in range(2)]
        o_ref[0, pl.ds(tok0, t), :] = (jnp.concatenate(halves, axis=-1) * gate).astype(o_ref.dtype)

    def body(i, carry, second_half):
        s_f, s_b = carry
        for u in range(GLA_UNROLL):
            c = i * GLA_UNROLL + u
            tok_f = pl.multiple_of(c * t, t)
            tok_b = pl.multiple_of((n_chunks - 1 - c) * t, t)
            o_f, s_f = chunk(tok_f, s_f, 0)
            o_b, s_b = chunk(tok_b, s_b, 1)
            if second_half:
                finish(tok_f, o_f + accb_ref[pl.ds(tok_f, t), :])
                finish(tok_b, o_b + accf_ref[pl.ds(tok_b, t), :])
            else:
                accf_ref[pl.ds(tok_f, t), :] = o_f
                accb_ref[pl.ds(tok_b, t), :] = o_b
        return s_f, s_b

    trips = n_chunks // GLA_UNROLL
    assert n_chunks % (2 * GLA_UNROLL) == 0
    carry = lax.fori_loop(0, trips // 2, functools.partial(body, second_half=False), (ctx_state(0), ctx_state(1)))
    lax.fori_loop(trips // 2, trips, functools.partial(body, second_half=True), carry)


def _gla(q, k, v, ad, g, ck, cv, cad, u, abias, gnorm, cmats):
    b, n, kw = q.shape
    n_ctx = ck.shape[1]
    pairs = kw // LANES
    vw = 2 * GLA_DV
    tok = lambda i, p: (i, 0, p)
    full3 = lambda i, p: (i, 0, 0)
    return pl.pallas_call(
        functools.partial(_gla_kernel, n_tok=n),
        out_shape=jax.ShapeDtypeStruct((b, n, v.shape[2]), BF16),
        grid=(b, pairs),
        in_specs=[pl.BlockSpec((1, n, LANES), tok),
                  pl.BlockSpec((1, n, LANES), tok),
                  pl.BlockSpec((1, n, vw), tok),
                  pl.BlockSpec((1, n, LANES), full3),
                  pl.BlockSpec((1, n, vw), tok),
                  pl.BlockSpec((1, n_ctx, LANES), tok),
                  pl.BlockSpec((1, n_ctx, vw), tok),
                  pl.BlockSpec((1, n_ctx, LANES), full3),
                  pl.BlockSpec((2, LANES, LANES), lambda i, p: (0, 0, p)),
                  pl.BlockSpec((2, 1, LANES), lambda i, p: (0, 0, p)),
                  pl.BlockSpec((1, GLA_DV), lambda i, p: (0, 0)),
                  pl.BlockSpec(cmats.shape, lambda i, p: (0, 0, 0))],
        out_specs=pl.BlockSpec((1, n, vw), tok),
        scratch_shapes=[pltpu.VMEM((n, vw), F32), pltpu.VMEM((n, vw), F32)],
        compiler_params=_cparams(("arbitrary", "arbitrary")),
        name="gla",
    )(q, k, v, ad, g, ck, cv, cad, u, abias, gnorm, cmats)


def _gla_decay_up(a_up):
    r = GLA_GATE_RANK
    pad = jnp.zeros((2, 2 * r, GLA_KEY_W), F32)
    pad = pad.at[0, :r].set(a_up[0]).at[1, r:].set(a_up[1])
    hi, lo = _split_bf16(pad)
    return jnp.concatenate([hi, hi, lo, jnp.zeros_like(hi)], axis=1)


def _outproj_kernel(ona_ref, ogla_ref, x_ref, w1_ref, w2_ref, gm_ref, shf_ref, scf_ref, npost_ref,
                    nfpre_ref, rt_ref, xnew_ref, hf_ref, afft_ref):
    for r0 in range(0, x_ref.shape[1], TOK_TILE):
        rs = slice(r0, r0 + TOK_TILE)
        mix = _dot(ona_ref[0, rs], w1_ref[...]) + _dot(ogla_ref[0, rs], w2_ref[...])
        xn = x_ref[0, rs] + gm_ref[0] * _rms(mix, npost_ref[...])
        xnew_ref[0, rs] = xn
        h = _rms(xn, nfpre_ref[...]) * (1.0 + scf_ref[0]) + shf_ref[0]
        h_hi, h_lo = _split_bf16(h)
        hf_ref[0, rs] = _pack_bf16_pairs(h_hi)
        res = _dot(h_hi, rt_ref[...])
        logits = res[:, :LANES] + res[:, LANES:] + _dot(h_lo, rt_ref[:, :LANES])
        lane = lax.broadcasted_iota(jnp.int32, logits.shape, 1)
        logits = jnp.where(lane < N_EXPERTS, logits, NEG_BIG)
        p = jnp.exp(logits - jnp.max(logits, axis=-1, keepdims=True))
        aff = p / jnp.sum(p, axis=-1, keepdims=True)
        afft_ref[0, :, rs] = aff.T[:N_EXPERTS, :]


def _outproj(o_na, o_gla, x, w1, w2, gm, shf, scf, npost, nfpre, router_cat):
    b, n, d = x.shape
    tm = OUTPROJ_SUBTILES * TOK_TILE
    tokmap = lambda i, j: (i, j, 0)
    smp = lambda i, j: (i, 0, 0)
    cst = lambda i, j: (0, 0)
    return pl.pallas_call(
        _outproj_kernel,
        out_shape=[jax.ShapeDtypeStruct((b, n, d), F32),
                   jax.ShapeDtypeStruct((b, n, d // 2), jnp.int32),
                   jax.ShapeDtypeStruct((b, N_EXPERTS, n), F32)],
        grid=(b, n // tm),
        in_specs=[pl.BlockSpec((1, tm, o_na.shape[2]), tokmap),
                  pl.BlockSpec((1, tm, o_gla.shape[2]), tokmap),
                  pl.BlockSpec((1, tm, d), tokmap),
                  pl.BlockSpec(w1.shape, cst),
                  pl.BlockSpec(w2.shape, cst),
                  pl.BlockSpec((1, 1, d), smp),
                  pl.BlockSpec((1, 1, d), smp),
                  pl.BlockSpec((1, 1, d), smp),
                  pl.BlockSpec((1, d), cst),
                  pl.BlockSpec((1, d), cst),
                  pl.BlockSpec(router_cat.shape, cst)],
        out_specs=[pl.BlockSpec((1, tm, d), tokmap),
                   pl.BlockSpec((1, tm, d // 2), tokmap),
                   pl.BlockSpec((1, N_EXPERTS, tm), lambda i, j: (i, 0, j))],
        compiler_params=_cparams(("arbitrary", "arbitrary")),
        name="outproj",
    )(o_na, o_gla, x, w1, w2, gm, shf, scf, npost.reshape(1, d), nfpre.reshape(1, d), router_cat)


def _route_kernel(afft_ref, rt_ref, *, cap):
    a = afft_ref[...]
    e, n = a.shape
    capf = jnp.float32(cap)

    def search(i, thr_bits):
        cand = thr_bits | lax.shift_left(jnp.int32(1), 30 - i)
        cnt = jnp.sum(jnp.where(a >= lax.bitcast_convert_type(cand, F32), 1.0, 0.0), axis=-1, keepdims=True)
        return jnp.where(cnt >= capf, cand, thr_bits)

    thr_bits = lax.fori_loop(0, 31, search, jnp.zeros((e, 1), jnp.int32))
    thr = lax.bitcast_convert_type(thr_bits, F32)
    need = capf - jnp.sum(jnp.where(a > thr, 1.0, 0.0), axis=-1, keepdims=True)
    r_i = lax.broadcasted_iota(jnp.int32, (LANES, LANES), 0)
    c_i = lax.broadcasted_iota(jnp.int32, (LANES, LANES), 1)
    incl = jnp.where(r_i <= c_i, 1.0, 0.0).astype(BF16)
    off_eq = jnp.zeros((e, 1), F32)
    off_sel = jnp.zeros((e, 1), F32)
    for j in range(n // LANES):
        sl = slice(j * LANES, (j + 1) * LANES)
        a_b = a[:, sl]
        eq_b = jnp.where(a_b == thr, 1.0, 0.0)
        tie_rank = _dot(eq_b.astype(BF16), incl) - eq_b + off_eq
        off_eq = off_eq + jnp.sum(eq_b, axis=-1, keepdims=True)
        sel_b = jnp.where(a_b > thr, 1.0, jnp.where(tie_rank < need, eq_b, 0.0))
        sel = sel_b > 0.5
        rank = _dot(sel_b.astype(BF16), incl) - sel_b + off_sel
        off_sel = off_sel + jnp.sum(sel_b, axis=-1, keepdims=True)
        rsel = jnp.where(sel, rank, -1.0)
        rt_ref[:, sl] = rsel.astype(jnp.int32)


def _route(afft, cap):
    b, e, n = afft.shape
    return pl.pallas_call(
        functools.partial(_route_kernel, cap=cap),
        out_shape=jax.ShapeDtypeStruct((b * e, n), jnp.int32),
        grid=(1,),
        in_specs=[pl.BlockSpec((b * e, n), lambda i: (0, 0))],
        out_specs=pl.BlockSpec((b * e, n), lambda i: (0, 0)),
        compiler_params=_cparams(("arbitrary",)),
        name="route",
    )(afft.reshape(b * e, n)).reshape(b, e, n)


def _sc_gather(rsel_t, hf2, cap):
    b, e, n = rsel_t.shape
    width = hf2.shape[1]
    info = plsc.get_sparse_core_info()
    nc, lanes = info.num_cores, info.num_lanes
    workers = nc * info.num_subcores
    items = b * e
    assert items % workers == 0 and n % lanes == 0 and cap % SC_GATHER_ROWS == 0
    per_worker = items // workers
    mesh = plsc.VectorSubcoreMesh(core_axis_name="c", subcore_axis_name="s")

    def body(rank_hbm, hf_hbm, out_hbm, rank_v, idx_v, rows_a, rows_b, sem_a, sem_b):
        bufs = ((rows_a, sem_a), (rows_b, sem_b))
        wid = lax.axis_index("s") * nc + lax.axis_index("c")
        for k in range(per_worker):
            item = wid * per_worker + k
            base_tok = (item // e) * n
            pltpu.sync_copy(rank_hbm.at[item], rank_v)

            @pl.loop(0, n // lanes)
            def _(j):
                r = rank_v[pl.ds(j * lanes, lanes)]
                tok = lax.iota(jnp.int32, lanes) + (j * lanes + base_tok)
                plsc.store_scatter(idx_v, [r], tok, mask=r >= 0)

            def gather(c):
                buf, sem = bufs[c % 2]
                rows = pl.ds(c * SC_GATHER_ROWS, SC_GATHER_ROWS)
                return pltpu.async_copy(hf_hbm.at[idx_v.at[rows]], buf, sem)

            n_chunks = cap // SC_GATHER_ROWS
            pending = gather(0)
            for c in range(n_chunks):
                nxt = gather(c + 1) if c + 1 < n_chunks else None
                pending.wait()
                pltpu.sync_copy(bufs[c % 2][0],
                                out_hbm.at[pl.ds(item * cap + c * SC_GATHER_ROWS, SC_GATHER_ROWS)])
                pending = nxt

    return pl.kernel(
        body, out_type=jax.ShapeDtypeStruct((items * cap, width), hf2.dtype), mesh=mesh,
        scratch_types=[pltpu.VMEM((n,), jnp.int32), pltpu.VMEM((cap,), jnp.int32),
                       pltpu.VMEM((SC_GATHER_ROWS, width), hf2.dtype),
                       pltpu.VMEM((SC_GATHER_ROWS, width), hf2.dtype),
                       pltpu.SemaphoreType.DMA, pltpu.SemaphoreType.DMA],
        compiler_params=pltpu.CompilerParams(needs_layout_passes=False),
        name="scgather",
    )(rsel_t.reshape(items, n), hf2)


def _ffn_kernel(x_ref, wg_ref, wu_ref, wd_ref, o_ref, acc_ref, xb_ref):
    f = pl.program_id(1)
    b = x_ref.shape[0]
    last = pl.num_programs(1) - 1

    def tile(first, final):
        wg = wg_ref[0].astype(BF16)
        wu = wu_ref[0].astype(BF16)
        wd = wd_ref[0].astype(BF16)
        for i in range(b):
            if first:
                xb_ref[i] = _unpack_bf16_pairs(x_ref[i, 0])
            x = xb_ref[i]
            g = _dot(x, wg)
            u = _dot(x, wu)
            hid = (g / (1.0 + jnp.exp(-g)) * u).astype(BF16)
            y = _dot(hid, wd)
            if not first:
                y = acc_ref[i] + y
            if final:
                o_ref[i, 0] = y.astype(o_ref.dtype)
            else:
                acc_ref[i] = y

    @pl.when(f == 0)
    def _():
        tile(True, False)

    @pl.when((f > 0) & (f < last))
    def _():
        tile(False, False)

    @pl.when(f == last)
    def _():
        tile(False, True)


def _ffn(xs, w_gate, w_up, w_down, e0):
    b, e, cap, dp = xs.shape
    d = 2 * dp
    dff = w_gate.shape[2]
    tf = FFN_TILE
    assert dff // tf >= 2
    return pl.pallas_call(
        _ffn_kernel,
        out_shape=jax.ShapeDtypeStruct((b, e, cap, d), BF16),
        grid=(e, dff // tf),
        in_specs=[pl.BlockSpec((b, 1, cap, dp), lambda i, f: (0, i, 0, 0)),
                  pl.BlockSpec((1, d, tf), lambda i, f: (i + e0, 0, f)),
                  pl.BlockSpec((1, d, tf), lambda i, f: (i + e0, 0, f)),
                  pl.BlockSpec((1, tf, d), lambda i, f: (i + e0, f, 0))],
        out_specs=pl.BlockSpec((b, 1, cap, d), lambda i, f: (0, i, 0, 0)),
        scratch_shapes=[pltpu.VMEM((b, cap, d), F32), pltpu.VMEM((b, cap, d), BF16)],
        compiler_params=_cparams(("arbitrary", "arbitrary")),
        name="ffn",
    )(xs, w_gate, w_up, w_down)


def _combine_kernel(off_ref, *refs, n_groups):
    ys_refs = refs[:n_groups]
    rt_ref, afft_ref, xn_ref, gf_ref, npost_ref, o_ref, acc_ref = refs[n_groups:]
    bi, tt = pl.program_id(0), pl.program_id(1)
    ts = COMBINE_TILE
    n_sub = rt_ref.shape[2] // ts
    group, cap = ys_refs[0].shape[1], ys_refs[0].shape[2]
    n_experts = n_groups * group
    blocks = ts // LANES
    slot = lax.broadcasted_iota(jnp.int32, (COMBINE_WIN, ts), 0)

    for sub in range(n_sub):
        toks = slice(sub * ts, (sub + 1) * ts)
        blk0 = (tt * n_sub + sub) * blocks

        def window(e, w0, j, toks=toks):
            nominal = w0 + j * COMBINE_WIN
            start = pl.multiple_of(jnp.minimum(nominal, cap - COMBINE_WIN), BF16_ROWS)
            rank = rt_ref[0, e:e + 1, toks]
            hit = ((rank - start) == slot) & (rank >= nominal)
            weights = jnp.where(hit, afft_ref[0, e:e + 1, toks], 0.0).astype(BF16)
            return weights, ys_refs[e // group][0, e % group, pl.ds(start, COMBINE_WIN), :]

        first, extra = [], []
        for e in range(n_experts):
            r0 = off_ref[bi, e, blk0]
            r1 = off_ref[bi, e, blk0 + blocks]
            w0 = (r0 // BF16_ROWS) * BF16_ROWS
            first.append(w0)
            extra.append(jnp.maximum((r1 - w0 + COMBINE_WIN - 1) // COMBINE_WIN - 1, 0))
        terms = []
        for e in range(0, n_experts, COMBINE_STACK):
            ws, ys = zip(*[window(e + k, first[e + k], 0) for k in range(COMBINE_STACK)])
            terms.append(_dot(jnp.concatenate(ws, axis=0).T, jnp.concatenate(ys, axis=0)))
        acc_ref[toks] = functools.reduce(lambda a, c: a + c, terms)

        @pl.when(functools.reduce(lambda a, c: a + c, extra) > 0)
        def _(window=window, first=first, extra=extra, toks=toks):
            for e in range(n_experts):
                def more(j, carry, e=e):
                    w, y = window(e, first[e], j)
                    acc_ref[toks] += _dot(w.T, y)
                    return carry
                lax.fori_loop(1, extra[e] + 1, more, 0)

    o_ref[0] = xn_ref[0] + gf_ref[0] * _rms(acc_ref[...], npost_ref[...])


def _combine(ys_groups, slot_off, rsel_t, aff_t, x_new, gf, npost):
    b, group, cap, d = ys_groups[0].shape
    e = group * len(ys_groups)
    n = x_new.shape[1]
    tm = COMBINE_SUBTILES * COMBINE_TILE
    tokmap = lambda i, j, off: (i, j, 0)
    return pl.pallas_call(
        functools.partial(_combine_kernel, n_groups=len(ys_groups)),
        out_shape=jax.ShapeDtypeStruct((b, n, d), F32),
        grid_spec=pltpu.PrefetchScalarGridSpec(
            num_scalar_prefetch=1,
            grid=(b, n // tm),
            in_specs=[pl.BlockSpec((1, group, cap, d), lambda i, j, off: (i, 0, 0, 0))] * len(ys_groups) + [
                      pl.BlockSpec((1, e, tm), lambda i, j, off: (i, 0, j)),
                      pl.BlockSpec((1, e, tm), lambda i, j, off: (i, 0, j)),
                      pl.BlockSpec((1, tm, d), tokmap),
                      pl.BlockSpec((1, 1, d), lambda i, j, off: (i, 0, 0)),
                      pl.BlockSpec((1, d), lambda i, j, off: (0, 0))],
            out_specs=pl.BlockSpec((1, tm, d), tokmap),
            scratch_shapes=[pltpu.VMEM((tm, d), F32)]),
        compiler_params=_cparams(("arbitrary", "arbitrary")),
        name="combine",
    )(slot_off, *ys_groups, rsel_t, aff_t, x_new, gf, npost.reshape(1, d))


def kernel(x, c, ctx, c_ctx, w_mod, b_mod, norm_mix_pre, norm_mix_post, norm_ffn_pre, norm_ffn_post,
           w_in, na_rpb, gla_a_up, gla_a_bias, gla_norm, w_out, router, w_gate, w_up, w_down):
    b, n, d = x.shape
    assert w_mod.shape[0] == 1 and d == D_MODEL and n % (GRID_W * NA_QROWS) == 0 and n % GLA_T == 0
    assert ctx.shape[1] == GLA_T
    rows = n // GRID_W
    cap = EC_CAPACITY_FACTOR * n // N_EXPERTS

    c8 = jnp.concatenate([c, c_ctx[None, :], jnp.zeros((8 - b - 1, d), F32)], axis=0)
    mod = _mod(c8, w_mod[0], b_mod[0])
    sh_m, sc_m, g_m, sh_f, sc_f, g_f = [m[:b, None, :] for m in jnp.split(mod, 6, axis=-1)]
    sh_c, sc_c = mod[b:b + 1, None, :d], mod[b:b + 1, None, d:2 * d]

    cuts = [int(c) for c in np.cumsum([0, NA_W, NA_W, GLA_KEY_W, GLA_VAL_W, 2 * GLA_GATE_RANK, NA_W, GLA_KEY_W, GLA_VAL_W])]
    c_nak, c_nav, c_gk, c_gv, c_ad, c_naq, c_gq, c_gg = [[(cuts[i], cuts[i + 1])] for i in range(8)]
    c_ad3 = c_ad * 3
    rope = _rope_tables(n)
    na_q, na_k, na_v, gq, gk, gv, ad, gg = _inproj(
        x, sh_m, sc_m, norm_mix_pre[0], w_in[0],
        [c_naq, c_nak, c_nav, c_gq, c_gk, c_gv, c_ad3, c_gg],
        [NA_W, NA_W, NA_W, GLA_KEY_W, GLA_KEY_W, GLA_VAL_W, LANES, GLA_VAL_W],
        ["plain", "plain", "plain", "rope", "rope", "plain", "split", "plain"],
        [1.0, 1.0, 1.0, GLA_DK ** -0.5, 1.0, 1.0, 1.0, 1.0],
        [BF16] * 8, rope=rope)
    c_nak, c_nav, c_gk, c_gv, c_ad = _inproj(
        ctx, sh_c, sc_c, norm_mix_pre[0], w_in[0],
        [c_nak, c_nav, c_gk, c_gv, c_ad3],
        [NA_W, NA_W, GLA_KEY_W, GLA_VAL_W, LANES],
        ["plain", "plain", "plain", "plain", "split"], [1.0] * 5, [BF16] * 5)

    o_na = _na(na_q, na_k, na_v, c_nak, c_nav, *_na_bias_tables(na_rpb[0], rows))

    cmats = jnp.asarray(_gla_prefix_matrices(GLA_T), BF16)
    o_gla = _gla(gq, gk, gv, ad, gg, c_gk, c_gv, c_ad, _gla_decay_up(gla_a_up[0]),
                 gla_a_bias[0][:, None, :], gla_norm[0][None, :], cmats)

    wo = w_out[0].astype(BF16)
    router_pad = jnp.zeros((d, LANES), F32).at[:, :N_EXPERTS].set(router[0])
    x_new, hf, aff_t = _outproj(o_na, o_gla, x, wo[:NA_W], wo[NA_W:], g_m, sh_f, sc_f,
                                     norm_mix_post[0], norm_ffn_pre[0],
                                     jnp.concatenate(_split_bf16(router_pad), axis=1))

    rsel_t = _route(aff_t, cap)
    group = N_EXPERTS // FFN_GROUPS
    ys = []
    for e0 in range(0, N_EXPERTS, group):
        xs = _sc_gather(rsel_t[:, e0:e0 + group], hf.reshape(b * n, d // 2), cap)
        ys.append(_ffn(xs.reshape(b, group, cap, d // 2), w_gate[0], w_up[0], w_down[0], e0))
    picked = (rsel_t >= 0).astype(jnp.int32).reshape(b, N_EXPERTS, n // LANES, LANES).sum(axis=-1)
    slot_off = jnp.concatenate([jnp.zeros((b, N_EXPERTS, 1), jnp.int32), jnp.cumsum(picked, axis=-1)], axis=-1)
    return _combine(ys, slot_off, rsel_t, aff_t, x_new, g_f, norm_ffn_post[0])
```

```python
import functools

import numpy as np
import jax
import jax.numpy as jnp
from jax import lax
from jax.experimental import pallas as pl
from jax.experimental.pallas import tpu as pltpu
from jax.experimental.pallas import tpu_sc as plsc

F32 = jnp.float32
BF16 = jnp.bfloat16
HIGHEST = lax.Precision.HIGHEST

D_MODEL = 1024
GRID_W = 64
NA_W = 512
NA_HEADS = 8
NA_HEAD_DIM = 64
NA_WIN_ROWS = 8
NA_WIN_COLS = 16
GLA_HEADS = 4
GLA_DV = 128
GLA_DK = 64
GLA_KEY_W = 256
GLA_VAL_W = 512
GLA_GATE_RANK = 16
GLA_GATE_TAU = 16.0
ROPE_BASE = 10000.0
N_EXPERTS = 16
EC_CAPACITY_FACTOR = 2
NORM_EPS = 1e-6
NEG_BIG = -1e30

LANES = 128
BF16_ROWS = 16
VMEM_LIMIT = 56 * 1024 * 1024

TOK_TILE = 512
NA_QROWS = 4
NA_KROWS = NA_QROWS + NA_WIN_ROWS
NA_TAB_PAD = NA_QROWS
NA_TAB_BLOCKS = NA_TAB_PAD + 2 * NA_WIN_ROWS - 1 + NA_QROWS + 1
GLA_T = 256
GLA_LEVELS = (128, 64, 32, 16)
GLA_DIAG = 16
GLA_UNROLL = 4
OUTPROJ_SUBTILES = 2
FFN_GROUP_SIZES = (4, 12)
FFN_TILE = 256
COMBINE_TILE = 256
COMBINE_SUBTILES = 4
COMBINE_WIN = 64
COMBINE_STACK = 4
SC_GATHER_ROWS = 64


def _cparams(sem):
    return pltpu.CompilerParams(dimension_semantics=sem, vmem_limit_bytes=VMEM_LIMIT)


def _rms(v, g):
    return v * lax.rsqrt(jnp.mean(v * v, axis=-1, keepdims=True) + NORM_EPS) * g


def _dot(a, b):
    return jnp.dot(a, b, preferred_element_type=F32)


def _dot_nt(a, b):
    return lax.dot_general(a, b, (((1,), (1,)), ((), ())), preferred_element_type=F32)


def _split_bf16(v):
    hi = v.astype(BF16)
    return hi, (v - hi.astype(F32)).astype(BF16)


_HIGH_HALF = -65536


def _pack_bf16_pairs(hb):
    bits = lax.bitcast_convert_type(hb.astype(F32), jnp.int32)
    half = hb.shape[1] // 2
    return lax.shift_right_logical(bits[:, :half], 16) | (bits[:, half:] & _HIGH_HALF)


def _unpack_bf16_pairs(w):
    lo = lax.bitcast_convert_type(lax.shift_left(w, 16), F32).astype(BF16)
    hi = lax.bitcast_convert_type(w & _HIGH_HALF, F32).astype(BF16)
    return jnp.concatenate([lo, hi], axis=1)


def _mod_kernel(c_ref, w_ref, b_ref, o_ref):
    c = c_ref[...]
    s = c / (1.0 + jnp.exp(-c))
    o_ref[...] = _dot(s.astype(BF16), w_ref[...].astype(BF16)) + b_ref[...]


def _mod(c8, w_mod, b_mod):
    d, n = w_mod.shape
    tn = 1536
    return pl.pallas_call(
        _mod_kernel,
        out_shape=jax.ShapeDtypeStruct((8, n), F32),
        grid=(n // tn,),
        in_specs=[pl.BlockSpec((8, d), lambda j: (0, 0)),
                  pl.BlockSpec((d, tn), lambda j: (0, j)),
                  pl.BlockSpec((1, tn), lambda j: (0, j))],
        out_specs=pl.BlockSpec((8, tn), lambda j: (0, j)),
        compiler_params=_cparams(("arbitrary",)),
        name="mod",
    )(c8, w_mod, b_mod.reshape(1, n))


def _inproj_kernel(x_ref, sh_ref, sc_ref, g_ref, *refs, modes, scales, col_segments):
    n_rope = 2 if "rope" in modes else 0
    rope_refs, refs = refs[:n_rope], refs[n_rope:]
    n_w = len(modes)
    w_ref, o_refs, wb_refs = refs[0], refs[1:1 + n_w], refs[1 + n_w:]

    @pl.when((pl.program_id(0) == 0) & (pl.program_id(1) == 0))
    def _():
        for segments, wb_ref in zip(col_segments, wb_refs):
            off = 0
            for lo, hi in segments:
                wb_ref[:, off:off + hi - lo] = w_ref[:, lo:hi].astype(BF16)
                off += hi - lo
            if off < wb_ref.shape[1]:
                wb_ref[:, off:] = jnp.zeros((wb_ref.shape[0], wb_ref.shape[1] - off), BF16)

    x = x_ref[0]
    h = _rms(x, g_ref[...]) * (1.0 + sc_ref[0]) + sh_ref[0]
    hb = h.astype(BF16)
    for w_ref, o_ref, mode, scale in zip(wb_refs, o_refs, modes, scales):
        y = _dot(hb, w_ref[...])
        if mode == "rope":
            cols = y.shape[1]
            quarter = GLA_DK // 4
            lane = lax.broadcasted_iota(jnp.int32, y.shape, 1)
            partner = jnp.where((lane & (2 * quarter - 1)) < quarter,
                                pltpu.roll(y, cols - quarter, axis=1), pltpu.roll(y, quarter, axis=1))
            y = (y * rope_refs[0][...] + partner * rope_refs[1][...]) * scale
        elif mode == "split":
            hi, lo = _split_bf16(y)
            lane = lax.broadcasted_iota(jnp.int32, y.shape, 1)
            rank2 = 2 * GLA_GATE_RANK
            y = jnp.where((lane >= rank2) & (lane < 2 * rank2), lo, hi)
        o_ref[0] = y.astype(o_ref.dtype)


def _inproj(x, shift, scale, g, w_all, col_segments, widths, modes, scales, out_dtypes, rope=None):
    b, n, d = x.shape
    tm = min(TOK_TILE, n)
    per_sample = shift.shape[0] == b
    mod_map = (lambda i, j: (i, 0, 0)) if per_sample else (lambda i, j: (0, 0, 0))
    in_specs = [pl.BlockSpec((1, tm, d), lambda i, j: (i, j, 0)),
                pl.BlockSpec((1, 1, d), mod_map),
                pl.BlockSpec((1, 1, d), mod_map),
                pl.BlockSpec((1, d), lambda i, j: (0, 0))]
    args = [x, shift, scale, g.reshape(1, d)]
    if rope is not None:
        in_specs += [pl.BlockSpec((tm, rope[0].shape[1]), lambda i, j: (j, 0))] * 2
        args += list(rope)
    in_specs.append(pl.BlockSpec(w_all.shape, lambda i, j: (0, 0), pipeline_mode=pl.Buffered(1)))
    out_shapes, out_specs = [], []
    for cols, dt in zip(widths, out_dtypes):
        out_shapes.append(jax.ShapeDtypeStruct((b, n, cols), dt))
        out_specs.append(pl.BlockSpec((1, tm, cols), lambda i, j: (i, j, 0)))
    return pl.pallas_call(
        functools.partial(_inproj_kernel, modes=tuple(modes), scales=tuple(scales),
                          col_segments=tuple(tuple(seg) for seg in col_segments)),
        out_shape=out_shapes,
        grid=(b, n // tm),
        in_specs=in_specs,
        out_specs=out_specs,
        scratch_shapes=[pltpu.VMEM((d, cols), BF16) for cols in widths],
        compiler_params=_cparams(("arbitrary", "arbitrary")),
        name="inproj",
    )(*args, w_all)


def _rope_tables(n):
    t = np.arange(n)
    pos_row, pos_col = (t // GRID_W).astype(np.float32), (t % GRID_W).astype(np.float32)
    quarter = GLA_DK // 4
    freqs = np.float32(ROPE_BASE) ** (-np.arange(quarter, dtype=np.float32) / quarter)
    ang_r = pos_row[:, None] * freqs
    ang_c = pos_col[:, None] * freqs
    cos = np.concatenate([np.cos(ang_r), np.cos(ang_r), np.cos(ang_c), np.cos(ang_c)], axis=-1)
    sin = np.concatenate([-np.sin(ang_r), np.sin(ang_r), -np.sin(ang_c), np.sin(ang_c)], axis=-1)
    return jnp.asarray(np.tile(cos, (1, GLA_HEADS))), jnp.asarray(np.tile(sin, (1, GLA_HEADS)))


def _na_patterns(rows):
    kr = min(NA_WIN_ROWS, rows)
    n_blocks = rows // NA_QROWS
    pats = []
    for blk in (0, 1, n_blocks - 1):
        r0 = blk * NA_QROWS
        k0 = int(np.clip(r0 - kr // 2, 0, rows - NA_KROWS))
        strips = []
        for a in range(NA_QROWS):
            r_start = int(np.clip(r0 + a - kr // 2, 0, rows - kr))
            start = k0 - (r0 + a) + NA_WIN_ROWS - 1 + NA_TAB_PAD
            assert 0 <= start and start + NA_KROWS <= NA_TAB_BLOCKS
            strips.append((start, [r_start <= k0 + c < r_start + kr for c in range(NA_KROWS)]))
        pats.append(strips)
    return pats


def _na_bias_tables(rpb, rows):
    heads = rpb.shape[0]
    col = np.arange(GRID_W)
    c_start = np.clip(col - NA_WIN_COLS // 2, 0, GRID_W - NA_WIN_COLS)
    col_ok = (col[None, :] >= c_start[:, None]) & (col[None, :] < c_start[:, None] + NA_WIN_COLS)
    dc = np.clip(col[None, :] - col[:, None] + NA_WIN_COLS - 1, 0, 2 * NA_WIN_COLS - 2)
    sel_c = (np.arange(2 * NA_WIN_COLS - 1)[:, None, None] == dc[None]) & col_ok[None]
    t = jnp.einsum("hrd,dqk->hqrk", rpb, jnp.asarray(sel_c, F32), precision=HIGHEST)
    t = jnp.where(jnp.asarray(col_ok)[None, :, None, :], t, NEG_BIG)
    n_dr = 2 * NA_WIN_ROWS - 1
    t = t.reshape(heads, GRID_W, n_dr * GRID_W)
    back = NA_TAB_BLOCKS + 1 - NA_TAB_PAD - n_dr
    t = jnp.pad(t, ((0, 0), (0, 0), (NA_TAB_PAD * GRID_W, back * GRID_W)), constant_values=NEG_BIG)
    width = NA_TAB_BLOCKS * GRID_W
    tab = jnp.stack([t[:, :, :width], t[:, :, GRID_W:GRID_W + width]], axis=1)
    row_mask = np.zeros((3, NA_QROWS, 1, NA_KROWS * GRID_W), np.float32)
    for pat, strips in enumerate(_na_patterns(rows)):
        for a, (_, valid) in enumerate(strips):
            row_mask[pat, a, 0] = np.repeat(np.where(valid, 0.0, NEG_BIG), GRID_W)
    return tab, jnp.asarray(row_mask)


def _na_kernel(q_ref, k_ref, v_ref, kc_ref, vc_ref, tab_ref, rmask_ref, o_ref, bias_ref,
               sw_ref, sc_ref, pw_ref, pc_ref, *, rows):
    nq, nk = NA_QROWS * GRID_W, NA_KROWS * GRID_W
    n_blocks = rows // NA_QROWS
    kr = min(NA_WIN_ROWS, rows)
    scale = NA_HEAD_DIM ** -0.5
    lane = lax.broadcasted_iota(jnp.int32, (nq, LANES), 1)
    first_head = lane < NA_HEAD_DIM
    kc = kc_ref[0]
    vc = vc_ref[0]
    lane_w = lax.broadcasted_iota(jnp.int32, (nk, LANES), 1)
    lane_c = lax.broadcasted_iota(jnp.int32, vc.shape, 1)
    @pl.when(pl.program_id(1) == 0)
    def _():
        for pat, strips in enumerate(_na_patterns(rows)):
            for a, (start, _) in enumerate(strips):
                parity = start % 2
                off = (start - parity) * GRID_W
                for h in range(2):
                    bias_ref[0, pat, h, a * GRID_W:(a + 1) * GRID_W, :] = (
                        tab_ref[h, parity, :, off:off + nk] + rmask_ref[pat, a])

    def key_start(i):
        return pl.multiple_of(jnp.clip(i * NA_QROWS - kr // 2, 0, rows - NA_KROWS) * GRID_W, GRID_W)

    def scores(i, slot):
        pat = jnp.where(i == 0, 0, jnp.where(i == n_blocks - 1, 2, 1))
        q = q_ref[0, pl.ds(pl.multiple_of(i * nq, nq), nq), :] * scale
        kw = k_ref[0, pl.ds(key_start(i), nk), :]
        for h in range(2):
            qh = jnp.where(first_head if h == 0 else jnp.logical_not(first_head), q, jnp.zeros_like(q))
            sw_ref[slot, h] = _dot_nt(qh, kw) + bias_ref[0, pat, h]
            sc_ref[slot, h] = _dot_nt(qh, kc)

    def softmax(slot):
        for h in range(2):
            s_w = sw_ref[slot, h]
            s_c = sc_ref[slot, h]
            m = jnp.maximum(jnp.max(s_w, axis=-1, keepdims=True), jnp.max(s_c, axis=-1, keepdims=True))
            pw_ref[slot, h] = jnp.exp((s_w - m).astype(BF16))
            pc_ref[slot, h] = jnp.exp((s_c - m).astype(BF16))

    def values(i, slot):
        vw = v_ref[0, pl.ds(key_start(i), nk), :]
        outs = []
        for h in range(2):
            sum_lane = NA_HEAD_DIM * (1 - h)
            vw_h = jnp.where(lane_w == sum_lane, jnp.ones_like(vw), vw)
            vc_h = jnp.where(lane_c == sum_lane, jnp.ones_like(vc), vc)
            o = _dot(pw_ref[slot, h], vw_h) + _dot(pc_ref[slot, h], vc_h)
            outs.append(o * (1.0 / o[:, sum_lane:sum_lane + 1]))
        o = jnp.where(first_head, outs[0], outs[1])
        o_ref[0, pl.ds(pl.multiple_of(i * nq, nq), nq), :] = o.astype(o_ref.dtype)

    assert n_blocks % 2 == 0 and n_blocks >= 4
    scores(0, 0)
    softmax(0)
    scores(1, 1)

    def trip(j, carry):
        i = 2 * j
        values(i - 2, 0)
        softmax(1)
        scores(i, 0)
        values(i - 1, 1)
        softmax(0)
        scores(i + 1, 1)
        return carry

    lax.fori_loop(1, n_blocks // 2, trip, 0)
    values(n_blocks - 2, 0)
    softmax(1)
    values(n_blocks - 1, 1)


def _na(q, k, v, kc, vc, tab, row_mask):
    b, n, w = q.shape
    n_ctx = kc.shape[1]
    pairs = w // LANES
    rows = n // GRID_W
    nq, nk = NA_QROWS * GRID_W, NA_KROWS * GRID_W
    tok = lambda p, i: (i, 0, p)
    return pl.pallas_call(
        functools.partial(_na_kernel, rows=rows),
        out_shape=jax.ShapeDtypeStruct((b, n, w), BF16),
        grid=(pairs, b),
        in_specs=[pl.BlockSpec((1, n, LANES), tok),
                  pl.BlockSpec((1, n, LANES), tok),
                  pl.BlockSpec((1, n, LANES), tok),
                  pl.BlockSpec((1, n_ctx, LANES), tok),
                  pl.BlockSpec((1, n_ctx, LANES), tok),
                  pl.BlockSpec((2,) + tab.shape[1:], lambda p, i: (p, 0, 0, 0)),
                  pl.BlockSpec(row_mask.shape, lambda p, i: (0, 0, 0, 0))],
        out_specs=pl.BlockSpec((1, n, LANES), tok),
        scratch_shapes=[pltpu.VMEM((1, 3, 2, nq, nk), F32),
                        pltpu.VMEM((2, 2, nq, nk), F32), pltpu.VMEM((2, 2, nq, n_ctx), F32),
                        pltpu.VMEM((2, 2, nq, nk), BF16), pltpu.VMEM((2, 2, nq, n_ctx), BF16)],
        compiler_params=_cparams(("arbitrary", "arbitrary")),
        name="na",
    )(q, k, v, kc, vc, tab, row_mask)


def _gla_prefix_matrices(t):
    i = np.arange(t)
    return np.stack([i[:, None] >= i[None, :], i[:, None] <= i[None, :]]).astype(np.float32)


def _gla_kernel(q_ref, k_ref, v_ref, ad_ref, g_ref, ck_ref, cv_ref, cad_ref,
                u_ref, ab_ref, gn_ref, cm_ref, o_ref, accf_ref, accb_ref, *, n_tok):
    t = GLA_T
    n_chunks = n_tok // t
    row = lax.broadcasted_iota(jnp.int32, (t, LANES), 0)
    hb = t // 2
    row_b = lax.broadcasted_iota(jnp.int32, (hb, LANES), 0)
    head0_b = lax.broadcasted_iota(jnp.int32, (hb, LANES), 1) < GLA_DK
    row2 = lax.broadcasted_iota(jnp.int32, (hb, 2 * hb), 0)
    col2 = lax.broadcasted_iota(jnp.int32, (hb, 2 * hb), 1) & (hb - 1)
    vrow = lax.broadcasted_iota(jnp.int32, (2 * t, 2 * GLA_DV), 0)
    vlane = lax.broadcasted_iota(jnp.int32, (2 * t, 2 * GLA_DV), 1)
    v_head_match = (vrow >= t) == (vlane >= GLA_DV)
    srow = lax.broadcasted_iota(jnp.int32, (2 * GLA_DV, LANES), 0)
    slane = lax.broadcasted_iota(jnp.int32, (2 * GLA_DV, LANES), 1)
    s_blockdiag = (srow >= GLA_DV) == (slane >= GLA_DK)
    blk_mask = {half: jnp.where((row2 & ~(2 * half - 1)) == (col2 & ~(2 * half - 1)), 1.0, 0.0)
                for half in GLA_LEVELS if 2 * half < hb}
    diag_blk = (row2 & ~(GLA_DIAG - 1)) == (col2 & ~(GLA_DIAG - 1))
    diag_mask = (jnp.where(diag_blk & (row2 >= col2), 1.0, 0.0), jnp.where(diag_blk & (row2 <= col2), 1.0, 0.0))

    def prefix_sums(ad, dirn):
        z = _dot(ad, u_ref[dirn]) + ab_ref[dirn]
        loga = (jnp.minimum(z, 0.0) - jnp.log(1.0 + jnp.exp(-jnp.abs(z)))) * (1.0 / GLA_GATE_TAU)
        hi, lo = _split_bf16(loga)
        p2 = _dot(cm_ref[dirn], jnp.concatenate([hi, lo], axis=-1))
        return p2[:, :LANES] + p2[:, LANES:]

    def chunk_end(p, dirn):
        return p[t - 1:t, :] if dirn == 0 else p[0:1, :]

    def level_sums(p, half, dirn):
        blk = 2 * half
        p3 = p.reshape(t // blk, blk, LANES)
        edge = half - 1 if dirn == 0 else half
        ref = jnp.broadcast_to(p3[:, edge:edge + 1, :], p3.shape).reshape(t, LANES)
        later = (row & half) != 0
        return jnp.where(later == (dirn == 0), p - ref, ref - p)

    def state_update(s, k, vt, p, dirn):
        kh = (k * jnp.exp(chunk_end(p, dirn) - p)).astype(BF16)
        return s * jnp.exp(chunk_end(p, dirn)) + jnp.where(s_blockdiag, _dot(vt, kh), 0.0)

    def chunk(tok0, s, dirn):
        q = q_ref[0, pl.ds(tok0, t), :].astype(F32)
        k = k_ref[0, pl.ds(tok0, t), :].astype(F32)
        v = v_ref[0, pl.ds(tok0, t), :]
        vt = v.T
        p = prefix_sums(ad_ref[0, pl.ds(tok0, t), :], dirn)
        qh = (q * jnp.exp(p)).astype(BF16)
        o = _dot_nt(qh, s.astype(BF16))
        def rows(x, b):
            return x[b * hb:(b + 1) * hb]

        def scores(qt, kt):
            kcat = jnp.concatenate([jnp.where(head0_b, kt, 0.0), jnp.where(head0_b, 0.0, kt)], axis=0)
            return _dot_nt(qt.astype(BF16), kcat.astype(BF16))

        assert GLA_LEVELS[0] == hb
        w = jnp.exp(level_sums(p, hb, dirn))
        qb, kb = (1, 0) if dirn == 0 else (0, 1)
        wide = scores(rows(q, qb) * rows(w, qb), rows(k, kb) * rows(w, kb))
        fine_w = [jnp.exp(level_sums(p, half, dirn)) for half in GLA_LEVELS[1:]]
        e_d = level_sums(p, GLA_DIAG // 2, dirn)
        w_d, wi_d = jnp.exp(e_d), jnp.exp(-e_d)
        fine = []
        for b in range(2):
            qs, ks = rows(q, b), rows(k, b)
            acc = None
            for half, w in zip(GLA_LEVELS[1:], fine_w):
                later = (row_b & half) != 0
                q_side = later if dirn == 0 else jnp.logical_not(later)
                part = scores(jnp.where(q_side, qs * rows(w, b), 0.0), jnp.where(q_side, 0.0, ks * rows(w, b)))
                if half in blk_mask:
                    part = part * blk_mask[half]
                acc = part if acc is None else acc + part
            later = (row_b & (GLA_DIAG // 2)) != 0
            shrink_q = later if dirn == 0 else jnp.logical_not(later)
            part = scores(qs * jnp.where(shrink_q, rows(w_d, b), rows(wi_d, b)),
                          ks * jnp.where(shrink_q, rows(wi_d, b), rows(w_d, b)))
            fine.append(acc + jnp.where(diag_mask[dirn] > 0.5, part, 0.0))
        zero = jnp.zeros((hb, hb), F32)
        h0, h1 = slice(0, hb), slice(hb, 2 * hb)
        if dirn == 0:
            top = [fine[0][:, h0], zero, fine[0][:, h1], zero]
            bot = [wide[:, h0], fine[1][:, h0], wide[:, h1], fine[1][:, h1]]
        else:
            top = [fine[0][:, h0], wide[:, h0], fine[0][:, h1], wide[:, h1]]
            bot = [zero, fine[1][:, h0], zero, fine[1][:, h1]]
        a = jnp.concatenate([jnp.concatenate(top, axis=1), jnp.concatenate(bot, axis=1)], axis=0)
        vcat = jnp.concatenate([v, v], axis=0)
        vcat = jnp.where(v_head_match, vcat, jnp.zeros_like(vcat))
        o = o + _dot(a.astype(BF16), vcat)
        return o, state_update(s, k, vt, p, dirn)

    def ctx_state(dirn):
        p = prefix_sums(cad_ref[0], dirn)
        s0 = jnp.zeros((2 * GLA_DV, LANES), F32)
        return state_update(s0, ck_ref[0].astype(F32), cv_ref[0].T, p, dirn)

    def finish(tok0, o):
        g = g_ref[0, pl.ds(tok0, t), :].astype(F32)
        gate = g / (1.0 + jnp.exp(-g))
        halves = [_rms(o[:, h * GLA_DV:(h + 1) * GLA_DV], gn_ref[...]) for h in range(2)]
        o_ref[0, pl.ds(tok0, t), :] = (jnp.concatenate(halves, axis=-1) * gate).astype(o_ref.dtype)

    def body(i, carry, second_half):
        s_f, s_b = carry
        for u in range(GLA_UNROLL):
            c = i * GLA_UNROLL + u
            tok_f = pl.multiple_of(c * t, t)
            tok_b = pl.multiple_of((n_chunks - 1 - c) * t, t)
            o_f, s_f = chunk(tok_f, s_f, 0)
            o_b, s_b = chunk(tok_b, s_b, 1)
            if second_half:
                finish(tok_f, o_f + accb_ref[pl.ds(tok_f, t), :])
                finish(tok_b, o_b + accf_ref[pl.ds(tok_b, t), :])
            else:
                accf_ref[pl.ds(tok_f, t), :] = o_f
                accb_ref[pl.ds(tok_b, t), :] = o_b
        return s_f, s_b

    trips = n_chunks // GLA_UNROLL
    assert n_chunks % (2 * GLA_UNROLL) == 0
    carry = lax.fori_loop(0, trips // 2, functools.partial(body, second_half=False), (ctx_state(0), ctx_state(1)))
    lax.fori_loop(trips // 2, trips, functools.partial(body, second_half=True), carry)


def _gla(q, k, v, ad, g, ck, cv, cad, u, abias, gnorm, cmats):
    b, n, kw = q.shape
    n_ctx = ck.shape[1]
    pairs = kw // LANES
    vw = 2 * GLA_DV
    tok = lambda i, p: (i, 0, p)
    full3 = lambda i, p: (i, 0, 0)
    return pl.pallas_call(
        functools.partial(_gla_kernel, n_tok=n),
        out_shape=jax.ShapeDtypeStruct((b, n, v.shape[2]), BF16),
        grid=(b, pairs),
        in_specs=[pl.BlockSpec((1, n, LANES), tok),
                  pl.BlockSpec((1, n, LANES), tok),
                  pl.BlockSpec((1, n, vw), tok),
                  pl.BlockSpec((1, n, LANES), full3),
                  pl.BlockSpec((1, n, vw), tok),
                  pl.BlockSpec((1, n_ctx, LANES), tok),
                  pl.BlockSpec((1, n_ctx, vw), tok),
                  pl.BlockSpec((1, n_ctx, LANES), full3),
                  pl.BlockSpec((2, LANES, LANES), lambda i, p: (0, 0, p)),
                  pl.BlockSpec((2, 1, LANES), lambda i, p: (0, 0, p)),
                  pl.BlockSpec((1, GLA_DV), lambda i, p: (0, 0)),
                  pl.BlockSpec(cmats.shape, lambda i, p: (0, 0, 0))],
        out_specs=pl.BlockSpec((1, n, vw), tok),
        scratch_shapes=[pltpu.VMEM((n, vw), F32), pltpu.VMEM((n, vw), F32)],
        compiler_params=_cparams(("arbitrary", "arbitrary")),
        name="gla",
    )(q, k, v, ad, g, ck, cv, cad, u, abias, gnorm, cmats)


def _gla_decay_up(a_up):
    r = GLA_GATE_RANK
    pad = jnp.zeros((2, 2 * r, GLA_KEY_W), F32)
    pad = pad.at[0, :r].set(a_up[0]).at[1, r:].set(a_up[1])
    hi, lo = _split_bf16(pad)
    return jnp.concatenate([hi, hi, lo, jnp.zeros_like(hi)], axis=1)


def _outproj_kernel(ona_ref, ogla_ref, x_ref, w1_ref, w2_ref, gm_ref, shf_ref, scf_ref, npost_ref,
                    nfpre_ref, rt_ref, xnew_ref, hf_ref, afft_ref):
    for r0 in range(0, x_ref.shape[1], TOK_TILE):
        rs = slice(r0, r0 + TOK_TILE)
        mix = _dot(ona_ref[0, rs], w1_ref[...]) + _dot(ogla_ref[0, rs], w2_ref[...])
        xn = x_ref[0, rs] + gm_ref[0] * _rms(mix, npost_ref[...])
        xnew_ref[0, rs] = xn
        h = _rms(xn, nfpre_ref[...]) * (1.0 + scf_ref[0]) + shf_ref[0]
        h_hi, h_lo = _split_bf16(h)
        hf_ref[0, rs] = _pack_bf16_pairs(h_hi)
        res = _dot(h_hi, rt_ref[...])
        logits = res[:, :LANES] + res[:, LANES:] + _dot(h_lo, rt_ref[:, :LANES])
        lane = lax.broadcasted_iota(jnp.int32, logits.shape, 1)
        logits = jnp.where(lane < N_EXPERTS, logits, NEG_BIG)
        p = jnp.exp(logits - jnp.max(logits, axis=-1, keepdims=True))
        aff = p / jnp.sum(p, axis=-1, keepdims=True)
        afft_ref[0, :, rs] = aff.T[:N_EXPERTS, :]


def _outproj(o_na, o_gla, x, w1, w2, gm, shf, scf, npost, nfpre, router_cat):
    b, n, d = x.shape
    tm = OUTPROJ_SUBTILES * TOK_TILE
    tokmap = lambda i, j: (i, j, 0)
    smp = lambda i, j: (i, 0, 0)
    cst = lambda i, j: (0, 0)
    return pl.pallas_call(
        _outproj_kernel,
        out_shape=[jax.ShapeDtypeStruct((b, n, d), F32),
                   jax.ShapeDtypeStruct((b, n, d // 2), jnp.int32),
                   jax.ShapeDtypeStruct((b, N_EXPERTS, n), F32)],
        grid=(b, n // tm),
        in_specs=[pl.BlockSpec((1, tm, o_na.shape[2]), tokmap),
                  pl.BlockSpec((1, tm, o_gla.shape[2]), tokmap),
                  pl.BlockSpec((1, tm, d), tokmap),
                  pl.BlockSpec(w1.shape, cst),
                  pl.BlockSpec(w2.shape, cst),
                  pl.BlockSpec((1, 1, d), smp),
                  pl.BlockSpec((1, 1, d), smp),
                  pl.BlockSpec((1, 1, d), smp),
                  pl.BlockSpec((1, d), cst),
                  pl.BlockSpec((1, d), cst),
                  pl.BlockSpec(router_cat.shape, cst)],
        out_specs=[pl.BlockSpec((1, tm, d), tokmap),
                   pl.BlockSpec((1, tm, d // 2), tokmap),
                   pl.BlockSpec((1, N_EXPERTS, tm), lambda i, j: (i, 0, j))],
        compiler_params=_cparams(("arbitrary", "arbitrary")),
        name="outproj",
    )(o_na, o_gla, x, w1, w2, gm, shf, scf, npost.reshape(1, d), nfpre.reshape(1, d), router_cat)


def _route_kernel(afft_ref, rt_ref, *, cap):
    a = afft_ref[...]
    e, n = a.shape
    capf = jnp.float32(cap)

    def search(i, thr_bits):
        cand = thr_bits | lax.shift_left(jnp.int32(1), 30 - i)
        cnt = jnp.sum(jnp.where(a >= lax.bitcast_convert_type(cand, F32), 1.0, 0.0), axis=-1, keepdims=True)
        return jnp.where(cnt >= capf, cand, thr_bits)

    thr_bits = lax.fori_loop(0, 31, search, jnp.zeros((e, 1), jnp.int32))
    thr = lax.bitcast_convert_type(thr_bits, F32)
    need = capf - jnp.sum(jnp.where(a > thr, 1.0, 0.0), axis=-1, keepdims=True)
    r_i = lax.broadcasted_iota(jnp.int32, (LANES, LANES), 0)
    c_i = lax.broadcasted_iota(jnp.int32, (LANES, LANES), 1)
    incl = jnp.where(r_i <= c_i, 1.0, 0.0).astype(BF16)
    off_eq = jnp.zeros((e, 1), F32)
    off_sel = jnp.zeros((e, 1), F32)
    for j in range(n // LANES):
        sl = slice(j * LANES, (j + 1) * LANES)
        a_b = a[:, sl]
        eq_b = jnp.where(a_b == thr, 1.0, 0.0)
        tie_rank = _dot(eq_b.astype(BF16), incl) - eq_b + off_eq
        off_eq = off_eq + jnp.sum(eq_b, axis=-1, keepdims=True)
        sel_b = jnp.where(a_b > thr, 1.0, jnp.where(tie_rank < need, eq_b, 0.0))
        sel = sel_b > 0.5
        rank = _dot(sel_b.astype(BF16), incl) - sel_b + off_sel
        off_sel = off_sel + jnp.sum(sel_b, axis=-1, keepdims=True)
        rsel = jnp.where(sel, rank, -1.0)
        rt_ref[:, sl] = rsel.astype(jnp.int32)


def _route(afft, cap):
    b, e, n = afft.shape
    return pl.pallas_call(
        functools.partial(_route_kernel, cap=cap),
        out_shape=jax.ShapeDtypeStruct((b * e, n), jnp.int32),
        grid=(1,),
        in_specs=[pl.BlockSpec((b * e, n), lambda i: (0, 0))],
        out_specs=pl.BlockSpec((b * e, n), lambda i: (0, 0)),
        compiler_params=_cparams(("arbitrary",)),
        name="route",
    )(afft.reshape(b * e, n)).reshape(b, e, n)


def _sc_gather(rsel_t, hf2, cap):
    b, e, n = rsel_t.shape
    width = hf2.shape[1]
    info = plsc.get_sparse_core_info()
    nc, lanes = info.num_cores, info.num_lanes
    workers = nc * info.num_subcores
    items = b * e
    assert n % lanes == 0 and cap % SC_GATHER_ROWS == 0
    per_worker = -(-items // workers)
    mesh = plsc.VectorSubcoreMesh(core_axis_name="c", subcore_axis_name="s")

    def body(rank_hbm, hf_hbm, out_hbm, rank_v, idx_v, rows_a, rows_b, sem_a, sem_b):
        bufs = ((rows_a, sem_a), (rows_b, sem_b))
        wid = lax.axis_index("s") * nc + lax.axis_index("c")
        def one_item(item):
            base_tok = (item // e) * n
            pltpu.sync_copy(rank_hbm.at[item], rank_v)

            @pl.loop(0, n // lanes)
            def _(j):
                r = rank_v[pl.ds(j * lanes, lanes)]
                tok = lax.iota(jnp.int32, lanes) + (j * lanes + base_tok)
                plsc.store_scatter(idx_v, [r], tok, mask=r >= 0)

            def gather(c):
                buf, sem = bufs[c % 2]
                rows = pl.ds(c * SC_GATHER_ROWS, SC_GATHER_ROWS)
                return pltpu.async_copy(hf_hbm.at[idx_v.at[rows]], buf, sem)

            n_chunks = cap // SC_GATHER_ROWS
            pending = gather(0)
            for c in range(n_chunks):
                nxt = gather(c + 1) if c + 1 < n_chunks else None
                pending.wait()
                pltpu.sync_copy(bufs[c % 2][0],
                                out_hbm.at[pl.ds(item * cap + c * SC_GATHER_ROWS, SC_GATHER_ROWS)])
                pending = nxt

        for k in range(per_worker):
            item = k * workers + wid
            if (k + 1) * workers <= items:
                one_item(item)
            else:
                pl.when(item < items)(functools.partial(one_item, item))

    return pl.kernel(
        body, out_type=jax.ShapeDtypeStruct((items * cap, width), hf2.dtype), mesh=mesh,
        scratch_types=[pltpu.VMEM((n,), jnp.int32), pltpu.VMEM((cap,), jnp.int32),
                       pltpu.VMEM((SC_GATHER_ROWS, width), hf2.dtype),
                       pltpu.VMEM((SC_GATHER_ROWS, width), hf2.dtype),
                       pltpu.SemaphoreType.DMA, pltpu.SemaphoreType.DMA],
        compiler_params=pltpu.CompilerParams(needs_layout_passes=False),
        name="scgather",
    )(rsel_t.reshape(items, n), hf2)


def _ffn_kernel(x_ref, wg_ref, wu_ref, wd_ref, o_ref, acc_ref, xb_ref):
    f = pl.program_id(1)
    b = x_ref.shape[0]
    last = pl.num_programs(1) - 1

    def tile(first, final):
        wg = wg_ref[0].astype(BF16)
        wu = wu_ref[0].astype(BF16)
        wd = wd_ref[0].astype(BF16)
        for i in range(b):
            if first:
                xb_ref[i] = _unpack_bf16_pairs(x_ref[i, 0])
            x = xb_ref[i]
            g = _dot(x, wg)
            u = _dot(x, wu)
            hid = (g / (1.0 + jnp.exp(-g)) * u).astype(BF16)
            y = _dot(hid, wd)
            if not first:
                y = acc_ref[i] + y
            if final:
                o_ref[i, 0] = y.astype(o_ref.dtype)
            else:
                acc_ref[i] = y

    @pl.when(f == 0)
    def _():
        tile(True, False)

    @pl.when((f > 0) & (f < last))
    def _():
        tile(False, False)

    @pl.when(f == last)
    def _():
        tile(False, True)


def _ffn(xs, w_gate, w_up, w_down, e0):
    b, e, cap, dp = xs.shape
    d = 2 * dp
    dff = w_gate.shape[2]
    tf = FFN_TILE
    assert dff // tf >= 2
    return pl.pallas_call(
        _ffn_kernel,
        out_shape=jax.ShapeDtypeStruct((b, e, cap, d), BF16),
        grid=(e, dff // tf),
        in_specs=[pl.BlockSpec((b, 1, cap, dp), lambda i, f: (0, i, 0, 0)),
                  pl.BlockSpec((1, d, tf), lambda i, f: (i + e0, 0, f)),
                  pl.BlockSpec((1, d, tf), lambda i, f: (i + e0, 0, f)),
                  pl.BlockSpec((1, tf, d), lambda i, f: (i + e0, f, 0))],
        out_specs=pl.BlockSpec((b, 1, cap, d), lambda i, f: (0, i, 0, 0)),
        scratch_shapes=[pltpu.VMEM((b, cap, d), F32), pltpu.VMEM((b, cap, d), BF16)],
        compiler_params=_cparams(("arbitrary", "arbitrary")),
        name="ffn",
    )(xs, w_gate, w_up, w_down)


def _combine_kernel(off_ref, *refs, n_groups):
    ys_refs = refs[:n_groups]
    rt_ref, afft_ref, xn_ref, gf_ref, npost_ref, o_ref, acc_ref = refs[n_groups:]
    bi, tt = pl.program_id(0), pl.program_id(1)
    ts = COMBINE_TILE
    n_sub = rt_ref.shape[2] // ts
    cap = ys_refs[0].shape[2]
    owner = [(g, k) for g, ref in enumerate(ys_refs) for k in range(ref.shape[1])]
    n_experts = len(owner)
    blocks = ts // LANES
    slot = lax.broadcasted_iota(jnp.int32, (COMBINE_WIN, ts), 0)

    for sub in range(n_sub):
        toks = slice(sub * ts, (sub + 1) * ts)
        blk0 = (tt * n_sub + sub) * blocks

        def window(e, w0, j, toks=toks):
            nominal = w0 + j * COMBINE_WIN
            start = pl.multiple_of(jnp.minimum(nominal, cap - COMBINE_WIN), BF16_ROWS)
            rank = rt_ref[0, e:e + 1, toks]
            hit = ((rank - start) == slot) & (rank >= nominal)
            weights = jnp.where(hit, afft_ref[0, e:e + 1, toks], 0.0).astype(BF16)
            g, k = owner[e]
            return weights, ys_refs[g][0, k, pl.ds(start, COMBINE_WIN), :]

        first, extra = [], []
        for e in range(n_experts):
            r0 = off_ref[bi, e, blk0]
            r1 = off_ref[bi, e, blk0 + blocks]
            w0 = (r0 // BF16_ROWS) * BF16_ROWS
            first.append(w0)
            extra.append(jnp.maximum((r1 - w0 + COMBINE_WIN - 1) // COMBINE_WIN - 1, 0))
        terms = []
        for e in range(0, n_experts, COMBINE_STACK):
            ws, ys = zip(*[window(e + k, first[e + k], 0) for k in range(COMBINE_STACK)])
            terms.append(_dot(jnp.concatenate(ws, axis=0).T, jnp.concatenate(ys, axis=0)))
        acc_ref[toks] = functools.reduce(lambda a, c: a + c, terms)

        @pl.when(functools.reduce(lambda a, c: a + c, extra) > 0)
        def _(window=window, first=first, extra=extra, toks=toks):
            for e in range(n_experts):
                def more(j, carry, e=e):
                    w, y = window(e, first[e], j)
                    acc_ref[toks] += _dot(w.T, y)
                    return carry
                lax.fori_loop(1, extra[e] + 1, more, 0)

    o_ref[0] = xn_ref[0] + gf_ref[0] * _rms(acc_ref[...], npost_ref[...])


def _combine(ys_groups, slot_off, rsel_t, aff_t, x_new, gf, npost):
    b, _, cap, d = ys_groups[0].shape
    e = sum(y.shape[1] for y in ys_groups)
    n = x_new.shape[1]
    tm = COMBINE_SUBTILES * COMBINE_TILE
    tokmap = lambda i, j, off: (i, j, 0)
    return pl.pallas_call(
        functools.partial(_combine_kernel, n_groups=len(ys_groups)),
        out_shape=jax.ShapeDtypeStruct((b, n, d), F32),
        grid_spec=pltpu.PrefetchScalarGridSpec(
            num_scalar_prefetch=1,
            grid=(b, n // tm),
            in_specs=[pl.BlockSpec((1, y.shape[1], cap, d), lambda i, j, off: (i, 0, 0, 0)) for y in ys_groups] + [
                      pl.BlockSpec((1, e, tm), lambda i, j, off: (i, 0, j)),
                      pl.BlockSpec((1, e, tm), lambda i, j, off: (i, 0, j)),
                      pl.BlockSpec((1, tm, d), tokmap),
                      pl.BlockSpec((1, 1, d), lambda i, j, off: (i, 0, 0)),
                      pl.BlockSpec((1, d), lambda i, j, off: (0, 0))],
            out_specs=pl.BlockSpec((1, tm, d), tokmap),
            scratch_shapes=[pltpu.VMEM((tm, d), F32)]),
        compiler_params=_cparams(("arbitrary", "arbitrary")),
        name="combine",
    )(slot_off, *ys_groups, rsel_t, aff_t, x_new, gf, npost.reshape(1, d))


def kernel(x, c, ctx, c_ctx, w_mod, b_mod, norm_mix_pre, norm_mix_post, norm_ffn_pre, norm_ffn_post,
           w_in, na_rpb, gla_a_up, gla_a_bias, gla_norm, w_out, router, w_gate, w_up, w_down):
    b, n, d = x.shape
    assert w_mod.shape[0] == 1 and d == D_MODEL and n % (GRID_W * NA_QROWS) == 0 and n % GLA_T == 0
    assert ctx.shape[1] == GLA_T
    rows = n // GRID_W
    cap = EC_CAPACITY_FACTOR * n // N_EXPERTS

    c8 = jnp.concatenate([c, c_ctx[None, :], jnp.zeros((8 - b - 1, d), F32)], axis=0)
    mod = _mod(c8, w_mod[0], b_mod[0])
    sh_m, sc_m, g_m, sh_f, sc_f, g_f = [m[:b, None, :] for m in jnp.split(mod, 6, axis=-1)]
    sh_c, sc_c = mod[b:b + 1, None, :d], mod[b:b + 1, None, d:2 * d]

    cuts = [int(c) for c in np.cumsum([0, NA_W, NA_W, GLA_KEY_W, GLA_VAL_W, 2 * GLA_GATE_RANK, NA_W, GLA_KEY_W, GLA_VAL_W])]
    c_nak, c_nav, c_gk, c_gv, c_ad, c_naq, c_gq, c_gg = [[(cuts[i], cuts[i + 1])] for i in range(8)]
    c_ad3 = c_ad * 3
    rope = _rope_tables(n)
    na_q, na_k, na_v, gq, gk, gv, ad, gg = _inproj(
        x, sh_m, sc_m, norm_mix_pre[0], w_in[0],
        [c_naq, c_nak, c_nav, c_gq, c_gk, c_gv, c_ad3, c_gg],
        [NA_W, NA_W, NA_W, GLA_KEY_W, GLA_KEY_W, GLA_VAL_W, LANES, GLA_VAL_W],
        ["plain", "plain", "plain", "rope", "rope", "plain", "split", "plain"],
        [1.0, 1.0, 1.0, GLA_DK ** -0.5, 1.0, 1.0, 1.0, 1.0],
        [BF16] * 8, rope=rope)
    c_nak, c_nav, c_gk, c_gv, c_ad = _inproj(
        ctx, sh_c, sc_c, norm_mix_pre[0], w_in[0],
        [c_nak, c_nav, c_gk, c_gv, c_ad3],
        [NA_W, NA_W, GLA_KEY_W, GLA_VAL_W, LANES],
        ["plain", "plain", "plain", "plain", "split"], [1.0] * 5, [BF16] * 5)

    o_na = _na(na_q, na_k, na_v, c_nak, c_nav, *_na_bias_tables(na_rpb[0], rows))

    cmats = jnp.asarray(_gla_prefix_matrices(GLA_T), BF16)
    o_gla = _gla(gq, gk, gv, ad, gg, c_gk, c_gv, c_ad, _gla_decay_up(gla_a_up[0]),
                 gla_a_bias[0][:, None, :], gla_norm[0][None, :], cmats)

    wo = w_out[0].astype(BF16)
    router_pad = jnp.zeros((d, LANES), F32).at[:, :N_EXPERTS].set(router[0])
    x_new, hf, aff_t = _outproj(o_na, o_gla, x, wo[:NA_W], wo[NA_W:], g_m, sh_f, sc_f,
                                     norm_mix_post[0], norm_ffn_pre[0],
                                     jnp.concatenate(_split_bf16(router_pad), axis=1))

    rsel_t = _route(aff_t, cap)
    ys, e0 = [], 0
    for group in FFN_GROUP_SIZES:
        xs = _sc_gather(rsel_t[:, e0:e0 + group], hf.reshape(b * n, d // 2), cap)
        ys.append(_ffn(xs.reshape(b, group, cap, d // 2), w_gate[0], w_up[0], w_down[0], e0))
        e0 += group
    picked = (rsel_t >= 0).astype(jnp.int32).reshape(b, N_EXPERTS, n // LANES, LANES).sum(axis=-1)
    slot_off = jnp.concatenate([jnp.zeros((b, N_EXPERTS, 1), jnp.int32), jnp.cumsum(picked, axis=-1)], axis=-1)
    return _combine(ys, slot_off, rsel_t, aff_t, x_new, g_f, norm_ffn_post[0])
```

```python
import functools

import numpy as np
import jax
import jax.numpy as jnp
from jax import lax
from jax.experimental import pallas as pl
from jax.experimental.pallas import tpu as pltpu
from jax.experimental.pallas import tpu_sc as plsc

F32 = jnp.float32
BF16 = jnp.bfloat16
HIGHEST = lax.Precision.HIGHEST

D_MODEL = 1024
GRID_W = 64
NA_W = 512
NA_HEADS = 8
NA_HEAD_DIM = 64
NA_WIN_ROWS = 8
NA_WIN_COLS = 16
GLA_HEADS = 4
GLA_DV = 128
GLA_DK = 64
GLA_KEY_W = 256
GLA_VAL_W = 512
GLA_GATE_RANK = 16
GLA_GATE_TAU = 16.0
ROPE_BASE = 10000.0
N_EXPERTS = 16
EC_CAPACITY_FACTOR = 2
NORM_EPS = 1e-6
NEG_BIG = -1e30

LANES = 128
BF16_ROWS = 16
VMEM_LIMIT = 56 * 1024 * 1024

TOK_TILE = 512
NA_QROWS = 4
NA_KROWS = NA_QROWS + NA_WIN_ROWS
NA_TAB_PAD = NA_QROWS
NA_TAB_BLOCKS = NA_TAB_PAD + 2 * NA_WIN_ROWS - 1 + NA_QROWS + 1
GLA_T = 256
GLA_LEVELS = (128, 64, 32, 16)
GLA_DIAG = 16
GLA_UNROLL = 4
OUTPROJ_SUBTILES = 2
FFN_GROUP_SIZES = (2, 14)
FFN_TILE = 256
COMBINE_TILE = 256
COMBINE_SUBTILES = 4
COMBINE_WIN = 64
COMBINE_STACK = 4
SC_GATHER_ROWS = 64


def _cparams(sem):
    return pltpu.CompilerParams(dimension_semantics=sem, vmem_limit_bytes=VMEM_LIMIT)


def _rms(v, g):
    return v * lax.rsqrt(jnp.mean(v * v, axis=-1, keepdims=True) + NORM_EPS) * g


def _dot(a, b):
    return jnp.dot(a, b, preferred_element_type=F32)


def _dot_nt(a, b):
    return lax.dot_general(a, b, (((1,), (1,)), ((), ())), preferred_element_type=F32)


def _split_bf16(v):
    hi = v.astype(BF16)
    return hi, (v - hi.astype(F32)).astype(BF16)


_HIGH_HALF = -65536


def _pack_bf16_pairs(hb):
    bits = lax.bitcast_convert_type(hb.astype(F32), jnp.int32)
    half = hb.shape[1] // 2
    return lax.shift_right_logical(bits[:, :half], 16) | (bits[:, half:] & _HIGH_HALF)


def _unpack_bf16_pairs(w):
    lo = lax.bitcast_convert_type(lax.shift_left(w, 16), F32).astype(BF16)
    hi = lax.bitcast_convert_type(w & _HIGH_HALF, F32).astype(BF16)
    return jnp.concatenate([lo, hi], axis=1)


def _mod_kernel(c_ref, w_ref, b_ref, o_ref):
    c = c_ref[...]
    s = c / (1.0 + jnp.exp(-c))
    o_ref[...] = _dot(s.astype(BF16), w_ref[...].astype(BF16)) + b_ref[...]


def _mod(c8, w_mod, b_mod):
    d, n = w_mod.shape
    tn = 1536
    return pl.pallas_call(
        _mod_kernel,
        out_shape=jax.ShapeDtypeStruct((8, n), F32),
        grid=(n // tn,),
        in_specs=[pl.BlockSpec((8, d), lambda j: (0, 0)),
                  pl.BlockSpec((d, tn), lambda j: (0, j)),
                  pl.BlockSpec((1, tn), lambda j: (0, j))],
        out_specs=pl.BlockSpec((8, tn), lambda j: (0, j)),
        compiler_params=_cparams(("arbitrary",)),
        name="mod",
    )(c8, w_mod, b_mod.reshape(1, n))


def _inproj_kernel(x_ref, sh_ref, sc_ref, g_ref, *refs, modes, scales, col_segments):
    n_rope = 2 if "rope" in modes else 0
    rope_refs, refs = refs[:n_rope], refs[n_rope:]
    n_w = len(modes)
    w_ref, o_refs, wb_refs = refs[0], refs[1:1 + n_w], refs[1 + n_w:]

    @pl.when((pl.program_id(0) == 0) & (pl.program_id(1) == 0))
    def _():
        for segments, wb_ref in zip(col_segments, wb_refs):
            off = 0
            for lo, hi in segments:
                wb_ref[:, off:off + hi - lo] = w_ref[:, lo:hi].astype(BF16)
                off += hi - lo
            if off < wb_ref.shape[1]:
                wb_ref[:, off:] = jnp.zeros((wb_ref.shape[0], wb_ref.shape[1] - off), BF16)

    x = x_ref[0]
    h = _rms(x, g_ref[...]) * (1.0 + sc_ref[0]) + sh_ref[0]
    hb = h.astype(BF16)
    for w_ref, o_ref, mode, scale in zip(wb_refs, o_refs, modes, scales):
        y = _dot(hb, w_ref[...])
        if mode == "rope":
            cols = y.shape[1]
            quarter = GLA_DK // 4
            lane = lax.broadcasted_iota(jnp.int32, y.shape, 1)
            partner = jnp.where((lane & (2 * quarter - 1)) < quarter,
                                pltpu.roll(y, cols - quarter, axis=1), pltpu.roll(y, quarter, axis=1))
            y = (y * rope_refs[0][...] + partner * rope_refs[1][...]) * scale
        elif mode == "split":
            hi, lo = _split_bf16(y)
            lane = lax.broadcasted_iota(jnp.int32, y.shape, 1)
            rank2 = 2 * GLA_GATE_RANK
            y = jnp.where((lane >= rank2) & (lane < 2 * rank2), lo, hi)
        o_ref[0] = y.astype(o_ref.dtype)


def _inproj(x, shift, scale, g, w_all, col_segments, widths, modes, scales, out_dtypes, rope=None):
    b, n, d = x.shape
    tm = min(TOK_TILE, n)
    per_sample = shift.shape[0] == b
    mod_map = (lambda i, j: (i, 0, 0)) if per_sample else (lambda i, j: (0, 0, 0))
    in_specs = [pl.BlockSpec((1, tm, d), lambda i, j: (i, j, 0)),
                pl.BlockSpec((1, 1, d), mod_map),
                pl.BlockSpec((1, 1, d), mod_map),
                pl.BlockSpec((1, d), lambda i, j: (0, 0))]
    args = [x, shift, scale, g.reshape(1, d)]
    if rope is not None:
        in_specs += [pl.BlockSpec((tm, rope[0].shape[1]), lambda i, j: (j, 0))] * 2
        args += list(rope)
    in_specs.append(pl.BlockSpec(w_all.shape, lambda i, j: (0, 0), pipeline_mode=pl.Buffered(1)))
    out_shapes, out_specs = [], []
    for cols, dt in zip(widths, out_dtypes):
        out_shapes.append(jax.ShapeDtypeStruct((b, n, cols), dt))
        out_specs.append(pl.BlockSpec((1, tm, cols), lambda i, j: (i, j, 0)))
    return pl.pallas_call(
        functools.partial(_inproj_kernel, modes=tuple(modes), scales=tuple(scales),
                          col_segments=tuple(tuple(seg) for seg in col_segments)),
        out_shape=out_shapes,
        grid=(b, n // tm),
        in_specs=in_specs,
        out_specs=out_specs,
        scratch_shapes=[pltpu.VMEM((d, cols), BF16) for cols in widths],
        compiler_params=_cparams(("arbitrary", "arbitrary")),
        name="inproj",
    )(*args, w_all)


def _rope_tables(n):
    t = np.arange(n)
    pos_row, pos_col = (t // GRID_W).astype(np.float32), (t % GRID_W).astype(np.float32)
    quarter = GLA_DK // 4
    freqs = np.float32(ROPE_BASE) ** (-np.arange(quarter, dtype=np.float32) / quarter)
    ang_r = pos_row[:, None] * freqs
    ang_c = pos_col[:, None] * freqs
    cos = np.concatenate([np.cos(ang_r), np.cos(ang_r), np.cos(ang_c), np.cos(ang_c)], axis=-1)
    sin = np.concatenate([-np.sin(ang_r), np.sin(ang_r), -np.sin(ang_c), np.sin(ang_c)], axis=-1)
    return jnp.asarray(np.tile(cos, (1, GLA_HEADS))), jnp.asarray(np.tile(sin, (1, GLA_HEADS)))


def _na_patterns(rows):
    kr = min(NA_WIN_ROWS, rows)
    n_blocks = rows // NA_QROWS
    pats = []
    for blk in (0, 1, n_blocks - 1):
        r0 = blk * NA_QROWS
        k0 = int(np.clip(r0 - kr // 2, 0, rows - NA_KROWS))
        strips = []
        for a in range(NA_QROWS):
            r_start = int(np.clip(r0 + a - kr // 2, 0, rows - kr))
            start = k0 - (r0 + a) + NA_WIN_ROWS - 1 + NA_TAB_PAD
            assert 0 <= start and start + NA_KROWS <= NA_TAB_BLOCKS
            strips.append((start, [r_start <= k0 + c < r_start + kr for c in range(NA_KROWS)]))
        pats.append(strips)
    return pats


def _na_bias_tables(rpb, rows):
    heads = rpb.shape[0]
    col = np.arange(GRID_W)
    c_start = np.clip(col - NA_WIN_COLS // 2, 0, GRID_W - NA_WIN_COLS)
    col_ok = (col[None, :] >= c_start[:, None]) & (col[None, :] < c_start[:, None] + NA_WIN_COLS)
    dc = np.clip(col[None, :] - col[:, None] + NA_WIN_COLS - 1, 0, 2 * NA_WIN_COLS - 2)
    sel_c = (np.arange(2 * NA_WIN_COLS - 1)[:, None, None] == dc[None]) & col_ok[None]
    t = jnp.einsum("hrd,dqk->hqrk", rpb, jnp.asarray(sel_c, F32), precision=HIGHEST)
    t = jnp.where(jnp.asarray(col_ok)[None, :, None, :], t, NEG_BIG)
    n_dr = 2 * NA_WIN_ROWS - 1
    t = t.reshape(heads, GRID_W, n_dr * GRID_W)
    back = NA_TAB_BLOCKS + 1 - NA_TAB_PAD - n_dr
    t = jnp.pad(t, ((0, 0), (0, 0), (NA_TAB_PAD * GRID_W, back * GRID_W)), constant_values=NEG_BIG)
    width = NA_TAB_BLOCKS * GRID_W
    tab = jnp.stack([t[:, :, :width], t[:, :, GRID_W:GRID_W + width]], axis=1)
    row_mask = np.zeros((3, NA_QROWS, 1, NA_KROWS * GRID_W), np.float32)
    for pat, strips in enumerate(_na_patterns(rows)):
        for a, (_, valid) in enumerate(strips):
            row_mask[pat, a, 0] = np.repeat(np.where(valid, 0.0, NEG_BIG), GRID_W)
    return tab, jnp.asarray(row_mask)


def _na_kernel(q_ref, k_ref, v_ref, kc_ref, vc_ref, tab_ref, rmask_ref, o_ref, bias_ref,
               sw_ref, sc_ref, pw_ref, pc_ref, *, rows):
    nq, nk = NA_QROWS * GRID_W, NA_KROWS * GRID_W
    n_blocks = rows // NA_QROWS
    kr = min(NA_WIN_ROWS, rows)
    scale = NA_HEAD_DIM ** -0.5
    lane = lax.broadcasted_iota(jnp.int32, (nq, LANES), 1)
    first_head = lane < NA_HEAD_DIM
    kc = kc_ref[0]
    vc = vc_ref[0]
    lane_w = lax.broadcasted_iota(jnp.int32, (nk, LANES), 1)
    lane_c = lax.broadcasted_iota(jnp.int32, vc.shape, 1)
    @pl.when(pl.program_id(1) == 0)
    def _():
        for pat, strips in enumerate(_na_patterns(rows)):
            for a, (start, _) in enumerate(strips):
                parity = start % 2
                off = (start - parity) * GRID_W
                for h in range(2):
                    bias_ref[0, pat, h, a * GRID_W:(a + 1) * GRID_W, :] = (
                        tab_ref[h, parity, :, off:off + nk] + rmask_ref[pat, a])

    def key_start(i):
        return pl.multiple_of(jnp.clip(i * NA_QROWS - kr // 2, 0, rows - NA_KROWS) * GRID_W, GRID_W)

    def scores(i, slot):
        pat = jnp.where(i == 0, 0, jnp.where(i == n_blocks - 1, 2, 1))
        q = q_ref[0, pl.ds(pl.multiple_of(i * nq, nq), nq), :] * scale
        kw = k_ref[0, pl.ds(key_start(i), nk), :]
        for h in range(2):
            qh = jnp.where(first_head if h == 0 else jnp.logical_not(first_head), q, jnp.zeros_like(q))
            sw_ref[slot, h] = _dot_nt(qh, kw) + bias_ref[0, pat, h]
            sc_ref[slot, h] = _dot_nt(qh, kc)

    def softmax(slot):
        for h in range(2):
            s_w = sw_ref[slot, h]
            s_c = sc_ref[slot, h]
            m = jnp.maximum(jnp.max(s_w, axis=-1, keepdims=True), jnp.max(s_c, axis=-1, keepdims=True))
            pw_ref[slot, h] = jnp.exp((s_w - m).astype(BF16))
            pc_ref[slot, h] = jnp.exp((s_c - m).astype(BF16))

    def values(i, slot):
        vw = v_ref[0, pl.ds(key_start(i), nk), :]
        outs = []
        for h in range(2):
            sum_lane = NA_HEAD_DIM * (1 - h)
            vw_h = jnp.where(lane_w == sum_lane, jnp.ones_like(vw), vw)
            vc_h = jnp.where(lane_c == sum_lane, jnp.ones_like(vc), vc)
            o = _dot(pw_ref[slot, h], vw_h) + _dot(pc_ref[slot, h], vc_h)
            outs.append(o * (1.0 / o[:, sum_lane:sum_lane + 1]))
        o = jnp.where(first_head, outs[0], outs[1])
        o_ref[0, pl.ds(pl.multiple_of(i * nq, nq), nq), :] = o.astype(o_ref.dtype)

    assert n_blocks % 2 == 0 and n_blocks >= 4
    scores(0, 0)
    softmax(0)
    scores(1, 1)

    def trip(j, carry):
        i = 2 * j
        values(i - 2, 0)
        softmax(1)
        scores(i, 0)
        values(i - 1, 1)
        softmax(0)
        scores(i + 1, 1)
        return carry

    lax.fori_loop(1, n_blocks // 2, trip, 0)
    values(n_blocks - 2, 0)
    softmax(1)
    values(n_blocks - 1, 1)


def _na(q, k, v, kc, vc, tab, row_mask):
    b, n, w = q.shape
    n_ctx = kc.shape[1]
    pairs = w // LANES
    rows = n // GRID_W
    nq, nk = NA_QROWS * GRID_W, NA_KROWS * GRID_W
    tok = lambda p, i: (i, 0, p)
    return pl.pallas_call(
        functools.partial(_na_kernel, rows=rows),
        out_shape=jax.ShapeDtypeStruct((b, n, w), BF16),
        grid=(pairs, b),
        in_specs=[pl.BlockSpec((1, n, LANES), tok),
                  pl.BlockSpec((1, n, LANES), tok),
                  pl.BlockSpec((1, n, LANES), tok),
                  pl.BlockSpec((1, n_ctx, LANES), tok),
                  pl.BlockSpec((1, n_ctx, LANES), tok),
                  pl.BlockSpec((2,) + tab.shape[1:], lambda p, i: (p, 0, 0, 0)),
                  pl.BlockSpec(row_mask.shape, lambda p, i: (0, 0, 0, 0))],
        out_specs=pl.BlockSpec((1, n, LANES), tok),
        scratch_shapes=[pltpu.VMEM((1, 3, 2, nq, nk), F32),
                        pltpu.VMEM((2, 2, nq, nk), F32), pltpu.VMEM((2, 2, nq, n_ctx), F32),
                        pltpu.VMEM((2, 2, nq, nk), BF16), pltpu.VMEM((2, 2, nq, n_ctx), BF16)],
        compiler_params=_cparams(("arbitrary", "arbitrary")),
        name="na",
    )(q, k, v, kc, vc, tab, row_mask)


def _gla_prefix_matrices(t):
    i = np.arange(t)
    return np.stack([i[:, None] >= i[None, :], i[:, None] <= i[None, :]]).astype(np.float32)


def _gla_kernel(q_ref, k_ref, v_ref, ad_ref, g_ref, ck_ref, cv_ref, cad_ref,
                u_ref, ab_ref, gn_ref, cm_ref, o_ref, accf_ref, accb_ref, *, n_tok):
    t = GLA_T
    n_chunks = n_tok // t
    row = lax.broadcasted_iota(jnp.int32, (t, LANES), 0)
    hb = t // 2
    row_b = lax.broadcasted_iota(jnp.int32, (hb, LANES), 0)
    head0_b = lax.broadcasted_iota(jnp.int32, (hb, LANES), 1) < GLA_DK
    row2 = lax.broadcasted_iota(jnp.int32, (hb, 2 * hb), 0)
    col2 = lax.broadcasted_iota(jnp.int32, (hb, 2 * hb), 1) & (hb - 1)
    vrow = lax.broadcasted_iota(jnp.int32, (2 * t, 2 * GLA_DV), 0)
    vlane = lax.broadcasted_iota(jnp.int32, (2 * t, 2 * GLA_DV), 1)
    v_head_match = (vrow >= t) == (vlane >= GLA_DV)
    srow = lax.broadcasted_iota(jnp.int32, (2 * GLA_DV, LANES), 0)
    slane = lax.broadcasted_iota(jnp.int32, (2 * GLA_DV, LANES), 1)
    s_blockdiag = (srow >= GLA_DV) == (slane >= GLA_DK)
    blk_mask = {half: jnp.where((row2 & ~(2 * half - 1)) == (col2 & ~(2 * half - 1)), 1.0, 0.0)
                for half in GLA_LEVELS if 2 * half < hb}
    diag_blk = (row2 & ~(GLA_DIAG - 1)) == (col2 & ~(GLA_DIAG - 1))
    diag_mask = (jnp.where(diag_blk & (row2 >= col2), 1.0, 0.0), jnp.where(diag_blk & (row2 <= col2), 1.0, 0.0))

    def prefix_sums(ad, dirn):
        z = _dot(ad, u_ref[dirn]) + ab_ref[dirn]
        loga = (jnp.minimum(z, 0.0) - jnp.log(1.0 + jnp.exp(-jnp.abs(z)))) * (1.0 / GLA_GATE_TAU)
        hi, lo = _split_bf16(loga)
        p2 = _dot(cm_ref[dirn], jnp.concatenate([hi, lo], axis=-1))
        return p2[:, :LANES] + p2[:, LANES:]

    def chunk_end(p, dirn):
        return p[t - 1:t, :] if dirn == 0 else p[0:1, :]

    def level_sums(p, half, dirn):
        blk = 2 * half
        p3 = p.reshape(t // blk, blk, LANES)
        edge = half - 1 if dirn == 0 else half
        ref = jnp.broadcast_to(p3[:, edge:edge + 1, :], p3.shape).reshape(t, LANES)
        later = (row & half) != 0
        return jnp.where(later == (dirn == 0), p - ref, ref - p)

    def state_update(s, k, vt, p, dirn):
        kh = (k * jnp.exp(chunk_end(p, dirn) - p)).astype(BF16)
        return s * jnp.exp(chunk_end(p, dirn)) + jnp.where(s_blockdiag, _dot(vt, kh), 0.0)

    def chunk(tok0, s, dirn):
        q = q_ref[0, pl.ds(tok0, t), :].astype(F32)
        k = k_ref[0, pl.ds(tok0, t), :].astype(F32)
        v = v_ref[0, pl.ds(tok0, t), :]
        vt = v.T
        p = prefix_sums(ad_ref[0, pl.ds(tok0, t), :], dirn)
        qh = (q * jnp.exp(p)).astype(BF16)
        o = _dot_nt(qh, s.astype(BF16))
        def rows(x, b):
            return x[b * hb:(b + 1) * hb]

        def scores(qt, kt):
            kcat = jnp.concatenate([jnp.where(head0_b, kt, 0.0), jnp.where(head0_b, 0.0, kt)], axis=0)
            return _dot_nt(qt.astype(BF16), kcat.astype(BF16))

        assert GLA_LEVELS[0] == hb
        w = jnp.exp(level_sums(p, hb, dirn))
        qb, kb = (1, 0) if dirn == 0 else (0, 1)
        wide = scores(rows(q, qb) * rows(w, qb), rows(k, kb) * rows(w, kb))
        fine_w = [jnp.exp(level_sums(p, half, dirn)) for half in GLA_LEVELS[1:]]
        e_d = level_sums(p, GLA_DIAG // 2, dirn)
        w_d, wi_d = jnp.exp(e_d), jnp.exp(-e_d)
        fine = []
        for b in range(2):
            qs, ks = rows(q, b), rows(k, b)
            acc = None
            for half, w in zip(GLA_LEVELS[1:], fine_w):
                later = (row_b & half) != 0
                q_side = later if dirn == 0 else jnp.logical_not(later)
                part = scores(jnp.where(q_side, qs * rows(w, b), 0.0), jnp.where(q_side, 0.0, ks * rows(w, b)))
                if half in blk_mask:
                    part = part * blk_mask[half]
                acc = part if acc is None else acc + part
            later = (row_b & (GLA_DIAG // 2)) != 0
            shrink_q = later if dirn == 0 else jnp.logical_not(later)
            part = scores(qs * jnp.where(shrink_q, rows(w_d, b), rows(wi_d, b)),
                          ks * jnp.where(shrink_q, rows(wi_d, b), rows(w_d, b)))
            fine.append(acc + jnp.where(diag_mask[dirn] > 0.5, part, 0.0))
        zero = jnp.zeros((hb, hb), F32)
        h0, h1 = slice(0, hb), slice(hb, 2 * hb)
        if dirn == 0:
            top = [fine[0][:, h0], zero, fine[0][:, h1], zero]
            bot = [wide[:, h0], fine[1][:, h0], wide[:, h1], fine[1][:, h1]]
        else:
            top = [fine[0][:, h0], wide[:, h0], fine[0][:, h1], wide[:, h1]]
            bot = [zero, fine[1][:, h0], zero, fine[1][:, h1]]
        a = jnp.concatenate([jnp.concatenate(top, axis=1), jnp.concatenate(bot, axis=1)], axis=0)
        vcat = jnp.concatenate([v, v], axis=0)
        vcat = jnp.where(v_head_match, vcat, jnp.zeros_like(vcat))
        o = o + _dot(a.astype(BF16), vcat)
        return o, state_update(s, k, vt, p, dirn)

    def ctx_state(dirn):
        p = prefix_sums(cad_ref[0], dirn)
        s0 = jnp.zeros((2 * GLA_DV, LANES), F32)
        return state_update(s0, ck_ref[0].astype(F32), cv_ref[0].T, p, dirn)

    def finish(tok0, o):
        g = g_ref[0, pl.ds(tok0, t), :].astype(F32)
        gate = g / (1.0 + jnp.exp(-g))
        halves = [_rms(o[:, h * GLA_DV:(h + 1) * GLA_DV], gn_ref[...]) for h in range(2)]
        o_ref[0, pl.ds(tok0, t), :] = (jnp.concatenate(halves, axis=-1) * gate).astype(o_ref.dtype)

    def body(i, carry, second_half):
        s_f, s_b = carry
        for u in range(GLA_UNROLL):
            c = i * GLA_UNROLL + u
            tok_f = pl.multiple_of(c * t, t)
            tok_b = pl.multiple_of((n_chunks - 1 - c) * t, t)
            o_f, s_f = chunk(tok_f, s_f, 0)
            o_b, s_b = chunk(tok_b, s_b, 1)
            if second_half:
                finish(tok_f, o_f + accb_ref[pl.ds(tok_f, t), :])
                finish(tok_b, o_b + accf_ref[pl.ds(tok_b, t), :])
            else:
                accf_ref[pl.ds(tok_f, t), :] = o_f
                accb_ref[pl.ds(tok_b, t), :] = o_b
        return s_f, s_b

    trips = n_chunks // GLA_UNROLL
    assert n_chunks % (2 * GLA_UNROLL) == 0
    carry = lax.fori_loop(0, trips // 2, functools.partial(body, second_half=False), (ctx_state(0), ctx_state(1)))
    lax.fori_loop(trips // 2, trips, functools.partial(body, second_half=True), carry)


def _gla(q, k, v, ad, g, ck, cv, cad, u, abias, gnorm, cmats):
    b, n, kw = q.shape
    n_ctx = ck.shape[1]
    pairs = kw // LANES
    vw = 2 * GLA_DV
    tok = lambda i, p: (i, 0, p)
    full3 = lambda i, p: (i, 0, 0)
    return pl.pallas_call(
        functools.partial(_gla_kernel, n_tok=n),
        out_shape=jax.ShapeDtypeStruct((b, n, v.shape[2]), BF16),
        grid=(b, pairs),
        in_specs=[pl.BlockSpec((1, n, LANES), tok),
                  pl.BlockSpec((1, n, LANES), tok),
                  pl.BlockSpec((1, n, vw), tok),
                  pl.BlockSpec((1, n, LANES), full3),
                  pl.BlockSpec((1, n, vw), tok),
                  pl.BlockSpec((1, n_ctx, LANES), tok),
                  pl.BlockSpec((1, n_ctx, vw), tok),
                  pl.BlockSpec((1, n_ctx, LANES), full3),
                  pl.BlockSpec((2, LANES, LANES), lambda i, p: (0, 0, p)),
                  pl.BlockSpec((2, 1, LANES), lambda i, p: (0, 0, p)),
                  pl.BlockSpec((1, GLA_DV), lambda i, p: (0, 0)),
                  pl.BlockSpec(cmats.shape, lambda i, p: (0, 0, 0))],
        out_specs=pl.BlockSpec((1, n, vw), tok),
        scratch_shapes=[pltpu.VMEM((n, vw), F32), pltpu.VMEM((n, vw), F32)],
        compiler_params=_cparams(("arbitrary", "arbitrary")),
        name="gla",
    )(q, k, v, ad, g, ck, cv, cad, u, abias, gnorm, cmats)


def _gla_decay_up(a_up):
    r = GLA_GATE_RANK
    pad = jnp.zeros((2, 2 * r, GLA_KEY_W), F32)
    pad = pad.at[0, :r].set(a_up[0]).at[1, r:].set(a_up[1])
    hi, lo = _split_bf16(pad)
    return jnp.concatenate([hi, hi, lo, jnp.zeros_like(hi)], axis=1)


def _outproj_kernel(ona_ref, ogla_ref, x_ref, w1_ref, w2_ref, gm_ref, shf_ref, scf_ref, npost_ref,
                    nfpre_ref, rt_ref, xnew_ref, hf_ref, afft_ref):
    for r0 in range(0, x_ref.shape[1], TOK_TILE):
        rs = slice(r0, r0 + TOK_TILE)
        mix = _dot(ona_ref[0, rs], w1_ref[...]) + _dot(ogla_ref[0, rs], w2_ref[...])
        xn = x_ref[0, rs] + gm_ref[0] * _rms(mix, npost_ref[...])
        xnew_ref[0, rs] = xn
        h = _rms(xn, nfpre_ref[...]) * (1.0 + scf_ref[0]) + shf_ref[0]
        h_hi, h_lo = _split_bf16(h)
        hf_ref[0, rs] = _pack_bf16_pairs(h_hi)
        res = _dot(h_hi, rt_ref[...])
        logits = res[:, :LANES] + res[:, LANES:] + _dot(h_lo, rt_ref[:, :LANES])
        lane = lax.broadcasted_iota(jnp.int32, logits.shape, 1)
        logits = jnp.where(lane < N_EXPERTS, logits, NEG_BIG)
        p = jnp.exp(logits - jnp.max(logits, axis=-1, keepdims=True))
        aff = p / jnp.sum(p, axis=-1, keepdims=True)
        afft_ref[0, :, rs] = aff.T[:N_EXPERTS, :]


def _outproj(o_na, o_gla, x, w1, w2, gm, shf, scf, npost, nfpre, router_cat):
    b, n, d = x.shape
    tm = OUTPROJ_SUBTILES * TOK_TILE
    tokmap = lambda i, j: (i, j, 0)
    smp = lambda i, j: (i, 0, 0)
    cst = lambda i, j: (0, 0)
    return pl.pallas_call(
        _outproj_kernel,
        out_shape=[jax.ShapeDtypeStruct((b, n, d), F32),
                   jax.ShapeDtypeStruct((b, n, d // 2), jnp.int32),
                   jax.ShapeDtypeStruct((b, N_EXPERTS, n), F32)],
        grid=(b, n // tm),
        in_specs=[pl.BlockSpec((1, tm, o_na.shape[2]), tokmap),
                  pl.BlockSpec((1, tm, o_gla.shape[2]), tokmap),
                  pl.BlockSpec((1, tm, d), tokmap),
                  pl.BlockSpec(w1.shape, cst),
                  pl.BlockSpec(w2.shape, cst),
                  pl.BlockSpec((1, 1, d), smp),
                  pl.BlockSpec((1, 1, d), smp),
                  pl.BlockSpec((1, 1, d), smp),
                  pl.BlockSpec((1, d), cst),
                  pl.BlockSpec((1, d), cst),
                  pl.BlockSpec(router_cat.shape, cst)],
        out_specs=[pl.BlockSpec((1, tm, d), tokmap),
                   pl.BlockSpec((1, tm, d // 2), tokmap),
                   pl.BlockSpec((1, N_EXPERTS, tm), lambda i, j: (i, 0, j))],
        compiler_params=_cparams(("arbitrary", "arbitrary")),
        name="outproj",
    )(o_na, o_gla, x, w1, w2, gm, shf, scf, npost.reshape(1, d), nfpre.reshape(1, d), router_cat)


def _route_kernel(afft_ref, rt_ref, *, cap):
    a = afft_ref[...]
    e, n = a.shape
    capf = jnp.float32(cap)

    def search(i, thr_bits):
        cand = thr_bits | lax.shift_left(jnp.int32(1), 30 - i)
        cnt = jnp.sum(jnp.where(a >= lax.bitcast_convert_type(cand, F32), 1.0, 0.0), axis=-1, keepdims=True)
        return jnp.where(cnt >= capf, cand, thr_bits)

    thr_bits = lax.fori_loop(0, 31, search, jnp.zeros((e, 1), jnp.int32))
    thr = lax.bitcast_convert_type(thr_bits, F32)
    need = capf - jnp.sum(jnp.where(a > thr, 1.0, 0.0), axis=-1, keepdims=True)
    r_i = lax.broadcasted_iota(jnp.int32, (LANES, LANES), 0)
    c_i = lax.broadcasted_iota(jnp.int32, (LANES, LANES), 1)
    incl = jnp.where(r_i <= c_i, 1.0, 0.0).astype(BF16)
    off_eq = jnp.zeros((e, 1), F32)
    off_sel = jnp.zeros((e, 1), F32)
    for j in range(n // LANES):
        sl = slice(j * LANES, (j + 1) * LANES)
        a_b = a[:, sl]
        eq_b = jnp.where(a_b == thr, 1.0, 0.0)
        tie_rank = _dot(eq_b.astype(BF16), incl) - eq_b + off_eq
        off_eq = off_eq + jnp.sum(eq_b, axis=-1, keepdims=True)
        sel_b = jnp.where(a_b > thr, 1.0, jnp.where(tie_rank < need, eq_b, 0.0))
        sel = sel_b > 0.5
        rank = _dot(sel_b.astype(BF16), incl) - sel_b + off_sel
        off_sel = off_sel + jnp.sum(sel_b, axis=-1, keepdims=True)
        rsel = jnp.where(sel, rank, -1.0)
        rt_ref[:, sl] = rsel.astype(jnp.int32)


def _route(afft, cap):
    b, e, n = afft.shape
    return pl.pallas_call(
        functools.partial(_route_kernel, cap=cap),
        out_shape=jax.ShapeDtypeStruct((b * e, n), jnp.int32),
        grid=(1,),
        in_specs=[pl.BlockSpec((b * e, n), lambda i: (0, 0))],
        out_specs=pl.BlockSpec((b * e, n), lambda i: (0, 0)),
        compiler_params=_cparams(("arbitrary",)),
        name="route",
    )(afft.reshape(b * e, n)).reshape(b, e, n)


def _sc_gather(rsel_t, hf2, cap):
    b, e, n = rsel_t.shape
    width = hf2.shape[1]
    info = plsc.get_sparse_core_info()
    nc, lanes = info.num_cores, info.num_lanes
    workers = nc * info.num_subcores
    items = b * e
    assert n % lanes == 0 and cap % SC_GATHER_ROWS == 0
    per_worker = -(-items // workers)
    mesh = plsc.VectorSubcoreMesh(core_axis_name="c", subcore_axis_name="s")

    def body(rank_hbm, hf_hbm, out_hbm, rank_v, idx_v, rows_a, rows_b, sem_a, sem_b):
        bufs = ((rows_a, sem_a), (rows_b, sem_b))
        wid = lax.axis_index("s") * nc + lax.axis_index("c")
        def one_item(item):
            base_tok = (item // e) * n
            pltpu.sync_copy(rank_hbm.at[item], rank_v)

            @pl.loop(0, n // lanes)
            def _(j):
                r = rank_v[pl.ds(j * lanes, lanes)]
                tok = lax.iota(jnp.int32, lanes) + (j * lanes + base_tok)
                plsc.store_scatter(idx_v, [r], tok, mask=r >= 0)

            def gather(c):
                buf, sem = bufs[c % 2]
                rows = pl.ds(c * SC_GATHER_ROWS, SC_GATHER_ROWS)
                return pltpu.async_copy(hf_hbm.at[idx_v.at[rows]], buf, sem)

            n_chunks = cap // SC_GATHER_ROWS
            pending = gather(0)
            for c in range(n_chunks):
                nxt = gather(c + 1) if c + 1 < n_chunks else None
                pending.wait()
                pltpu.sync_copy(bufs[c % 2][0],
                                out_hbm.at[pl.ds(item * cap + c * SC_GATHER_ROWS, SC_GATHER_ROWS)])
                pending = nxt

        for k in range(per_worker):
            item = k * workers + wid
            if (k + 1) * workers <= items:
                one_item(item)
            else:
                pl.when(item < items)(functools.partial(one_item, item))

    return pl.kernel(
        body, out_type=jax.ShapeDtypeStruct((items * cap, width), hf2.dtype), mesh=mesh,
        scratch_types=[pltpu.VMEM((n,), jnp.int32), pltpu.VMEM((cap,), jnp.int32),
                       pltpu.VMEM((SC_GATHER_ROWS, width), hf2.dtype),
                       pltpu.VMEM((SC_GATHER_ROWS, width), hf2.dtype),
                       pltpu.SemaphoreType.DMA, pltpu.SemaphoreType.DMA],
        compiler_params=pltpu.CompilerParams(needs_layout_passes=False),
        name="scgather",
    )(rsel_t.reshape(items, n), hf2)


def _ffn_kernel(x_ref, wg_ref, wu_ref, wd_ref, o_ref, acc_ref, xb_ref):
    f = pl.program_id(1)
    b = x_ref.shape[0]
    last = pl.num_programs(1) - 1

    def tile(first, final):
        wg = wg_ref[0].astype(BF16)
        wu = wu_ref[0].astype(BF16)
        wd = wd_ref[0].astype(BF16)
        for i in range(b):
            if first:
                xb_ref[i] = _unpack_bf16_pairs(x_ref[i, 0])
            x = xb_ref[i]
            g = _dot(x, wg)
            u = _dot(x, wu)
            hid = (g / (1.0 + jnp.exp(-g)) * u).astype(BF16)
            y = _dot(hid, wd)
            if not first:
                y = acc_ref[i] + y
            if final:
                o_ref[i, 0] = y.astype(o_ref.dtype)
            else:
                acc_ref[i] = y

    @pl.when(f == 0)
    def _():
        tile(True, False)

    @pl.when((f > 0) & (f < last))
    def _():
        tile(False, False)

    @pl.when(f == last)
    def _():
        tile(False, True)


def _ffn(xs, w_gate, w_up, w_down, e0):
    b, e, cap, dp = xs.shape
    d = 2 * dp
    dff = w_gate.shape[2]
    tf = FFN_TILE
    assert dff // tf >= 2
    return pl.pallas_call(
        _ffn_kernel,
        out_shape=jax.ShapeDtypeStruct((b, e, cap, d), BF16),
        grid=(e, dff // tf),
        in_specs=[pl.BlockSpec((b, 1, cap, dp), lambda i, f: (0, i, 0, 0)),
                  pl.BlockSpec((1, d, tf), lambda i, f: (i + e0, 0, f)),
                  pl.BlockSpec((1, d, tf), lambda i, f: (i + e0, 0, f)),
                  pl.BlockSpec((1, tf, d), lambda i, f: (i + e0, f, 0))],
        out_specs=pl.BlockSpec((b, 1, cap, d), lambda i, f: (0, i, 0, 0)),
        scratch_shapes=[pltpu.VMEM((b, cap, d), F32), pltpu.VMEM((b, cap, d), BF16)],
        compiler_params=_cparams(("arbitrary", "arbitrary")),
        name="ffn",
    )(xs, w_gate, w_up, w_down)


def _combine_kernel(off_ref, *refs, n_groups):
    ys_refs = refs[:n_groups]
    rt_ref, afft_ref, xn_ref, gf_ref, npost_ref, o_ref, acc_ref = refs[n_groups:]
    bi, tt = pl.program_id(0), pl.program_id(1)
    ts = COMBINE_TILE
    n_sub = rt_ref.shape[2] // ts
    cap = ys_refs[0].shape[2]
    owner = [(g, k) for g, ref in enumerate(ys_refs) for k in range(ref.shape[1])]
    n_experts = len(owner)
    blocks = ts // LANES
    slot = lax.broadcasted_iota(jnp.int32, (COMBINE_WIN, ts), 0)

    for sub in range(n_sub):
        toks = slice(sub * ts, (sub + 1) * ts)
        blk0 = (tt * n_sub + sub) * blocks

        def window(e, w0, j, toks=toks):
            nominal = w0 + j * COMBINE_WIN
            start = pl.multiple_of(jnp.minimum(nominal, cap - COMBINE_WIN), BF16_ROWS)
            rank = rt_ref[0, e:e + 1, toks]
            hit = ((rank - start) == slot) & (rank >= nominal)
            weights = jnp.where(hit, afft_ref[0, e:e + 1, toks], 0.0).astype(BF16)
            g, k = owner[e]
            return weights, ys_refs[g][0, k, pl.ds(start, COMBINE_WIN), :]

        first, extra = [], []
        for e in range(n_experts):
            r0 = off_ref[bi, e, blk0]
            r1 = off_ref[bi, e, blk0 + blocks]
            w0 = (r0 // BF16_ROWS) * BF16_ROWS
            first.append(w0)
            extra.append(jnp.maximum((r1 - w0 + COMBINE_WIN - 1) // COMBINE_WIN - 1, 0))
        terms = []
        for e in range(0, n_experts, COMBINE_STACK):
            ws, ys = zip(*[window(e + k, first[e + k], 0) for k in range(COMBINE_STACK)])
            terms.append(_dot(jnp.concatenate(ws, axis=0).T, jnp.concatenate(ys, axis=0)))
        acc_ref[toks] = functools.reduce(lambda a, c: a + c, terms)

        @pl.when(functools.reduce(lambda a, c: a + c, extra) > 0)
        def _(window=window, first=first, extra=extra, toks=toks):
            for e in range(n_experts):
                def more(j, carry, e=e):
                    w, y = window(e, first[e], j)
                    acc_ref[toks] += _dot(w.T, y)
                    return carry
                lax.fori_loop(1, extra[e] + 1, more, 0)

    o_ref[0] = xn_ref[0] + gf_ref[0] * _rms(acc_ref[...], npost_ref[...])


def _combine(ys_groups, slot_off, rsel_t, aff_t, x_new, gf, npost):
    b, _, cap, d = ys_groups[0].shape
    e = sum(y.shape[1] for y in ys_groups)
    n = x_new.shape[1]
    tm = COMBINE_SUBTILES * COMBINE_TILE
    tokmap = lambda i, j, off: (i, j, 0)
    return pl.pallas_call(
        functools.partial(_combine_kernel, n_groups=len(ys_groups)),
        out_shape=jax.ShapeDtypeStruct((b, n, d), F32),
        grid_spec=pltpu.PrefetchScalarGridSpec(
            num_scalar_prefetch=1,
            grid=(b, n // tm),
            in_specs=[pl.BlockSpec((1, y.shape[1], cap, d), lambda i, j, off: (i, 0, 0, 0)) for y in ys_groups] + [
                      pl.BlockSpec((1, e, tm), lambda i, j, off: (i, 0, j)),
                      pl.BlockSpec((1, e, tm), lambda i, j, off: (i, 0, j)),
                      pl.BlockSpec((1, tm, d), tokmap),
                      pl.BlockSpec((1, 1, d), lambda i, j, off: (i, 0, 0)),
                      pl.BlockSpec((1, d), lambda i, j, off: (0, 0))],
            out_specs=pl.BlockSpec((1, tm, d), tokmap),
            scratch_shapes=[pltpu.VMEM((tm, d), F32)]),
        compiler_params=_cparams(("arbitrary", "arbitrary")),
        name="combine",
    )(slot_off, *ys_groups, rsel_t, aff_t, x_new, gf, npost.reshape(1, d))


def kernel(x, c, ctx, c_ctx, w_mod, b_mod, norm_mix_pre, norm_mix_post, norm_ffn_pre, norm_ffn_post,
           w_in, na_rpb, gla_a_up, gla_a_bias, gla_norm, w_out, router, w_gate, w_up, w_down):
    b, n, d = x.shape
    assert w_mod.shape[0] == 1 and d == D_MODEL and n % (GRID_W * NA_QROWS) == 0 and n % GLA_T == 0
    assert ctx.shape[1] == GLA_T
    rows = n // GRID_W
    cap = EC_CAPACITY_FACTOR * n // N_EXPERTS

    c8 = jnp.concatenate([c, c_ctx[None, :], jnp.zeros((8 - b - 1, d), F32)], axis=0)
    mod = _mod(c8, w_mod[0], b_mod[0])
    sh_m, sc_m, g_m, sh_f, sc_f, g_f = [m[:b, None, :] for m in jnp.split(mod, 6, axis=-1)]
    sh_c, sc_c = mod[b:b + 1, None, :d], mod[b:b + 1, None, d:2 * d]

    cuts = [int(c) for c in np.cumsum([0, NA_W, NA_W, GLA_KEY_W, GLA_VAL_W, 2 * GLA_GATE_RANK, NA_W, GLA_KEY_W, GLA_VAL_W])]
    c_nak, c_nav, c_gk, c_gv, c_ad, c_naq, c_gq, c_gg = [[(cuts[i], cuts[i + 1])] for i in range(8)]
    c_ad3 = c_ad * 3
    rope = _rope_tables(n)
    na_q, na_k, na_v, gq, gk, gv, ad, gg = _inproj(
        x, sh_m, sc_m, norm_mix_pre[0], w_in[0],
        [c_naq, c_nak, c_nav, c_gq, c_gk, c_gv, c_ad3, c_gg],
        [NA_W, NA_W, NA_W, GLA_KEY_W, GLA_KEY_W, GLA_VAL_W, LANES, GLA_VAL_W],
        ["plain", "plain", "plain", "rope", "rope", "plain", "split", "plain"],
        [1.0, 1.0, 1.0, GLA_DK ** -0.5, 1.0, 1.0, 1.0, 1.0],
        [BF16] * 8, rope=rope)
    c_nak, c_nav, c_gk, c_gv, c_ad = _inproj(
        ctx, sh_c, sc_c, norm_mix_pre[0], w_in[0],
        [c_nak, c_nav, c_gk, c_gv, c_ad3],
        [NA_W, NA_W, GLA_KEY_W, GLA_VAL_W, LANES],
        ["plain", "plain", "plain", "plain", "split"], [1.0] * 5, [BF16] * 5)

    o_na = _na(na_q, na_k, na_v, c_nak, c_nav, *_na_bias_tables(na_rpb[0], rows))

    cmats = jnp.asarray(_gla_prefix_matrices(GLA_T), BF16)
    o_gla = _gla(gq, gk, gv, ad, gg, c_gk, c_gv, c_ad, _gla_decay_up(gla_a_up[0]),
                 gla_a_bias[0][:, None, :], gla_norm[0][None, :], cmats)

    wo = w_out[0].astype(BF16)
    router_pad = jnp.zeros((d, LANES), F32).at[:, :N_EXPERTS].set(router[0])
    x_new, hf, aff_t = _outproj(o_na, o_gla, x, wo[:NA_W], wo[NA_W:], g_m, sh_f, sc_f,
                                     norm_mix_post[0], norm_ffn_pre[0],
                                     jnp.concatenate(_split_bf16(router_pad), axis=1))

    rsel_t = _route(aff_t, cap)
    ys, e0 = [], 0
    for group in FFN_GROUP_SIZES:
        xs = _sc_gather(rsel_t[:, e0:e0 + group], hf.reshape(b * n, d // 2), cap)
        ys.append(_ffn(xs.reshape(b, group, cap, d // 2), w_gate[0], w_up[0], w_down[0], e0))
        e0 += group
    picked = (rsel_t >= 0).astype(jnp.int32).reshape(b, N_EXPERTS, n // LANES, LANES).sum(axis=-1)
    slot_off = jnp.concatenate([jnp.zeros((b, N_EXPERTS, 1), jnp.int32), jnp.cumsum(picked, axis=-1)], axis=-1)
    return _combine(ys, slot_off, rsel_t, aff_t, x_new, g_f, norm_ffn_post[0])
```

```python
import functools

import numpy as np
import jax
import jax.numpy as jnp
from jax import lax
from jax.experimental import pallas as pl
from jax.experimental.pallas import tpu as pltpu
from jax.experimental.pallas import tpu_sc as plsc

F32 = jnp.float32
BF16 = jnp.bfloat16
HIGHEST = lax.Precision.HIGHEST

D_MODEL = 1024
GRID_W = 64
NA_W = 512
NA_HEADS = 8
NA_HEAD_DIM = 64
NA_WIN_ROWS = 8
NA_WIN_COLS = 16
GLA_HEADS = 4
GLA_DV = 128
GLA_DK = 64
GLA_KEY_W = 256
GLA_VAL_W = 512
GLA_GATE_RANK = 16
GLA_GATE_TAU = 16.0
ROPE_BASE = 10000.0
N_EXPERTS = 16
EC_CAPACITY_FACTOR = 2
NORM_EPS = 1e-6
NEG_BIG = -1e30

LANES = 128
BF16_ROWS = 16
VMEM_LIMIT = 56 * 1024 * 1024

TOK_TILE = 512
NA_QROWS = 4
NA_KROWS = NA_QROWS + NA_WIN_ROWS
NA_TAB_PAD = NA_QROWS
NA_TAB_BLOCKS = NA_TAB_PAD + 2 * NA_WIN_ROWS - 1 + NA_QROWS + 1
GLA_T = 256
GLA_LEVELS = (128, 64, 32, 16)
GLA_DIAG = 16
GLA_UNROLL = 4
OUTPROJ_SUBTILES = 2
FFN_GROUP_SIZES = (4, 12)
FFN_TILE = 256
COMBINE_TILE = 256
COMBINE_SUBTILES = 4
COMBINE_WIN = 64
COMBINE_STACK = 4
SC_GATHER_ROWS = 64


def _cparams(sem):
    return pltpu.CompilerParams(dimension_semantics=sem, vmem_limit_bytes=VMEM_LIMIT)


def _rms(v, g):
    return v * lax.rsqrt(jnp.mean(v * v, axis=-1, keepdims=True) + NORM_EPS) * g


def _dot(a, b):
    return jnp.dot(a, b, preferred_element_type=F32)


def _dot_nt(a, b):
    return lax.dot_general(a, b, (((1,), (1,)), ((), ())), preferred_element_type=F32)


def _split_bf16(v):
    hi = v.astype(BF16)
    return hi, (v - hi.astype(F32)).astype(BF16)


_HIGH_HALF = -65536


def _pack_bf16_pairs(hb):
    bits = lax.bitcast_convert_type(hb.astype(F32), jnp.int32)
    half = hb.shape[1] // 2
    return lax.shift_right_logical(bits[:, :half], 16) | (bits[:, half:] & _HIGH_HALF)


def _unpack_bf16_pairs(w):
    lo = lax.bitcast_convert_type(lax.shift_left(w, 16), F32).astype(BF16)
    hi = lax.bitcast_convert_type(w & _HIGH_HALF, F32).astype(BF16)
    return jnp.concatenate([lo, hi], axis=1)


def _mod_kernel(c_ref, w_ref, b_ref, o_ref):
    c = c_ref[...]
    s = c / (1.0 + jnp.exp(-c))
    o_ref[...] = _dot(s.astype(BF16), w_ref[...].astype(BF16)) + b_ref[...]


def _mod(c8, w_mod, b_mod):
    d, n = w_mod.shape
    tn = 1536
    return pl.pallas_call(
        _mod_kernel,
        out_shape=jax.ShapeDtypeStruct((8, n), F32),
        grid=(n // tn,),
        in_specs=[pl.BlockSpec((8, d), lambda j: (0, 0)),
                  pl.BlockSpec((d, tn), lambda j: (0, j)),
                  pl.BlockSpec((1, tn), lambda j: (0, j))],
        out_specs=pl.BlockSpec((8, tn), lambda j: (0, j)),
        compiler_params=_cparams(("arbitrary",)),
        name="mod",
    )(c8, w_mod, b_mod.reshape(1, n))


def _inproj_kernel(x_ref, sh_ref, sc_ref, g_ref, *refs, modes, scales, col_segments):
    n_rope = 2 if "rope" in modes else 0
    rope_refs, refs = refs[:n_rope], refs[n_rope:]
    n_w = len(modes)
    w_hbm, o_refs, wb_refs = refs[0], refs[1:1 + n_w], refs[1 + n_w:1 + 2 * n_w]
    w_ref, w_sem = refs[1 + 2 * n_w:]

    @pl.when((pl.program_id(0) == 0) & (pl.program_id(1) == 0))
    def _():
        load = pltpu.make_async_copy(w_hbm, w_ref, w_sem)
        load.start()
        load.wait()
        for segments, wb_ref in zip(col_segments, wb_refs):
            off = 0
            for lo, hi in segments:
                wb_ref[:, off:off + hi - lo] = w_ref[:, lo:hi].astype(BF16)
                off += hi - lo
            if off < wb_ref.shape[1]:
                wb_ref[:, off:] = jnp.zeros((wb_ref.shape[0], wb_ref.shape[1] - off), BF16)

    x = x_ref[0]
    h = _rms(x, g_ref[...]) * (1.0 + sc_ref[0]) + sh_ref[0]
    hb = h.astype(BF16)
    for w_ref, o_ref, mode, scale in zip(wb_refs, o_refs, modes, scales):
        y = _dot(hb, w_ref[...])
        if mode == "rope":
            cols = y.shape[1]
            quarter = GLA_DK // 4
            lane = lax.broadcasted_iota(jnp.int32, y.shape, 1)
            partner = jnp.where((lane & (2 * quarter - 1)) < quarter,
                                pltpu.roll(y, cols - quarter, axis=1), pltpu.roll(y, quarter, axis=1))
            y = (y * rope_refs[0][...] + partner * rope_refs[1][...]) * scale
        elif mode == "split":
            hi, lo = _split_bf16(y)
            lane = lax.broadcasted_iota(jnp.int32, y.shape, 1)
            rank2 = 2 * GLA_GATE_RANK
            y = jnp.where((lane >= rank2) & (lane < 2 * rank2), lo, hi)
        o_ref[0] = y.astype(o_ref.dtype)


def _inproj(x, shift, scale, g, w_all, col_segments, widths, modes, scales, out_dtypes, rope=None):
    b, n, d = x.shape
    tm = min(TOK_TILE, n)
    per_sample = shift.shape[0] == b
    mod_map = (lambda i, j: (i, 0, 0)) if per_sample else (lambda i, j: (0, 0, 0))
    in_specs = [pl.BlockSpec((1, tm, d), lambda i, j: (i, j, 0)),
                pl.BlockSpec((1, 1, d), mod_map),
                pl.BlockSpec((1, 1, d), mod_map),
                pl.BlockSpec((1, d), lambda i, j: (0, 0))]
    args = [x, shift, scale, g.reshape(1, d)]
    if rope is not None:
        in_specs += [pl.BlockSpec((tm, rope[0].shape[1]), lambda i, j: (j, 0))] * 2
        args += list(rope)
    in_specs.append(pl.BlockSpec(memory_space=pl.ANY))
    out_shapes, out_specs = [], []
    for cols, dt in zip(widths, out_dtypes):
        out_shapes.append(jax.ShapeDtypeStruct((b, n, cols), dt))
        out_specs.append(pl.BlockSpec((1, tm, cols), lambda i, j: (i, j, 0)))
    return pl.pallas_call(
        functools.partial(_inproj_kernel, modes=tuple(modes), scales=tuple(scales),
                          col_segments=tuple(tuple(seg) for seg in col_segments)),
        out_shape=out_shapes,
        grid=(b, n // tm),
        in_specs=in_specs,
        out_specs=out_specs,
        scratch_shapes=[pltpu.VMEM((d, cols), BF16) for cols in widths]
        + [pltpu.VMEM(w_all.shape, F32), pltpu.SemaphoreType.DMA],
        compiler_params=_cparams(("arbitrary", "arbitrary")),
        name="inproj",
    )(*args, w_all)


def _rope_tables(n):
    t = np.arange(n)
    pos_row, pos_col = (t // GRID_W).astype(np.float32), (t % GRID_W).astype(np.float32)
    quarter = GLA_DK // 4
    freqs = np.float32(ROPE_BASE) ** (-np.arange(quarter, dtype=np.float32) / quarter)
    ang_r = pos_row[:, None] * freqs
    ang_c = pos_col[:, None] * freqs
    cos = np.concatenate([np.cos(ang_r), np.cos(ang_r), np.cos(ang_c), np.cos(ang_c)], axis=-1)
    sin = np.concatenate([-np.sin(ang_r), np.sin(ang_r), -np.sin(ang_c), np.sin(ang_c)], axis=-1)
    return jnp.asarray(np.tile(cos, (1, GLA_HEADS))), jnp.asarray(np.tile(sin, (1, GLA_HEADS)))


def _na_patterns(rows):
    kr = min(NA_WIN_ROWS, rows)
    n_blocks = rows // NA_QROWS
    pats = []
    for blk in (0, 1, n_blocks - 1):
        r0 = blk * NA_QROWS
        k0 = int(np.clip(r0 - kr // 2, 0, rows - NA_KROWS))
        strips = []
        for a in range(NA_QROWS):
            r_start = int(np.clip(r0 + a - kr // 2, 0, rows - kr))
            start = k0 - (r0 + a) + NA_WIN_ROWS - 1 + NA_TAB_PAD
            assert 0 <= start and start + NA_KROWS <= NA_TAB_BLOCKS
            strips.append((start, [r_start <= k0 + c < r_start + kr for c in range(NA_KROWS)]))
        pats.append(strips)
    return pats


def _na_bias_tables(rpb, rows):
    heads = rpb.shape[0]
    col = np.arange(GRID_W)
    c_start = np.clip(col - NA_WIN_COLS // 2, 0, GRID_W - NA_WIN_COLS)
    col_ok = (col[None, :] >= c_start[:, None]) & (col[None, :] < c_start[:, None] + NA_WIN_COLS)
    dc = np.clip(col[None, :] - col[:, None] + NA_WIN_COLS - 1, 0, 2 * NA_WIN_COLS - 2)
    sel_c = (np.arange(2 * NA_WIN_COLS - 1)[:, None, None] == dc[None]) & col_ok[None]
    t = jnp.einsum("hrd,dqk->hqrk", rpb, jnp.asarray(sel_c, F32), precision=HIGHEST)
    t = jnp.where(jnp.asarray(col_ok)[None, :, None, :], t, NEG_BIG)
    n_dr = 2 * NA_WIN_ROWS - 1
    t = t.reshape(heads, GRID_W, n_dr * GRID_W)
    back = NA_TAB_BLOCKS + 1 - NA_TAB_PAD - n_dr
    t = jnp.pad(t, ((0, 0), (0, 0), (NA_TAB_PAD * GRID_W, back * GRID_W)), constant_values=NEG_BIG)
    width = NA_TAB_BLOCKS * GRID_W
    tab = jnp.stack([t[:, :, :width], t[:, :, GRID_W:GRID_W + width]], axis=1)
    row_mask = np.zeros((3, NA_QROWS, 1, NA_KROWS * GRID_W), np.float32)
    for pat, strips in enumerate(_na_patterns(rows)):
        for a, (_, valid) in enumerate(strips):
            row_mask[pat, a, 0] = np.repeat(np.where(valid, 0.0, NEG_BIG), GRID_W)
    return tab, jnp.asarray(row_mask)


def _na_kernel(q_ref, k_ref, v_ref, kc_ref, vc_ref, tab_ref, rmask_ref, o_ref, bias_ref,
               sw_ref, sc_ref, pw_ref, pc_ref, *, rows):
    nq, nk = NA_QROWS * GRID_W, NA_KROWS * GRID_W
    n_blocks = rows // NA_QROWS
    kr = min(NA_WIN_ROWS, rows)
    scale = NA_HEAD_DIM ** -0.5
    lane = lax.broadcasted_iota(jnp.int32, (nq, LANES), 1)
    first_head = lane < NA_HEAD_DIM
    kc = kc_ref[0]
    vc = vc_ref[0]
    lane_w = lax.broadcasted_iota(jnp.int32, (nk, LANES), 1)
    lane_c = lax.broadcasted_iota(jnp.int32, vc.shape, 1)
    @pl.when(pl.program_id(1) == 0)
    def _():
        for pat, strips in enumerate(_na_patterns(rows)):
            for a, (start, _) in enumerate(strips):
                parity = start % 2
                off = (start - parity) * GRID_W
                for h in range(2):
                    bias_ref[0, pat, h, a * GRID_W:(a + 1) * GRID_W, :] = (
                        tab_ref[h, parity, :, off:off + nk] + rmask_ref[pat, a])

    def key_start(i):
        return pl.multiple_of(jnp.clip(i * NA_QROWS - kr // 2, 0, rows - NA_KROWS) * GRID_W, GRID_W)

    def scores(i, slot):
        pat = jnp.where(i == 0, 0, jnp.where(i == n_blocks - 1, 2, 1))
        q = q_ref[0, pl.ds(pl.multiple_of(i * nq, nq), nq), :] * scale
        kw = k_ref[0, pl.ds(key_start(i), nk), :]
        for h in range(2):
            qh = jnp.where(first_head if h == 0 else jnp.logical_not(first_head), q, jnp.zeros_like(q))
            sw_ref[slot, h] = _dot_nt(qh, kw) + bias_ref[0, pat, h]
            sc_ref[slot, h] = _dot_nt(qh, kc)

    def softmax(slot):
        for h in range(2):
            s_w = sw_ref[slot, h]
            s_c = sc_ref[slot, h]
            m = jnp.maximum(jnp.max(s_w, axis=-1, keepdims=True), jnp.max(s_c, axis=-1, keepdims=True))
            pw_ref[slot, h] = jnp.exp((s_w - m).astype(BF16))
            pc_ref[slot, h] = jnp.exp((s_c - m).astype(BF16))

    def values(i, slot):
        vw = v_ref[0, pl.ds(key_start(i), nk), :]
        outs = []
        for h in range(2):
            sum_lane = NA_HEAD_DIM * (1 - h)
            vw_h = jnp.where(lane_w == sum_lane, jnp.ones_like(vw), vw)
            vc_h = jnp.where(lane_c == sum_lane, jnp.ones_like(vc), vc)
            o = _dot(pw_ref[slot, h], vw_h) + _dot(pc_ref[slot, h], vc_h)
            outs.append(o * (1.0 / o[:, sum_lane:sum_lane + 1]))
        o = jnp.where(first_head, outs[0], outs[1])
        o_ref[0, pl.ds(pl.multiple_of(i * nq, nq), nq), :] = o.astype(o_ref.dtype)

    assert n_blocks % 2 == 0 and n_blocks >= 4
    scores(0, 0)
    softmax(0)
    scores(1, 1)

    def trip(j, carry):
        i = 2 * j
        values(i - 2, 0)
        softmax(1)
        scores(i, 0)
        values(i - 1, 1)
        softmax(0)
        scores(i + 1, 1)
        return carry

    lax.fori_loop(1, n_blocks // 2, trip, 0)
    values(n_blocks - 2, 0)
    softmax(1)
    values(n_blocks - 1, 1)


def _na(q, k, v, kc, vc, tab, row_mask):
    b, n, w = q.shape
    n_ctx = kc.shape[1]
    pairs = w // LANES
    rows = n // GRID_W
    nq, nk = NA_QROWS * GRID_W, NA_KROWS * GRID_W
    tok = lambda p, i: (i, 0, p)
    return pl.pallas_call(
        functools.partial(_na_kernel, rows=rows),
        out_shape=jax.ShapeDtypeStruct((b, n, w), BF16),
        grid=(pairs, b),
        in_specs=[pl.BlockSpec((1, n, LANES), tok),
                  pl.BlockSpec((1, n, LANES), tok),
                  pl.BlockSpec((1, n, LANES), tok),
                  pl.BlockSpec((1, n_ctx, LANES), tok),
                  pl.BlockSpec((1, n_ctx, LANES), tok),
                  pl.BlockSpec((2,) + tab.shape[1:], lambda p, i: (p, 0, 0, 0)),
                  pl.BlockSpec(row_mask.shape, lambda p, i: (0, 0, 0, 0))],
        out_specs=pl.BlockSpec((1, n, LANES), tok),
        scratch_shapes=[pltpu.VMEM((1, 3, 2, nq, nk), F32),
                        pltpu.VMEM((2, 2, nq, nk), F32), pltpu.VMEM((2, 2, nq, n_ctx), F32),
                        pltpu.VMEM((2, 2, nq, nk), BF16), pltpu.VMEM((2, 2, nq, n_ctx), BF16)],
        compiler_params=_cparams(("arbitrary", "arbitrary")),
        name="na",
    )(q, k, v, kc, vc, tab, row_mask)


def _gla_prefix_matrices(t):
    i = np.arange(t)
    return np.stack([i[:, None] >= i[None, :], i[:, None] <= i[None, :]]).astype(np.float32)


def _gla_kernel(q_ref, k_ref, v_ref, ad_ref, g_ref, ck_ref, cv_ref, cad_ref,
                u_ref, ab_ref, gn_ref, cm_ref, o_ref, accf_ref, accb_ref, *, n_tok):
    t = GLA_T
    n_chunks = n_tok // t
    row = lax.broadcasted_iota(jnp.int32, (t, LANES), 0)
    hb = t // 2
    row_b = lax.broadcasted_iota(jnp.int32, (hb, LANES), 0)
    head0_b = lax.broadcasted_iota(jnp.int32, (hb, LANES), 1) < GLA_DK
    row2 = lax.broadcasted_iota(jnp.int32, (hb, 2 * hb), 0)
    col2 = lax.broadcasted_iota(jnp.int32, (hb, 2 * hb), 1) & (hb - 1)
    vrow = lax.broadcasted_iota(jnp.int32, (2 * t, 2 * GLA_DV), 0)
    vlane = lax.broadcasted_iota(jnp.int32, (2 * t, 2 * GLA_DV), 1)
    v_head_match = (vrow >= t) == (vlane >= GLA_DV)
    srow = lax.broadcasted_iota(jnp.int32, (2 * GLA_DV, LANES), 0)
    slane = lax.broadcasted_iota(jnp.int32, (2 * GLA_DV, LANES), 1)
    s_blockdiag = (srow >= GLA_DV) == (slane >= GLA_DK)
    blk_mask = {half: jnp.where((row2 & ~(2 * half - 1)) == (col2 & ~(2 * half - 1)), 1.0, 0.0)
                for half in GLA_LEVELS if 2 * half < hb}
    diag_blk = (row2 & ~(GLA_DIAG - 1)) == (col2 & ~(GLA_DIAG - 1))
    diag_mask = (jnp.where(diag_blk & (row2 >= col2), 1.0, 0.0), jnp.where(diag_blk & (row2 <= col2), 1.0, 0.0))

    def prefix_sums(ad, dirn):
        z = _dot(ad, u_ref[dirn]) + ab_ref[dirn]
        loga = (jnp.minimum(z, 0.0) - jnp.log(1.0 + jnp.exp(-jnp.abs(z)))) * (1.0 / GLA_GATE_TAU)
        hi, lo = _split_bf16(loga)
        p2 = _dot(cm_ref[dirn], jnp.concatenate([hi, lo], axis=-1))
        return p2[:, :LANES] + p2[:, LANES:]

    def chunk_end(p, dirn):
        return p[t - 1:t, :] if dirn == 0 else p[0:1, :]

    def level_sums(p, half, dirn):
        blk = 2 * half
        p3 = p.reshape(t // blk, blk, LANES)
        edge = half - 1 if dirn == 0 else half
        ref = jnp.broadcast_to(p3[:, edge:edge + 1, :], p3.shape).reshape(t, LANES)
        later = (row & half) != 0
        return jnp.where(later == (dirn == 0), p - ref, ref - p)

    def state_update(s, k, vt, p, dirn):
        kh = (k * jnp.exp(chunk_end(p, dirn) - p)).astype(BF16)
        return s * jnp.exp(chunk_end(p, dirn)) + jnp.where(s_blockdiag, _dot(vt, kh), 0.0)

    def chunk(tok0, s, dirn):
        q = q_ref[0, pl.ds(tok0, t), :].astype(F32)
        k = k_ref[0, pl.ds(tok0, t), :].astype(F32)
        v = v_ref[0, pl.ds(tok0, t), :]
        vt = v.T
        p = prefix_sums(ad_ref[0, pl.ds(tok0, t), :], dirn)
        qh = (q * jnp.exp(p)).astype(BF16)
        o = _dot_nt(qh, s.astype(BF16))
        def rows(x, b):
            return x[b * hb:(b + 1) * hb]

        def scores(qt, kt):
            kcat = jnp.concatenate([jnp.where(head0_b, kt, 0.0), jnp.where(head0_b, 0.0, kt)], axis=0)
            return _dot_nt(qt.astype(BF16), kcat.astype(BF16))

        assert GLA_LEVELS[0] == hb
        w = jnp.exp(level_sums(p, hb, dirn))
        qb, kb = (1, 0) if dirn == 0 else (0, 1)
        wide = scores(rows(q, qb) * rows(w, qb), rows(k, kb) * rows(w, kb))
        fine_w = [jnp.exp(level_sums(p, half, dirn)) for half in GLA_LEVELS[1:]]
        e_d = level_sums(p, GLA_DIAG // 2, dirn)
        w_d, wi_d = jnp.exp(e_d), jnp.exp(-e_d)
        fine = []
        for b in range(2):
            qs, ks = rows(q, b), rows(k, b)
            acc = None
            for half, w in zip(GLA_LEVELS[1:], fine_w):
                later = (row_b & half) != 0
                q_side = later if dirn == 0 else jnp.logical_not(later)
                part = scores(jnp.where(q_side, qs * rows(w, b), 0.0), jnp.where(q_side, 0.0, ks * rows(w, b)))
                if half in blk_mask:
                    part = part * blk_mask[half]
                acc = part if acc is None else acc + part
            later = (row_b & (GLA_DIAG // 2)) != 0
            shrink_q = later if dirn == 0 else jnp.logical_not(later)
            part = scores(qs * jnp.where(shrink_q, rows(w_d, b), rows(wi_d, b)),
                          ks * jnp.where(shrink_q, rows(wi_d, b), rows(w_d, b)))
            fine.append(acc + jnp.where(diag_mask[dirn] > 0.5, part, 0.0))
        zero = jnp.zeros((hb, hb), F32)
        h0, h1 = slice(0, hb), slice(hb, 2 * hb)
        if dirn == 0:
            top = [fine[0][:, h0], zero, fine[0][:, h1], zero]
            bot = [wide[:, h0], fine[1][:, h0], wide[:, h1], fine[1][:, h1]]
        else:
            top = [fine[0][:, h0], wide[:, h0], fine[0][:, h1], wide[:, h1]]
            bot = [zero, fine[1][:, h0], zero, fine[1][:, h1]]
        a = jnp.concatenate([jnp.concatenate(top, axis=1), jnp.concatenate(bot, axis=1)], axis=0)
        vcat = jnp.concatenate([v, v], axis=0)
        vcat = jnp.where(v_head_match, vcat, jnp.zeros_like(vcat))
        o = o + _dot(a.astype(BF16), vcat)
        return o, state_update(s, k, vt, p, dirn)

    def ctx_state(dirn):
        p = prefix_sums(cad_ref[0], dirn)
        s0 = jnp.zeros((2 * GLA_DV, LANES), F32)
        return state_update(s0, ck_ref[0].astype(F32), cv_ref[0].T, p, dirn)

    def finish(tok0, o):
        g = g_ref[0, pl.ds(tok0, t), :].astype(F32)
        gate = g / (1.0 + jnp.exp(-g))
        halves = [_rms(o[:, h * GLA_DV:(h + 1) * GLA_DV], gn_ref[...]) for h in range(2)]
        o_ref[0, pl.ds(tok0, t), :] = (jnp.concatenate(halves, axis=-1) * gate).astype(o_ref.dtype)

    def body(i, carry, second_half):
        s_f, s_b = carry
        for u in range(GLA_UNROLL):
            c = i * GLA_UNROLL + u
            tok_f = pl.multiple_of(c * t, t)
            tok_b = pl.multiple_of((n_chunks - 1 - c) * t, t)
            o_f, s_f = chunk(tok_f, s_f, 0)
            o_b, s_b = chunk(tok_b, s_b, 1)
            if second_half:
                finish(tok_f, o_f + accb_ref[pl.ds(tok_f, t), :])
                finish(tok_b, o_b + accf_ref[pl.ds(tok_b, t), :])
            else:
                accf_ref[pl.ds(tok_f, t), :] = o_f
                accb_ref[pl.ds(tok_b, t), :] = o_b
        return s_f, s_b

    trips = n_chunks // GLA_UNROLL
    assert n_chunks % (2 * GLA_UNROLL) == 0
    carry = lax.fori_loop(0, trips // 2, functools.partial(body, second_half=False), (ctx_state(0), ctx_state(1)))
    lax.fori_loop(trips // 2, trips, functools.partial(body, second_half=True), carry)


def _gla(q, k, v, ad, g, ck, cv, cad, u, abias, gnorm, cmats):
    b, n, kw = q.shape
    n_ctx = ck.shape[1]
    pairs = kw // LANES
    vw = 2 * GLA_DV
    tok = lambda i, p: (i, 0, p)
    full3 = lambda i, p: (i, 0, 0)
    return pl.pallas_call(
        functools.partial(_gla_kernel, n_tok=n),
        out_shape=jax.ShapeDtypeStruct((b, n, v.shape[2]), BF16),
        grid=(b, pairs),
        in_specs=[pl.BlockSpec((1, n, LANES), tok),
                  pl.BlockSpec((1, n, LANES), tok),
                  pl.BlockSpec((1, n, vw), tok),
                  pl.BlockSpec((1, n, LANES), full3),
                  pl.BlockSpec((1, n, vw), tok),
                  pl.BlockSpec((1, n_ctx, LANES), tok),
                  pl.BlockSpec((1, n_ctx, vw), tok),
                  pl.BlockSpec((1, n_ctx, LANES), full3),
                  pl.BlockSpec((2, LANES, LANES), lambda i, p: (0, 0, p)),
                  pl.BlockSpec((2, 1, LANES), lambda i, p: (0, 0, p)),
                  pl.BlockSpec((1, GLA_DV), lambda i, p: (0, 0)),
                  pl.BlockSpec(cmats.shape, lambda i, p: (0, 0, 0))],
        out_specs=pl.BlockSpec((1, n, vw), tok),
        scratch_shapes=[pltpu.VMEM((n, vw), F32), pltpu.VMEM((n, vw), F32)],
        compiler_params=_cparams(("arbitrary", "arbitrary")),
        name="gla",
    )(q, k, v, ad, g, ck, cv, cad, u, abias, gnorm, cmats)


def _gla_decay_up(a_up):
    r = GLA_GATE_RANK
    pad = jnp.zeros((2, 2 * r, GLA_KEY_W), F32)
    pad = pad.at[0, :r].set(a_up[0]).at[1, r:].set(a_up[1])
    hi, lo = _split_bf16(pad)
    return jnp.concatenate([hi, hi, lo, jnp.zeros_like(hi)], axis=1)


def _outproj_kernel(ona_ref, ogla_ref, x_ref, w1_ref, w2_ref, gm_ref, shf_ref, scf_ref, npost_ref,
                    nfpre_ref, rt_ref, xnew_ref, hf_ref, afft_ref):
    for r0 in range(0, x_ref.shape[1], TOK_TILE):
        rs = slice(r0, r0 + TOK_TILE)
        mix = _dot(ona_ref[0, rs], w1_ref[...]) + _dot(ogla_ref[0, rs], w2_ref[...])
        xn = x_ref[0, rs] + gm_ref[0] * _rms(mix, npost_ref[...])
        xnew_ref[0, rs] = xn
        h = _rms(xn, nfpre_ref[...]) * (1.0 + scf_ref[0]) + shf_ref[0]
        h_hi, h_lo = _split_bf16(h)
        hf_ref[0, rs] = _pack_bf16_pairs(h_hi)
        res = _dot(h_hi, rt_ref[...])
        logits = res[:, :LANES] + res[:, LANES:] + _dot(h_lo, rt_ref[:, :LANES])
        lane = lax.broadcasted_iota(jnp.int32, logits.shape, 1)
        logits = jnp.where(lane < N_EXPERTS, logits, NEG_BIG)
        p = jnp.exp(logits - jnp.max(logits, axis=-1, keepdims=True))
        aff = p / jnp.sum(p, axis=-1, keepdims=True)
        afft_ref[0, :, rs] = aff.T[:N_EXPERTS, :]


def _outproj(o_na, o_gla, x, w1, w2, gm, shf, scf, npost, nfpre, router_cat):
    b, n, d = x.shape
    tm = OUTPROJ_SUBTILES * TOK_TILE
    tokmap = lambda i, j: (i, j, 0)
    smp = lambda i, j: (i, 0, 0)
    cst = lambda i, j: (0, 0)
    return pl.pallas_call(
        _outproj_kernel,
        out_shape=[jax.ShapeDtypeStruct((b, n, d), F32),
                   jax.ShapeDtypeStruct((b, n, d // 2), jnp.int32),
                   jax.ShapeDtypeStruct((b, N_EXPERTS, n), F32)],
        grid=(b, n // tm),
        in_specs=[pl.BlockSpec((1, tm, o_na.shape[2]), tokmap),
                  pl.BlockSpec((1, tm, o_gla.shape[2]), tokmap),
                  pl.BlockSpec((1, tm, d), tokmap),
                  pl.BlockSpec(w1.shape, cst),
                  pl.BlockSpec(w2.shape, cst),
                  pl.BlockSpec((1, 1, d), smp),
                  pl.BlockSpec((1, 1, d), smp),
                  pl.BlockSpec((1, 1, d), smp),
                  pl.BlockSpec((1, d), cst),
                  pl.BlockSpec((1, d), cst),
                  pl.BlockSpec(router_cat.shape, cst)],
        out_specs=[pl.BlockSpec((1, tm, d), tokmap),
                   pl.BlockSpec((1, tm, d // 2), tokmap),
                   pl.BlockSpec((1, N_EXPERTS, tm), lambda i, j: (i, 0, j))],
        compiler_params=_cparams(("arbitrary", "arbitrary")),
        name="outproj",
    )(o_na, o_gla, x, w1, w2, gm, shf, scf, npost.reshape(1, d), nfpre.reshape(1, d), router_cat)


def _route_kernel(afft_ref, rt_ref, *, cap):
    a = afft_ref[...]
    e, n = a.shape
    capf = jnp.float32(cap)

    def search(i, thr_bits):
        cand = thr_bits | lax.shift_left(jnp.int32(1), 30 - i)
        cnt = jnp.sum(jnp.where(a >= lax.bitcast_convert_type(cand, F32), 1.0, 0.0), axis=-1, keepdims=True)
        return jnp.where(cnt >= capf, cand, thr_bits)

    thr_bits = lax.fori_loop(0, 31, search, jnp.zeros((e, 1), jnp.int32))
    thr = lax.bitcast_convert_type(thr_bits, F32)
    need = capf - jnp.sum(jnp.where(a > thr, 1.0, 0.0), axis=-1, keepdims=True)
    r_i = lax.broadcasted_iota(jnp.int32, (LANES, LANES), 0)
    c_i = lax.broadcasted_iota(jnp.int32, (LANES, LANES), 1)
    incl = jnp.where(r_i <= c_i, 1.0, 0.0).astype(BF16)
    off_eq = jnp.zeros((e, 1), F32)
    off_sel = jnp.zeros((e, 1), F32)
    for j in range(n // LANES):
        sl = slice(j * LANES, (j + 1) * LANES)
        a_b = a[:, sl]
        eq_b = jnp.where(a_b == thr, 1.0, 0.0)
        tie_rank = _dot(eq_b.astype(BF16), incl) - eq_b + off_eq
        off_eq = off_eq + jnp.sum(eq_b, axis=-1, keepdims=True)
        sel_b = jnp.where(a_b > thr, 1.0, jnp.where(tie_rank < need, eq_b, 0.0))
        sel = sel_b > 0.5
        rank = _dot(sel_b.astype(BF16), incl) - sel_b + off_sel
        off_sel = off_sel + jnp.sum(sel_b, axis=-1, keepdims=True)
        rsel = jnp.where(sel, rank, -1.0)
        rt_ref[:, sl] = rsel.astype(jnp.int32)


def _route(afft, cap):
    b, e, n = afft.shape
    return pl.pallas_call(
        functools.partial(_route_kernel, cap=cap),
        out_shape=jax.ShapeDtypeStruct((b * e, n), jnp.int32),
        grid=(1,),
        in_specs=[pl.BlockSpec((b * e, n), lambda i: (0, 0))],
        out_specs=pl.BlockSpec((b * e, n), lambda i: (0, 0)),
        compiler_params=_cparams(("arbitrary",)),
        name="route",
    )(afft.reshape(b * e, n)).reshape(b, e, n)


def _sc_gather(rsel_t, hf2, cap):
    b, e, n = rsel_t.shape
    width = hf2.shape[1]
    info = plsc.get_sparse_core_info()
    nc, lanes = info.num_cores, info.num_lanes
    workers = nc * info.num_subcores
    items = b * e
    assert n % lanes == 0 and cap % SC_GATHER_ROWS == 0
    per_worker = -(-items // workers)
    mesh = plsc.VectorSubcoreMesh(core_axis_name="c", subcore_axis_name="s")

    def body(rank_hbm, hf_hbm, out_hbm, rank_v, idx_v, rows_a, rows_b, sem_a, sem_b):
        bufs = ((rows_a, sem_a), (rows_b, sem_b))
        wid = lax.axis_index("s") * nc + lax.axis_index("c")
        def one_item(item):
            base_tok = (item // e) * n
            pltpu.sync_copy(rank_hbm.at[item], rank_v)

            @pl.loop(0, n // lanes)
            def _(j):
                r = rank_v[pl.ds(j * lanes, lanes)]
                tok = lax.iota(jnp.int32, lanes) + (j * lanes + base_tok)
                plsc.store_scatter(idx_v, [r], tok, mask=r >= 0)

            def gather(c):
                buf, sem = bufs[c % 2]
                rows = pl.ds(c * SC_GATHER_ROWS, SC_GATHER_ROWS)
                return pltpu.async_copy(hf_hbm.at[idx_v.at[rows]], buf, sem)

            n_chunks = cap // SC_GATHER_ROWS
            pending = gather(0)
            for c in range(n_chunks):
                nxt = gather(c + 1) if c + 1 < n_chunks else None
                pending.wait()
                pltpu.sync_copy(bufs[c % 2][0],
                                out_hbm.at[pl.ds(item * cap + c * SC_GATHER_ROWS, SC_GATHER_ROWS)])
                pending = nxt

        for k in range(per_worker):
            item = k * workers + wid
            if (k + 1) * workers <= items:
                one_item(item)
            else:
                pl.when(item < items)(functools.partial(one_item, item))

    return pl.kernel(
        body, out_type=jax.ShapeDtypeStruct((items * cap, width), hf2.dtype), mesh=mesh,
        scratch_types=[pltpu.VMEM((n,), jnp.int32), pltpu.VMEM((cap,), jnp.int32),
                       pltpu.VMEM((SC_GATHER_ROWS, width), hf2.dtype),
                       pltpu.VMEM((SC_GATHER_ROWS, width), hf2.dtype),
                       pltpu.SemaphoreType.DMA, pltpu.SemaphoreType.DMA],
        compiler_params=pltpu.CompilerParams(needs_layout_passes=False),
        name="scgather",
    )(rsel_t.reshape(items, n), hf2)


def _ffn_kernel(x_ref, wg_ref, wu_ref, wd_ref, o_ref, acc_ref, xb_ref):
    f = pl.program_id(1)
    b = x_ref.shape[0]
    last = pl.num_programs(1) - 1

    def tile(first, final):
        wg = wg_ref[0].astype(BF16)
        wu = wu_ref[0].astype(BF16)
        wd = wd_ref[0].astype(BF16)
        for i in range(b):
            if first:
                xb_ref[i] = _unpack_bf16_pairs(x_ref[i, 0])
            x = xb_ref[i]
            g = _dot(x, wg)
            u = _dot(x, wu)
            hid = (g / (1.0 + jnp.exp(-g)) * u).astype(BF16)
            y = _dot(hid, wd)
            if not first:
                y = acc_ref[i] + y
            if final:
                o_ref[i, 0] = y.astype(o_ref.dtype)
            else:
                acc_ref[i] = y

    @pl.when(f == 0)
    def _():
        tile(True, False)

    @pl.when((f > 0) & (f < last))
    def _():
        tile(False, False)

    @pl.when(f == last)
    def _():
        tile(False, True)


def _ffn(xs, w_gate, w_up, w_down, e0):
    b, e, cap, dp = xs.shape
    d = 2 * dp
    dff = w_gate.shape[2]
    tf = FFN_TILE
    assert dff // tf >= 2
    return pl.pallas_call(
        _ffn_kernel,
        out_shape=jax.ShapeDtypeStruct((b, e, cap, d), BF16),
        grid=(e, dff // tf),
        in_specs=[pl.BlockSpec((b, 1, cap, dp), lambda i, f: (0, i, 0, 0)),
                  pl.BlockSpec((1, d, tf), lambda i, f: (i + e0, 0, f)),
                  pl.BlockSpec((1, d, tf), lambda i, f: (i + e0, 0, f)),
                  pl.BlockSpec((1, tf, d), lambda i, f: (i + e0, f, 0))],
        out_specs=pl.BlockSpec((b, 1, cap, d), lambda i, f: (0, i, 0, 0)),
        scratch_shapes=[pltpu.VMEM((b, cap, d), F32), pltpu.VMEM((b, cap, d), BF16)],
        compiler_params=_cparams(("arbitrary", "arbitrary")),
        name="ffn",
    )(xs, w_gate, w_up, w_down)


def _combine_kernel(off_ref, *refs, n_groups):
    ys_refs = refs[:n_groups]
    rt_ref, afft_ref, xn_ref, gf_ref, npost_ref, o_ref, acc_ref = refs[n_groups:]
    bi, tt = pl.program_id(0), pl.program_id(1)
    ts = COMBINE_TILE
    n_sub = rt_ref.shape[2] // ts
    cap = ys_refs[0].shape[2]
    owner = [(g, k) for g, ref in enumerate(ys_refs) for k in range(ref.shape[1])]
    n_experts = len(owner)
    blocks = ts // LANES
    slot = lax.broadcasted_iota(jnp.int32, (COMBINE_WIN, ts), 0)

    for sub in range(n_sub):
        toks = slice(sub * ts, (sub + 1) * ts)
        blk0 = (tt * n_sub + sub) * blocks

        def window(e, w0, j, toks=toks):
            nominal = w0 + j * COMBINE_WIN
            start = pl.multiple_of(jnp.minimum(nominal, cap - COMBINE_WIN), BF16_ROWS)
            rank = rt_ref[0, e:e + 1, toks]
            hit = ((rank - start) == slot) & (rank >= nominal)
            weights = jnp.where(hit, afft_ref[0, e:e + 1, toks], 0.0).astype(BF16)
            g, k = owner[e]
            return weights, ys_refs[g][0, k, pl.ds(start, COMBINE_WIN), :]

        first, extra = [], []
        for e in range(n_experts):
            r0 = off_ref[bi, e, blk0]
            r1 = off_ref[bi, e, blk0 + blocks]
            w0 = (r0 // BF16_ROWS) * BF16_ROWS
            first.append(w0)
            extra.append(jnp.maximum((r1 - w0 + COMBINE_WIN - 1) // COMBINE_WIN - 1, 0))
        terms = []
        for e in range(0, n_experts, COMBINE_STACK):
            ws, ys = zip(*[window(e + k, first[e + k], 0) for k in range(COMBINE_STACK)])
            terms.append(_dot(jnp.concatenate(ws, axis=0).T, jnp.concatenate(ys, axis=0)))
        acc_ref[toks] = functools.reduce(lambda a, c: a + c, terms)

        @pl.when(functools.reduce(lambda a, c: a + c, extra) > 0)
        def _(window=window, first=first, extra=extra, toks=toks):
            for e in range(n_experts):
                def more(j, carry, e=e):
                    w, y = window(e, first[e], j)
                    acc_ref[toks] += _dot(w.T, y)
                    return carry
                lax.fori_loop(1, extra[e] + 1, more, 0)

    o_ref[0] = xn_ref[0] + gf_ref[0] * _rms(acc_ref[...], npost_ref[...])


def _combine(ys_groups, slot_off, rsel_t, aff_t, x_new, gf, npost):
    b, _, cap, d = ys_groups[0].shape
    e = sum(y.shape[1] for y in ys_groups)
    n = x_new.shape[1]
    tm = COMBINE_SUBTILES * COMBINE_TILE
    tokmap = lambda i, j, off: (i, j, 0)
    return pl.pallas_call(
        functools.partial(_combine_kernel, n_groups=len(ys_groups)),
        out_shape=jax.ShapeDtypeStruct((b, n, d), F32),
        grid_spec=pltpu.PrefetchScalarGridSpec(
            num_scalar_prefetch=1,
            grid=(b, n // tm),
            in_specs=[pl.BlockSpec((1, y.shape[1], cap, d), lambda i, j, off: (i, 0, 0, 0)) for y in ys_groups] + [
                      pl.BlockSpec((1, e, tm), lambda i, j, off: (i, 0, j)),
                      pl.BlockSpec((1, e, tm), lambda i, j, off: (i, 0, j)),
                      pl.BlockSpec((1, tm, d), tokmap),
                      pl.BlockSpec((1, 1, d), lambda i, j, off: (i, 0, 0)),
                      pl.BlockSpec((1, d), lambda i, j, off: (0, 0))],
            out_specs=pl.BlockSpec((1, tm, d), tokmap),
            scratch_shapes=[pltpu.VMEM((tm, d), F32)]),
        compiler_params=_cparams(("arbitrary", "arbitrary")),
        name="combine",
    )(slot_off, *ys_groups, rsel_t, aff_t, x_new, gf, npost.reshape(1, d))


def kernel(x, c, ctx, c_ctx, w_mod, b_mod, norm_mix_pre, norm_mix_post, norm_ffn_pre, norm_ffn_post,
           w_in, na_rpb, gla_a_up, gla_a_bias, gla_norm, w_out, router, w_gate, w_up, w_down):
    b, n, d = x.shape
    assert w_mod.shape[0] == 1 and d == D_MODEL and n % (GRID_W * NA_QROWS) == 0 and n % GLA_T == 0
    assert ctx.shape[1] == GLA_T
    rows = n // GRID_W
    cap = EC_CAPACITY_FACTOR * n // N_EXPERTS

    c8 = jnp.concatenate([c, c_ctx[None, :], jnp.zeros((8 - b - 1, d), F32)], axis=0)
    mod = _mod(c8, w_mod[0], b_mod[0])
    sh_m, sc_m, g_m, sh_f, sc_f, g_f = [m[:b, None, :] for m in jnp.split(mod, 6, axis=-1)]
    sh_c, sc_c = mod[b:b + 1, None, :d], mod[b:b + 1, None, d:2 * d]

    cuts = [int(c) for c in np.cumsum([0, NA_W, NA_W, GLA_KEY_W, GLA_VAL_W, 2 * GLA_GATE_RANK, NA_W, GLA_KEY_W, GLA_VAL_W])]
    c_nak, c_nav, c_gk, c_gv, c_ad, c_naq, c_gq, c_gg = [[(cuts[i], cuts[i + 1])] for i in range(8)]
    c_ad3 = c_ad * 3
    rope = _rope_tables(n)
    na_q, na_k, na_v, gq, gk, gv, ad, gg = _inproj(
        x, sh_m, sc_m, norm_mix_pre[0], w_in[0],
        [c_naq, c_nak, c_nav, c_gq, c_gk, c_gv, c_ad3, c_gg],
        [NA_W, NA_W, NA_W, GLA_KEY_W, GLA_KEY_W, GLA_VAL_W, LANES, GLA_VAL_W],
        ["plain", "plain", "plain", "rope", "rope", "plain", "split", "plain"],
        [1.0, 1.0, 1.0, GLA_DK ** -0.5, 1.0, 1.0, 1.0, 1.0],
        [BF16] * 8, rope=rope)
    c_nak, c_nav, c_gk, c_gv, c_ad = _inproj(
        ctx, sh_c, sc_c, norm_mix_pre[0], w_in[0],
        [c_nak, c_nav, c_gk, c_gv, c_ad3],
        [NA_W, NA_W, GLA_KEY_W, GLA_VAL_W, LANES],
        ["plain", "plain", "plain", "plain", "split"], [1.0] * 5, [BF16] * 5)

    o_na = _na(na_q, na_k, na_v, c_nak, c_nav, *_na_bias_tables(na_rpb[0], rows))

    cmats = jnp.asarray(_gla_prefix_matrices(GLA_T), BF16)
    o_gla = _gla(gq, gk, gv, ad, gg, c_gk, c_gv, c_ad, _gla_decay_up(gla_a_up[0]),
                 gla_a_bias[0][:, None, :], gla_norm[0][None, :], cmats)

    wo = w_out[0].astype(BF16)
    router_pad = jnp.zeros((d, LANES), F32).at[:, :N_EXPERTS].set(router[0])
    x_new, hf, aff_t = _outproj(o_na, o_gla, x, wo[:NA_W], wo[NA_W:], g_m, sh_f, sc_f,
                                     norm_mix_post[0], norm_ffn_pre[0],
                                     jnp.concatenate(_split_bf16(router_pad), axis=1))

    rsel_t = _route(aff_t, cap)
    ys, e0 = [], 0
    for group in FFN_GROUP_SIZES:
        xs = _sc_gather(rsel_t[:, e0:e0 + group], hf.reshape(b * n, d // 2), cap)
        ys.append(_ffn(xs.reshape(b, group, cap, d // 2), w_gate[0], w_up[0], w_down[0], e0))
        e0 += group
    picked = (rsel_t >= 0).astype(jnp.int32).reshape(b, N_EXPERTS, n // LANES, LANES).sum(axis=-1)
    slot_off = jnp.concatenate([jnp.zeros((b, N_EXPERTS, 1), jnp.int32), jnp.cumsum(picked, axis=-1)], axis=-1)
    return _combine(ys, slot_off, rsel_t, aff_t, x_new, g_f, norm_ffn_post[0])
```
